```python
import jax
import jax.numpy as jnp
from jax import lax
import numpy as np

D_MODEL = 2048
BATCH = 8
SEQ = 2048
DEPTH = 1
DEC_BATCH = 32
DEC_SEQ = 4
PAST_LEN = 16384
PAGE_SIZE = 128

HEAD_DIM = 128
NSA_WIDTH = D_MODEL // 2
NSA_HEADS = NSA_WIDTH // HEAD_DIM
NSA_KV_HEADS = 2
HPG = NSA_HEADS // NSA_KV_HEADS
KV_WIDTH = NSA_KV_HEADS * HEAD_DIM
CMP_BLOCK = 32
CMP_STRIDE = 16
SEL_BLOCK = 64
N_SELECT = 16
WINDOW = 512
FORCE_BONUS = 1e4
SEL_QBLOCK = 32
WIN_QBLOCK = 128
RWKV_WIDTH = D_MODEL - NSA_WIDTH
RWKV_HEAD_DIM = 64
RWKV_HEADS = RWKV_WIDTH // RWKV_HEAD_DIM
DECAY_LORA = 64
ICLR_LORA = 64
GATE_LORA = 128
GN_EPS = 64e-5
MIX_WIDTH = NSA_WIDTH + RWKV_WIDTH
N_GROUPS = 4
EXPERTS_PER_GROUP = 8
N_EXPERTS = N_GROUPS * EXPERTS_PER_GROUP
EXPERT_TOP_K = 2
D_EXPERT = 512
MOE_BLOCK = 64
NORM_EPS = 1e-6
NEG_INF = -1e30
Q_OFF = 0
CMP_OFF = Q_OFF + NSA_WIDTH
SLC_OFF = CMP_OFF + 2 * KV_WIDTH
WIN_OFF = SLC_OFF + 2 * KV_WIDTH
RKV_OFF = WIN_OFF + 2 * KV_WIDTH
GATE_OFF = RKV_OFF + 3 * RWKV_WIDTH
IN_WIDTH = GATE_OFF + 3 * NSA_HEADS

kernel_name = 'hymba_nsa_rwkv7_hmoe_step'


def rmsnorm(x, g):
    xf = x.astype(jnp.float32)
    y = xf * lax.rsqrt(jnp.mean(xf * xf, axis=-1, keepdims=True) + NORM_EPS)
    return (y * g.astype(jnp.float32)).astype(x.dtype)


def alibi_slopes():
    n = jnp.arange(1, NSA_HEADS + 1, dtype=jnp.float32)
    return jnp.exp2(-8.0 * n / NSA_HEADS).reshape(NSA_KV_HEADS, HPG)


def compress_blocks(rows, w1, pe, w2):
    B, L = rows.shape[:2]
    n_chunk = L // CMP_STRIDE
    r = CMP_BLOCK // CMP_STRIDE
    nc = n_chunk - r + 1
    ch = rows.reshape(B, n_chunk, CMP_STRIDE, NSA_KV_HEADS, HEAD_DIM)
    pre = jnp.einsum('pd,pde->e', pe, w1)
    for i in range(r):
        pre = pre + jnp.einsum('bcpgd,pde->bcge', ch[:, i:i + nc], w1[i * CMP_STRIDE:(i + 1) * CMP_STRIDE])
    return jnp.einsum('bcge,ef->bcgf', jax.nn.silu(pre), w2)


def overlap_matrix(nc, ns):
    c_start = jnp.arange(nc) * CMP_STRIDE
    c_end = c_start + CMP_BLOCK - 1
    s_start = jnp.arange(ns) * SEL_BLOCK
    s_end = s_start + SEL_BLOCK - 1
    return ((c_start[:, None] <= s_end[None]) & (c_end[:, None] >= s_start[None])).astype(jnp.float32)


def compressed_branch(qg, pos_q, kc, vc, n_sel, slopes):
    nc = kc.shape[1]
    end_c = jnp.arange(nc) * CMP_STRIDE + CMP_BLOCK - 1
    dist = pos_q[:, None] - end_c[None, :]
    valid = (dist >= 0)[None, :, None, None, :]
    s = jnp.einsum('btghd,bcgd->btghc', qg, kc).astype(jnp.float32) * HEAD_DIM ** -0.5
    s = s - slopes[None, None, :, :, None] * dist.astype(jnp.float32)[None, :, None, None, :]
    p = jax.nn.softmax(jnp.where(valid, s, NEG_INF), axis=-1) * valid
    o = jnp.einsum('btghc,bcgd->btghd', p.astype(vc.dtype), vc)
    imp = jnp.einsum('btghc,cj->btgj', p, overlap_matrix(nc, n_sel))
    blk_t = (pos_q // SEL_BLOCK)[:, None]
    j = jnp.arange(n_sel)[None, :]
    forced = (j == 0) | (j == blk_t) | (j == blk_t - 1)
    imp = jnp.where(forced[None, :, None, :], imp + FORCE_BONUS, imp)
    imp = jnp.where((j <= blk_t)[None, :, None, :], imp, NEG_INF)
    top_val, top_idx = lax.top_k(imp, min(N_SELECT, n_sel))
    return o, top_idx, top_val > 0.5 * NEG_INF


def gather_selected(idx, new_blk, n_past_blk, pool, layer, page_table):
    b_idx = jnp.arange(idx.shape[0])[:, None, None, None]
    g_idx = jnp.arange(NSA_KV_HEADS)[None, None, :, None]
    local = jnp.clip(idx - n_past_blk, 0, new_blk.shape[1] - 1)
    blk = new_blk[b_idx, local, :, :, g_idx]
    if pool is not None:
        sub_per_page = PAGE_SIZE // SEL_BLOCK
        past = jnp.clip(idx, 0, n_past_blk - 1)
        phys = page_table[b_idx, past // sub_per_page]
        pool_r = pool.reshape(pool.shape[0], pool.shape[1], sub_per_page, SEL_BLOCK, 2, NSA_KV_HEADS, HEAD_DIM)
        l_idx = jnp.full_like(phys, layer)
        pblk = pool_r[l_idx, phys, past % sub_per_page, :, :, g_idx]
        blk = jnp.where((idx < n_past_blk)[..., None, None, None], pblk, blk)
    return blk[..., 0, :], blk[..., 1, :]


def selected_branch(qg, pos_q, kb, vb, idx, sel_valid, slopes):
    B, Tq, G, H, _ = qg.shape
    s = jnp.einsum('btghd,btgkpd->btghkp', qg, kb).astype(jnp.float32) * HEAD_DIM ** -0.5
    key_pos = idx[..., None] * SEL_BLOCK + jnp.arange(SEL_BLOCK)
    dist = pos_q[None, :, None, None, None] - key_pos
    mask = ((dist >= 0) & sel_valid[..., None])[:, :, :, None]
    s = s - slopes[None, None, :, :, None, None] * dist.astype(jnp.float32)[:, :, :, None]
    s = jnp.where(mask, s, NEG_INF)
    p = jax.nn.softmax(s.reshape(B, Tq, G, H, -1), axis=-1).reshape(s.shape)
    return jnp.einsum('btghkp,btgkpd->btghd', p.astype(vb.dtype), vb)


def window_attend(qg, pos_q, k, v, key_pos, slopes):
    s = jnp.einsum('btghd,bsgd->btghs', qg, k).astype(jnp.float32) * HEAD_DIM ** -0.5
    dist = pos_q[:, None] - key_pos[None, :]
    mask = ((dist >= 0) & (dist < WINDOW) & (key_pos[None, :] >= 0))[None, :, None, None, :]
    s = s - slopes[None, None, :, :, None] * dist.astype(jnp.float32)[None, :, None, None, :]
    p = jax.nn.softmax(jnp.where(mask, s, NEG_INF), axis=-1)
    return jnp.einsum('btghs,bsgd->btghd', p.astype(v.dtype), v)


def nsa_group(proj, pos_q, lp, layer, pool_cmp, pool_slc, page_table, win_buf):
    B, T = proj.shape[:2]
    G, dh = NSA_KV_HEADS, HEAD_DIM
    slopes = alibi_slopes()
    qg = proj[..., Q_OFF:CMP_OFF].reshape(B, T, G, HPG, dh)
    cmp_new = proj[..., CMP_OFF:SLC_OFF].reshape(B, T, 2, G, dh)
    slc_new = proj[..., SLC_OFF:WIN_OFF].reshape(B, T, 2, G, dh)
    win_new = proj[..., WIN_OFF:RKV_OFF].reshape(B, T, 2, G, dh)
    gates = jax.nn.sigmoid(proj[..., GATE_OFF:IN_WIDTH].astype(jnp.float32)).reshape(B, T, 3, G, HPG, 1)

    if pool_cmp is None:
        n_past = 0
        cmp_rows = cmp_new
    else:
        n_past = page_table.shape[1] * PAGE_SIZE
        past_rows = pool_cmp[layer, page_table].reshape(B, -1, 2, G, dh)
        cmp_rows = jnp.concatenate([past_rows, cmp_new], axis=1)
    l16 = (cmp_rows.shape[1] // CMP_STRIDE) * CMP_STRIDE
    kc = compress_blocks(cmp_rows[:, :l16, 0], lp['cmp_k_w1'], lp['cmp_k_pe'], lp['cmp_k_w2'])
    vc = compress_blocks(cmp_rows[:, :l16, 1], lp['cmp_v_w1'], lp['cmp_v_pe'], lp['cmp_v_w2'])
    n_sel = -(-(n_past + T) // SEL_BLOCK)
    o_cmp, idx, sel_valid = compressed_branch(qg, pos_q, kc, vc, n_sel, slopes)

    n_past_blk = n_past // SEL_BLOCK
    n_new_blk = -(-T // SEL_BLOCK)
    new_blk = jnp.pad(slc_new, ((0, 0), (0, n_new_blk * SEL_BLOCK - T), (0, 0), (0, 0), (0, 0)))
    new_blk = new_blk.reshape(B, n_new_blk, SEL_BLOCK, 2, G, dh)

    def sel_block(args):
        q_b, pos_b, idx_b, valid_b = args
        kb, vb = gather_selected(idx_b, new_blk, n_past_blk, pool_slc, layer, page_table)
        return selected_branch(q_b, pos_b, kb, vb, idx_b, valid_b, slopes)

    if T > SEL_QBLOCK and T % SEL_QBLOCK == 0:
        nqb = T // SEL_QBLOCK

        def blocks(a):
            return a.reshape((B, nqb, SEL_QBLOCK) + a.shape[2:]).swapaxes(0, 1)

        o_sel = lax.map(sel_block, (blocks(qg), pos_q.reshape(nqb, SEL_QBLOCK), blocks(idx), blocks(sel_valid)))
        o_sel = o_sel.swapaxes(0, 1).reshape(B, T, G, HPG, dh)
    else:
        o_sel = sel_block((qg, pos_q, idx, sel_valid))

    win_len = min(WINDOW, PAST_LEN)
    if win_buf is None:
        nq = T // WIN_QBLOCK
        n_pre = -(-WINDOW // WIN_QBLOCK)
        kp = jnp.pad(win_new, ((0, 0), (n_pre * WIN_QBLOCK, 0), (0, 0), (0, 0), (0, 0)))
        kp = kp.reshape(B, nq + n_pre, WIN_QBLOCK, 2, G, dh)
        band = jnp.concatenate([kp[:, i:i + nq] for i in range(n_pre + 1)], axis=2)
        key_pos = (jnp.arange(nq)[:, None] - n_pre) * WIN_QBLOCK + jnp.arange((n_pre + 1) * WIN_QBLOCK)[None]
        o_win = jax.vmap(window_attend, in_axes=(1, 0, 1, 1, 0, None), out_axes=1)(
            qg.reshape(B, nq, WIN_QBLOCK, G, HPG, dh), pos_q.reshape(nq, WIN_QBLOCK),
            band[:, :, :, 0], band[:, :, :, 1], key_pos, slopes).reshape(B, T, G, HPG, dh)
        pad_rows = jnp.zeros((B, win_len, 2, G, dh), win_new.dtype)
        win_state = jnp.concatenate([pad_rows, win_new], axis=1)[:, -win_len:]
    else:
        keys = jnp.concatenate([win_buf, win_new], axis=1)
        key_pos = n_past - win_buf.shape[1] + jnp.arange(keys.shape[1])
        o_win = window_attend(qg, pos_q, keys[:, :, 0], keys[:, :, 1], key_pos, slopes)
        win_state = keys[:, -win_buf.shape[1]:]

    o = gates[:, :, 0] * o_cmp + gates[:, :, 1] * o_sel + gates[:, :, 2] * o_win
    return o.reshape(B, T, NSA_WIDTH).astype(proj.dtype), cmp_new, slc_new, win_state


def rwkv_group(h, h_shift, p_rkv, p_rkv_prev, wkv0, lp):
    B, T, _ = h.shape
    f32 = jnp.float32
    rkv = (p_rkv + lp['mu_rkv'] * (p_rkv_prev - p_rkv)).astype(f32)
    r, k, v = jnp.split(rkv, 3, axis=-1)
    xx = h_shift - h
    xw = h + xx * lp['mu_wag'][0]
    xa = h + xx * lp['mu_wag'][1]
    xg = h + xx * lp['mu_wag'][2]
    w_log = -jax.nn.softplus(-(lp['decay_w0'] + jnp.tanh(xw @ lp['decay_w1']) @ lp['decay_w2']).astype(f32)) - 0.5
    decay = jnp.exp(-jnp.exp(w_log))
    a = jax.nn.sigmoid((lp['iclr_a0'] + (xa @ lp['iclr_a1']) @ lp['iclr_a2']).astype(f32))
    g = (jax.nn.sigmoid(xg @ lp['gate_g1']) @ lp['gate_g2']).astype(f32)

    def heads(t):
        return t.reshape(B, T, RWKV_HEADS, RWKV_HEAD_DIM)

    kk = heads(k * lp['k_k'])
    kk = kk / jnp.maximum(jnp.sqrt(jnp.sum(kk * kk, axis=-1, keepdims=True)), 1e-12)
    k = k * (1.0 + (a - 1.0) * lp['k_a'])
    r_h, k_h, v_h, w_h, a_h = heads(r), heads(k), heads(v), heads(decay), heads(a)

    def step(S, inp):
        r_t, w_t, k_t, v_t, kk_t, a_t = inp
        sa = jnp.einsum('bhij,bhj->bhi', S, -kk_t)
        S = S * w_t[:, :, None, :] + sa[..., None] * (kk_t * a_t)[:, :, None, :] + v_t[..., None] * k_t[:, :, None, :]
        return S, jnp.einsum('bhij,bhj->bhi', S, r_t)

    seq = tuple(jnp.moveaxis(t, 1, 0) for t in (r_h, w_h, k_h, v_h, kk, a_h))
    S_T, ys = lax.scan(step, wkv0.astype(f32), seq)
    y = jnp.moveaxis(ys, 0, 1)
    mu = jnp.mean(y, axis=-1, keepdims=True)
    var = jnp.mean(jnp.square(y - mu), axis=-1, keepdims=True)
    y = ((y - mu) * lax.rsqrt(var + GN_EPS)).reshape(B, T, RWKV_WIDTH) * lp['ln_x_w'] + lp['ln_x_b']
    bonus = (jnp.sum(r_h * k_h * lp['r_k'], axis=-1, keepdims=True) * v_h).reshape(B, T, RWKV_WIDTH)
    return (y + bonus) * g, S_T


def hier_route(h, w_rg, b_rg, w_re, b_re):
    n = h.shape[0]
    pg = jax.nn.softmax((h @ w_rg + b_rg).astype(jnp.float32), axis=-1)
    g_val, g_sel = lax.top_k(pg, 1)
    le = (h @ w_re + b_re).astype(jnp.float32).reshape(n, N_GROUPS, EXPERTS_PER_GROUP)
    le_g = jnp.take_along_axis(le, g_sel[:, :, None], axis=1)[:, 0]
    e_val, e_sel = lax.top_k(le_g, EXPERT_TOP_K)
    weights = jax.nn.softmax(e_val, axis=-1) * g_val
    return g_sel * EXPERTS_PER_GROUP + e_sel, weights


def moe_ffn(h, eid, ew, w_gate, w_up, w_down):
    n, d = h.shape
    a_tot = n * EXPERT_TOP_K
    flat_e = eid.reshape(-1)
    flat_t = jnp.arange(a_tot) // EXPERT_TOP_K
    flat_w = ew.reshape(-1)
    order = jnp.argsort(flat_e)
    se, st, sw = flat_e[order], flat_t[order], flat_w[order]
    counts = jnp.bincount(flat_e, length=N_EXPERTS)
    padded = (counts + MOE_BLOCK - 1) // MOE_BLOCK * MOE_BLOCK
    pad_end = jnp.cumsum(padded)
    pad_start = pad_end - padded
    start = jnp.cumsum(counts) - counts
    dest = pad_start[se] + jnp.arange(a_tot) - start[se]
    n_blk = (a_tot + N_EXPERTS * (MOE_BLOCK - 1) + MOE_BLOCK - 1) // MOE_BLOCK
    tok_buf = jnp.full((n_blk * MOE_BLOCK,), n, jnp.int32).at[dest].set(st)
    h_pad = jnp.concatenate([h, jnp.zeros((1, d), h.dtype)], axis=0)
    xb = h_pad[tok_buf].reshape(n_blk, MOE_BLOCK, d)
    blk_e = jnp.clip(jnp.searchsorted(pad_end, jnp.arange(n_blk) * MOE_BLOCK, side='right'), 0, N_EXPERTS - 1)

    def expert_block(args):
        xblk, e = args
        return (jax.nn.silu(xblk @ w_gate[e]) * (xblk @ w_up[e])) @ w_down[e]

    yb = lax.map(expert_block, (xb, blk_e)).reshape(n_blk * MOE_BLOCK, d)
    out = jnp.zeros((n, d), jnp.float32).at[st].add(yb[dest].astype(jnp.float32) * sw[:, None])
    return out.astype(h.dtype)


def decoder_layer(x, c, pos_q, lp, layer, pool_cmp, pool_slc, page_table, win_buf, wkv0, shift0):
    B, T, D = x.shape
    mod = jax.nn.silu(c) @ lp['w_ada'] + lp['b_ada']
    sh1, sc1, gt1, sh2, sc2, gt2 = jnp.split(mod[:, None, :], 6, axis=-1)
    h = rmsnorm(x, lp['norm1']) * (1.0 + sc1) + sh1
    proj = h @ lp['w_in']
    shift0 = shift0.astype(h.dtype)
    h_shift = jnp.concatenate([shift0[:, None], h[:, :-1]], axis=1)
    p_rkv = proj[..., RKV_OFF:GATE_OFF]
    p_rkv_prev = jnp.concatenate([(shift0 @ lp['w_in'][:, RKV_OFF:GATE_OFF])[:, None], p_rkv[:, :-1]], axis=1)
    o_nsa, cmp_new, slc_new, win_state = nsa_group(proj, pos_q, lp, layer, pool_cmp, pool_slc, page_table, win_buf)
    o_rwkv, wkv_T = rwkv_group(h, h_shift, p_rkv, p_rkv_prev, wkv0, lp)
    mixed = jnp.concatenate([o_nsa, o_rwkv.astype(o_nsa.dtype)], axis=-1) @ lp['w_out']
    x = x + gt1 * mixed
    h2 = (rmsnorm(x, lp['norm2']) * (1.0 + sc2) + sh2).reshape(B * T, D)
    eid, ew = hier_route(h2, lp['w_router_group'], lp['b_router_group'], lp['w_router_expert'], lp['b_router_expert'])
    ffn = moe_ffn(h2, eid, ew, lp['w_gate'], lp['w_up'], lp['w_down']).reshape(B, T, D)
    x = x + gt2 * ffn
    return x, cmp_new, slc_new, win_state, wkv_T, h[:, -1]


def setup_inputs(seed: int = 0) -> dict:
    key = jax.random.key(seed)
    keys = iter(jax.random.split(key, 64))

    def nrm(shape, scale=1.0):
        return jax.random.normal(next(keys), shape, jnp.float32) * scale

    def gain(shape):
        return 1.0 + nrm(shape, 0.02)

    def unif(shape):
        return jax.random.uniform(next(keys), shape, jnp.float32)

    n_pages = PAST_LEN // PAGE_SIZE
    n_phys = (DEC_BATCH * n_pages * 5) // 4
    win_len = min(WINDOW, PAST_LEN)
    row = (2, NSA_KV_HEADS, HEAD_DIM)
    perm = jax.random.permutation(next(keys), n_phys)
    page_table = perm[:DEC_BATCH * n_pages].reshape(DEC_BATCH, n_pages).astype(jnp.int32)
    L = DEPTH
    D = D_MODEL
    return {
        'x_prompt': nrm((BATCH, SEQ, D)),
        'x_sample': nrm((DEC_BATCH, DEC_SEQ, D)),
        'c_prompt': nrm((BATCH, D)),
        'c_sample': nrm((DEC_BATCH, D)),
        'cache_cmp_kv': nrm((L, n_phys, PAGE_SIZE) + row),
        'cache_slc_kv': nrm((L, n_phys, PAGE_SIZE) + row),
        'page_table': page_table,
        'state_win_kv': nrm((L, DEC_BATCH, win_len) + row),
        'state_wkv': nrm((L, DEC_BATCH, RWKV_HEADS, RWKV_HEAD_DIM, RWKV_HEAD_DIM), 0.5),
        'state_shift': nrm((L, DEC_BATCH, D)),
        'w_ada': nrm((L, D, 6 * D), 0.5 * D ** -0.5),
        'b_ada': nrm((L, 6 * D), 0.02),
        'norm1': gain((L, D)),
        'w_in': nrm((L, D, IN_WIDTH), D ** -0.5),
        'cmp_k_w1': nrm((L, CMP_BLOCK, HEAD_DIM, HEAD_DIM), (CMP_BLOCK * HEAD_DIM) ** -0.5),
        'cmp_k_pe': nrm((L, CMP_BLOCK, HEAD_DIM), 0.5),
        'cmp_k_w2': nrm((L, HEAD_DIM, HEAD_DIM), HEAD_DIM ** -0.5),
        'cmp_v_w1': nrm((L, CMP_BLOCK, HEAD_DIM, HEAD_DIM), (CMP_BLOCK * HEAD_DIM) ** -0.5),
        'cmp_v_pe': nrm((L, CMP_BLOCK, HEAD_DIM), 0.5),
        'cmp_v_w2': nrm((L, HEAD_DIM, HEAD_DIM), HEAD_DIM ** -0.5),
        'mu_rkv': unif((L, 3 * RWKV_WIDTH)),
        'mu_wag': unif((L, 3, D)),
        'decay_w0': nrm((L, RWKV_WIDTH), 0.5),
        'decay_w1': nrm((L, D, DECAY_LORA), D ** -0.5),
        'decay_w2': nrm((L, DECAY_LORA, RWKV_WIDTH), 0.5 * DECAY_LORA ** -0.5),
        'iclr_a0': nrm((L, RWKV_WIDTH), 0.5),
        'iclr_a1': nrm((L, D, ICLR_LORA), D ** -0.5),
        'iclr_a2': nrm((L, ICLR_LORA, RWKV_WIDTH), 0.5 * ICLR_LORA ** -0.5),
        'gate_g1': nrm((L, D, GATE_LORA), D ** -0.5),
        'gate_g2': nrm((L, GATE_LORA, RWKV_WIDTH), GATE_LORA ** -0.5),
        'k_k': 0.85 + nrm((L, RWKV_WIDTH), 0.05),
        'k_a': gain((L, RWKV_WIDTH)),
        'r_k': nrm((L, RWKV_HEADS, RWKV_HEAD_DIM), 0.1),
        'ln_x_w': gain((L, RWKV_WIDTH)),
        'ln_x_b': nrm((L, RWKV_WIDTH), 0.02),
        'w_out': nrm((L, MIX_WIDTH, D), MIX_WIDTH ** -0.5),
        'norm2': gain((L, D)),
        'w_router_group': nrm((L, D, N_GROUPS), D ** -0.5),
        'b_router_group': nrm((L, N_GROUPS), 0.01),
        'w_router_expert': nrm((L, D, N_EXPERTS), D ** -0.5),
        'b_router_expert': nrm((L, N_EXPERTS), 0.01),
        'w_gate': nrm((L, N_EXPERTS, D, D_EXPERT), D ** -0.5),
        'w_up': nrm((L, N_EXPERTS, D, D_EXPERT), D ** -0.5),
        'w_down': nrm((L, N_EXPERTS, D_EXPERT, D), D_EXPERT ** -0.5),
        'norm_f': gain((D,)),
    }


def reference(x_prompt, x_sample, c_prompt, c_sample, cache_cmp_kv, cache_slc_kv, page_table,
              state_win_kv, state_wkv, state_shift, w_ada, b_ada, norm1, w_in,
              cmp_k_w1, cmp_k_pe, cmp_k_w2, cmp_v_w1, cmp_v_pe, cmp_v_w2,
              mu_rkv, mu_wag, decay_w0, decay_w1, decay_w2, iclr_a0, iclr_a1, iclr_a2,
              gate_g1, gate_g2, k_k, k_a, r_k, ln_x_w, ln_x_b, w_out, norm2,
              w_router_group, b_router_group, w_router_expert, b_router_expert,
              w_gate, w_up, w_down, norm_f):
    bp, tp = x_prompt.shape[:2]
    bs, ts = x_sample.shape[:2]
    pos_p = jnp.arange(tp)
    pos_s = page_table.shape[1] * PAGE_SIZE + jnp.arange(ts)
    hp, hs = x_prompt, x_sample
    cmp_p, cmp_s, slc_p, slc_s, win_p, win_s = [], [], [], [], [], []
    wkv_p, wkv_s, sft_p, sft_s = [], [], [], []
    for l in range(DEPTH):
        lp = dict(w_ada=w_ada[l], b_ada=b_ada[l], norm1=norm1[l], w_in=w_in[l],
                  cmp_k_w1=cmp_k_w1[l], cmp_k_pe=cmp_k_pe[l], cmp_k_w2=cmp_k_w2[l],
                  cmp_v_w1=cmp_v_w1[l], cmp_v_pe=cmp_v_pe[l], cmp_v_w2=cmp_v_w2[l],
                  mu_rkv=mu_rkv[l], mu_wag=mu_wag[l], decay_w0=decay_w0[l], decay_w1=decay_w1[l],
                  decay_w2=decay_w2[l], iclr_a0=iclr_a0[l], iclr_a1=iclr_a1[l], iclr_a2=iclr_a2[l],
                  gate_g1=gate_g1[l], gate_g2=gate_g2[l], k_k=k_k[l], k_a=k_a[l], r_k=r_k[l],
                  ln_x_w=ln_x_w[l], ln_x_b=ln_x_b[l], w_out=w_out[l], norm2=norm2[l],
                  w_router_group=w_router_group[l], b_router_group=b_router_group[l],
                  w_router_expert=w_router_expert[l], b_router_expert=b_router_expert[l],
                  w_gate=w_gate[l], w_up=w_up[l], w_down=w_down[l])
        wkv_zero = jnp.zeros((bp, RWKV_HEADS, RWKV_HEAD_DIM, RWKV_HEAD_DIM), jnp.float32)
        shift_zero = jnp.zeros((bp, D_MODEL), x_prompt.dtype)
        hp, a1, a2, a3, a4, a5 = decoder_layer(hp, c_prompt, pos_p, lp, l, None, None, None, None, wkv_zero, shift_zero)
        hs, b1, b2, b3, b4, b5 = decoder_layer(hs, c_sample, pos_s, lp, l, cache_cmp_kv, cache_slc_kv, page_table,
                                               state_win_kv[l], state_wkv[l], state_shift[l])
        cmp_p.append(a1); slc_p.append(a2); win_p.append(a3); wkv_p.append(a4); sft_p.append(a5)
        cmp_s.append(b1); slc_s.append(b2); win_s.append(b3); wkv_s.append(b4); sft_s.append(b5)
    y_prompt = rmsnorm(hp, norm_f)
    y_sample = rmsnorm(hs, norm_f)
    return (y_prompt, y_sample,
            jnp.stack(cmp_p), jnp.stack(cmp_s), jnp.stack(slc_p), jnp.stack(slc_s),
            jnp.stack(win_p), jnp.stack(win_s), jnp.stack(wkv_p), jnp.stack(wkv_s),
            jnp.stack(sft_p), jnp.stack(sft_s))
```

```python
import functools

import jax
import jax.numpy as jnp
from jax import lax
from jax.experimental import pallas as pl
from jax.experimental.pallas import tpu as pltpu

D_MODEL = 2048
DEPTH = 1
PAGE_SIZE = 128
HEAD_DIM = 128
NSA_WIDTH = D_MODEL // 2
NSA_HEADS = NSA_WIDTH // HEAD_DIM
NSA_KV_HEADS = 2
HPG = NSA_HEADS // NSA_KV_HEADS
KV_WIDTH = NSA_KV_HEADS * HEAD_DIM
CMP_BLOCK = 32
CMP_STRIDE = 16
SEL_BLOCK = 64
N_SELECT = 16
WINDOW = 512
FORCE_BONUS = 1e4
SEL_QBLOCK = 32
WIN_QBLOCK = 128
RWKV_WIDTH = D_MODEL - NSA_WIDTH
RWKV_HEAD_DIM = 64
RWKV_HEADS = RWKV_WIDTH // RWKV_HEAD_DIM
GN_EPS = 64e-5
N_GROUPS = 4
EXPERTS_PER_GROUP = 8
N_EXPERTS = N_GROUPS * EXPERTS_PER_GROUP
EXPERT_TOP_K = 2
D_EXPERT = 512
MOE_BLOCK = 64
NORM_EPS = 1e-6
NEG_INF = -1e30
Q_OFF = 0
CMP_OFF = Q_OFF + NSA_WIDTH
SLC_OFF = CMP_OFF + 2 * KV_WIDTH
WIN_OFF = SLC_OFF + 2 * KV_WIDTH
RKV_OFF = WIN_OFF + 2 * KV_WIDTH
GATE_OFF = RKV_OFF + 3 * RWKV_WIDTH
IN_WIDTH = GATE_OFF + 3 * NSA_HEADS

VMEM_LIMIT_BYTES = 48 * 1024 * 1024


def _mm_kernel(x_ref, w_ref, o_ref):
    o_ref[...] = jnp.dot(x_ref[...].astype(jnp.bfloat16), w_ref[...].astype(jnp.bfloat16),
                         preferred_element_type=jnp.float32)


def _matmul(x, w, tm, tn):
    m, k = x.shape
    n = w.shape[1]
    return pl.pallas_call(
        _mm_kernel,
        grid=(pl.cdiv(m, tm), pl.cdiv(n, tn)),
        in_specs=[pl.BlockSpec((tm, k), lambda i, j: (i, 0)),
                  pl.BlockSpec((k, tn), lambda i, j: (0, j))],
        out_specs=pl.BlockSpec((tm, tn), lambda i, j: (i, j)),
        out_shape=jax.ShapeDtypeStruct((m, n), jnp.float32),
        compiler_params=pltpu.CompilerParams(
            dimension_semantics=("arbitrary", "arbitrary"), vmem_limit_bytes=VMEM_LIMIT_BYTES),
        name="matmul",
    )(x, w)


def _moe_block_kernel(blk_e_ref, x_ref, wg_ref, wu_ref, wd_ref, o_ref):
    del blk_e_ref
    x = x_ref[0].astype(jnp.bfloat16)
    g = jnp.dot(x, wg_ref[0].astype(jnp.bfloat16), preferred_element_type=jnp.float32)
    u = jnp.dot(x, wu_ref[0].astype(jnp.bfloat16), preferred_element_type=jnp.float32)
    hmid = (g * jax.nn.sigmoid(g)) * u
    o_ref[0] = jnp.dot(hmid.astype(jnp.bfloat16), wd_ref[0].astype(jnp.bfloat16),
                       preferred_element_type=jnp.float32)


def _moe_blocks(xb, blk_e, w_gate, w_up, w_down):
    n_blk, mb, d = xb.shape
    de = w_gate.shape[2]
    grid_spec = pltpu.PrefetchScalarGridSpec(
        num_scalar_prefetch=1,
        grid=(n_blk,),
        in_specs=[pl.BlockSpec((1, mb, d), lambda i, e: (i, 0, 0)),
                  pl.BlockSpec((1, d, de), lambda i, e: (e[i], 0, 0)),
                  pl.BlockSpec((1, d, de), lambda i, e: (e[i], 0, 0)),
                  pl.BlockSpec((1, de, d), lambda i, e: (e[i], 0, 0))],
        out_specs=pl.BlockSpec((1, mb, d), lambda i, e: (i, 0, 0)),
    )
    return pl.pallas_call(
        _moe_block_kernel,
        grid_spec=grid_spec,
        out_shape=jax.ShapeDtypeStruct((n_blk, mb, d), jnp.float32),
        compiler_params=pltpu.CompilerParams(
            dimension_semantics=("arbitrary",), vmem_limit_bytes=VMEM_LIMIT_BYTES),
        name="moe_blocks",
    )(blk_e, xb, w_gate, w_up, w_down)


def _rmsnorm(x, g):
    xf = x.astype(jnp.float32)
    y = xf * lax.rsqrt(jnp.mean(xf * xf, axis=-1, keepdims=True) + NORM_EPS)
    return (y * g.astype(jnp.float32)).astype(x.dtype)


def _alibi_slopes():
    n = jnp.arange(1, NSA_HEADS + 1, dtype=jnp.float32)
    return jnp.exp2(-8.0 * n / NSA_HEADS).reshape(NSA_KV_HEADS, HPG)


def _compress_blocks(rows, w1, pe, w2):
    B, L = rows.shape[:2]
    n_chunk = L // CMP_STRIDE
    r = CMP_BLOCK // CMP_STRIDE
    nc = n_chunk - r + 1
    ch = rows.reshape(B, n_chunk, CMP_STRIDE, NSA_KV_HEADS, HEAD_DIM)
    pre = jnp.einsum('pd,pde->e', pe, w1)
    for i in range(r):
        pre = pre + jnp.einsum('bcpgd,pde->bcge', ch[:, i:i + nc], w1[i * CMP_STRIDE:(i + 1) * CMP_STRIDE])
    return jnp.einsum('bcge,ef->bcgf', jax.nn.silu(pre), w2)


def _overlap_matrix(nc, ns):
    c_start = jnp.arange(nc) * CMP_STRIDE
    c_end = c_start + CMP_BLOCK - 1
    s_start = jnp.arange(ns) * SEL_BLOCK
    s_end = s_start + SEL_BLOCK - 1
    return ((c_start[:, None] <= s_end[None]) & (c_end[:, None] >= s_start[None])).astype(jnp.float32)


def _compressed_branch(qg, pos_q, kc, vc, n_sel, slopes):
    nc = kc.shape[1]
    end_c = jnp.arange(nc) * CMP_STRIDE + CMP_BLOCK - 1
    dist = pos_q[:, None] - end_c[None, :]
    valid = (dist >= 0)[None, :, None, None, :]
    s = jnp.einsum('btghd,bcgd->btghc', qg, kc).astype(jnp.float32) * HEAD_DIM ** -0.5
    s = s - slopes[None, None, :, :, None] * dist.astype(jnp.float32)[None, :, None, None, :]
    p = jax.nn.softmax(jnp.where(valid, s, NEG_INF), axis=-1) * valid
    o = jnp.einsum('btghc,bcgd->btghd', p.astype(vc.dtype), vc)
    imp = jnp.einsum('btghc,cj->btgj', p, _overlap_matrix(nc, n_sel))
    blk_t = (pos_q // SEL_BLOCK)[:, None]
    j = jnp.arange(n_sel)[None, :]
    forced = (j == 0) | (j == blk_t) | (j == blk_t - 1)
    imp = jnp.where(forced[None, :, None, :], imp + FORCE_BONUS, imp)
    imp = jnp.where((j <= blk_t)[None, :, None, :], imp, NEG_INF)
    top_val, top_idx = lax.top_k(imp, min(N_SELECT, n_sel))
    return o, top_idx, top_val > 0.5 * NEG_INF


def _gather_selected(idx, new_blk, n_past_blk, pool, layer, page_table):
    b_idx = jnp.arange(idx.shape[0])[:, None, None, None]
    g_idx = jnp.arange(NSA_KV_HEADS)[None, None, :, None]
    local = jnp.clip(idx - n_past_blk, 0, new_blk.shape[1] - 1)
    blk = new_blk[b_idx, local, :, :, g_idx]
    if pool is not None:
        sub_per_page = PAGE_SIZE // SEL_BLOCK
        past = jnp.clip(idx, 0, n_past_blk - 1)
        phys = page_table[b_idx, past // sub_per_page]
        pool_r = pool.reshape(pool.shape[0], pool.shape[1], sub_per_page, SEL_BLOCK, 2, NSA_KV_HEADS, HEAD_DIM)
        l_idx = jnp.full_like(phys, layer)
        pblk = pool_r[l_idx, phys, past % sub_per_page, :, :, g_idx]
        blk = jnp.where((idx < n_past_blk)[..., None, None, None], pblk, blk)
    return blk[..., 0, :], blk[..., 1, :]


def _selected_branch(qg, pos_q, kb, vb, idx, sel_valid, slopes):
    B, Tq, G, H, _ = qg.shape
    s = jnp.einsum('btghd,btgkpd->btghkp', qg, kb).astype(jnp.float32) * HEAD_DIM ** -0.5
    key_pos = idx[..., None] * SEL_BLOCK + jnp.arange(SEL_BLOCK)
    dist = pos_q[None, :, None, None, None] - key_pos
    mask = ((dist >= 0) & sel_valid[..., None])[:, :, :, None]
    s = s - slopes[None, None, :, :, None, None] * dist.astype(jnp.float32)[:, :, :, None]
    s = jnp.where(mask, s, NEG_INF)
    p = jax.nn.softmax(s.reshape(B, Tq, G, H, -1), axis=-1).reshape(s.shape)
    return jnp.einsum('btghkp,btgkpd->btghd', p.astype(vb.dtype), vb)


def _window_attend(qg, pos_q, k, v, key_pos, slopes):
    s = jnp.einsum('btghd,bsgd->btghs', qg, k).astype(jnp.float32) * HEAD_DIM ** -0.5
    dist = pos_q[:, None] - key_pos[None, :]
    mask = ((dist >= 0) & (dist < WINDOW) & (key_pos[None, :] >= 0))[None, :, None, None, :]
    s = s - slopes[None, None, :, :, None] * dist.astype(jnp.float32)[None, :, None, None, :]
    p = jax.nn.softmax(jnp.where(mask, s, NEG_INF), axis=-1)
    return jnp.einsum('btghs,bsgd->btghd', p.astype(v.dtype), v)


def _nsa_group(proj, pos_q, lp, layer, pool_cmp, pool_slc, page_table, win_buf, past_len):
    B, T = proj.shape[:2]
    G, dh = NSA_KV_HEADS, HEAD_DIM
    slopes = _alibi_slopes()
    qg = proj[..., Q_OFF:CMP_OFF].reshape(B, T, G, HPG, dh)
    cmp_new = proj[..., CMP_OFF:SLC_OFF].reshape(B, T, 2, G, dh)
    slc_new = proj[..., SLC_OFF:WIN_OFF].reshape(B, T, 2, G, dh)
    win_new = proj[..., WIN_OFF:RKV_OFF].reshape(B, T, 2, G, dh)
    gates = jax.nn.sigmoid(proj[..., GATE_OFF:IN_WIDTH].astype(jnp.float32)).reshape(B, T, 3, G, HPG, 1)

    if pool_cmp is None:
        n_past = 0
        cmp_rows = cmp_new
    else:
        n_past = page_table.shape[1] * PAGE_SIZE
        past_rows = pool_cmp[layer, page_table].reshape(B, -1, 2, G, dh)
        cmp_rows = jnp.concatenate([past_rows, cmp_new], axis=1)
    l16 = (cmp_rows.shape[1] // CMP_STRIDE) * CMP_STRIDE
    kc = _compress_blocks(cmp_rows[:, :l16, 0], lp['cmp_k_w1'], lp['cmp_k_pe'], lp['cmp_k_w2'])
    vc = _compress_blocks(cmp_rows[:, :l16, 1], lp['cmp_v_w1'], lp['cmp_v_pe'], lp['cmp_v_w2'])
    n_sel = -(-(n_past + T) // SEL_BLOCK)
    o_cmp, idx, sel_valid = _compressed_branch(qg, pos_q, kc, vc, n_sel, slopes)

    n_past_blk = n_past // SEL_BLOCK
    n_new_blk = -(-T // SEL_BLOCK)
    new_blk = jnp.pad(slc_new, ((0, 0), (0, n_new_blk * SEL_BLOCK - T), (0, 0), (0, 0), (0, 0)))
    new_blk = new_blk.reshape(B, n_new_blk, SEL_BLOCK, 2, G, dh)

    def sel_block(args):
        q_b, pos_b, idx_b, valid_b = args
        kb, vb = _gather_selected(idx_b, new_blk, n_past_blk, pool_slc, layer, page_table)
        return _selected_branch(q_b, pos_b, kb, vb, idx_b, valid_b, slopes)

    if T > SEL_QBLOCK and T % SEL_QBLOCK == 0:
        nqb = T // SEL_QBLOCK

        def blocks(a):
            return a.reshape((B, nqb, SEL_QBLOCK) + a.shape[2:]).swapaxes(0, 1)

        o_sel = lax.map(sel_block, (blocks(qg), pos_q.reshape(nqb, SEL_QBLOCK), blocks(idx), blocks(sel_valid)))
        o_sel = o_sel.swapaxes(0, 1).reshape(B, T, G, HPG, dh)
    else:
        o_sel = sel_block((qg, pos_q, idx, sel_valid))

    win_len = min(WINDOW, past_len)
    if win_buf is None:
        nq = T // WIN_QBLOCK
        n_pre = -(-WINDOW // WIN_QBLOCK)
        kp = jnp.pad(win_new, ((0, 0), (n_pre * WIN_QBLOCK, 0), (0, 0), (0, 0), (0, 0)))
        kp = kp.reshape(B, nq + n_pre, WIN_QBLOCK, 2, G, dh)
        band = jnp.concatenate([kp[:, i:i + nq] for i in range(n_pre + 1)], axis=2)
        key_pos = (jnp.arange(nq)[:, None] - n_pre) * WIN_QBLOCK + jnp.arange((n_pre + 1) * WIN_QBLOCK)[None]
        o_win = jax.vmap(_window_attend, in_axes=(1, 0, 1, 1, 0, None), out_axes=1)(
            qg.reshape(B, nq, WIN_QBLOCK, G, HPG, dh), pos_q.reshape(nq, WIN_QBLOCK),
            band[:, :, :, 0], band[:, :, :, 1], key_pos, slopes).reshape(B, T, G, HPG, dh)
        pad_rows = jnp.zeros((B, win_len, 2, G, dh), win_new.dtype)
        win_state = jnp.concatenate([pad_rows, win_new], axis=1)[:, -win_len:]
    else:
        keys = jnp.concatenate([win_buf, win_new], axis=1)
        key_pos = n_past - win_buf.shape[1] + jnp.arange(keys.shape[1])
        o_win = _window_attend(qg, pos_q, keys[:, :, 0], keys[:, :, 1], key_pos, slopes)
        win_state = keys[:, -win_buf.shape[1]:]

    o = gates[:, :, 0] * o_cmp + gates[:, :, 1] * o_sel + gates[:, :, 2] * o_win
    return o.reshape(B, T, NSA_WIDTH).astype(proj.dtype), cmp_new, slc_new, win_state


def _rwkv_group(h, h_shift, p_rkv, p_rkv_prev, wkv0, lp):
    B, T, _ = h.shape
    f32 = jnp.float32
    rkv = (p_rkv + lp['mu_rkv'] * (p_rkv_prev - p_rkv)).astype(f32)
    r, k, v = jnp.split(rkv, 3, axis=-1)
    xx = h_shift - h
    xw = h + xx * lp['mu_wag'][0]
    xa = h + xx * lp['mu_wag'][1]
    xg = h + xx * lp['mu_wag'][2]
    w_log = -jax.nn.softplus(-(lp['decay_w0'] + jnp.tanh(xw @ lp['decay_w1']) @ lp['decay_w2']).astype(f32)) - 0.5
    decay = jnp.exp(-jnp.exp(w_log))
    a = jax.nn.sigmoid((lp['iclr_a0'] + (xa @ lp['iclr_a1']) @ lp['iclr_a2']).astype(f32))
    g = (jax.nn.sigmoid(xg @ lp['gate_g1']) @ lp['gate_g2']).astype(f32)

    def heads(t):
        return t.reshape(B, T, RWKV_HEADS, RWKV_HEAD_DIM)

    kk = heads(k * lp['k_k'])
    kk = kk / jnp.maximum(jnp.sqrt(jnp.sum(kk * kk, axis=-1, keepdims=True)), 1e-12)
    k = k * (1.0 + (a - 1.0) * lp['k_a'])
    r_h, k_h, v_h, w_h, a_h = heads(r), heads(k), heads(v), heads(decay), heads(a)

    def step(S, inp):
        r_t, w_t, k_t, v_t, kk_t, a_t = inp
        sa = jnp.einsum('bhij,bhj->bhi', S, -kk_t)
        S = S * w_t[:, :, None, :] + sa[..., None] * (kk_t * a_t)[:, :, None, :] + v_t[..., None] * k_t[:, :, None, :]
        return S, jnp.einsum('bhij,bhj->bhi', S, r_t)

    seq = tuple(jnp.moveaxis(t, 1, 0) for t in (r_h, w_h, k_h, v_h, kk, a_h))
    S_T, ys = lax.scan(step, wkv0.astype(f32), seq)
    y = jnp.moveaxis(ys, 0, 1)
    mu = jnp.mean(y, axis=-1, keepdims=True)
    var = jnp.mean(jnp.square(y - mu), axis=-1, keepdims=True)
    y = ((y - mu) * lax.rsqrt(var + GN_EPS)).reshape(B, T, RWKV_WIDTH) * lp['ln_x_w'] + lp['ln_x_b']
    bonus = (jnp.sum(r_h * k_h * lp['r_k'], axis=-1, keepdims=True) * v_h).reshape(B, T, RWKV_WIDTH)
    return (y + bonus) * g, S_T


def _hier_route(h, w_rg, b_rg, w_re, b_re):
    n = h.shape[0]
    pg = jax.nn.softmax((h @ w_rg + b_rg).astype(jnp.float32), axis=-1)
    g_val, g_sel = lax.top_k(pg, 1)
    le = (h @ w_re + b_re).astype(jnp.float32).reshape(n, N_GROUPS, EXPERTS_PER_GROUP)
    le_g = jnp.take_along_axis(le, g_sel[:, :, None], axis=1)[:, 0]
    e_val, e_sel = lax.top_k(le_g, EXPERT_TOP_K)
    weights = jax.nn.softmax(e_val, axis=-1) * g_val
    return g_sel * EXPERTS_PER_GROUP + e_sel, weights


def _moe_ffn(h, eid, ew, w_gate, w_up, w_down):
    n, d = h.shape
    a_tot = n * EXPERT_TOP_K
    flat_e = eid.reshape(-1)
    flat_t = jnp.arange(a_tot) // EXPERT_TOP_K
    flat_w = ew.reshape(-1)
    order = jnp.argsort(flat_e)
    se, st, sw = flat_e[order], flat_t[order], flat_w[order]
    counts = jnp.bincount(flat_e, length=N_EXPERTS)
    padded = (counts + MOE_BLOCK - 1) // MOE_BLOCK * MOE_BLOCK
    pad_end = jnp.cumsum(padded)
    pad_start = pad_end - padded
    start = jnp.cumsum(counts) - counts
    dest = pad_start[se] + jnp.arange(a_tot) - start[se]
    n_blk = (a_tot + N_EXPERTS * (MOE_BLOCK - 1) + MOE_BLOCK - 1) // MOE_BLOCK
    tok_buf = jnp.full((n_blk * MOE_BLOCK,), n, jnp.int32).at[dest].set(st)
    h_pad = jnp.concatenate([h, jnp.zeros((1, d), h.dtype)], axis=0)
    xb = h_pad[tok_buf].reshape(n_blk, MOE_BLOCK, d)
    blk_e = jnp.clip(jnp.searchsorted(pad_end, jnp.arange(n_blk) * MOE_BLOCK, side='right'), 0, N_EXPERTS - 1)
    yb = _moe_blocks(xb, blk_e.astype(jnp.int32), w_gate, w_up, w_down).reshape(n_blk * MOE_BLOCK, d)
    out = jnp.zeros((n, d), jnp.float32).at[st].add(yb[dest].astype(jnp.float32) * sw[:, None])
    return out.astype(h.dtype)


def _decoder_layer(x, c, pos_q, lp, layer, pool_cmp, pool_slc, page_table, win_buf, wkv0, shift0, past_len):
    B, T, D = x.shape
    mod = jax.nn.silu(c) @ lp['w_ada'] + lp['b_ada']
    sh1, sc1, gt1, sh2, sc2, gt2 = jnp.split(mod[:, None, :], 6, axis=-1)
    h = _rmsnorm(x, lp['norm1']) * (1.0 + sc1) + sh1
    tm = 512 if (B * T) % 512 == 0 else B * T
    proj = _matmul(h.reshape(B * T, D), lp['w_in'], tm, 512).reshape(B, T, IN_WIDTH)
    shift0 = shift0.astype(h.dtype)
    h_shift = jnp.concatenate([shift0[:, None], h[:, :-1]], axis=1)
    p_rkv = proj[..., RKV_OFF:GATE_OFF]
    p_rkv_prev = jnp.concatenate([(shift0 @ lp['w_in'][:, RKV_OFF:GATE_OFF])[:, None], p_rkv[:, :-1]], axis=1)
    o_nsa, cmp_new, slc_new, win_state = _nsa_group(proj, pos_q, lp, layer, pool_cmp, pool_slc, page_table,
                                                    win_buf, past_len)
    o_rwkv, wkv_T = _rwkv_group(h, h_shift, p_rkv, p_rkv_prev, wkv0, lp)
    mix_in = jnp.concatenate([o_nsa, o_rwkv.astype(o_nsa.dtype)], axis=-1).reshape(B * T, D)
    mixed = _matmul(mix_in, lp['w_out'], tm, 512).reshape(B, T, D)
    x = x + gt1 * mixed
    h2 = (_rmsnorm(x, lp['norm2']) * (1.0 + sc2) + sh2).reshape(B * T, D)
    eid, ew = _hier_route(h2, lp['w_router_group'], lp['b_router_group'], lp['w_router_expert'],
                          lp['b_router_expert'])
    ffn = _moe_ffn(h2, eid, ew, lp['w_gate'], lp['w_up'], lp['w_down']).reshape(B, T, D)
    x = x + gt2 * ffn
    return x, cmp_new, slc_new, win_state, wkv_T, h[:, -1]


def kernel(x_prompt, x_sample, c_prompt, c_sample, cache_cmp_kv, cache_slc_kv, page_table, state_win_kv, state_wkv, state_shift, w_ada, b_ada, norm1, w_in, cmp_k_w1, cmp_k_pe, cmp_k_w2, cmp_v_w1, cmp_v_pe, cmp_v_w2, mu_rkv, mu_wag, decay_w0, decay_w1, decay_w2, iclr_a0, iclr_a1, iclr_a2, gate_g1, gate_g2, k_k, k_a, r_k, ln_x_w, ln_x_b, w_out, norm2, w_router_group, b_router_group, w_router_expert, b_router_expert, w_gate, w_up, w_down, norm_f):
    bp, tp = x_prompt.shape[:2]
    ts = x_sample.shape[1]
    past_len = page_table.shape[1] * PAGE_SIZE
    pos_p = jnp.arange(tp)
    pos_s = past_len + jnp.arange(ts)
    l = 0
    lp = dict(w_ada=w_ada[l], b_ada=b_ada[l], norm1=norm1[l], w_in=w_in[l],
              cmp_k_w1=cmp_k_w1[l], cmp_k_pe=cmp_k_pe[l], cmp_k_w2=cmp_k_w2[l],
              cmp_v_w1=cmp_v_w1[l], cmp_v_pe=cmp_v_pe[l], cmp_v_w2=cmp_v_w2[l],
              mu_rkv=mu_rkv[l], mu_wag=mu_wag[l], decay_w0=decay_w0[l], decay_w1=decay_w1[l],
              decay_w2=decay_w2[l], iclr_a0=iclr_a0[l], iclr_a1=iclr_a1[l], iclr_a2=iclr_a2[l],
              gate_g1=gate_g1[l], gate_g2=gate_g2[l], k_k=k_k[l], k_a=k_a[l], r_k=r_k[l],
              ln_x_w=ln_x_w[l], ln_x_b=ln_x_b[l], w_out=w_out[l], norm2=norm2[l],
              w_router_group=w_router_group[l], b_router_group=b_router_group[l],
              w_router_expert=w_router_expert[l], b_router_expert=b_router_expert[l],
              w_gate=w_gate[l], w_up=w_up[l], w_down=w_down[l])
    wkv_zero = jnp.zeros((bp, RWKV_HEADS, RWKV_HEAD_DIM, RWKV_HEAD_DIM), jnp.float32)
    shift_zero = jnp.zeros((bp, D_MODEL), x_prompt.dtype)
    hp, a1, a2, a3, a4, a5 = _decoder_layer(x_prompt, c_prompt, pos_p, lp, l, None, None, None, None,
                                            wkv_zero, shift_zero, past_len)
    hs, b1, b2, b3, b4, b5 = _decoder_layer(x_sample, c_sample, pos_s, lp, l, cache_cmp_kv, cache_slc_kv,
                                            page_table, state_win_kv[l], state_wkv[l], state_shift[l], past_len)
    y_prompt = _rmsnorm(hp, norm_f)
    y_sample = _rmsnorm(hs, norm_f)
    st = lambda a: a[None]
    return (y_prompt, y_sample, st(a1), st(b1), st(a2), st(b2), st(a3), st(b3), st(a4), st(b4), st(a5), st(b5))
```

```python
import functools

import jax
import jax.numpy as jnp
from jax import lax
from jax.experimental import pallas as pl
from jax.experimental.pallas import tpu as pltpu

D_MODEL = 2048
DEPTH = 1
PAGE_SIZE = 128
HEAD_DIM = 128
NSA_WIDTH = D_MODEL // 2
NSA_HEADS = NSA_WIDTH // HEAD_DIM
NSA_KV_HEADS = 2
HPG = NSA_HEADS // NSA_KV_HEADS
KV_WIDTH = NSA_KV_HEADS * HEAD_DIM
CMP_BLOCK = 32
CMP_STRIDE = 16
SEL_BLOCK = 64
N_SELECT = 16
WINDOW = 512
FORCE_BONUS = 1e4
SEL_QBLOCK = 32
WIN_QBLOCK = 128
RWKV_WIDTH = D_MODEL - NSA_WIDTH
RWKV_HEAD_DIM = 64
RWKV_HEADS = RWKV_WIDTH // RWKV_HEAD_DIM
GN_EPS = 64e-5
N_GROUPS = 4
EXPERTS_PER_GROUP = 8
N_EXPERTS = N_GROUPS * EXPERTS_PER_GROUP
EXPERT_TOP_K = 2
D_EXPERT = 512
MOE_BLOCK = 64
NORM_EPS = 1e-6
NEG_INF = -1e30
Q_OFF = 0
CMP_OFF = Q_OFF + NSA_WIDTH
SLC_OFF = CMP_OFF + 2 * KV_WIDTH
WIN_OFF = SLC_OFF + 2 * KV_WIDTH
RKV_OFF = WIN_OFF + 2 * KV_WIDTH
GATE_OFF = RKV_OFF + 3 * RWKV_WIDTH
IN_WIDTH = GATE_OFF + 3 * NSA_HEADS

VMEM_LIMIT_BYTES = 48 * 1024 * 1024


def _mm_kernel(x_ref, w_ref, o_ref):
    o_ref[...] = jnp.dot(x_ref[...].astype(jnp.bfloat16), w_ref[...].astype(jnp.bfloat16),
                         preferred_element_type=jnp.float32)


def _matmul(x, w, tm, tn):
    m, k = x.shape
    n = w.shape[1]
    return pl.pallas_call(
        _mm_kernel,
        grid=(pl.cdiv(m, tm), pl.cdiv(n, tn)),
        in_specs=[pl.BlockSpec((tm, k), lambda i, j: (i, 0)),
                  pl.BlockSpec((k, tn), lambda i, j: (0, j))],
        out_specs=pl.BlockSpec((tm, tn), lambda i, j: (i, j)),
        out_shape=jax.ShapeDtypeStruct((m, n), jnp.float32),
        compiler_params=pltpu.CompilerParams(
            dimension_semantics=("arbitrary", "arbitrary"), vmem_limit_bytes=VMEM_LIMIT_BYTES),
        name="matmul",
    )(x, w)


def _moe_block_kernel(blk_e_ref, x_ref, wg_ref, wu_ref, wd_ref, o_ref):
    del blk_e_ref
    x = x_ref[0].astype(jnp.bfloat16)
    g = jnp.dot(x, wg_ref[0].astype(jnp.bfloat16), preferred_element_type=jnp.float32)
    u = jnp.dot(x, wu_ref[0].astype(jnp.bfloat16), preferred_element_type=jnp.float32)
    hmid = (g * jax.nn.sigmoid(g)) * u
    o_ref[0] = jnp.dot(hmid.astype(jnp.bfloat16), wd_ref[0].astype(jnp.bfloat16),
                       preferred_element_type=jnp.float32)


def _moe_blocks(xb, blk_e, w_gate, w_up, w_down):
    n_blk, mb, d = xb.shape
    de = w_gate.shape[2]
    grid_spec = pltpu.PrefetchScalarGridSpec(
        num_scalar_prefetch=1,
        grid=(n_blk,),
        in_specs=[pl.BlockSpec((1, mb, d), lambda i, e: (i, 0, 0)),
                  pl.BlockSpec((1, d, de), lambda i, e: (e[i], 0, 0)),
                  pl.BlockSpec((1, d, de), lambda i, e: (e[i], 0, 0)),
                  pl.BlockSpec((1, de, d), lambda i, e: (e[i], 0, 0))],
        out_specs=pl.BlockSpec((1, mb, d), lambda i, e: (i, 0, 0)),
    )
    return pl.pallas_call(
        _moe_block_kernel,
        grid_spec=grid_spec,
        out_shape=jax.ShapeDtypeStruct((n_blk, mb, d), jnp.float32),
        compiler_params=pltpu.CompilerParams(
            dimension_semantics=("arbitrary",), vmem_limit_bytes=VMEM_LIMIT_BYTES),
        name="moe_blocks",
    )(blk_e, xb, w_gate, w_up, w_down)


def _cmp_partial_kernel(*refs, n_src, rows_per_src):
    x_refs = refs[:4 * n_src]
    w1k_ref, w1v_ref, a_ref, b_ref = refs[4 * n_src:]
    nch_src = rows_per_src // CMP_STRIDE
    for kvg in range(4):
        w_ref = w1k_ref if kvg < 2 else w1v_ref
        acc = None
        for p in range(CMP_STRIDE):
            parts = [x_refs[s * 4 + kvg][pl.ds(p, nch_src, stride=CMP_STRIDE), :] for s in range(n_src)]
            xp = parts[0] if n_src == 1 else jnp.concatenate(parts, axis=0)
            d = jnp.dot(xp.astype(jnp.bfloat16), w_ref[p], preferred_element_type=jnp.float32)
            acc = d if acc is None else acc + d
        a_ref[0, :, kvg * HEAD_DIM:(kvg + 1) * HEAD_DIM] = acc[:, :HEAD_DIM]
        b_ref[0, :, kvg * HEAD_DIM:(kvg + 1) * HEAD_DIM] = acc[:, HEAD_DIM:]


def _cmp_finish_kernel(a_ref, b_ref, pek_ref, pev_ref, w1k_ref, w1v_ref, w2k_ref, w2v_ref, o_ref):
    nch = a_ref.shape[1]
    for kv, (pe_ref, w1_ref, w2_ref) in enumerate(((pek_ref, w1k_ref, w2k_ref), (pev_ref, w1v_ref, w2v_ref))):
        pe8 = jnp.broadcast_to(pe_ref[...], (8, pe_ref.shape[1])).astype(jnp.bfloat16)
        pterm = jnp.dot(pe8, w1_ref[...], preferred_element_type=jnp.float32)[0:1]
        w2 = w2_ref[...]
        for g in range(NSA_KV_HEADS):
            lo = (kv * NSA_KV_HEADS + g) * HEAD_DIM
            nxt = pltpu.roll(b_ref[0, :, lo:lo + HEAD_DIM], nch - 1, 0)
            pre = a_ref[0, :, lo:lo + HEAD_DIM] + nxt + pterm
            act = pre * jax.nn.sigmoid(pre)
            o_ref[0, :, lo:lo + HEAD_DIM] = jnp.dot(act.astype(jnp.bfloat16), w2, preferred_element_type=jnp.float32)


def _cmp_weights(lp):
    bf = jnp.bfloat16
    half = CMP_BLOCK // 2
    cat = lambda w: jnp.concatenate([w[:half], w[half:]], axis=-1).astype(bf)
    flat = lambda w: w.reshape(CMP_BLOCK * HEAD_DIM, HEAD_DIM).astype(bf)
    return dict(w1k_cat=cat(lp['cmp_k_w1']), w1v_cat=cat(lp['cmp_v_w1']),
                w1k_flat=flat(lp['cmp_k_w1']), w1v_flat=flat(lp['cmp_v_w1']),
                pek=lp['cmp_k_pe'].reshape(1, -1), pev=lp['cmp_v_pe'].reshape(1, -1),
                w2k=lp['cmp_k_w2'].astype(bf), w2v=lp['cmp_v_w2'].astype(bf))


def _cmp_finish(a, b, cw):
    bsz, nch, _ = a.shape
    full = lambda arr: pl.BlockSpec(arr.shape, lambda i: (0,) * arr.ndim)
    blk = pl.BlockSpec((1, nch, 4 * HEAD_DIM), lambda i: (i, 0, 0))
    ws = [cw['pek'], cw['pev'], cw['w1k_flat'], cw['w1v_flat'], cw['w2k'], cw['w2v']]
    return pl.pallas_call(
        _cmp_finish_kernel,
        grid=(bsz,),
        in_specs=[blk, blk] + [full(w) for w in ws],
        out_specs=blk,
        out_shape=jax.ShapeDtypeStruct((bsz, nch, 4 * HEAD_DIM), jnp.float32),
        compiler_params=pltpu.CompilerParams(dimension_semantics=("arbitrary",), vmem_limit_bytes=VMEM_LIMIT_BYTES),
        name="cmp_finish",
    )(a, b, *ws)


def _compress_prompt(proj2d, bsz, seq, cw):
    nch = seq // CMP_STRIDE
    col0 = CMP_OFF // HEAD_DIM
    x_specs = [pl.BlockSpec((seq, HEAD_DIM), functools.partial(lambda i, c: (i, c), c=col0 + kvg)) for kvg in range(4)]
    w_spec = pl.BlockSpec(cw['w1k_cat'].shape, lambda i: (0, 0, 0))
    out_spec = pl.BlockSpec((1, nch, 4 * HEAD_DIM), lambda i: (i, 0, 0))
    shp = jax.ShapeDtypeStruct((bsz, nch, 4 * HEAD_DIM), jnp.float32)
    a, b = pl.pallas_call(
        functools.partial(_cmp_partial_kernel, n_src=1, rows_per_src=seq),
        grid=(bsz,),
        in_specs=x_specs + [w_spec, w_spec],
        out_specs=[out_spec, out_spec],
        out_shape=[shp, shp],
        compiler_params=pltpu.CompilerParams(dimension_semantics=("arbitrary",), vmem_limit_bytes=VMEM_LIMIT_BYTES),
        name="cmp_partial_prompt",
    )(proj2d, proj2d, proj2d, proj2d, cw['w1k_cat'], cw['w1v_cat'])
    return _cmp_finish(a, b, cw)


_NT = (((1,), (1,)), ((), ()))
SEL_TK = 512
WIN_TK = 256


def _flash_update(s, v, m_ref, l_ref, acc_ref, h):
    tk = s.shape[1]
    m_prev = m_ref[h]
    m_new = jnp.maximum(m_prev, jnp.max(s, axis=-1, keepdims=True))
    alpha = jnp.exp(m_prev - m_new)
    p = jnp.exp(s - jnp.concatenate([m_new] * (tk // HEAD_DIM), axis=1))
    l_ref[h] = alpha * l_ref[h] + jnp.sum(p, axis=-1, keepdims=True)
    acc_ref[h] = alpha * acc_ref[h] + jnp.dot(p.astype(jnp.bfloat16), v, preferred_element_type=jnp.float32)
    m_ref[h] = m_new


def _nsa_prompt_kernel(q_ref, slc_ref, win_ref, gate_ref, kvc_ref, o_ref, m_ref, l_ref, acc_ref, *, tq, seq):
    f32, bf16 = jnp.float32, jnp.bfloat16
    qi = pl.program_id(1)
    t0 = qi * tq
    scale = HEAD_DIM ** -0.5
    nc_valid = seq // CMP_STRIDE - CMP_BLOCK // CMP_STRIDE + 1
    n_sel = seq // SEL_BLOCK
    pos = t0 + lax.broadcasted_iota(jnp.int32, (tq, 1), 0)
    lane = lax.broadcasted_iota(jnp.int32, (1, HEAD_DIM), 1)
    blk_t = jnp.right_shift(pos, 6)
    gates = jax.nn.sigmoid(gate_ref[...])
    dist_c = pos - (lane * CMP_STRIDE + (CMP_BLOCK - 1))
    valid_c = (dist_c >= 0) & (lane < nc_valid)
    dist_cf = dist_c.astype(f32)
    c_row = lax.broadcasted_iota(jnp.int32, (HEAD_DIM, 1), 0)
    overlap = jnp.where((c_row * CMP_STRIDE <= lane * SEL_BLOCK + (SEL_BLOCK - 1))
                        & (c_row * CMP_STRIDE + (CMP_BLOCK - 1) >= lane * SEL_BLOCK), 1.0, 0.0).astype(bf16)
    forced = (lane == 0) | (lane == blk_t) | (lane == blk_t - 1)

    def gate_col(branch, hh):
        c = branch * NSA_HEADS + hh
        return gates[:, c:c + 1]

    def reset():
        m_ref[...] = jnp.full(m_ref.shape, NEG_INF, f32)
        l_ref[...] = jnp.zeros(l_ref.shape, f32)
        acc_ref[...] = jnp.zeros(acc_ref.shape, f32)

    for g in range(NSA_KV_HEADS):
        kcol = slice(g * HEAD_DIM, (g + 1) * HEAD_DIM)
        vcol = slice((NSA_KV_HEADS + g) * HEAD_DIM, (NSA_KV_HEADS + g + 1) * HEAD_DIM)
        heads = [g * HPG + h for h in range(HPG)]
        slopes = [2.0 ** -(hh + 1) for hh in heads]

        kc = kvc_ref[0, :, kcol].astype(bf16)
        vc = kvc_ref[0, :, vcol].astype(bf16)
        psum = jnp.zeros((tq, HEAD_DIM), f32)
        for h, hh in enumerate(heads):
            qh = q_ref[:, hh * HEAD_DIM:(hh + 1) * HEAD_DIM].astype(bf16)
            s = lax.dot_general(qh, kc, _NT, preferred_element_type=f32) * scale - slopes[h] * dist_cf
            s = jnp.where(valid_c, s, NEG_INF)
            e = jnp.exp(s - jnp.max(s, axis=-1, keepdims=True))
            p = e / jnp.sum(e, axis=-1, keepdims=True)
            p = jnp.where(valid_c, p, 0.0)
            o_cmp = jnp.dot(p.astype(bf16), vc, preferred_element_type=f32)
            o_ref[:, hh * HEAD_DIM:(hh + 1) * HEAD_DIM] = gate_col(0, hh) * o_cmp
            psum = psum + p
        p_hi = psum.astype(bf16)
        p_lo = (psum - p_hi.astype(f32)).astype(bf16)
        imp = (jnp.dot(p_hi, overlap, preferred_element_type=f32)
               + jnp.dot(p_lo, overlap, preferred_element_type=f32))
        imp = jnp.where(forced, imp + FORCE_BONUS, imp)
        imp = jnp.where(lane <= blk_t, imp, NEG_INF)
        beaten = jnp.zeros((tq, HEAD_DIM), f32)
        for jp in range(n_sel):
            col = imp[:, jp:jp + 1]
            tie = jnp.where(lane > jp, 1.0, 0.0)
            beaten = beaten + jnp.where(col > imp, 1.0, jnp.where(col == imp, tie, 0.0))
        sel = jnp.where(beaten < N_SELECT, jnp.where(imp > 0.5 * NEG_INF, 1.0, 0.0), 0.0).astype(bf16)

        reset()
        j_row = lax.broadcasted_iota(jnp.int32, (HEAD_DIM, 1), 0)

        def sel_body(kt, carry):
            k0 = pl.multiple_of(kt * SEL_TK, SEL_TK)
            k = slc_ref[pl.ds(k0, SEL_TK), kcol].astype(bf16)
            v = slc_ref[pl.ds(k0, SEL_TK), vcol].astype(bf16)
            kpos = k0 + lax.broadcasted_iota(jnp.int32, (1, SEL_TK), 1)
            dist = pos - kpos
            expand = jnp.where(jnp.right_shift(kpos, 6) == j_row, 1.0, 0.0).astype(bf16)
            picked = jnp.dot(sel, expand, preferred_element_type=f32)
            keep = jnp.where(dist >= 0, picked, 0.0) > 0.5
            dist_f = dist.astype(f32)
            for h, hh in enumerate(heads):
                qh = q_ref[:, hh * HEAD_DIM:(hh + 1) * HEAD_DIM].astype(bf16)
                s = lax.dot_general(qh, k, _NT, preferred_element_type=f32) * scale - slopes[h] * dist_f
                _flash_update(jnp.where(keep, s, NEG_INF), v, m_ref, l_ref, acc_ref, h)
            return carry

        lax.fori_loop(0, (t0 + tq - 1) // SEL_TK + 1, sel_body, 0)
        for h, hh in enumerate(heads):
            hs = slice(hh * HEAD_DIM, (hh + 1) * HEAD_DIM)
            o_ref[:, hs] = o_ref[:, hs] + gate_col(1, hh) * (acc_ref[h] / l_ref[h])

        reset()

        def win_body(kt, carry):
            k0 = pl.multiple_of(kt * WIN_TK, WIN_TK)
            k = win_ref[pl.ds(k0, WIN_TK), kcol].astype(bf16)
            v = win_ref[pl.ds(k0, WIN_TK), vcol].astype(bf16)
            dist = pos - (k0 + lax.broadcasted_iota(jnp.int32, (1, WIN_TK), 1))
            keep = (dist >= 0) & (dist < WINDOW)
            dist_f = dist.astype(f32)
            for h, hh in enumerate(heads):
                qh = q_ref[:, hh * HEAD_DIM:(hh + 1) * HEAD_DIM].astype(bf16)
                s = lax.dot_general(qh, k, _NT, preferred_element_type=f32) * scale - slopes[h] * dist_f
                _flash_update(jnp.where(keep, s, NEG_INF), v, m_ref, l_ref, acc_ref, h)
            return carry

        lax.fori_loop(jnp.maximum(t0 - (WINDOW - 1), 0) // WIN_TK, (t0 + tq - 1) // WIN_TK + 1, win_body, 0)
        for h, hh in enumerate(heads):
            hs = slice(hh * HEAD_DIM, (hh + 1) * HEAD_DIM)
            o_ref[:, hs] = o_ref[:, hs] + gate_col(2, hh) * (acc_ref[h] / l_ref[h])


def _nsa_prompt(proj2d, kvc, bsz, seq, tq=256):
    nq = seq // tq
    kvw = 2 * KV_WIDTH
    return pl.pallas_call(
        functools.partial(_nsa_prompt_kernel, tq=tq, seq=seq),
        grid=(bsz, nq),
        in_specs=[pl.BlockSpec((tq, NSA_WIDTH), lambda b, i: (b * nq + i, 0)),
                  pl.BlockSpec((seq, kvw), lambda b, i: (b, SLC_OFF // kvw)),
                  pl.BlockSpec((seq, kvw), lambda b, i: (b, WIN_OFF // kvw)),
                  pl.BlockSpec((tq, HEAD_DIM), lambda b, i: (b * nq + i, GATE_OFF // HEAD_DIM)),
                  pl.BlockSpec((1, seq // CMP_STRIDE, kvw), lambda b, i: (b, 0, 0))],
        out_specs=pl.BlockSpec((tq, NSA_WIDTH), lambda b, i: (b * nq + i, 0)),
        out_shape=jax.ShapeDtypeStruct((bsz * seq, NSA_WIDTH), jnp.float32),
        scratch_shapes=[pltpu.VMEM((HPG, tq, HEAD_DIM), jnp.float32),
                        pltpu.VMEM((HPG, tq, HEAD_DIM), jnp.float32),
                        pltpu.VMEM((HPG, tq, HEAD_DIM), jnp.float32)],
        compiler_params=pltpu.CompilerParams(
            dimension_semantics=("arbitrary", "arbitrary"), vmem_limit_bytes=VMEM_LIMIT_BYTES),
        name="nsa_prompt",
    )(proj2d, proj2d, proj2d, proj2d, kvc)


CMP_PAGES_PER_STEP = 16
TOPK_LANES = 384
IDX_LANES = 128


def _cmp_partial_paged_kernel(pt_ref, *refs, n_src, rows_per_src):
    del pt_ref
    _cmp_partial_kernel(*refs, n_src=n_src, rows_per_src=rows_per_src)


def _compress_paged(pool, page_table, cw):
    bsz, n_pages = page_table.shape
    nps = CMP_PAGES_PER_STEP
    n_tiles = n_pages // nps
    nch_tile = nps * PAGE_SIZE // CMP_STRIDE

    def page_spec(s, kvg):
        return pl.BlockSpec((None, PAGE_SIZE, HEAD_DIM), lambda b, i, pt: (pt[b * n_pages + i * nps + s], 0, kvg))

    x_specs = [page_spec(s, kvg) for s in range(nps) for kvg in range(4)]
    w_spec = pl.BlockSpec(cw['w1k_cat'].shape, lambda b, i, pt: (0, 0, 0))
    out_spec = pl.BlockSpec((1, nch_tile, 4 * HEAD_DIM), lambda b, i, pt: (b, i, 0))
    shp = jax.ShapeDtypeStruct((bsz, n_tiles * nch_tile, 4 * HEAD_DIM), jnp.float32)
    a, b = pl.pallas_call(
        functools.partial(_cmp_partial_paged_kernel, n_src=nps, rows_per_src=PAGE_SIZE),
        grid_spec=pltpu.PrefetchScalarGridSpec(
            num_scalar_prefetch=1, grid=(bsz, n_tiles),
            in_specs=x_specs + [w_spec, w_spec], out_specs=[out_spec, out_spec]),
        out_shape=[shp, shp],
        compiler_params=pltpu.CompilerParams(
            dimension_semantics=("arbitrary", "arbitrary"), vmem_limit_bytes=VMEM_LIMIT_BYTES),
        name="cmp_partial_paged",
    )(page_table.reshape(-1), *([pool] * (4 * nps)), cw['w1k_cat'], cw['w1v_cat'])
    return _cmp_finish(a, b, cw)


def _nsa_decode_a_kernel(proj_ref, kvc_ref, win_ref, o_ref, gsel_ref, idx_ref, *, t_new, n_past):
    f32, bf16 = jnp.float32, jnp.bfloat16
    scale = HEAD_DIM ** -0.5
    nch = kvc_ref.shape[1]
    n_win = win_ref.shape[1]
    rows = HPG * t_new
    r_iota = lax.broadcasted_iota(jnp.int32, (rows, 1), 0)
    t_row = r_iota % t_new
    h_row = r_iota // t_new
    pos_row = n_past + t_row
    gates = jax.nn.sigmoid(proj_ref[0, :, GATE_OFF:IN_WIDTH])
    c_lane = lax.broadcasted_iota(jnp.int32, (1, nch), 1)
    dist_c = pos_row - (c_lane * CMP_STRIDE + (CMP_BLOCK - 1))
    valid_c = (dist_c >= 0) & (c_lane < nch - 1)
    c_col = lax.broadcasted_iota(jnp.int32, (nch, 1), 0)
    j_lane = lax.broadcasted_iota(jnp.int32, (1, TOPK_LANES), 1)
    overlap = jnp.where((c_col * CMP_STRIDE <= j_lane * SEL_BLOCK + (SEL_BLOCK - 1))
                        & (c_col * CMP_STRIDE + (CMP_BLOCK - 1) >= j_lane * SEL_BLOCK), 1.0, 0.0).astype(bf16)
    pos_t = n_past + lax.broadcasted_iota(jnp.int32, (t_new, 1), 0)
    blk_t = pos_t // SEL_BLOCK
    forced = (j_lane == 0) | (j_lane == blk_t) | (j_lane == blk_t - 1)
    j_f = j_lane.astype(f32)
    k_lane = lax.broadcasted_iota(jnp.int32, (1, IDX_LANES), 1)
    i_win = lax.broadcasted_iota(jnp.int32, (1, n_win), 1)
    dist_w = pos_row - (n_past - n_win + i_win)
    keep_w = (dist_w >= 0) & (dist_w < WINDOW)
    j_new = lax.broadcasted_iota(jnp.int32, (1, 8), 1)
    dist_n = t_row - j_new
    keep_n = (dist_n >= 0) & (j_new < t_new)
    zpad = jnp.zeros((8 - t_new, HEAD_DIM), f32)

    for g in range(NSA_KV_HEADS):
        kcol = slice(g * HEAD_DIM, (g + 1) * HEAD_DIM)
        vcol = slice((NSA_KV_HEADS + g) * HEAD_DIM, (NSA_KV_HEADS + g + 1) * HEAD_DIM)
        heads = [g * HPG + h for h in range(HPG)]
        slope_row = jnp.zeros((rows, 1), f32)
        for h, hh in enumerate(heads):
            slope_row = jnp.where(h_row == h, 2.0 ** -(hh + 1), slope_row)
        q = jnp.concatenate([proj_ref[0, :, hh * HEAD_DIM:(hh + 1) * HEAD_DIM] for hh in heads], axis=0).astype(bf16)

        kc = kvc_ref[0, :, kcol].astype(bf16)
        vc = kvc_ref[0, :, vcol].astype(bf16)
        s = lax.dot_general(q, kc, _NT, preferred_element_type=f32) * scale - slope_row * dist_c.astype(f32)
        s = jnp.where(valid_c, s, NEG_INF)
        e = jnp.exp(s - jnp.max(s, axis=-1, keepdims=True))
        p = e / jnp.sum(e, axis=-1, keepdims=True)
        p = jnp.where(valid_c, p, 0.0)
        o_cmp = jnp.dot(p.astype(bf16), vc, preferred_element_type=f32)
        psum = p[0:t_new]
        for h in range(1, HPG):
            psum = psum + p[h * t_new:(h + 1) * t_new]

        p_hi = psum.astype(bf16)
        p_lo = (psum - p_hi.astype(f32)).astype(bf16)
        imp = (jnp.dot(p_hi, overlap, preferred_element_type=f32)
               + jnp.dot(p_lo, overlap, preferred_element_type=f32))
        imp = jnp.where(forced, imp + FORCE_BONUS, imp)
        imp = jnp.where(j_lane <= blk_t, imp, NEG_INF)
        picked = jnp.full((t_new, IDX_LANES), -1.0, f32)
        for k in range(N_SELECT):
            best = jnp.max(imp, axis=-1, keepdims=True)
            first = jnp.min(jnp.where(imp == best, j_f, 1e9), axis=-1, keepdims=True)
            picked = jnp.where(k_lane == k, jnp.where(best > 0.5 * NEG_INF, first, -1.0), picked)
            imp = jnp.where(j_f == first, -3e38, imp)
        idx_ref[0, g * t_new:(g + 1) * t_new, :] = picked.astype(jnp.int32)

        kw = win_ref[0, :, kcol].astype(bf16)
        vw = win_ref[0, :, vcol].astype(bf16)
        kn = jnp.concatenate([proj_ref[0, :, WIN_OFF + g * HEAD_DIM:WIN_OFF + (g + 1) * HEAD_DIM], zpad], axis=0)
        vn = jnp.concatenate([proj_ref[0, :, WIN_OFF + KV_WIDTH + g * HEAD_DIM:
                                       WIN_OFF + KV_WIDTH + (g + 1) * HEAD_DIM], zpad], axis=0)
        s_w = lax.dot_general(q, kw, _NT, preferred_element_type=f32) * scale - slope_row * dist_w.astype(f32)
        s_n = (lax.dot_general(q, kn.astype(bf16), _NT, preferred_element_type=f32) * scale
               - slope_row * dist_n.astype(f32))
        s_w = jnp.where(keep_w, s_w, NEG_INF)
        s_n = jnp.where(keep_n, s_n, NEG_INF)
        m = jnp.maximum(jnp.max(s_w, axis=-1, keepdims=True), jnp.max(s_n, axis=-1, keepdims=True))
        e_w = jnp.exp(s_w - m)
        e_n = jnp.exp(s_n - m)
        den = jnp.sum(e_w, axis=-1, keepdims=True) + jnp.sum(e_n, axis=-1, keepdims=True)
        o_win = (jnp.dot(e_w.astype(bf16), vw, preferred_element_type=f32)
                 + jnp.dot(e_n.astype(bf16), vn.astype(bf16), preferred_element_type=f32)) / den

        for h, hh in enumerate(heads):
            rs = slice(h * t_new, (h + 1) * t_new)
            hs = slice(h * HEAD_DIM, (h + 1) * HEAD_DIM)
            o_ref[0, g, :, hs] = (gates[:, hh:hh + 1] * o_cmp[rs]
                                  + gates[:, 2 * NSA_HEADS + hh:2 * NSA_HEADS + hh + 1] * o_win[rs])
            gsel_ref[0, g, :, hs] = jnp.broadcast_to(gates[:, NSA_HEADS + hh:NSA_HEADS + hh + 1], (t_new, HEAD_DIM))


def _nsa_decode_b_kernel(idx_ref, pt_ref, q_ref, part_ref, gsel_ref, new_ref, *refs, t_new, n_past, n_pages):
    del pt_ref
    f32, bf16 = jnp.float32, jnp.bfloat16
    k_refs, v_refs, o_ref = refs[:N_SELECT], refs[N_SELECT:2 * N_SELECT], refs[2 * N_SELECT]
    b, g, t = pl.program_id(0), pl.program_id(1), pl.program_id(2)
    scale = HEAD_DIM ** -0.5
    n_past_blk = n_past // SEL_BLOCK
    base = ((b * NSA_KV_HEADS + g) * t_new + t) * N_SELECT
    n_keys = N_SELECT * SEL_BLOCK
    lane = lax.broadcasted_iota(jnp.int32, (1, n_keys), 1)
    slot = lane // SEL_BLOCK
    blk_of_lane = jnp.full((1, n_keys), -1, jnp.int32)
    ks, vs = [], []
    for k in range(N_SELECT):
        blk = idx_ref[base + k]
        blk_of_lane = jnp.where(slot == k, blk, blk_of_lane)
        is_new = blk >= n_past_blk
        ks.append(jnp.where(is_new, new_ref[:, 0:HEAD_DIM], k_refs[k][...]).astype(bf16))
        vs.append(jnp.where(is_new, new_ref[:, HEAD_DIM:2 * HEAD_DIM], v_refs[k][...]).astype(bf16))
    k_all = jnp.concatenate(ks, axis=0)
    v_all = jnp.concatenate(vs, axis=0)
    dist = (n_past + t) - (blk_of_lane * SEL_BLOCK + lane % SEL_BLOCK)
    keep = (dist >= 0) & (blk_of_lane >= 0)
    q = jnp.concatenate([q_ref[:, h * HEAD_DIM:(h + 1) * HEAD_DIM] for h in range(HPG)]
                        + [jnp.zeros((8 - HPG, HEAD_DIM), f32)], axis=0).astype(bf16)
    h_row = lax.broadcasted_iota(jnp.int32, (8, 1), 0)
    slope_row = jnp.zeros((8, 1), f32)
    for h in range(HPG):
        slope_row = jnp.where(h_row == h, jnp.where(g == 0, 2.0 ** -(h + 1), 2.0 ** -(HPG + h + 1)), slope_row)
    s = lax.dot_general(q, k_all, _NT, preferred_element_type=f32) * scale - slope_row * dist.astype(f32)
    s = jnp.where(keep, s, NEG_INF)
    e = jnp.exp(s - jnp.max(s, axis=-1, keepdims=True))
    p = e / jnp.sum(e, axis=-1, keepdims=True)
    o_sel = jnp.dot(p.astype(bf16), v_all, preferred_element_type=f32)
    o_row = jnp.concatenate([o_sel[h:h + 1] for h in range(HPG)], axis=1)
    o_ref[...] = part_ref[...] + gsel_ref[...] * o_row


def _nsa_decode(proj3, kvc, win_buf, pool_slc, page_table, n_past):
    bsz, t_new, _ = proj3.shape
    n_pages = page_table.shape[1]
    gw = HPG * HEAD_DIM
    grp = jax.ShapeDtypeStruct((bsz, NSA_KV_HEADS, t_new, gw), jnp.float32)
    part, gsel, idx = pl.pallas_call(
        functools.partial(_nsa_decode_a_kernel, t_new=t_new, n_past=n_past),
        grid=(bsz,),
        in_specs=[pl.BlockSpec((1, t_new, IN_WIDTH), lambda b: (b, 0, 0)),
                  pl.BlockSpec((1,) + kvc.shape[1:], lambda b: (b, 0, 0)),
                  pl.BlockSpec((1,) + win_buf.shape[1:], lambda b: (b, 0, 0))],
        out_specs=[pl.BlockSpec((1, NSA_KV_HEADS, t_new, gw), lambda b: (b, 0, 0, 0)),
                   pl.BlockSpec((1, NSA_KV_HEADS, t_new, gw), lambda b: (b, 0, 0, 0)),
                   pl.BlockSpec((1, NSA_KV_HEADS * t_new, IDX_LANES), lambda b: (b, 0, 0))],
        out_shape=[grp, grp, jax.ShapeDtypeStruct((bsz, NSA_KV_HEADS * t_new, IDX_LANES), jnp.int32)],
        compiler_params=pltpu.CompilerParams(dimension_semantics=("arbitrary",), vmem_limit_bytes=VMEM_LIMIT_BYTES),
        name="nsa_decode_a",
    )(proj3, kvc, win_buf)

    n_rows = bsz * NSA_KV_HEADS * t_new
    n_past_blk = n_past // SEL_BLOCK
    sub = PAGE_SIZE // SEL_BLOCK
    pool2 = pool_slc.reshape(-1, SEL_BLOCK, 2 * KV_WIDTH)
    new_rows = jnp.pad(proj3[:, :, SLC_OFF:WIN_OFF], ((0, 0), (0, SEL_BLOCK - t_new), (0, 0)))
    new_rows = new_rows.reshape(bsz, SEL_BLOCK, 2, NSA_KV_HEADS, HEAD_DIM).transpose(0, 3, 1, 2, 4)
    new_rows = new_rows.reshape(bsz * NSA_KV_HEADS, SEL_BLOCK, 2 * HEAD_DIM)

    def row_map(b, g, t, idx, pt):
        return (b * NSA_KV_HEADS + g) * t_new + t

    def pool_spec(k, is_v):
        def index(b, g, t, idx, pt):
            past = jnp.clip(idx[row_map(b, g, t, idx, pt) * N_SELECT + k], 0, n_past_blk - 1)
            return (pt[b * n_pages + past // sub] * sub + past % sub, 0, is_v * NSA_KV_HEADS + g)
        return pl.BlockSpec((None, SEL_BLOCK, HEAD_DIM), index)

    row_spec = pl.BlockSpec((None, 1, gw), lambda b, g, t, idx, pt: (row_map(b, g, t, idx, pt), 0, 0))
    out = pl.pallas_call(
        functools.partial(_nsa_decode_b_kernel, t_new=t_new, n_past=n_past, n_pages=n_pages),
        grid_spec=pltpu.PrefetchScalarGridSpec(
            num_scalar_prefetch=2, grid=(bsz, NSA_KV_HEADS, t_new),
            in_specs=[pl.BlockSpec((None, 1, gw), lambda b, g, t, idx, pt: (b * t_new + t, 0, g)),
                      row_spec, row_spec,
                      pl.BlockSpec((None, SEL_BLOCK, 2 * HEAD_DIM), lambda b, g, t, idx, pt: (b * NSA_KV_HEADS + g, 0, 0))]
            + [pool_spec(k, 0) for k in range(N_SELECT)] + [pool_spec(k, 1) for k in range(N_SELECT)],
            out_specs=row_spec),
        out_shape=jax.ShapeDtypeStruct((n_rows, 1, gw), jnp.float32),
        compiler_params=pltpu.CompilerParams(
            dimension_semantics=("arbitrary", "arbitrary", "arbitrary"), vmem_limit_bytes=VMEM_LIMIT_BYTES),
        name="nsa_decode_b",
    )(idx[:, :, :N_SELECT].reshape(-1), page_table.reshape(-1),
      proj3.reshape(bsz * t_new, 1, IN_WIDTH), part.reshape(n_rows, 1, gw), gsel.reshape(n_rows, 1, gw),
      new_rows, *([pool2] * (2 * N_SELECT)))
    return out.reshape(bsz, NSA_KV_HEADS, t_new, gw).transpose(0, 2, 1, 3).reshape(bsz, t_new, NSA_WIDTH)


WKV_LANES = 2 * RWKV_HEAD_DIM
WKV_PAIRS = RWKV_HEADS // 2
WKV_STACK = 4
WKV_BB = 2
WKV_CHUNK = 128
WKV_MIN_CHUNK = 16
WKV_FLUSH = RWKV_HEAD_DIM


def _wkv_kernel(r_ref, w_ref, k_ref, v_ref, kk_ref, kka_ref, s0_ref, y_ref, st_ref,
                s_scr, y_scr, *, tc):
    f32, bf16 = jnp.float32, jnp.bfloat16
    ti = pl.program_id(1)
    hd = RWKV_HEAD_DIM
    n_tiles = WKV_BB * WKV_PAIRS
    n_stacks = n_tiles // WKV_STACK
    tile = lambda q: (q // WKV_PAIRS, q % WKV_PAIRS)

    @pl.when(ti == 0)
    def _():
        for q in range(n_tiles):
            b, p = tile(q)
            s_scr[q] = jnp.concatenate([s0_ref[b, 2 * p], s0_ref[b, 2 * p + 1]], axis=1)

    lane = lax.broadcasted_iota(jnp.int32, (1, WKV_LANES), 1)
    r2 = lax.broadcasted_iota(jnp.int32, (2 * WKV_LANES, 1), 0)
    c2 = lax.broadcasted_iota(jnp.int32, (1, 2 * WKV_LANES), 1)
    same_head2 = jnp.where(r2 // hd == c2 // hd, 1.0, 0.0).astype(bf16)
    on_diag = lax.broadcasted_iota(jnp.int32, (hd, 1), 0) == lane % hd
    n_flush = min(tc, WKV_FLUSH)
    y_scr[...] = jnp.zeros(y_scr.shape, f32)
    pairs = [(2 * i, 2 * i + 1) for i in range(n_stacks // 2)]
    stack_tiles = lambda st: range(st * WKV_STACK, (st + 1) * WKV_STACK)

    def row_sums(per_tile):
        out = [None] * n_tiles
        for s0, s1 in pairs:
            lhs = jnp.concatenate([jnp.concatenate([per_tile[q] for q in stack_tiles(st)], axis=0)
                                   for st in (s0, s1)], axis=1)
            res = jnp.dot(lhs.astype(bf16), same_head2, preferred_element_type=f32)
            for half, st in enumerate((s0, s1)):
                for n, q in enumerate(stack_tiles(st)):
                    out[q] = res[n * hd:(n + 1) * hd, half * WKV_LANES:(half + 1) * WKV_LANES]
        return out

    def step(t, carry):
        here = (lane % hd) == (t % n_flush)
        get = lambda ref, q: ref[tile(q)[0], tile(q)[1], pl.ds(t, 1), :]
        s_old = [s_scr[q] for q in range(n_tiles)]
        sa = row_sums([s_old[q] * get(kk_ref, q) for q in range(n_tiles)])
        v_col = row_sums([jnp.where(on_diag, get(v_ref, q), 0.0) for q in range(n_tiles)])
        s_new = []
        for q in range(n_tiles):
            s = s_old[q] * get(w_ref, q) - sa[q] * get(kka_ref, q) + v_col[q] * get(k_ref, q)
            s_scr[q] = s
            s_new.append(s)
        y_col = row_sums([s_new[q] * get(r_ref, q) for q in range(n_tiles)])
        for q in range(n_tiles):
            y_scr[q] = jnp.where(here, y_col[q], y_scr[q])
        return carry

    for sub in range(tc // n_flush):
        lax.fori_loop(sub * n_flush, (sub + 1) * n_flush, step, 0)
        for q in range(n_tiles):
            b, p = tile(q)
            yt = y_scr[q].T
            y_ref[b, p, sub * n_flush:(sub + 1) * n_flush, :] = jnp.concatenate(
                [yt[:n_flush], yt[hd:hd + n_flush]], axis=1)

    @pl.when(ti == pl.num_programs(1) - 1)
    def _():
        for q in range(n_tiles):
            b, p = tile(q)
            st_ref[b, 2 * p] = s_scr[q][:, :hd]
            st_ref[b, 2 * p + 1] = s_scr[q][:, hd:]


def _wkv_scan(r, w, k, v, kk, kka, s0, tc):
    bsz, n_pairs, seq, _ = r.shape
    assert n_pairs == WKV_PAIRS and bsz % WKV_BB == 0 and seq % tc == 0
    n_tiles = WKV_BB * WKV_PAIRS
    x_spec = pl.BlockSpec((WKV_BB, WKV_PAIRS, tc, WKV_LANES), lambda b, i: (b, 0, i, 0))
    s_spec = pl.BlockSpec((WKV_BB, RWKV_HEADS, RWKV_HEAD_DIM, RWKV_HEAD_DIM), lambda b, i: (b, 0, 0, 0))
    return pl.pallas_call(
        functools.partial(_wkv_kernel, tc=tc),
        grid=(bsz // WKV_BB, seq // tc),
        in_specs=[x_spec] * 6 + [s_spec],
        out_specs=[x_spec, s_spec],
        out_shape=[jax.ShapeDtypeStruct(r.shape, jnp.float32),
                   jax.ShapeDtypeStruct(s0.shape, jnp.float32)],
        scratch_shapes=[pltpu.VMEM((n_tiles, RWKV_HEAD_DIM, WKV_LANES), jnp.float32),
                        pltpu.VMEM((n_tiles, RWKV_HEAD_DIM, WKV_LANES), jnp.float32)],
        compiler_params=pltpu.CompilerParams(
            dimension_semantics=("arbitrary", "arbitrary"), vmem_limit_bytes=VMEM_LIMIT_BYTES),
        name="wkv_scan",
    )(r, w, k, v, kk, kka, s0)


def _rmsnorm(x, g):
    xf = x.astype(jnp.float32)
    y = xf * lax.rsqrt(jnp.mean(xf * xf, axis=-1, keepdims=True) + NORM_EPS)
    return (y * g.astype(jnp.float32)).astype(x.dtype)


def _alibi_slopes():
    n = jnp.arange(1, NSA_HEADS + 1, dtype=jnp.float32)
    return jnp.exp2(-8.0 * n / NSA_HEADS).reshape(NSA_KV_HEADS, HPG)


def _compress_blocks(rows, w1, pe, w2):
    B, L = rows.shape[:2]
    n_chunk = L // CMP_STRIDE
    r = CMP_BLOCK // CMP_STRIDE
    nc = n_chunk - r + 1
    ch = rows.reshape(B, n_chunk, CMP_STRIDE, NSA_KV_HEADS, HEAD_DIM)
    pre = jnp.einsum('pd,pde->e', pe, w1)
    for i in range(r):
        pre = pre + jnp.einsum('bcpgd,pde->bcge', ch[:, i:i + nc], w1[i * CMP_STRIDE:(i + 1) * CMP_STRIDE])
    return jnp.einsum('bcge,ef->bcgf', jax.nn.silu(pre), w2)


def _overlap_matrix(nc, ns):
    c_start = jnp.arange(nc) * CMP_STRIDE
    c_end = c_start + CMP_BLOCK - 1
    s_start = jnp.arange(ns) * SEL_BLOCK
    s_end = s_start + SEL_BLOCK - 1
    return ((c_start[:, None] <= s_end[None]) & (c_end[:, None] >= s_start[None])).astype(jnp.float32)


def _compressed_branch(qg, pos_q, kc, vc, n_sel, slopes):
    nc = kc.shape[1]
    end_c = jnp.arange(nc) * CMP_STRIDE + CMP_BLOCK - 1
    dist = pos_q[:, None] - end_c[None, :]
    valid = (dist >= 0)[None, :, None, None, :]
    s = jnp.einsum('btghd,bcgd->btghc', qg, kc).astype(jnp.float32) * HEAD_DIM ** -0.5
    s = s - slopes[None, None, :, :, None] * dist.astype(jnp.float32)[None, :, None, None, :]
    p = jax.nn.softmax(jnp.where(valid, s, NEG_INF), axis=-1) * valid
    o = jnp.einsum('btghc,bcgd->btghd', p.astype(vc.dtype), vc)
    imp = jnp.einsum('btghc,cj->btgj', p, _overlap_matrix(nc, n_sel))
    blk_t = (pos_q // SEL_BLOCK)[:, None]
    j = jnp.arange(n_sel)[None, :]
    forced = (j == 0) | (j == blk_t) | (j == blk_t - 1)
    imp = jnp.where(forced[None, :, None, :], imp + FORCE_BONUS, imp)
    imp = jnp.where((j <= blk_t)[None, :, None, :], imp, NEG_INF)
    top_val, top_idx = lax.top_k(imp, min(N_SELECT, n_sel))
    return o, top_idx, top_val > 0.5 * NEG_INF


def _gather_selected(idx, new_blk, n_past_blk, pool, layer, page_table):
    b_idx = jnp.arange(idx.shape[0])[:, None, None, None]
    g_idx = jnp.arange(NSA_KV_HEADS)[None, None, :, None]
    local = jnp.clip(idx - n_past_blk, 0, new_blk.shape[1] - 1)
    blk = new_blk[b_idx, local, :, :, g_idx]
    if pool is not None:
        sub_per_page = PAGE_SIZE // SEL_BLOCK
        past = jnp.clip(idx, 0, n_past_blk - 1)
        phys = page_table[b_idx, past // sub_per_page]
        pool_r = pool.reshape(pool.shape[0], pool.shape[1], sub_per_page, SEL_BLOCK, 2, NSA_KV_HEADS, HEAD_DIM)
        l_idx = jnp.full_like(phys, layer)
        pblk = pool_r[l_idx, phys, past % sub_per_page, :, :, g_idx]
        blk = jnp.where((idx < n_past_blk)[..., None, None, None], pblk, blk)
    return blk[..., 0, :], blk[..., 1, :]


def _selected_branch(qg, pos_q, kb, vb, idx, sel_valid, slopes):
    B, Tq, G, H, _ = qg.shape
    s = jnp.einsum('btghd,btgkpd->btghkp', qg, kb).astype(jnp.float32) * HEAD_DIM ** -0.5
    key_pos = idx[..., None] * SEL_BLOCK + jnp.arange(SEL_BLOCK)
    dist = pos_q[None, :, None, None, None] - key_pos
    mask = ((dist >= 0) & sel_valid[..., None])[:, :, :, None]
    s = s - slopes[None, None, :, :, None, None] * dist.astype(jnp.float32)[:, :, :, None]
    s = jnp.where(mask, s, NEG_INF)
    p = jax.nn.softmax(s.reshape(B, Tq, G, H, -1), axis=-1).reshape(s.shape)
    return jnp.einsum('btghkp,btgkpd->btghd', p.astype(vb.dtype), vb)


def _window_attend(qg, pos_q, k, v, key_pos, slopes):
    s = jnp.einsum('btghd,bsgd->btghs', qg, k).astype(jnp.float32) * HEAD_DIM ** -0.5
    dist = pos_q[:, None] - key_pos[None, :]
    mask = ((dist >= 0) & (dist < WINDOW) & (key_pos[None, :] >= 0))[None, :, None, None, :]
    s = s - slopes[None, None, :, :, None] * dist.astype(jnp.float32)[None, :, None, None, :]
    p = jax.nn.softmax(jnp.where(mask, s, NEG_INF), axis=-1)
    return jnp.einsum('btghs,bsgd->btghd', p.astype(v.dtype), v)


def _nsa_group(proj, pos_q, lp, layer, pool_cmp, pool_slc, page_table, win_buf, past_len):
    B, T = proj.shape[:2]
    G, dh = NSA_KV_HEADS, HEAD_DIM
    slopes = _alibi_slopes()
    qg = proj[..., Q_OFF:CMP_OFF].reshape(B, T, G, HPG, dh)
    cmp_new = proj[..., CMP_OFF:SLC_OFF].reshape(B, T, 2, G, dh)
    slc_new = proj[..., SLC_OFF:WIN_OFF].reshape(B, T, 2, G, dh)
    win_new = proj[..., WIN_OFF:RKV_OFF].reshape(B, T, 2, G, dh)
    gates = jax.nn.sigmoid(proj[..., GATE_OFF:IN_WIDTH].astype(jnp.float32)).reshape(B, T, 3, G, HPG, 1)

    if pool_cmp is None:
        proj2d = proj.reshape(B * T, IN_WIDTH)
        kvc = _compress_prompt(proj2d, B, T, _cmp_weights(lp))
        o = _nsa_prompt(proj2d, kvc, B, T).reshape(B, T, NSA_WIDTH)
        pad_rows = jnp.zeros((B, min(WINDOW, past_len), 2, G, dh), win_new.dtype)
        win_state = jnp.concatenate([pad_rows, win_new], axis=1)[:, -min(WINDOW, past_len):]
        return o, cmp_new, slc_new, win_state

    n_past = page_table.shape[1] * PAGE_SIZE
    assert n_past % CMP_STRIDE == 0 and T < CMP_STRIDE and T <= 8
    assert -(-(n_past + T) // SEL_BLOCK) <= TOPK_LANES and n_past % SEL_BLOCK == 0 and T <= SEL_BLOCK
    assert win_buf.shape[1] == WINDOW
    kvw = 2 * KV_WIDTH
    kvc = _compress_paged(pool_cmp[layer].reshape(-1, PAGE_SIZE, kvw), page_table, _cmp_weights(lp))
    o = _nsa_decode(proj, kvc, win_buf.reshape(B, WINDOW, kvw), pool_slc[layer], page_table, n_past)
    win_state = jnp.concatenate([win_buf, win_new], axis=1)[:, -win_buf.shape[1]:]
    return o, cmp_new, slc_new, win_state


def _rwkv_group(h, h_shift, p_rkv, p_rkv_prev, wkv0, lp):
    B, T, _ = h.shape
    f32 = jnp.float32
    rkv = (p_rkv + lp['mu_rkv'] * (p_rkv_prev - p_rkv)).astype(f32)
    r, k, v = jnp.split(rkv, 3, axis=-1)
    xx = h_shift - h
    xw = h + xx * lp['mu_wag'][0]
    xa = h + xx * lp['mu_wag'][1]
    xg = h + xx * lp['mu_wag'][2]
    w_log = -jax.nn.softplus(-(lp['decay_w0'] + jnp.tanh(xw @ lp['decay_w1']) @ lp['decay_w2']).astype(f32)) - 0.5
    decay = jnp.exp(-jnp.exp(w_log))
    a = jax.nn.sigmoid((lp['iclr_a0'] + (xa @ lp['iclr_a1']) @ lp['iclr_a2']).astype(f32))
    g = (jax.nn.sigmoid(xg @ lp['gate_g1']) @ lp['gate_g2']).astype(f32)

    def heads(t):
        return t.reshape(B, T, RWKV_HEADS, RWKV_HEAD_DIM)

    kk = heads(k * lp['k_k'])
    kk = kk / jnp.maximum(jnp.sqrt(jnp.sum(kk * kk, axis=-1, keepdims=True)), 1e-12)
    k = k * (1.0 + (a - 1.0) * lp['k_a'])
    r_h, k_h, v_h, w_h, a_h = heads(r), heads(k), heads(v), heads(decay), heads(a)

    t_pad = -(-T // WKV_MIN_CHUNK) * WKV_MIN_CHUNK
    tc = WKV_CHUNK if t_pad % WKV_CHUNK == 0 else WKV_MIN_CHUNK

    def pairs(t, fill):
        t = t.reshape(B, T, RWKV_HEADS // 2, WKV_LANES)
        if t_pad != T:
            t = jnp.pad(t, ((0, 0), (0, t_pad - T), (0, 0), (0, 0)), constant_values=fill)
        return t.transpose(0, 2, 1, 3)

    y, S_T = _wkv_scan(pairs(r, 0.0), pairs(decay, 1.0), pairs(k, 0.0), pairs(v, 0.0),
                       pairs(kk, 0.0), pairs(kk * a_h, 0.0), wkv0.astype(f32), tc)
    y = y.transpose(0, 2, 1, 3)[:, :T].reshape(B, T, RWKV_HEADS, RWKV_HEAD_DIM)
    mu = jnp.mean(y, axis=-1, keepdims=True)
    var = jnp.mean(jnp.square(y - mu), axis=-1, keepdims=True)
    y = ((y - mu) * lax.rsqrt(var + GN_EPS)).reshape(B, T, RWKV_WIDTH) * lp['ln_x_w'] + lp['ln_x_b']
    bonus = (jnp.sum(r_h * k_h * lp['r_k'], axis=-1, keepdims=True) * v_h).reshape(B, T, RWKV_WIDTH)
    return (y + bonus) * g, S_T


def _hier_route(h, w_rg, b_rg, w_re, b_re):
    n = h.shape[0]
    pg = jax.nn.softmax((h @ w_rg + b_rg).astype(jnp.float32), axis=-1)
    g_val, g_sel = lax.top_k(pg, 1)
    le = (h @ w_re + b_re).astype(jnp.float32).reshape(n, N_GROUPS, EXPERTS_PER_GROUP)
    le_g = jnp.take_along_axis(le, g_sel[:, :, None], axis=1)[:, 0]
    e_val, e_sel = lax.top_k(le_g, EXPERT_TOP_K)
    weights = jax.nn.softmax(e_val, axis=-1) * g_val
    return g_sel * EXPERTS_PER_GROUP + e_sel, weights


def _moe_ffn(h, eid, ew, w_gate, w_up, w_down):
    n, d = h.shape
    a_tot = n * EXPERT_TOP_K
    flat_e = eid.reshape(-1)
    flat_t = jnp.arange(a_tot) // EXPERT_TOP_K
    flat_w = ew.reshape(-1)
    order = jnp.argsort(flat_e)
    se, st, sw = flat_e[order], flat_t[order], flat_w[order]
    counts = jnp.bincount(flat_e, length=N_EXPERTS)
    padded = (counts + MOE_BLOCK - 1) // MOE_BLOCK * MOE_BLOCK
    pad_end = jnp.cumsum(padded)
    pad_start = pad_end - padded
    start = jnp.cumsum(counts) - counts
    dest = pad_start[se] + jnp.arange(a_tot) - start[se]
    n_blk = (a_tot + N_EXPERTS * (MOE_BLOCK - 1) + MOE_BLOCK - 1) // MOE_BLOCK
    tok_buf = jnp.full((n_blk * MOE_BLOCK,), n, jnp.int32).at[dest].set(st)
    h_pad = jnp.concatenate([h, jnp.zeros((1, d), h.dtype)], axis=0)
    xb = h_pad[tok_buf].reshape(n_blk, MOE_BLOCK, d)
    blk_e = jnp.clip(jnp.searchsorted(pad_end, jnp.arange(n_blk) * MOE_BLOCK, side='right'), 0, N_EXPERTS - 1)
    yb = _moe_blocks(xb, blk_e.astype(jnp.int32), w_gate, w_up, w_down).reshape(n_blk * MOE_BLOCK, d)
    out = jnp.zeros((n, d), jnp.float32).at[st].add(yb[dest].astype(jnp.float32) * sw[:, None])
    return out.astype(h.dtype)


def _decoder_layer(x, c, pos_q, lp, layer, pool_cmp, pool_slc, page_table, win_buf, wkv0, shift0, past_len):
    B, T, D = x.shape
    mod = jax.nn.silu(c) @ lp['w_ada'] + lp['b_ada']
    sh1, sc1, gt1, sh2, sc2, gt2 = jnp.split(mod[:, None, :], 6, axis=-1)
    h = _rmsnorm(x, lp['norm1']) * (1.0 + sc1) + sh1
    tm = 512 if (B * T) % 512 == 0 else B * T
    proj = _matmul(h.reshape(B * T, D), lp['w_in'], tm, 512).reshape(B, T, IN_WIDTH)
    shift0 = shift0.astype(h.dtype)
    h_shift = jnp.concatenate([shift0[:, None], h[:, :-1]], axis=1)
    p_rkv = proj[..., RKV_OFF:GATE_OFF]
    p_rkv_prev = jnp.concatenate([(shift0 @ lp['w_in'][:, RKV_OFF:GATE_OFF])[:, None], p_rkv[:, :-1]], axis=1)
    o_nsa, cmp_new, slc_new, win_state = _nsa_group(proj, pos_q, lp, layer, pool_cmp, pool_slc, page_table,
                                                    win_buf, past_len)
    o_rwkv, wkv_T = _rwkv_group(h, h_shift, p_rkv, p_rkv_prev, wkv0, lp)
    mix_in = jnp.concatenate([o_nsa, o_rwkv.astype(o_nsa.dtype)], axis=-1).reshape(B * T, D)
    mixed = _matmul(mix_in, lp['w_out'], tm, 512).reshape(B, T, D)
    x = x + gt1 * mixed
    h2 = (_rmsnorm(x, lp['norm2']) * (1.0 + sc2) + sh2).reshape(B * T, D)
    eid, ew = _hier_route(h2, lp['w_router_group'], lp['b_router_group'], lp['w_router_expert'],
                          lp['b_router_expert'])
    ffn = _moe_ffn(h2, eid, ew, lp['w_gate'], lp['w_up'], lp['w_down']).reshape(B, T, D)
    x = x + gt2 * ffn
    return x, cmp_new, slc_new, win_state, wkv_T, h[:, -1]


def kernel(x_prompt, x_sample, c_prompt, c_sample, cache_cmp_kv, cache_slc_kv, page_table, state_win_kv, state_wkv, state_shift, w_ada, b_ada, norm1, w_in, cmp_k_w1, cmp_k_pe, cmp_k_w2, cmp_v_w1, cmp_v_pe, cmp_v_w2, mu_rkv, mu_wag, decay_w0, decay_w1, decay_w2, iclr_a0, iclr_a1, iclr_a2, gate_g1, gate_g2, k_k, k_a, r_k, ln_x_w, ln_x_b, w_out, norm2, w_router_group, b_router_group, w_router_expert, b_router_expert, w_gate, w_up, w_down, norm_f):
    bp, tp = x_prompt.shape[:2]
    ts = x_sample.shape[1]
    past_len = page_table.shape[1] * PAGE_SIZE
    pos_p = jnp.arange(tp)
    pos_s = past_len + jnp.arange(ts)
    l = 0
    lp = dict(w_ada=w_ada[l], b_ada=b_ada[l], norm1=norm1[l], w_in=w_in[l],
              cmp_k_w1=cmp_k_w1[l], cmp_k_pe=cmp_k_pe[l], cmp_k_w2=cmp_k_w2[l],
              cmp_v_w1=cmp_v_w1[l], cmp_v_pe=cmp_v_pe[l], cmp_v_w2=cmp_v_w2[l],
              mu_rkv=mu_rkv[l], mu_wag=mu_wag[l], decay_w0=decay_w0[l], decay_w1=decay_w1[l],
              decay_w2=decay_w2[l], iclr_a0=iclr_a0[l], iclr_a1=iclr_a1[l], iclr_a2=iclr_a2[l],
              gate_g1=gate_g1[l], gate_g2=gate_g2[l], k_k=k_k[l], k_a=k_a[l], r_k=r_k[l],
              ln_x_w=ln_x_w[l], ln_x_b=ln_x_b[l], w_out=w_out[l], norm2=norm2[l],
              w_router_group=w_router_group[l], b_router_group=b_router_group[l],
              w_router_expert=w_router_expert[l], b_router_expert=b_router_expert[l],
              w_gate=w_gate[l], w_up=w_up[l], w_down=w_down[l])
    wkv_zero = jnp.zeros((bp, RWKV_HEADS, RWKV_HEAD_DIM, RWKV_HEAD_DIM), jnp.float32)
    shift_zero = jnp.zeros((bp, D_MODEL), x_prompt.dtype)
    hp, a1, a2, a3, a4, a5 = _decoder_layer(x_prompt, c_prompt, pos_p, lp, l, None, None, None, None,
                                            wkv_zero, shift_zero, past_len)
    hs, b1, b2, b3, b4, b5 = _decoder_layer(x_sample, c_sample, pos_s, lp, l, cache_cmp_kv, cache_slc_kv,
                                            page_table, state_win_kv[l], state_wkv[l], state_shift[l], past_len)
    y_prompt = _rmsnorm(hp, norm_f)
    y_sample = _rmsnorm(hs, norm_f)
    st = lambda a: a[None]
    return (y_prompt, y_sample, st(a1), st(b1), st(a2), st(b2), st(a3), st(b3), st(a4), st(b4), st(a5), st(b5))
```

```python
import functools

import jax
import jax.numpy as jnp
from jax import lax
from jax.experimental import pallas as pl
from jax.experimental.pallas import tpu as pltpu

D_MODEL = 2048
DEPTH = 1
PAGE_SIZE = 128
HEAD_DIM = 128
NSA_WIDTH = D_MODEL // 2
NSA_HEADS = NSA_WIDTH // HEAD_DIM
NSA_KV_HEADS = 2
HPG = NSA_HEADS // NSA_KV_HEADS
KV_WIDTH = NSA_KV_HEADS * HEAD_DIM
CMP_BLOCK = 32
CMP_STRIDE = 16
SEL_BLOCK = 64
N_SELECT = 16
WINDOW = 512
FORCE_BONUS = 1e4
SEL_QBLOCK = 32
WIN_QBLOCK = 128
RWKV_WIDTH = D_MODEL - NSA_WIDTH
RWKV_HEAD_DIM = 64
RWKV_HEADS = RWKV_WIDTH // RWKV_HEAD_DIM
GN_EPS = 64e-5
N_GROUPS = 4
EXPERTS_PER_GROUP = 8
N_EXPERTS = N_GROUPS * EXPERTS_PER_GROUP
EXPERT_TOP_K = 2
D_EXPERT = 512
MOE_BLOCK = 64
NORM_EPS = 1e-6
NEG_INF = -1e30
Q_OFF = 0
CMP_OFF = Q_OFF + NSA_WIDTH
SLC_OFF = CMP_OFF + 2 * KV_WIDTH
WIN_OFF = SLC_OFF + 2 * KV_WIDTH
RKV_OFF = WIN_OFF + 2 * KV_WIDTH
GATE_OFF = RKV_OFF + 3 * RWKV_WIDTH
IN_WIDTH = GATE_OFF + 3 * NSA_HEADS

VMEM_LIMIT_BYTES = 48 * 1024 * 1024


def _mm_kernel(x_ref, w_ref, o_ref):
    o_ref[...] = jnp.dot(x_ref[...].astype(jnp.bfloat16), w_ref[...].astype(jnp.bfloat16),
                         preferred_element_type=jnp.float32)


def _matmul(x, w, tm, tn):
    m, k = x.shape
    n = w.shape[1]
    return pl.pallas_call(
        _mm_kernel,
        grid=(pl.cdiv(m, tm), pl.cdiv(n, tn)),
        in_specs=[pl.BlockSpec((tm, k), lambda i, j: (i, 0)),
                  pl.BlockSpec((k, tn), lambda i, j: (0, j))],
        out_specs=pl.BlockSpec((tm, tn), lambda i, j: (i, j)),
        out_shape=jax.ShapeDtypeStruct((m, n), jnp.float32),
        compiler_params=pltpu.CompilerParams(
            dimension_semantics=("arbitrary", "arbitrary"), vmem_limit_bytes=VMEM_LIMIT_BYTES),
        name="matmul",
    )(x, w)


MOE_TM = 256


def _moe_block_kernel(meta_ref, x_ref, wg_ref, wu_ref, wd_ref, o_ref, wg_b, wu_b, wd_b):
    bf16 = jnp.bfloat16
    i = pl.program_id(0)
    used = i < meta_ref[pl.num_programs(0)]
    new_expert = (i == 0) | (meta_ref[i] != meta_ref[jnp.maximum(i - 1, 0)])

    @pl.when(used & new_expert)
    def _():
        wg_b[...] = wg_ref[0].astype(bf16)
        wu_b[...] = wu_ref[0].astype(bf16)
        wd_b[...] = wd_ref[0].astype(bf16)

    @pl.when(used)
    def _():
        x = x_ref[0]
        g = jnp.dot(x, wg_b[...], preferred_element_type=jnp.float32)
        u = jnp.dot(x, wu_b[...], preferred_element_type=jnp.float32)
        hmid = (g * jax.nn.sigmoid(g)) * u
        o_ref[0] = jnp.dot(hmid.astype(bf16), wd_b[...], preferred_element_type=jnp.float32)

    @pl.when(jnp.logical_not(used))
    def _():
        o_ref[...] = jnp.zeros(o_ref.shape, o_ref.dtype)


def _moe_blocks(xb, meta, w_gate, w_up, w_down):
    n_blk, mb, d = xb.shape
    de = w_gate.shape[2]
    grid_spec = pltpu.PrefetchScalarGridSpec(
        num_scalar_prefetch=1,
        grid=(n_blk,),
        in_specs=[pl.BlockSpec((1, mb, d), lambda i, e: (i, 0, 0)),
                  pl.BlockSpec((1, d, de), lambda i, e: (e[i], 0, 0)),
                  pl.BlockSpec((1, d, de), lambda i, e: (e[i], 0, 0)),
                  pl.BlockSpec((1, de, d), lambda i, e: (e[i], 0, 0))],
        out_specs=pl.BlockSpec((1, mb, d), lambda i, e: (i, 0, 0)),
        scratch_shapes=[pltpu.VMEM((d, de), jnp.bfloat16), pltpu.VMEM((d, de), jnp.bfloat16),
                        pltpu.VMEM((de, d), jnp.bfloat16)],
    )
    return pl.pallas_call(
        _moe_block_kernel,
        grid_spec=grid_spec,
        out_shape=jax.ShapeDtypeStruct((n_blk, mb, d), jnp.float32),
        compiler_params=pltpu.CompilerParams(
            dimension_semantics=("arbitrary",), vmem_limit_bytes=VMEM_LIMIT_BYTES),
        name="moe_blocks",
    )(meta, xb, w_gate, w_up, w_down)


ROUTER_LANES = 128


def _norm_in_kernel(x_ref, n1_ref, sc_ref, sh_ref, w_ref, proj_ref, h_ref, hb_scr):
    @pl.when(pl.program_id(2) == 0)
    def _():
        x = x_ref[0]
        y = x * lax.rsqrt(jnp.mean(x * x, axis=-1, keepdims=True) + NORM_EPS) * n1_ref[...]
        h = y * (1.0 + sc_ref[0]) + sh_ref[0]
        h_ref[0] = h
        hb_scr[...] = h.astype(jnp.bfloat16)

    proj_ref[0] = jnp.dot(hb_scr[...], w_ref[...], preferred_element_type=jnp.float32)


def _norm_in(x, norm1, sc, sh, w_in_b, tm, tn=512):
    g, t, d = x.shape
    n = w_in_b.shape[1]
    mrows = sc.shape[1]
    mod_spec = pl.BlockSpec((1, mrows if mrows == 1 else tm, d),
                            (lambda b, i, j: (b, 0, 0)) if mrows == 1 else (lambda b, i, j: (b, i, 0)))
    return pl.pallas_call(
        _norm_in_kernel,
        grid=(g, t // tm, pl.cdiv(n, tn)),
        in_specs=[pl.BlockSpec((1, tm, d), lambda b, i, j: (b, i, 0)),
                  pl.BlockSpec((1, d), lambda b, i, j: (0, 0)),
                  mod_spec, mod_spec,
                  pl.BlockSpec((d, tn), lambda b, i, j: (0, j))],
        out_specs=[pl.BlockSpec((1, tm, tn), lambda b, i, j: (b, i, j)),
                   pl.BlockSpec((1, tm, d), lambda b, i, j: (b, i, 0))],
        out_shape=[jax.ShapeDtypeStruct((g, t, n), jnp.float32), jax.ShapeDtypeStruct((g, t, d), jnp.float32)],
        scratch_shapes=[pltpu.VMEM((tm, d), jnp.bfloat16)],
        compiler_params=pltpu.CompilerParams(
            dimension_semantics=("arbitrary", "arbitrary", "arbitrary"), vmem_limit_bytes=VMEM_LIMIT_BYTES),
        name="norm_in",
    )(x, norm1.reshape(1, d), sc, sh, w_in_b)


def _mix_out_kernel(on_ref, orw_ref, w_ref, x_ref, gt_ref, sc_ref, sh_ref, n2_ref, wr_hi_ref, wr_lo_ref, br_ref,
                    x1_ref, h2_ref, lg_ref):
    f32, bf16 = jnp.float32, jnp.bfloat16
    half = on_ref.shape[2]
    mixed = (jnp.dot(on_ref[0].astype(bf16), w_ref[0:half, :], preferred_element_type=f32)
             + jnp.dot(orw_ref[0].astype(bf16), w_ref[half:, :], preferred_element_type=f32))
    x1 = x_ref[0] + gt_ref[0] * mixed
    x1_ref[0] = x1
    y = x1 * lax.rsqrt(jnp.mean(x1 * x1, axis=-1, keepdims=True) + NORM_EPS) * n2_ref[...]
    h2 = y * (1.0 + sc_ref[0]) + sh_ref[0]
    hi = h2.astype(bf16)
    h2_ref[0] = hi
    lo = (h2 - hi.astype(f32)).astype(bf16)
    lg_ref[0] = (jnp.dot(hi, wr_hi_ref[...], preferred_element_type=f32)
                 + jnp.dot(hi, wr_lo_ref[...], preferred_element_type=f32)
                 + jnp.dot(lo, wr_hi_ref[...], preferred_element_type=f32) + br_ref[...])


def _mix_out(o_nsa, o_rwkv, w_out_b, x, gt, sc, sh, norm2, wr_hi, wr_lo, br, tm):
    g, t, d = x.shape
    half = o_nsa.shape[2]
    mrows = sc.shape[1]
    mod_spec = pl.BlockSpec((1, mrows if mrows == 1 else tm, d),
                            (lambda b, i: (b, 0, 0)) if mrows == 1 else (lambda b, i: (b, i, 0)))
    row = lambda w: pl.BlockSpec((1, tm, w), lambda b, i: (b, i, 0))
    full = lambda a: pl.BlockSpec(a.shape, lambda b, i: (0,) * a.ndim)
    n2 = norm2.reshape(1, d)
    return pl.pallas_call(
        _mix_out_kernel,
        grid=(g, t // tm),
        in_specs=[row(half), row(half), full(w_out_b), row(d), mod_spec, mod_spec, mod_spec, full(n2),
                  full(wr_hi), full(wr_lo), full(br)],
        out_specs=[row(d), row(d), row(ROUTER_LANES)],
        out_shape=[jax.ShapeDtypeStruct((g, t, d), jnp.float32), jax.ShapeDtypeStruct((g, t, d), jnp.bfloat16),
                   jax.ShapeDtypeStruct((g, t, ROUTER_LANES), jnp.float32)],
        compiler_params=pltpu.CompilerParams(
            dimension_semantics=("arbitrary", "arbitrary"), vmem_limit_bytes=VMEM_LIMIT_BYTES),
        name="mix_out",
    )(o_nsa, o_rwkv, w_out_b, x, gt, sc, sh, n2, wr_hi, wr_lo, br)


def _final_kernel(x_ref, y2_ref, ew_ref, gt_ref, nf_ref, o_ref):
    d = x_ref.shape[2]
    ew = ew_ref[0]
    ffn = y2_ref[0, :, 0:d] * ew[:, 0:1] + y2_ref[0, :, d:2 * d] * ew[:, 1:2]
    x2 = x_ref[0] + gt_ref[0] * ffn
    o_ref[0] = x2 * lax.rsqrt(jnp.mean(x2 * x2, axis=-1, keepdims=True) + NORM_EPS) * nf_ref[...]


def _final(x1, y2, ew, gt, norm_f, tm):
    g, t, d = x1.shape
    mrows = gt.shape[1]
    mod_spec = pl.BlockSpec((1, mrows if mrows == 1 else tm, d),
                            (lambda b, i: (b, 0, 0)) if mrows == 1 else (lambda b, i: (b, i, 0)))
    row = lambda w: pl.BlockSpec((1, tm, w), lambda b, i: (b, i, 0))
    return pl.pallas_call(
        _final_kernel,
        grid=(g, t // tm),
        in_specs=[row(d), row(2 * d), row(EXPERT_TOP_K), mod_spec, pl.BlockSpec((1, d), lambda b, i: (0, 0))],
        out_specs=row(d),
        out_shape=jax.ShapeDtypeStruct((g, t, d), jnp.float32),
        compiler_params=pltpu.CompilerParams(
            dimension_semantics=("arbitrary", "arbitrary"), vmem_limit_bytes=VMEM_LIMIT_BYTES),
        name="final_norm",
    )(x1, y2, ew, gt, norm_f.reshape(1, d))


def _cmp_partial_kernel(*refs, n_src, rows_per_src):
    x_refs = refs[:4 * n_src]
    w1k_ref, w1v_ref, a_ref, b_ref = refs[4 * n_src:]
    nch_src = rows_per_src // CMP_STRIDE
    for kvg in range(4):
        w_ref = w1k_ref if kvg < 2 else w1v_ref
        acc = None
        for p in range(CMP_STRIDE):
            parts = [x_refs[s * 4 + kvg][pl.ds(p, nch_src, stride=CMP_STRIDE), :] for s in range(n_src)]
            xp = parts[0] if n_src == 1 else jnp.concatenate(parts, axis=0)
            d = jnp.dot(xp.astype(jnp.bfloat16), w_ref[p], preferred_element_type=jnp.float32)
            acc = d if acc is None else acc + d
        a_ref[0, :, kvg * HEAD_DIM:(kvg + 1) * HEAD_DIM] = acc[:, :HEAD_DIM]
        b_ref[0, :, kvg * HEAD_DIM:(kvg + 1) * HEAD_DIM] = acc[:, HEAD_DIM:]


def _cmp_finish_kernel(a_ref, b_ref, pek_ref, pev_ref, w1k_ref, w1v_ref, w2k_ref, w2v_ref, o_ref):
    nch = a_ref.shape[1]
    for kv, (pe_ref, w1_ref, w2_ref) in enumerate(((pek_ref, w1k_ref, w2k_ref), (pev_ref, w1v_ref, w2v_ref))):
        pe8 = jnp.broadcast_to(pe_ref[...], (8, pe_ref.shape[1])).astype(jnp.bfloat16)
        pterm = jnp.dot(pe8, w1_ref[...], preferred_element_type=jnp.float32)[0:1]
        w2 = w2_ref[...]
        for g in range(NSA_KV_HEADS):
            lo = (kv * NSA_KV_HEADS + g) * HEAD_DIM
            nxt = pltpu.roll(b_ref[0, :, lo:lo + HEAD_DIM], nch - 1, 0)
            pre = a_ref[0, :, lo:lo + HEAD_DIM] + nxt + pterm
            act = pre * jax.nn.sigmoid(pre)
            o_ref[0, :, lo:lo + HEAD_DIM] = jnp.dot(act.astype(jnp.bfloat16), w2, preferred_element_type=jnp.float32)


def _cmp_weights(lp):
    bf = jnp.bfloat16
    half = CMP_BLOCK // 2
    cat = lambda w: jnp.concatenate([w[:half], w[half:]], axis=-1).astype(bf)
    flat = lambda w: w.reshape(CMP_BLOCK * HEAD_DIM, HEAD_DIM).astype(bf)
    return dict(w1k_cat=cat(lp['cmp_k_w1']), w1v_cat=cat(lp['cmp_v_w1']),
                w1k_flat=flat(lp['cmp_k_w1']), w1v_flat=flat(lp['cmp_v_w1']),
                pek=lp['cmp_k_pe'].reshape(1, -1), pev=lp['cmp_v_pe'].reshape(1, -1),
                w2k=lp['cmp_k_w2'].astype(bf), w2v=lp['cmp_v_w2'].astype(bf))


def _cmp_finish(a, b, cw):
    bsz, nch, _ = a.shape
    full = lambda arr: pl.BlockSpec(arr.shape, lambda i: (0,) * arr.ndim)
    blk = pl.BlockSpec((1, nch, 4 * HEAD_DIM), lambda i: (i, 0, 0))
    ws = [cw['pek'], cw['pev'], cw['w1k_flat'], cw['w1v_flat'], cw['w2k'], cw['w2v']]
    return pl.pallas_call(
        _cmp_finish_kernel,
        grid=(bsz,),
        in_specs=[blk, blk] + [full(w) for w in ws],
        out_specs=blk,
        out_shape=jax.ShapeDtypeStruct((bsz, nch, 4 * HEAD_DIM), jnp.float32),
        compiler_params=pltpu.CompilerParams(dimension_semantics=("arbitrary",), vmem_limit_bytes=VMEM_LIMIT_BYTES),
        name="cmp_finish",
    )(a, b, *ws)


def _compress_prompt(proj2d, bsz, seq, cw):
    nch = seq // CMP_STRIDE
    col0 = CMP_OFF // HEAD_DIM
    x_specs = [pl.BlockSpec((seq, HEAD_DIM), functools.partial(lambda i, c: (i, c), c=col0 + kvg)) for kvg in range(4)]
    w_spec = pl.BlockSpec(cw['w1k_cat'].shape, lambda i: (0, 0, 0))
    out_spec = pl.BlockSpec((1, nch, 4 * HEAD_DIM), lambda i: (i, 0, 0))
    shp = jax.ShapeDtypeStruct((bsz, nch, 4 * HEAD_DIM), jnp.float32)
    a, b = pl.pallas_call(
        functools.partial(_cmp_partial_kernel, n_src=1, rows_per_src=seq),
        grid=(bsz,),
        in_specs=x_specs + [w_spec, w_spec],
        out_specs=[out_spec, out_spec],
        out_shape=[shp, shp],
        compiler_params=pltpu.CompilerParams(dimension_semantics=("arbitrary",), vmem_limit_bytes=VMEM_LIMIT_BYTES),
        name="cmp_partial_prompt",
    )(proj2d, proj2d, proj2d, proj2d, cw['w1k_cat'], cw['w1v_cat'])
    return _cmp_finish(a, b, cw)


_NT = (((1,), (1,)), ((), ()))
SEL_TK = 512
WIN_TK = 256


def _flash_update(s, v, m_ref, l_ref, acc_ref, h):
    tk = s.shape[1]
    m_prev = m_ref[h]
    m_new = jnp.maximum(m_prev, jnp.max(s, axis=-1, keepdims=True))
    alpha = jnp.exp(m_prev - m_new)
    p = jnp.exp(s - jnp.concatenate([m_new] * (tk // HEAD_DIM), axis=1))
    l_ref[h] = alpha * l_ref[h] + jnp.sum(p, axis=-1, keepdims=True)
    acc_ref[h] = alpha * acc_ref[h] + jnp.dot(p.astype(jnp.bfloat16), v, preferred_element_type=jnp.float32)
    m_ref[h] = m_new


def _nsa_prompt_kernel(q_ref, slc_ref, win_ref, gate_ref, kvc_ref, o_ref, m_ref, l_ref, acc_ref, *, tq, seq):
    f32, bf16 = jnp.float32, jnp.bfloat16
    qi = pl.program_id(1)
    t0 = qi * tq
    scale = HEAD_DIM ** -0.5
    nc_valid = seq // CMP_STRIDE - CMP_BLOCK // CMP_STRIDE + 1
    n_sel = seq // SEL_BLOCK
    pos = t0 + lax.broadcasted_iota(jnp.int32, (tq, 1), 0)
    lane = lax.broadcasted_iota(jnp.int32, (1, HEAD_DIM), 1)
    blk_t = jnp.right_shift(pos, 6)
    gates = jax.nn.sigmoid(gate_ref[...])
    dist_c = pos - (lane * CMP_STRIDE + (CMP_BLOCK - 1))
    valid_c = (dist_c >= 0) & (lane < nc_valid)
    dist_cf = dist_c.astype(f32)
    c_row = lax.broadcasted_iota(jnp.int32, (HEAD_DIM, 1), 0)
    overlap = jnp.where((c_row * CMP_STRIDE <= lane * SEL_BLOCK + (SEL_BLOCK - 1))
                        & (c_row * CMP_STRIDE + (CMP_BLOCK - 1) >= lane * SEL_BLOCK), 1.0, 0.0).astype(bf16)
    forced = (lane == 0) | (lane == blk_t) | (lane == blk_t - 1)

    def gate_col(branch, hh):
        c = branch * NSA_HEADS + hh
        return gates[:, c:c + 1]

    def reset():
        m_ref[...] = jnp.full(m_ref.shape, NEG_INF, f32)
        l_ref[...] = jnp.zeros(l_ref.shape, f32)
        acc_ref[...] = jnp.zeros(acc_ref.shape, f32)

    for g in range(NSA_KV_HEADS):
        kcol = slice(g * HEAD_DIM, (g + 1) * HEAD_DIM)
        vcol = slice((NSA_KV_HEADS + g) * HEAD_DIM, (NSA_KV_HEADS + g + 1) * HEAD_DIM)
        heads = [g * HPG + h for h in range(HPG)]
        slopes = [2.0 ** -(hh + 1) for hh in heads]

        kc = kvc_ref[0, :, kcol].astype(bf16)
        vc = kvc_ref[0, :, vcol].astype(bf16)
        psum = jnp.zeros((tq, HEAD_DIM), f32)
        for h, hh in enumerate(heads):
            qh = q_ref[:, hh * HEAD_DIM:(hh + 1) * HEAD_DIM].astype(bf16)
            s = lax.dot_general(qh, kc, _NT, preferred_element_type=f32) * scale - slopes[h] * dist_cf
            s = jnp.where(valid_c, s, NEG_INF)
            e = jnp.exp(s - jnp.max(s, axis=-1, keepdims=True))
            p = e / jnp.sum(e, axis=-1, keepdims=True)
            p = jnp.where(valid_c, p, 0.0)
            o_cmp = jnp.dot(p.astype(bf16), vc, preferred_element_type=f32)
            o_ref[:, hh * HEAD_DIM:(hh + 1) * HEAD_DIM] = gate_col(0, hh) * o_cmp
            psum = psum + p
        p_hi = psum.astype(bf16)
        p_lo = (psum - p_hi.astype(f32)).astype(bf16)
        imp = (jnp.dot(p_hi, overlap, preferred_element_type=f32)
               + jnp.dot(p_lo, overlap, preferred_element_type=f32))
        imp = jnp.where(forced, imp + FORCE_BONUS, imp)
        imp = jnp.where(lane <= blk_t, imp, NEG_INF)
        beaten = jnp.zeros((tq, HEAD_DIM), f32)
        for jp in range(n_sel):
            col = imp[:, jp:jp + 1]
            tie = jnp.where(lane > jp, 1.0, 0.0)
            beaten = beaten + jnp.where(col > imp, 1.0, jnp.where(col == imp, tie, 0.0))
        sel = jnp.where(beaten < N_SELECT, jnp.where(imp > 0.5 * NEG_INF, 1.0, 0.0), 0.0).astype(bf16)

        reset()
        j_row = lax.broadcasted_iota(jnp.int32, (HEAD_DIM, 1), 0)

        def sel_body(kt, carry):
            k0 = pl.multiple_of(kt * SEL_TK, SEL_TK)
            k = slc_ref[pl.ds(k0, SEL_TK), kcol].astype(bf16)
            v = slc_ref[pl.ds(k0, SEL_TK), vcol].astype(bf16)
            kpos = k0 + lax.broadcasted_iota(jnp.int32, (1, SEL_TK), 1)
            dist = pos - kpos
            expand = jnp.where(jnp.right_shift(kpos, 6) == j_row, 1.0, 0.0).astype(bf16)
            picked = jnp.dot(sel, expand, preferred_element_type=f32)
            keep = jnp.where(dist >= 0, picked, 0.0) > 0.5
            dist_f = dist.astype(f32)
            for h, hh in enumerate(heads):
                qh = q_ref[:, hh * HEAD_DIM:(hh + 1) * HEAD_DIM].astype(bf16)
                s = lax.dot_general(qh, k, _NT, preferred_element_type=f32) * scale - slopes[h] * dist_f
                _flash_update(jnp.where(keep, s, NEG_INF), v, m_ref, l_ref, acc_ref, h)
            return carry

        lax.fori_loop(0, (t0 + tq - 1) // SEL_TK + 1, sel_body, 0)
        for h, hh in enumerate(heads):
            hs = slice(hh * HEAD_DIM, (hh + 1) * HEAD_DIM)
            o_ref[:, hs] = o_ref[:, hs] + gate_col(1, hh) * (acc_ref[h] / l_ref[h])

        reset()

        def win_body(kt, carry):
            k0 = pl.multiple_of(kt * WIN_TK, WIN_TK)
            k = win_ref[pl.ds(k0, WIN_TK), kcol].astype(bf16)
            v = win_ref[pl.ds(k0, WIN_TK), vcol].astype(bf16)
            dist = pos - (k0 + lax.broadcasted_iota(jnp.int32, (1, WIN_TK), 1))
            keep = (dist >= 0) & (dist < WINDOW)
            dist_f = dist.astype(f32)
            for h, hh in enumerate(heads):
                qh = q_ref[:, hh * HEAD_DIM:(hh + 1) * HEAD_DIM].astype(bf16)
                s = lax.dot_general(qh, k, _NT, preferred_element_type=f32) * scale - slopes[h] * dist_f
                _flash_update(jnp.where(keep, s, NEG_INF), v, m_ref, l_ref, acc_ref, h)
            return carry

        lax.fori_loop(jnp.maximum(t0 - (WINDOW - 1), 0) // WIN_TK, (t0 + tq - 1) // WIN_TK + 1, win_body, 0)
        for h, hh in enumerate(heads):
            hs = slice(hh * HEAD_DIM, (hh + 1) * HEAD_DIM)
            o_ref[:, hs] = o_ref[:, hs] + gate_col(2, hh) * (acc_ref[h] / l_ref[h])


def _nsa_prompt(proj2d, kvc, bsz, seq, tq=256):
    nq = seq // tq
    kvw = 2 * KV_WIDTH
    return pl.pallas_call(
        functools.partial(_nsa_prompt_kernel, tq=tq, seq=seq),
        grid=(bsz, nq),
        in_specs=[pl.BlockSpec((tq, NSA_WIDTH), lambda b, i: (b * nq + i, 0)),
                  pl.BlockSpec((seq, kvw), lambda b, i: (b, SLC_OFF // kvw)),
                  pl.BlockSpec((seq, kvw), lambda b, i: (b, WIN_OFF // kvw)),
                  pl.BlockSpec((tq, HEAD_DIM), lambda b, i: (b * nq + i, GATE_OFF // HEAD_DIM)),
                  pl.BlockSpec((1, seq // CMP_STRIDE, kvw), lambda b, i: (b, 0, 0))],
        out_specs=pl.BlockSpec((tq, NSA_WIDTH), lambda b, i: (b * nq + i, 0)),
        out_shape=jax.ShapeDtypeStruct((bsz * seq, NSA_WIDTH), jnp.float32),
        scratch_shapes=[pltpu.VMEM((HPG, tq, HEAD_DIM), jnp.float32),
                        pltpu.VMEM((HPG, tq, HEAD_DIM), jnp.float32),
                        pltpu.VMEM((HPG, tq, HEAD_DIM), jnp.float32)],
        compiler_params=pltpu.CompilerParams(
            dimension_semantics=("arbitrary", "arbitrary"), vmem_limit_bytes=VMEM_LIMIT_BYTES),
        name="nsa_prompt",
    )(proj2d, proj2d, proj2d, proj2d, kvc)


CMP_PAGES_PER_STEP = 16
TOPK_LANES = 384
IDX_LANES = 128


def _cmp_partial_paged_kernel(pt_ref, *refs, n_src, rows_per_src):
    del pt_ref
    _cmp_partial_kernel(*refs, n_src=n_src, rows_per_src=rows_per_src)


def _compress_paged(pool, page_table, cw):
    bsz, n_pages = page_table.shape
    nps = CMP_PAGES_PER_STEP
    n_tiles = n_pages // nps
    nch_tile = nps * PAGE_SIZE // CMP_STRIDE

    def page_spec(s, kvg):
        return pl.BlockSpec((None, PAGE_SIZE, HEAD_DIM), lambda b, i, pt: (pt[b * n_pages + i * nps + s], 0, kvg))

    x_specs = [page_spec(s, kvg) for s in range(nps) for kvg in range(4)]
    w_spec = pl.BlockSpec(cw['w1k_cat'].shape, lambda b, i, pt: (0, 0, 0))
    out_spec = pl.BlockSpec((1, nch_tile, 4 * HEAD_DIM), lambda b, i, pt: (b, i, 0))
    shp = jax.ShapeDtypeStruct((bsz, n_tiles * nch_tile, 4 * HEAD_DIM), jnp.float32)
    a, b = pl.pallas_call(
        functools.partial(_cmp_partial_paged_kernel, n_src=nps, rows_per_src=PAGE_SIZE),
        grid_spec=pltpu.PrefetchScalarGridSpec(
            num_scalar_prefetch=1, grid=(bsz, n_tiles),
            in_specs=x_specs + [w_spec, w_spec], out_specs=[out_spec, out_spec]),
        out_shape=[shp, shp],
        compiler_params=pltpu.CompilerParams(
            dimension_semantics=("arbitrary", "arbitrary"), vmem_limit_bytes=VMEM_LIMIT_BYTES),
        name="cmp_partial_paged",
    )(page_table.reshape(-1), *([pool] * (4 * nps)), cw['w1k_cat'], cw['w1v_cat'])
    return _cmp_finish(a, b, cw)


def _nsa_decode_a_kernel(proj_ref, kvc_ref, win_ref, o_ref, gsel_ref, idx_ref, *, t_new, n_past):
    f32, bf16 = jnp.float32, jnp.bfloat16
    scale = HEAD_DIM ** -0.5
    nch = kvc_ref.shape[1]
    n_win = win_ref.shape[1]
    rows = HPG * t_new
    r_iota = lax.broadcasted_iota(jnp.int32, (rows, 1), 0)
    t_row = r_iota % t_new
    h_row = r_iota // t_new
    pos_row = n_past + t_row
    gates = jax.nn.sigmoid(proj_ref[0, :, GATE_OFF:IN_WIDTH])
    c_lane = lax.broadcasted_iota(jnp.int32, (1, nch), 1)
    dist_c = pos_row - (c_lane * CMP_STRIDE + (CMP_BLOCK - 1))
    valid_c = (dist_c >= 0) & (c_lane < nch - 1)
    c_col = lax.broadcasted_iota(jnp.int32, (nch, 1), 0)
    j_lane = lax.broadcasted_iota(jnp.int32, (1, TOPK_LANES), 1)
    overlap = jnp.where((c_col * CMP_STRIDE <= j_lane * SEL_BLOCK + (SEL_BLOCK - 1))
                        & (c_col * CMP_STRIDE + (CMP_BLOCK - 1) >= j_lane * SEL_BLOCK), 1.0, 0.0).astype(bf16)
    pos_t = n_past + lax.broadcasted_iota(jnp.int32, (t_new, 1), 0)
    blk_t = pos_t // SEL_BLOCK
    forced = (j_lane == 0) | (j_lane == blk_t) | (j_lane == blk_t - 1)
    j_f = j_lane.astype(f32)
    k_lane = lax.broadcasted_iota(jnp.int32, (1, IDX_LANES), 1)
    i_win = lax.broadcasted_iota(jnp.int32, (1, n_win), 1)
    dist_w = pos_row - (n_past - n_win + i_win)
    keep_w = (dist_w >= 0) & (dist_w < WINDOW)
    j_new = lax.broadcasted_iota(jnp.int32, (1, 8), 1)
    dist_n = t_row - j_new
    keep_n = (dist_n >= 0) & (j_new < t_new)
    zpad = jnp.zeros((8 - t_new, HEAD_DIM), f32)

    for g in range(NSA_KV_HEADS):
        kcol = slice(g * HEAD_DIM, (g + 1) * HEAD_DIM)
        vcol = slice((NSA_KV_HEADS + g) * HEAD_DIM, (NSA_KV_HEADS + g + 1) * HEAD_DIM)
        heads = [g * HPG + h for h in range(HPG)]
        slope_row = jnp.zeros((rows, 1), f32)
        for h, hh in enumerate(heads):
            slope_row = jnp.where(h_row == h, 2.0 ** -(hh + 1), slope_row)
        q = jnp.concatenate([proj_ref[0, :, hh * HEAD_DIM:(hh + 1) * HEAD_DIM] for hh in heads], axis=0).astype(bf16)

        kc = kvc_ref[0, :, kcol].astype(bf16)
        vc = kvc_ref[0, :, vcol].astype(bf16)
        s = lax.dot_general(q, kc, _NT, preferred_element_type=f32) * scale - slope_row * dist_c.astype(f32)
        s = jnp.where(valid_c, s, NEG_INF)
        e = jnp.exp(s - jnp.max(s, axis=-1, keepdims=True))
        p = e / jnp.sum(e, axis=-1, keepdims=True)
        p = jnp.where(valid_c, p, 0.0)
        o_cmp = jnp.dot(p.astype(bf16), vc, preferred_element_type=f32)
        psum = p[0:t_new]
        for h in range(1, HPG):
            psum = psum + p[h * t_new:(h + 1) * t_new]

        p_hi = psum.astype(bf16)
        p_lo = (psum - p_hi.astype(f32)).astype(bf16)
        imp = (jnp.dot(p_hi, overlap, preferred_element_type=f32)
               + jnp.dot(p_lo, overlap, preferred_element_type=f32))
        imp = jnp.where(forced, imp + FORCE_BONUS, imp)
        imp = jnp.where(j_lane <= blk_t, imp, NEG_INF)
        picked = jnp.full((t_new, IDX_LANES), -1.0, f32)
        for k in range(N_SELECT):
            best = jnp.max(imp, axis=-1, keepdims=True)
            first = jnp.min(jnp.where(imp == best, j_f, 1e9), axis=-1, keepdims=True)
            picked = jnp.where(k_lane == k, jnp.where(best > 0.5 * NEG_INF, first, -1.0), picked)
            imp = jnp.where(j_f == first, -3e38, imp)
        idx_ref[0, g * t_new:(g + 1) * t_new, :] = picked.astype(jnp.int32)

        kw = win_ref[0, :, kcol].astype(bf16)
        vw = win_ref[0, :, vcol].astype(bf16)
        kn = jnp.concatenate([proj_ref[0, :, WIN_OFF + g * HEAD_DIM:WIN_OFF + (g + 1) * HEAD_DIM], zpad], axis=0)
        vn = jnp.concatenate([proj_ref[0, :, WIN_OFF + KV_WIDTH + g * HEAD_DIM:
                                       WIN_OFF + KV_WIDTH + (g + 1) * HEAD_DIM], zpad], axis=0)
        s_w = lax.dot_general(q, kw, _NT, preferred_element_type=f32) * scale - slope_row * dist_w.astype(f32)
        s_n = (lax.dot_general(q, kn.astype(bf16), _NT, preferred_element_type=f32) * scale
               - slope_row * dist_n.astype(f32))
        s_w = jnp.where(keep_w, s_w, NEG_INF)
        s_n = jnp.where(keep_n, s_n, NEG_INF)
        m = jnp.maximum(jnp.max(s_w, axis=-1, keepdims=True), jnp.max(s_n, axis=-1, keepdims=True))
        e_w = jnp.exp(s_w - m)
        e_n = jnp.exp(s_n - m)
        den = jnp.sum(e_w, axis=-1, keepdims=True) + jnp.sum(e_n, axis=-1, keepdims=True)
        o_win = (jnp.dot(e_w.astype(bf16), vw, preferred_element_type=f32)
                 + jnp.dot(e_n.astype(bf16), vn.astype(bf16), preferred_element_type=f32)) / den

        for h, hh in enumerate(heads):
            rs = slice(h * t_new, (h + 1) * t_new)
            hs = slice(h * HEAD_DIM, (h + 1) * HEAD_DIM)
            o_ref[0, g, :, hs] = (gates[:, hh:hh + 1] * o_cmp[rs]
                                  + gates[:, 2 * NSA_HEADS + hh:2 * NSA_HEADS + hh + 1] * o_win[rs])
            gsel_ref[0, g, :, hs] = jnp.broadcast_to(gates[:, NSA_HEADS + hh:NSA_HEADS + hh + 1], (t_new, HEAD_DIM))


def _nsa_decode_b_kernel(idx_ref, pt_ref, q_ref, part_ref, gsel_ref, new_ref, *refs, t_new, n_past, n_pages):
    del pt_ref
    f32, bf16 = jnp.float32, jnp.bfloat16
    k_refs, v_refs, o_ref = refs[:N_SELECT], refs[N_SELECT:2 * N_SELECT], refs[2 * N_SELECT]
    b, g, t = pl.program_id(0), pl.program_id(1), pl.program_id(2)
    scale = HEAD_DIM ** -0.5
    n_past_blk = n_past // SEL_BLOCK
    base = ((b * NSA_KV_HEADS + g) * t_new + t) * N_SELECT
    n_keys = N_SELECT * SEL_BLOCK
    lane = lax.broadcasted_iota(jnp.int32, (1, n_keys), 1)
    slot = lane // SEL_BLOCK
    blk_of_lane = jnp.full((1, n_keys), -1, jnp.int32)
    ks, vs = [], []
    for k in range(N_SELECT):
        blk = idx_ref[base + k]
        blk_of_lane = jnp.where(slot == k, blk, blk_of_lane)
        is_new = blk >= n_past_blk
        ks.append(jnp.where(is_new, new_ref[:, 0:HEAD_DIM], k_refs[k][...]).astype(bf16))
        vs.append(jnp.where(is_new, new_ref[:, HEAD_DIM:2 * HEAD_DIM], v_refs[k][...]).astype(bf16))
    k_all = jnp.concatenate(ks, axis=0)
    v_all = jnp.concatenate(vs, axis=0)
    dist = (n_past + t) - (blk_of_lane * SEL_BLOCK + lane % SEL_BLOCK)
    keep = (dist >= 0) & (blk_of_lane >= 0)
    q = jnp.concatenate([q_ref[:, h * HEAD_DIM:(h + 1) * HEAD_DIM] for h in range(HPG)]
                        + [jnp.zeros((8 - HPG, HEAD_DIM), f32)], axis=0).astype(bf16)
    h_row = lax.broadcasted_iota(jnp.int32, (8, 1), 0)
    slope_row = jnp.zeros((8, 1), f32)
    for h in range(HPG):
        slope_row = jnp.where(h_row == h, jnp.where(g == 0, 2.0 ** -(h + 1), 2.0 ** -(HPG + h + 1)), slope_row)
    s = lax.dot_general(q, k_all, _NT, preferred_element_type=f32) * scale - slope_row * dist.astype(f32)
    s = jnp.where(keep, s, NEG_INF)
    e = jnp.exp(s - jnp.max(s, axis=-1, keepdims=True))
    p = e / jnp.sum(e, axis=-1, keepdims=True)
    o_sel = jnp.dot(p.astype(bf16), v_all, preferred_element_type=f32)
    o_row = jnp.concatenate([o_sel[h:h + 1] for h in range(HPG)], axis=1)
    o_ref[...] = part_ref[...] + gsel_ref[...] * o_row


def _nsa_decode(proj3, kvc, win_buf, pool_slc, page_table, n_past):
    bsz, t_new, _ = proj3.shape
    n_pages = page_table.shape[1]
    gw = HPG * HEAD_DIM
    grp = jax.ShapeDtypeStruct((bsz, NSA_KV_HEADS, t_new, gw), jnp.float32)
    part, gsel, idx = pl.pallas_call(
        functools.partial(_nsa_decode_a_kernel, t_new=t_new, n_past=n_past),
        grid=(bsz,),
        in_specs=[pl.BlockSpec((1, t_new, IN_WIDTH), lambda b: (b, 0, 0)),
                  pl.BlockSpec((1,) + kvc.shape[1:], lambda b: (b, 0, 0)),
                  pl.BlockSpec((1,) + win_buf.shape[1:], lambda b: (b, 0, 0))],
        out_specs=[pl.BlockSpec((1, NSA_KV_HEADS, t_new, gw), lambda b: (b, 0, 0, 0)),
                   pl.BlockSpec((1, NSA_KV_HEADS, t_new, gw), lambda b: (b, 0, 0, 0)),
                   pl.BlockSpec((1, NSA_KV_HEADS * t_new, IDX_LANES), lambda b: (b, 0, 0))],
        out_shape=[grp, grp, jax.ShapeDtypeStruct((bsz, NSA_KV_HEADS * t_new, IDX_LANES), jnp.int32)],
        compiler_params=pltpu.CompilerParams(dimension_semantics=("arbitrary",), vmem_limit_bytes=VMEM_LIMIT_BYTES),
        name="nsa_decode_a",
    )(proj3, kvc, win_buf)

    n_rows = bsz * NSA_KV_HEADS * t_new
    n_past_blk = n_past // SEL_BLOCK
    sub = PAGE_SIZE // SEL_BLOCK
    pool2 = pool_slc.reshape(-1, SEL_BLOCK, 2 * KV_WIDTH)
    new_rows = jnp.pad(proj3[:, :, SLC_OFF:WIN_OFF], ((0, 0), (0, SEL_BLOCK - t_new), (0, 0)))
    new_rows = new_rows.reshape(bsz, SEL_BLOCK, 2, NSA_KV_HEADS, HEAD_DIM).transpose(0, 3, 1, 2, 4)
    new_rows = new_rows.reshape(bsz * NSA_KV_HEADS, SEL_BLOCK, 2 * HEAD_DIM)

    def row_map(b, g, t, idx, pt):
        return (b * NSA_KV_HEADS + g) * t_new + t

    def pool_spec(k, is_v):
        def index(b, g, t, idx, pt):
            past = jnp.clip(idx[row_map(b, g, t, idx, pt) * N_SELECT + k], 0, n_past_blk - 1)
            return (pt[b * n_pages + past // sub] * sub + past % sub, 0, is_v * NSA_KV_HEADS + g)
        return pl.BlockSpec((None, SEL_BLOCK, HEAD_DIM), index)

    row_spec = pl.BlockSpec((None, 1, gw), lambda b, g, t, idx, pt: (row_map(b, g, t, idx, pt), 0, 0))
    out = pl.pallas_call(
        functools.partial(_nsa_decode_b_kernel, t_new=t_new, n_past=n_past, n_pages=n_pages),
        grid_spec=pltpu.PrefetchScalarGridSpec(
            num_scalar_prefetch=2, grid=(bsz, NSA_KV_HEADS, t_new),
            in_specs=[pl.BlockSpec((None, 1, gw), lambda b, g, t, idx, pt: (b * t_new + t, 0, g)),
                      row_spec, row_spec,
                      pl.BlockSpec((None, SEL_BLOCK, 2 * HEAD_DIM), lambda b, g, t, idx, pt: (b * NSA_KV_HEADS + g, 0, 0))]
            + [pool_spec(k, 0) for k in range(N_SELECT)] + [pool_spec(k, 1) for k in range(N_SELECT)],
            out_specs=row_spec),
        out_shape=jax.ShapeDtypeStruct((n_rows, 1, gw), jnp.float32),
        compiler_params=pltpu.CompilerParams(
            dimension_semantics=("arbitrary", "arbitrary", "arbitrary"), vmem_limit_bytes=VMEM_LIMIT_BYTES),
        name="nsa_decode_b",
    )(idx[:, :, :N_SELECT].reshape(-1), page_table.reshape(-1),
      proj3.reshape(bsz * t_new, 1, IN_WIDTH), part.reshape(n_rows, 1, gw), gsel.reshape(n_rows, 1, gw),
      new_rows, *([pool2] * (2 * N_SELECT)))
    return out.reshape(bsz, NSA_KV_HEADS, t_new, gw).transpose(0, 2, 1, 3).reshape(bsz, t_new, NSA_WIDTH)


WKV_LANES = 2 * RWKV_HEAD_DIM
WKV_PAIRS = RWKV_HEADS // 2
WKV_STACK = 4
WKV_BB = 4
WKV_CHUNK = 128
WKV_MIN_CHUNK = 16
WKV_FLUSH = RWKV_HEAD_DIM


def _wkv_kernel(r_ref, w_ref, k_ref, v_ref, kk_ref, kka_ref, s0_ref, y_ref, st_ref,
                s_scr, y_scr, *, tc):
    f32, bf16 = jnp.float32, jnp.bfloat16
    ti = pl.program_id(1)
    hd = RWKV_HEAD_DIM
    n_tiles = WKV_BB * WKV_PAIRS
    n_stacks = n_tiles // WKV_STACK
    tile = lambda q: (q // WKV_PAIRS, q % WKV_PAIRS)

    @pl.when(ti == 0)
    def _():
        for q in range(n_tiles):
            b, p = tile(q)
            s_scr[q] = jnp.concatenate([s0_ref[b, 2 * p], s0_ref[b, 2 * p + 1]], axis=1)

    lane = lax.broadcasted_iota(jnp.int32, (1, WKV_LANES), 1)
    r2 = lax.broadcasted_iota(jnp.int32, (2 * WKV_LANES, 1), 0)
    c2 = lax.broadcasted_iota(jnp.int32, (1, 2 * WKV_LANES), 1)
    same_head2 = jnp.where(r2 // hd == c2 // hd, 1.0, 0.0).astype(bf16)
    on_diag = lax.broadcasted_iota(jnp.int32, (hd, 1), 0) == lane % hd
    n_flush = min(tc, WKV_FLUSH)
    y_scr[...] = jnp.zeros(y_scr.shape, f32)
    pairs = [(2 * i, 2 * i + 1) for i in range(n_stacks // 2)]
    stack_tiles = lambda st: range(st * WKV_STACK, (st + 1) * WKV_STACK)

    def row_sums(per_tile):
        out = [None] * n_tiles
        for s0, s1 in pairs:
            lhs = jnp.concatenate([jnp.concatenate([per_tile[q] for q in stack_tiles(st)], axis=0)
                                   for st in (s0, s1)], axis=1)
            res = jnp.dot(lhs.astype(bf16), same_head2, preferred_element_type=f32)
            for half, st in enumerate((s0, s1)):
                for n, q in enumerate(stack_tiles(st)):
                    out[q] = res[n * hd:(n + 1) * hd, half * WKV_LANES:(half + 1) * WKV_LANES]
        return out

    def step(t, carry):
        here = (lane % hd) == (t % n_flush)
        get = lambda ref, q: ref[tile(q)[0], tile(q)[1], pl.ds(t, 1), :]
        s_old = [s_scr[q] for q in range(n_tiles)]
        sa = row_sums([s_old[q] * get(kk_ref, q) for q in range(n_tiles)])
        v_col = row_sums([jnp.where(on_diag, get(v_ref, q), 0.0) for q in range(n_tiles)])
        s_new = []
        for q in range(n_tiles):
            s = s_old[q] * get(w_ref, q) - sa[q] * get(kka_ref, q) + v_col[q] * get(k_ref, q)
            s_scr[q] = s
            s_new.append(s)
        y_col = row_sums([s_new[q] * get(r_ref, q) for q in range(n_tiles)])
        for q in range(n_tiles):
            y_scr[q] = jnp.where(here, y_col[q], y_scr[q])
        return carry

    for sub in range(tc // n_flush):
        lax.fori_loop(sub * n_flush, (sub + 1) * n_flush, step, 0)
        for q in range(n_tiles):
            b, p = tile(q)
            yt = y_scr[q].T
            y_ref[b, p, sub * n_flush:(sub + 1) * n_flush, :] = jnp.concatenate(
                [yt[:n_flush], yt[hd:hd + n_flush]], axis=1)

    @pl.when(ti == pl.num_programs(1) - 1)
    def _():
        for q in range(n_tiles):
            b, p = tile(q)
            st_ref[b, 2 * p] = s_scr[q][:, :hd]
            st_ref[b, 2 * p + 1] = s_scr[q][:, hd:]


def _wkv_scan(r, w, k, v, kk, kka, s0, tc):
    bsz, n_pairs, seq, _ = r.shape
    assert n_pairs == WKV_PAIRS and bsz % WKV_BB == 0 and seq % tc == 0
    n_tiles = WKV_BB * WKV_PAIRS
    x_spec = pl.BlockSpec((WKV_BB, WKV_PAIRS, tc, WKV_LANES), lambda b, i: (b, 0, i, 0))
    s_spec = pl.BlockSpec((WKV_BB, RWKV_HEADS, RWKV_HEAD_DIM, RWKV_HEAD_DIM), lambda b, i: (b, 0, 0, 0))
    return pl.pallas_call(
        functools.partial(_wkv_kernel, tc=tc),
        grid=(bsz // WKV_BB, seq // tc),
        in_specs=[x_spec] * 6 + [s_spec],
        out_specs=[x_spec, s_spec],
        out_shape=[jax.ShapeDtypeStruct(r.shape, jnp.float32),
                   jax.ShapeDtypeStruct(s0.shape, jnp.float32)],
        scratch_shapes=[pltpu.VMEM((n_tiles, RWKV_HEAD_DIM, WKV_LANES), jnp.float32),
                        pltpu.VMEM((n_tiles, RWKV_HEAD_DIM, WKV_LANES), jnp.float32)],
        compiler_params=pltpu.CompilerParams(
            dimension_semantics=("arbitrary", "arbitrary"), vmem_limit_bytes=VMEM_LIMIT_BYTES),
        name="wkv_scan",
    )(r, w, k, v, kk, kka, s0)


def _rmsnorm(x, g):
    xf = x.astype(jnp.float32)
    y = xf * lax.rsqrt(jnp.mean(xf * xf, axis=-1, keepdims=True) + NORM_EPS)
    return (y * g.astype(jnp.float32)).astype(x.dtype)


def _alibi_slopes():
    n = jnp.arange(1, NSA_HEADS + 1, dtype=jnp.float32)
    return jnp.exp2(-8.0 * n / NSA_HEADS).reshape(NSA_KV_HEADS, HPG)


def _compress_blocks(rows, w1, pe, w2):
    B, L = rows.shape[:2]
    n_chunk = L // CMP_STRIDE
    r = CMP_BLOCK // CMP_STRIDE
    nc = n_chunk - r + 1
    ch = rows.reshape(B, n_chunk, CMP_STRIDE, NSA_KV_HEADS, HEAD_DIM)
    pre = jnp.einsum('pd,pde->e', pe, w1)
    for i in range(r):
        pre = pre + jnp.einsum('bcpgd,pde->bcge', ch[:, i:i + nc], w1[i * CMP_STRIDE:(i + 1) * CMP_STRIDE])
    return jnp.einsum('bcge,ef->bcgf', jax.nn.silu(pre), w2)


def _overlap_matrix(nc, ns):
    c_start = jnp.arange(nc) * CMP_STRIDE
    c_end = c_start + CMP_BLOCK - 1
    s_start = jnp.arange(ns) * SEL_BLOCK
    s_end = s_start + SEL_BLOCK - 1
    return ((c_start[:, None] <= s_end[None]) & (c_end[:, None] >= s_start[None])).astype(jnp.float32)


def _compressed_branch(qg, pos_q, kc, vc, n_sel, slopes):
    nc = kc.shape[1]
    end_c = jnp.arange(nc) * CMP_STRIDE + CMP_BLOCK - 1
    dist = pos_q[:, None] - end_c[None, :]
    valid = (dist >= 0)[None, :, None, None, :]
    s = jnp.einsum('btghd,bcgd->btghc', qg, kc).astype(jnp.float32) * HEAD_DIM ** -0.5
    s = s - slopes[None, None, :, :, None] * dist.astype(jnp.float32)[None, :, None, None, :]
    p = jax.nn.softmax(jnp.where(valid, s, NEG_INF), axis=-1) * valid
    o = jnp.einsum('btghc,bcgd->btghd', p.astype(vc.dtype), vc)
    imp = jnp.einsum('btghc,cj->btgj', p, _overlap_matrix(nc, n_sel))
    blk_t = (pos_q // SEL_BLOCK)[:, None]
    j = jnp.arange(n_sel)[None, :]
    forced = (j == 0) | (j == blk_t) | (j == blk_t - 1)
    imp = jnp.where(forced[None, :, None, :], imp + FORCE_BONUS, imp)
    imp = jnp.where((j <= blk_t)[None, :, None, :], imp, NEG_INF)
    top_val, top_idx = lax.top_k(imp, min(N_SELECT, n_sel))
    return o, top_idx, top_val > 0.5 * NEG_INF


def _gather_selected(idx, new_blk, n_past_blk, pool, layer, page_table):
    b_idx = jnp.arange(idx.shape[0])[:, None, None, None]
    g_idx = jnp.arange(NSA_KV_HEADS)[None, None, :, None]
    local = jnp.clip(idx - n_past_blk, 0, new_blk.shape[1] - 1)
    blk = new_blk[b_idx, local, :, :, g_idx]
    if pool is not None:
        sub_per_page = PAGE_SIZE // SEL_BLOCK
        past = jnp.clip(idx, 0, n_past_blk - 1)
        phys = page_table[b_idx, past // sub_per_page]
        pool_r = pool.reshape(pool.shape[0], pool.shape[1], sub_per_page, SEL_BLOCK, 2, NSA_KV_HEADS, HEAD_DIM)
        l_idx = jnp.full_like(phys, layer)
        pblk = pool_r[l_idx, phys, past % sub_per_page, :, :, g_idx]
        blk = jnp.where((idx < n_past_blk)[..., None, None, None], pblk, blk)
    return blk[..., 0, :], blk[..., 1, :]


def _selected_branch(qg, pos_q, kb, vb, idx, sel_valid, slopes):
    B, Tq, G, H, _ = qg.shape
    s = jnp.einsum('btghd,btgkpd->btghkp', qg, kb).astype(jnp.float32) * HEAD_DIM ** -0.5
    key_pos = idx[..., None] * SEL_BLOCK + jnp.arange(SEL_BLOCK)
    dist = pos_q[None, :, None, None, None] - key_pos
    mask = ((dist >= 0) & sel_valid[..., None])[:, :, :, None]
    s = s - slopes[None, None, :, :, None, None] * dist.astype(jnp.float32)[:, :, :, None]
    s = jnp.where(mask, s, NEG_INF)
    p = jax.nn.softmax(s.reshape(B, Tq, G, H, -1), axis=-1).reshape(s.shape)
    return jnp.einsum('btghkp,btgkpd->btghd', p.astype(vb.dtype), vb)


def _window_attend(qg, pos_q, k, v, key_pos, slopes):
    s = jnp.einsum('btghd,bsgd->btghs', qg, k).astype(jnp.float32) * HEAD_DIM ** -0.5
    dist = pos_q[:, None] - key_pos[None, :]
    mask = ((dist >= 0) & (dist < WINDOW) & (key_pos[None, :] >= 0))[None, :, None, None, :]
    s = s - slopes[None, None, :, :, None] * dist.astype(jnp.float32)[None, :, None, None, :]
    p = jax.nn.softmax(jnp.where(mask, s, NEG_INF), axis=-1)
    return jnp.einsum('btghs,bsgd->btghd', p.astype(v.dtype), v)


def _nsa_group(proj, pos_q, lp, layer, pool_cmp, pool_slc, page_table, win_buf, past_len):
    B, T = proj.shape[:2]
    G, dh = NSA_KV_HEADS, HEAD_DIM
    slopes = _alibi_slopes()
    qg = proj[..., Q_OFF:CMP_OFF].reshape(B, T, G, HPG, dh)
    cmp_new = proj[..., CMP_OFF:SLC_OFF].reshape(B, T, 2, G, dh)
    slc_new = proj[..., SLC_OFF:WIN_OFF].reshape(B, T, 2, G, dh)
    win_new = proj[..., WIN_OFF:RKV_OFF].reshape(B, T, 2, G, dh)
    gates = jax.nn.sigmoid(proj[..., GATE_OFF:IN_WIDTH].astype(jnp.float32)).reshape(B, T, 3, G, HPG, 1)

    if pool_cmp is None:
        proj2d = proj.reshape(B * T, IN_WIDTH)
        kvc = _compress_prompt(proj2d, B, T, _cmp_weights(lp))
        o = _nsa_prompt(proj2d, kvc, B, T).reshape(B, T, NSA_WIDTH)
        pad_rows = jnp.zeros((B, min(WINDOW, past_len), 2, G, dh), win_new.dtype)
        win_state = jnp.concatenate([pad_rows, win_new], axis=1)[:, -min(WINDOW, past_len):]
        return o, cmp_new, slc_new, win_state

    n_past = page_table.shape[1] * PAGE_SIZE
    assert n_past % CMP_STRIDE == 0 and T < CMP_STRIDE and T <= 8
    assert -(-(n_past + T) // SEL_BLOCK) <= TOPK_LANES and n_past % SEL_BLOCK == 0 and T <= SEL_BLOCK
    assert win_buf.shape[1] == WINDOW
    kvw = 2 * KV_WIDTH
    pages = page_table + layer * pool_cmp.shape[1]
    kvc = _compress_paged(pool_cmp.reshape(-1, PAGE_SIZE, kvw), pages, _cmp_weights(lp))
    o = _nsa_decode(proj, kvc, win_buf.reshape(B, WINDOW, kvw), pool_slc, pages, n_past)
    win_state = jnp.concatenate([win_buf, win_new], axis=1)[:, -win_buf.shape[1]:]
    return o, cmp_new, slc_new, win_state


def _rwkv_group(h, h_shift, p_rkv, p_rkv_prev, wkv0, lp):
    B, T, _ = h.shape
    f32 = jnp.float32
    rkv = (p_rkv + lp['mu_rkv'] * (p_rkv_prev - p_rkv)).astype(f32)
    r, k, v = jnp.split(rkv, 3, axis=-1)
    xx = h_shift - h
    xw = h + xx * lp['mu_wag'][0]
    xa = h + xx * lp['mu_wag'][1]
    xg = h + xx * lp['mu_wag'][2]
    w_log = -jax.nn.softplus(-(lp['decay_w0'] + jnp.tanh(xw @ lp['decay_w1']) @ lp['decay_w2']).astype(f32)) - 0.5
    decay = jnp.exp(-jnp.exp(w_log))
    a = jax.nn.sigmoid((lp['iclr_a0'] + (xa @ lp['iclr_a1']) @ lp['iclr_a2']).astype(f32))
    g = (jax.nn.sigmoid(xg @ lp['gate_g1']) @ lp['gate_g2']).astype(f32)

    def heads(t):
        return t.reshape(B, T, RWKV_HEADS, RWKV_HEAD_DIM)

    kk = heads(k * lp['k_k'])
    kk = kk / jnp.maximum(jnp.sqrt(jnp.sum(kk * kk, axis=-1, keepdims=True)), 1e-12)
    k = k * (1.0 + (a - 1.0) * lp['k_a'])
    r_h, k_h, v_h, w_h, a_h = heads(r), heads(k), heads(v), heads(decay), heads(a)

    t_pad = -(-T // WKV_MIN_CHUNK) * WKV_MIN_CHUNK
    tc = WKV_CHUNK if t_pad % WKV_CHUNK == 0 else WKV_MIN_CHUNK

    def pairs(t, fill):
        t = t.reshape(B, T, RWKV_HEADS // 2, WKV_LANES)
        if t_pad != T:
            t = jnp.pad(t, ((0, 0), (0, t_pad - T), (0, 0), (0, 0)), constant_values=fill)
        return t.transpose(0, 2, 1, 3)

    y, S_T = _wkv_scan(pairs(r, 0.0), pairs(decay, 1.0), pairs(k, 0.0), pairs(v, 0.0),
                       pairs(kk, 0.0), pairs(kk * a_h, 0.0), wkv0.astype(f32), tc)
    y = y.transpose(0, 2, 1, 3)[:, :T].reshape(B, T, RWKV_HEADS, RWKV_HEAD_DIM)
    mu = jnp.mean(y, axis=-1, keepdims=True)
    var = jnp.mean(jnp.square(y - mu), axis=-1, keepdims=True)
    y = ((y - mu) * lax.rsqrt(var + GN_EPS)).reshape(B, T, RWKV_WIDTH) * lp['ln_x_w'] + lp['ln_x_b']
    bonus = (jnp.sum(r_h * k_h * lp['r_k'], axis=-1, keepdims=True) * v_h).reshape(B, T, RWKV_WIDTH)
    return (y + bonus) * g, S_T


def _hier_route(logits):
    n = logits.shape[0]
    pg = jax.nn.softmax(logits[:, :N_GROUPS], axis=-1)
    g_val, g_sel = lax.top_k(pg, 1)
    le = logits[:, N_GROUPS:N_GROUPS + N_EXPERTS].reshape(n, N_GROUPS, EXPERTS_PER_GROUP)
    le_g = jnp.take_along_axis(le, g_sel[:, :, None], axis=1)[:, 0]
    e_val, e_sel = lax.top_k(le_g, EXPERT_TOP_K)
    weights = jax.nn.softmax(e_val, axis=-1) * g_val
    return g_sel * EXPERTS_PER_GROUP + e_sel, weights


def _moe_ffn(h_pad, eid, w_gate, w_up, w_down):
    n, d = h_pad.shape[0] - 1, h_pad.shape[1]
    a_tot = n * EXPERT_TOP_K
    flat_e = eid.reshape(-1)
    onehot = (flat_e[:, None] == jnp.arange(N_EXPERTS)[None, :]).astype(jnp.int32)
    csum = jnp.cumsum(onehot, axis=0)
    rank = jnp.take_along_axis(csum, flat_e[:, None], axis=1)[:, 0] - 1
    counts = csum[-1]
    padded = (counts + MOE_TM - 1) // MOE_TM * MOE_TM
    pad_end = jnp.cumsum(padded)
    dest = (pad_end - padded)[flat_e] + rank
    n_blk = (a_tot + N_EXPERTS * (MOE_TM - 1)) // MOE_TM
    tok_buf = jnp.full((n_blk * MOE_TM,), n, jnp.int32).at[dest].set(jnp.arange(a_tot, dtype=jnp.int32) // EXPERT_TOP_K)
    xb = h_pad[tok_buf].reshape(n_blk, MOE_TM, d)
    blk_e = jnp.clip(jnp.searchsorted(pad_end, jnp.arange(n_blk) * MOE_TM, side='right'), 0, N_EXPERTS - 1)
    meta = jnp.concatenate([blk_e, pad_end[-1:] // MOE_TM]).astype(jnp.int32)
    yb = _moe_blocks(xb, meta, w_gate, w_up, w_down).reshape(n_blk * MOE_TM, d)
    return yb[dest].reshape(n, EXPERT_TOP_K * d)


def _layer_front(x, mod, pos_q, lp, layer, pool_cmp, pool_slc, page_table, win_buf, wkv0, shift0, past_len, rows):
    B, T, D = x.shape
    groups = B * T // rows
    per_token = rows > T
    mods = jnp.repeat(mod, T, axis=0).reshape(groups, rows, 6 * D) if per_token else mod[:, None, :]
    sh1, sc1, gt1, sh2, sc2, gt2 = jnp.split(mods, 6, axis=-1)
    xg = x.reshape(groups, rows, D)
    tm_in = min(rows, 512)
    proj, h = _norm_in(xg, lp['norm1'], sc1, sh1, lp['w_in_b'], tm_in)
    proj = proj.reshape(B, T, IN_WIDTH)
    h = h.reshape(B, T, D)
    shift0 = shift0.astype(h.dtype)
    h_shift = jnp.concatenate([shift0[:, None], h[:, :-1]], axis=1)
    p_rkv = proj[..., RKV_OFF:GATE_OFF]
    p_first = _matmul(shift0, lp['w_in_b'][:, RKV_OFF:GATE_OFF], B, 1024)
    p_rkv_prev = jnp.concatenate([p_first[:, None], p_rkv[:, :-1]], axis=1)
    o_nsa, cmp_new, slc_new, win_state = _nsa_group(proj, pos_q, lp, layer, pool_cmp, pool_slc, page_table,
                                                    win_buf, past_len)
    o_rwkv, wkv_T = _rwkv_group(h, h_shift, p_rkv, p_rkv_prev, wkv0, lp)
    x1, h2, logits = _mix_out(o_nsa.reshape(groups, rows, NSA_WIDTH), o_rwkv.reshape(groups, rows, RWKV_WIDTH),
                              lp['w_out_b'], xg, gt1, sc2, sh2, lp['norm2'], lp['wr_hi'], lp['wr_lo'], lp['br'],
                              min(rows, 256))
    return (x1, h2, logits, gt2), (cmp_new, slc_new, win_state, wkv_T, h[:, -1])


def kernel(x_prompt, x_sample, c_prompt, c_sample, cache_cmp_kv, cache_slc_kv, page_table, state_win_kv, state_wkv, state_shift, w_ada, b_ada, norm1, w_in, cmp_k_w1, cmp_k_pe, cmp_k_w2, cmp_v_w1, cmp_v_pe, cmp_v_w2, mu_rkv, mu_wag, decay_w0, decay_w1, decay_w2, iclr_a0, iclr_a1, iclr_a2, gate_g1, gate_g2, k_k, k_a, r_k, ln_x_w, ln_x_b, w_out, norm2, w_router_group, b_router_group, w_router_expert, b_router_expert, w_gate, w_up, w_down, norm_f):
    bp, tp = x_prompt.shape[:2]
    ts = x_sample.shape[1]
    past_len = page_table.shape[1] * PAGE_SIZE
    pos_p = jnp.arange(tp)
    pos_s = past_len + jnp.arange(ts)
    assert DEPTH == 1 and w_in.shape[0] == 1
    l = 0
    bs = x_sample.shape[0]
    f32, bf16 = jnp.float32, jnp.bfloat16
    lyr = lambda a: a.reshape(a.shape[1:])
    wr = jnp.concatenate([lyr(w_router_group), lyr(w_router_expert),
                          jnp.zeros((D_MODEL, ROUTER_LANES - N_GROUPS - N_EXPERTS), f32)], axis=1)
    wr_hi = wr.astype(bf16)
    br = jnp.concatenate([lyr(b_router_group), lyr(b_router_expert),
                          jnp.zeros((ROUTER_LANES - N_GROUPS - N_EXPERTS,), f32)]).reshape(1, ROUTER_LANES)
    lp = dict(norm1=lyr(norm1), w_in=lyr(w_in), w_in_b=lyr(w_in).astype(bf16),
              cmp_k_w1=lyr(cmp_k_w1), cmp_k_pe=lyr(cmp_k_pe), cmp_k_w2=lyr(cmp_k_w2),
              cmp_v_w1=lyr(cmp_v_w1), cmp_v_pe=lyr(cmp_v_pe), cmp_v_w2=lyr(cmp_v_w2),
              mu_rkv=lyr(mu_rkv), mu_wag=lyr(mu_wag), decay_w0=lyr(decay_w0), decay_w1=lyr(decay_w1),
              decay_w2=lyr(decay_w2), iclr_a0=lyr(iclr_a0), iclr_a1=lyr(iclr_a1), iclr_a2=lyr(iclr_a2),
              gate_g1=lyr(gate_g1), gate_g2=lyr(gate_g2), k_k=lyr(k_k), k_a=lyr(k_a), r_k=lyr(r_k),
              ln_x_w=lyr(ln_x_w), ln_x_b=lyr(ln_x_b), w_out_b=lyr(w_out).astype(bf16), norm2=lyr(norm2),
              wr_hi=wr_hi, wr_lo=(wr - wr_hi.astype(f32)).astype(bf16), br=br)

    c_all = jnp.concatenate([c_prompt, c_sample], axis=0)
    mod_all = _matmul(jax.nn.silu(c_all), lyr(w_ada), c_all.shape[0], 1024) + lyr(b_ada)
    mod_p, mod_s = mod_all[:bp], mod_all[bp:]

    wkv_zero = jnp.zeros((bp, RWKV_HEADS, RWKV_HEAD_DIM, RWKV_HEAD_DIM), f32)
    shift_zero = jnp.zeros((bp, D_MODEL), x_prompt.dtype)
    (x1p, h2p, lgp, gt2p), (a1, a2, a3, a4, a5) = _layer_front(
        x_prompt, mod_p, pos_p, lp, l, None, None, None, None, wkv_zero, shift_zero, past_len, tp)
    (x1s, h2s, lgs, gt2s), (b1, b2, b3, b4, b5) = _layer_front(
        x_sample, mod_s, pos_s, lp, l, cache_cmp_kv, cache_slc_kv, page_table, lyr(state_win_kv), lyr(state_wkv),
        lyr(state_shift), past_len, bs * ts)

    n_p, n_s = bp * tp, bs * ts
    h2 = jnp.concatenate([h2p.reshape(n_p, D_MODEL), h2s.reshape(n_s, D_MODEL), jnp.zeros((1, D_MODEL), bf16)], axis=0)
    eid, ew = _hier_route(jnp.concatenate([lgp.reshape(n_p, ROUTER_LANES), lgs.reshape(n_s, ROUTER_LANES)], axis=0))
    y2 = _moe_ffn(h2, eid, lyr(w_gate), lyr(w_up), lyr(w_down))
    y_prompt = _final(x1p, y2[:n_p].reshape(x1p.shape[:2] + (-1,)), ew[:n_p].reshape(x1p.shape[:2] + (-1,)),
                      gt2p, norm_f, 256).reshape(x_prompt.shape)
    y_sample = _final(x1s, y2[n_p:].reshape(x1s.shape[:2] + (-1,)), ew[n_p:].reshape(x1s.shape[:2] + (-1,)),
                      gt2s, norm_f, x1s.shape[1]).reshape(x_sample.shape)
    st = lambda a: a[None]
    return (y_prompt, y_sample, st(a1), st(b1), st(a2), st(b2), st(a3), st(b3), st(a4), st(b4), st(a5), st(b5))
```

```python
import functools

import jax
import jax.numpy as jnp
from jax import lax
from jax.experimental import pallas as pl
from jax.experimental.pallas import tpu as pltpu

D_MODEL = 2048
DEPTH = 1
PAGE_SIZE = 128
HEAD_DIM = 128
NSA_WIDTH = D_MODEL // 2
NSA_HEADS = NSA_WIDTH // HEAD_DIM
NSA_KV_HEADS = 2
HPG = NSA_HEADS // NSA_KV_HEADS
KV_WIDTH = NSA_KV_HEADS * HEAD_DIM
CMP_BLOCK = 32
CMP_STRIDE = 16
SEL_BLOCK = 64
N_SELECT = 16
WINDOW = 512
FORCE_BONUS = 1e4
SEL_QBLOCK = 32
WIN_QBLOCK = 128
RWKV_WIDTH = D_MODEL - NSA_WIDTH
RWKV_HEAD_DIM = 64
RWKV_HEADS = RWKV_WIDTH // RWKV_HEAD_DIM
GN_EPS = 64e-5
N_GROUPS = 4
EXPERTS_PER_GROUP = 8
N_EXPERTS = N_GROUPS * EXPERTS_PER_GROUP
EXPERT_TOP_K = 2
D_EXPERT = 512
MOE_BLOCK = 64
NORM_EPS = 1e-6
NEG_INF = -1e30
W_Q_OFF = 0
W_CMP_OFF = W_Q_OFF + NSA_WIDTH
W_RKV_OFF = W_CMP_OFF + 6 * KV_WIDTH
W_GATE_OFF = W_RKV_OFF + 3 * RWKV_WIDTH
IN_WIDTH = W_GATE_OFF + 3 * NSA_HEADS
Q_OFF = 0
RKV_OFF = Q_OFF + NSA_WIDTH
CMP_OFF = RKV_OFF + 3 * RWKV_WIDTH
SLC_OFF = CMP_OFF + 2 * KV_WIDTH
WIN_OFF = SLC_OFF + 2 * KV_WIDTH
GATE_OFF = WIN_OFF + 2 * KV_WIDTH
assert GATE_OFF == W_GATE_OFF


def _permute_w_in(w):
    return jnp.concatenate([w[:, W_Q_OFF:W_CMP_OFF], w[:, W_RKV_OFF:W_GATE_OFF], w[:, W_CMP_OFF:W_RKV_OFF],
                            w[:, W_GATE_OFF:]], axis=1)

VMEM_LIMIT_BYTES = 48 * 1024 * 1024


def _mm_kernel(x_ref, w_ref, o_ref):
    o_ref[...] = jnp.dot(x_ref[...].astype(jnp.bfloat16), w_ref[...].astype(jnp.bfloat16),
                         preferred_element_type=jnp.float32)


def _matmul(x, w, tm, tn):
    m, k = x.shape
    n = w.shape[1]
    return pl.pallas_call(
        _mm_kernel,
        grid=(pl.cdiv(m, tm), pl.cdiv(n, tn)),
        in_specs=[pl.BlockSpec((tm, k), lambda i, j: (i, 0)),
                  pl.BlockSpec((k, tn), lambda i, j: (0, j))],
        out_specs=pl.BlockSpec((tm, tn), lambda i, j: (i, j)),
        out_shape=jax.ShapeDtypeStruct((m, n), jnp.float32),
        compiler_params=pltpu.CompilerParams(
            dimension_semantics=("arbitrary", "arbitrary"), vmem_limit_bytes=VMEM_LIMIT_BYTES),
        name="matmul",
    )(x, w)


MOE_TM = 256


def _moe_block_kernel(meta_ref, x_ref, wg_ref, wu_ref, wd_ref, o_ref, wg_b, wu_b, wd_b):
    bf16 = jnp.bfloat16
    i = pl.program_id(0)
    used = i < meta_ref[pl.num_programs(0)]
    new_expert = (i == 0) | (meta_ref[i] != meta_ref[jnp.maximum(i - 1, 0)])

    @pl.when(used & new_expert)
    def _():
        wg_b[...] = wg_ref[0].astype(bf16)
        wu_b[...] = wu_ref[0].astype(bf16)
        wd_b[...] = wd_ref[0].astype(bf16)

    @pl.when(used)
    def _():
        x = x_ref[0]
        g = jnp.dot(x, wg_b[...], preferred_element_type=jnp.float32)
        u = jnp.dot(x, wu_b[...], preferred_element_type=jnp.float32)
        hmid = (g * jax.nn.sigmoid(g)) * u
        o_ref[0] = jnp.dot(hmid.astype(bf16), wd_b[...], preferred_element_type=jnp.float32)

    @pl.when(jnp.logical_not(used))
    def _():
        o_ref[...] = jnp.zeros(o_ref.shape, o_ref.dtype)


def _moe_blocks(xb, meta, w_gate, w_up, w_down):
    n_blk, mb, d = xb.shape
    de = w_gate.shape[2]
    grid_spec = pltpu.PrefetchScalarGridSpec(
        num_scalar_prefetch=1,
        grid=(n_blk,),
        in_specs=[pl.BlockSpec((1, mb, d), lambda i, e: (i, 0, 0)),
                  pl.BlockSpec((1, d, de), lambda i, e: (e[i], 0, 0)),
                  pl.BlockSpec((1, d, de), lambda i, e: (e[i], 0, 0)),
                  pl.BlockSpec((1, de, d), lambda i, e: (e[i], 0, 0))],
        out_specs=pl.BlockSpec((1, mb, d), lambda i, e: (i, 0, 0)),
        scratch_shapes=[pltpu.VMEM((d, de), jnp.bfloat16), pltpu.VMEM((d, de), jnp.bfloat16),
                        pltpu.VMEM((de, d), jnp.bfloat16)],
    )
    return pl.pallas_call(
        _moe_block_kernel,
        grid_spec=grid_spec,
        out_shape=jax.ShapeDtypeStruct((n_blk, mb, d), jnp.float32),
        compiler_params=pltpu.CompilerParams(
            dimension_semantics=("arbitrary",), vmem_limit_bytes=VMEM_LIMIT_BYTES),
        name="moe_blocks",
    )(meta, xb, w_gate, w_up, w_down)


ROUTER_LANES = 128


KV_TILES = 3
KV_PARTS = 2 * NSA_KV_HEADS


def _norm_in_kernel(x_ref, n1_ref, sc_ref, sh_ref, w_ref, proj_ref, h_ref, kv_ref, hb_scr, *, kv_tile0):
    j = pl.program_id(2)

    @pl.when(j == 0)
    def _():
        x = x_ref[0]
        y = x * lax.rsqrt(jnp.mean(x * x, axis=-1, keepdims=True) + NORM_EPS) * n1_ref[...]
        h = y * (1.0 + sc_ref[0]) + sh_ref[0]
        h_ref[0] = h
        hb_scr[...] = h.astype(jnp.bfloat16)

    acc = jnp.dot(hb_scr[...], w_ref[...], preferred_element_type=jnp.float32)
    proj_ref[0] = acc

    @pl.when((j >= kv_tile0) & (j < kv_tile0 + KV_TILES))
    def _():
        tm = acc.shape[0]
        for part in range(KV_PARTS):
            kv_ref[0, 0, pl.ds(part, tm, stride=KV_PARTS), :] = acc[:, part * HEAD_DIM:(part + 1) * HEAD_DIM]


def _norm_in(x, norm1, sc, sh, w_in_b, tm):
    g, t, d = x.shape
    n = w_in_b.shape[1]
    tn = 2 * KV_WIDTH
    kv_tile0 = CMP_OFF // tn
    assert CMP_OFF % tn == 0 and GATE_OFF == CMP_OFF + KV_TILES * tn
    mrows = sc.shape[1]
    mod_spec = pl.BlockSpec((1, mrows if mrows == 1 else tm, d),
                            (lambda b, i, j: (b, 0, 0)) if mrows == 1 else (lambda b, i, j: (b, i, 0)))
    return pl.pallas_call(
        functools.partial(_norm_in_kernel, kv_tile0=kv_tile0),
        grid=(g, t // tm, pl.cdiv(n, tn)),
        in_specs=[pl.BlockSpec((1, tm, d), lambda b, i, j: (b, i, 0)),
                  pl.BlockSpec((1, d), lambda b, i, j: (0, 0)),
                  mod_spec, mod_spec,
                  pl.BlockSpec((d, tn), lambda b, i, j: (0, j))],
        out_specs=[pl.BlockSpec((1, tm, tn), lambda b, i, j: (b, i, j)),
                   pl.BlockSpec((1, tm, d), lambda b, i, j: (b, i, 0)),
                   pl.BlockSpec((1, 1, tm * KV_PARTS, HEAD_DIM),
                                lambda b, i, j: (jnp.clip(j - kv_tile0, 0, KV_TILES - 1), b, i, 0))],
        out_shape=[jax.ShapeDtypeStruct((g, t, n), jnp.float32), jax.ShapeDtypeStruct((g, t, d), jnp.float32),
                   jax.ShapeDtypeStruct((KV_TILES, g, t * KV_PARTS, HEAD_DIM), jnp.float32)],
        scratch_shapes=[pltpu.VMEM((tm, d), jnp.bfloat16)],
        compiler_params=pltpu.CompilerParams(
            dimension_semantics=("arbitrary", "arbitrary", "arbitrary"), vmem_limit_bytes=VMEM_LIMIT_BYTES),
        name="norm_in",
    )(x, norm1.reshape(1, d), sc, sh, w_in_b)


def _mix_out_kernel(on_ref, orw_ref, w_ref, x_ref, gt_ref, sc_ref, sh_ref, n2_ref, wr_hi_ref, wr_lo_ref, br_ref,
                    x1_ref, h2_ref, lg_ref):
    f32, bf16 = jnp.float32, jnp.bfloat16
    half = on_ref.shape[2]
    mixed = (jnp.dot(on_ref[0].astype(bf16), w_ref[0:half, :], preferred_element_type=f32)
             + jnp.dot(orw_ref[0].astype(bf16), w_ref[half:, :], preferred_element_type=f32))
    x1 = x_ref[0] + gt_ref[0] * mixed
    x1_ref[0] = x1
    y = x1 * lax.rsqrt(jnp.mean(x1 * x1, axis=-1, keepdims=True) + NORM_EPS) * n2_ref[...]
    h2 = y * (1.0 + sc_ref[0]) + sh_ref[0]
    hi = h2.astype(bf16)
    h2_ref[0] = hi
    lo = (h2 - hi.astype(f32)).astype(bf16)
    lg_ref[0] = (jnp.dot(hi, wr_hi_ref[...], preferred_element_type=f32)
                 + jnp.dot(hi, wr_lo_ref[...], preferred_element_type=f32)
                 + jnp.dot(lo, wr_hi_ref[...], preferred_element_type=f32) + br_ref[...])


def _mix_out(o_nsa, o_rwkv, w_out_b, x, gt, sc, sh, norm2, wr_hi, wr_lo, br, tm):
    g, t, d = x.shape
    half = o_nsa.shape[2]
    mrows = sc.shape[1]
    mod_spec = pl.BlockSpec((1, mrows if mrows == 1 else tm, d),
                            (lambda b, i: (b, 0, 0)) if mrows == 1 else (lambda b, i: (b, i, 0)))
    row = lambda w: pl.BlockSpec((1, tm, w), lambda b, i: (b, i, 0))
    full = lambda a: pl.BlockSpec(a.shape, lambda b, i: (0,) * a.ndim)
    n2 = norm2.reshape(1, d)
    return pl.pallas_call(
        _mix_out_kernel,
        grid=(g, t // tm),
        in_specs=[row(half), row(half), full(w_out_b), row(d), mod_spec, mod_spec, mod_spec, full(n2),
                  full(wr_hi), full(wr_lo), full(br)],
        out_specs=[row(d), row(d), row(ROUTER_LANES)],
        out_shape=[jax.ShapeDtypeStruct((g, t, d), jnp.float32), jax.ShapeDtypeStruct((g, t, d), jnp.bfloat16),
                   jax.ShapeDtypeStruct((g, t, ROUTER_LANES), jnp.float32)],
        compiler_params=pltpu.CompilerParams(
            dimension_semantics=("arbitrary", "arbitrary"), vmem_limit_bytes=VMEM_LIMIT_BYTES),
        name="mix_out",
    )(o_nsa, o_rwkv, w_out_b, x, gt, sc, sh, n2, wr_hi, wr_lo, br)


def _final_kernel(x_ref, y2_ref, ew_ref, gt_ref, nf_ref, o_ref):
    d = x_ref.shape[2]
    ew = ew_ref[0]
    ffn = y2_ref[0, :, 0:d] * ew[:, 0:1] + y2_ref[0, :, d:2 * d] * ew[:, 1:2]
    x2 = x_ref[0] + gt_ref[0] * ffn
    o_ref[0] = x2 * lax.rsqrt(jnp.mean(x2 * x2, axis=-1, keepdims=True) + NORM_EPS) * nf_ref[...]


def _final(x1, y2, ew, gt, norm_f, tm):
    g, t, d = x1.shape
    mrows = gt.shape[1]
    mod_spec = pl.BlockSpec((1, mrows if mrows == 1 else tm, d),
                            (lambda b, i: (b, 0, 0)) if mrows == 1 else (lambda b, i: (b, i, 0)))
    row = lambda w: pl.BlockSpec((1, tm, w), lambda b, i: (b, i, 0))
    return pl.pallas_call(
        _final_kernel,
        grid=(g, t // tm),
        in_specs=[row(d), row(2 * d), row(EXPERT_TOP_K), mod_spec, pl.BlockSpec((1, d), lambda b, i: (0, 0))],
        out_specs=row(d),
        out_shape=jax.ShapeDtypeStruct((g, t, d), jnp.float32),
        compiler_params=pltpu.CompilerParams(
            dimension_semantics=("arbitrary", "arbitrary"), vmem_limit_bytes=VMEM_LIMIT_BYTES),
        name="final_norm",
    )(x1, y2, ew, gt, norm_f.reshape(1, d))


def _cmp_partial_kernel(*refs, n_src, rows_per_src):
    x_refs = refs[:n_src]
    w1k_ref, w1v_ref, a_ref, b_ref = refs[n_src:]
    nch_src = rows_per_src // CMP_STRIDE
    for kvg in range(4):
        w_ref = w1k_ref if kvg < 2 else w1v_ref
        acc = None
        for p in range(CMP_STRIDE):
            parts = [x_refs[s][pl.ds(4 * p + kvg, nch_src, stride=4 * CMP_STRIDE), :] for s in range(n_src)]
            xp = parts[0] if n_src == 1 else jnp.concatenate(parts, axis=0)
            d = jnp.dot(xp.astype(jnp.bfloat16), w_ref[p], preferred_element_type=jnp.float32)
            acc = d if acc is None else acc + d
        a_ref[0, :, kvg * HEAD_DIM:(kvg + 1) * HEAD_DIM] = acc[:, :HEAD_DIM]
        b_ref[0, :, kvg * HEAD_DIM:(kvg + 1) * HEAD_DIM] = acc[:, HEAD_DIM:]


def _cmp_finish_kernel(a_ref, b_ref, pek_ref, pev_ref, w1k_ref, w1v_ref, w2k_ref, w2v_ref, o_ref):
    nch = a_ref.shape[1]
    for kv, (pe_ref, w1_ref, w2_ref) in enumerate(((pek_ref, w1k_ref, w2k_ref), (pev_ref, w1v_ref, w2v_ref))):
        pe8 = jnp.broadcast_to(pe_ref[...], (8, pe_ref.shape[1])).astype(jnp.bfloat16)
        pterm = jnp.dot(pe8, w1_ref[...], preferred_element_type=jnp.float32)[0:1]
        w2 = w2_ref[...]
        for g in range(NSA_KV_HEADS):
            lo = (kv * NSA_KV_HEADS + g) * HEAD_DIM
            nxt = pltpu.roll(b_ref[0, :, lo:lo + HEAD_DIM], nch - 1, 0)
            pre = a_ref[0, :, lo:lo + HEAD_DIM] + nxt + pterm
            act = pre * jax.nn.sigmoid(pre)
            o_ref[0, :, lo:lo + HEAD_DIM] = jnp.dot(act.astype(jnp.bfloat16), w2, preferred_element_type=jnp.float32)


def _cmp_weights(lp):
    bf = jnp.bfloat16
    half = CMP_BLOCK // 2
    cat = lambda w: jnp.concatenate([w[:half], w[half:]], axis=-1).astype(bf)
    flat = lambda w: w.reshape(CMP_BLOCK * HEAD_DIM, HEAD_DIM).astype(bf)
    return dict(w1k_cat=cat(lp['cmp_k_w1']), w1v_cat=cat(lp['cmp_v_w1']),
                w1k_flat=flat(lp['cmp_k_w1']), w1v_flat=flat(lp['cmp_v_w1']),
                pek=lp['cmp_k_pe'].reshape(1, -1), pev=lp['cmp_v_pe'].reshape(1, -1),
                w2k=lp['cmp_k_w2'].astype(bf), w2v=lp['cmp_v_w2'].astype(bf))


def _cmp_finish(a, b, cw):
    bsz, nch, _ = a.shape
    full = lambda arr: pl.BlockSpec(arr.shape, lambda i: (0,) * arr.ndim)
    blk = pl.BlockSpec((1, nch, 4 * HEAD_DIM), lambda i: (i, 0, 0))
    ws = [cw['pek'], cw['pev'], cw['w1k_flat'], cw['w1v_flat'], cw['w2k'], cw['w2v']]
    return pl.pallas_call(
        _cmp_finish_kernel,
        grid=(bsz,),
        in_specs=[blk, blk] + [full(w) for w in ws],
        out_specs=blk,
        out_shape=jax.ShapeDtypeStruct((bsz, nch, 4 * HEAD_DIM), jnp.float32),
        compiler_params=pltpu.CompilerParams(dimension_semantics=("arbitrary",), vmem_limit_bytes=VMEM_LIMIT_BYTES),
        name="cmp_finish",
    )(a, b, *ws)


def _compress_prompt(rows4, cw):
    bsz, seq = rows4.shape[0], rows4.shape[1] // 4
    nch = seq // CMP_STRIDE
    x_specs = [pl.BlockSpec((None, seq * 4, HEAD_DIM), lambda i: (i, 0, 0))]
    w_spec = pl.BlockSpec(cw['w1k_cat'].shape, lambda i: (0, 0, 0))
    out_spec = pl.BlockSpec((1, nch, 4 * HEAD_DIM), lambda i: (i, 0, 0))
    shp = jax.ShapeDtypeStruct((bsz, nch, 4 * HEAD_DIM), jnp.float32)
    a, b = pl.pallas_call(
        functools.partial(_cmp_partial_kernel, n_src=1, rows_per_src=seq),
        grid=(bsz,),
        in_specs=x_specs + [w_spec, w_spec],
        out_specs=[out_spec, out_spec],
        out_shape=[shp, shp],
        compiler_params=pltpu.CompilerParams(dimension_semantics=("arbitrary",), vmem_limit_bytes=VMEM_LIMIT_BYTES),
        name="cmp_partial_prompt",
    )(rows4, cw['w1k_cat'], cw['w1v_cat'])
    return _cmp_finish(a, b, cw)


_NT = (((1,), (1,)), ((), ()))
SEL_TK = 512
WIN_TK = 256


def _flash_update(s, v, m_ref, l_ref, acc_ref, h):
    tk = s.shape[1]
    m_prev = m_ref[h]
    m_new = jnp.maximum(m_prev, jnp.max(s, axis=-1, keepdims=True))
    alpha = jnp.exp(m_prev - m_new)
    p = jnp.exp(s - jnp.concatenate([m_new] * (tk // HEAD_DIM), axis=1))
    l_ref[h] = alpha * l_ref[h] + jnp.sum(p, axis=-1, keepdims=True)
    acc_ref[h] = alpha * acc_ref[h] + jnp.dot(p.astype(jnp.bfloat16), v, preferred_element_type=jnp.float32)
    m_ref[h] = m_new


def _nsa_prompt_kernel(q_ref, slc_ref, win_ref, gate_ref, kvc_ref, o_ref, m_ref, l_ref, acc_ref, *, tq, seq):
    f32, bf16 = jnp.float32, jnp.bfloat16
    qi = pl.program_id(1)
    t0 = qi * tq
    scale = HEAD_DIM ** -0.5
    nc_valid = seq // CMP_STRIDE - CMP_BLOCK // CMP_STRIDE + 1
    n_sel = seq // SEL_BLOCK
    pos = t0 + lax.broadcasted_iota(jnp.int32, (tq, 1), 0)
    lane = lax.broadcasted_iota(jnp.int32, (1, HEAD_DIM), 1)
    blk_t = jnp.right_shift(pos, 6)
    gates = jax.nn.sigmoid(gate_ref[...])
    dist_c = pos - (lane * CMP_STRIDE + (CMP_BLOCK - 1))
    valid_c = (dist_c >= 0) & (lane < nc_valid)
    dist_cf = dist_c.astype(f32)
    c_row = lax.broadcasted_iota(jnp.int32, (HEAD_DIM, 1), 0)
    overlap = jnp.where((c_row * CMP_STRIDE <= lane * SEL_BLOCK + (SEL_BLOCK - 1))
                        & (c_row * CMP_STRIDE + (CMP_BLOCK - 1) >= lane * SEL_BLOCK), 1.0, 0.0).astype(bf16)
    forced = (lane == 0) | (lane == blk_t) | (lane == blk_t - 1)

    def gate_col(branch, hh):
        c = branch * NSA_HEADS + hh
        return gates[:, c:c + 1]

    def reset():
        m_ref[...] = jnp.full(m_ref.shape, NEG_INF, f32)
        l_ref[...] = jnp.zeros(l_ref.shape, f32)
        acc_ref[...] = jnp.zeros(acc_ref.shape, f32)

    for g in range(NSA_KV_HEADS):
        kcol = slice(g * HEAD_DIM, (g + 1) * HEAD_DIM)
        vcol = slice((NSA_KV_HEADS + g) * HEAD_DIM, (NSA_KV_HEADS + g + 1) * HEAD_DIM)
        heads = [g * HPG + h for h in range(HPG)]
        slopes = [2.0 ** -(hh + 1) for hh in heads]

        kc = kvc_ref[0, :, kcol].astype(bf16)
        vc = kvc_ref[0, :, vcol].astype(bf16)
        psum = jnp.zeros((tq, HEAD_DIM), f32)
        for h, hh in enumerate(heads):
            qh = q_ref[:, hh * HEAD_DIM:(hh + 1) * HEAD_DIM].astype(bf16)
            s = lax.dot_general(qh, kc, _NT, preferred_element_type=f32) * scale - slopes[h] * dist_cf
            s = jnp.where(valid_c, s, NEG_INF)
            e = jnp.exp(s - jnp.max(s, axis=-1, keepdims=True))
            p = e / jnp.sum(e, axis=-1, keepdims=True)
            p = jnp.where(valid_c, p, 0.0)
            o_cmp = jnp.dot(p.astype(bf16), vc, preferred_element_type=f32)
            o_ref[:, hh * HEAD_DIM:(hh + 1) * HEAD_DIM] = gate_col(0, hh) * o_cmp
            psum = psum + p
        p_hi = psum.astype(bf16)
        p_lo = (psum - p_hi.astype(f32)).astype(bf16)
        imp = (jnp.dot(p_hi, overlap, preferred_element_type=f32)
               + jnp.dot(p_lo, overlap, preferred_element_type=f32))
        imp = jnp.where(forced, imp + FORCE_BONUS, imp)
        imp = jnp.where(lane <= blk_t, imp, NEG_INF)
        beaten = jnp.zeros((tq, HEAD_DIM), f32)
        for jp in range(n_sel):
            col = imp[:, jp:jp + 1]
            tie = jnp.where(lane > jp, 1.0, 0.0)
            beaten = beaten + jnp.where(col > imp, 1.0, jnp.where(col == imp, tie, 0.0))
        sel = jnp.where(beaten < N_SELECT, jnp.where(imp > 0.5 * NEG_INF, 1.0, 0.0), 0.0).astype(bf16)

        reset()
        j_row = lax.broadcasted_iota(jnp.int32, (HEAD_DIM, 1), 0)

        def sel_body(kt, carry):
            k0 = pl.multiple_of(kt * SEL_TK, SEL_TK)
            k = slc_ref[pl.ds(k0, SEL_TK), kcol].astype(bf16)
            v = slc_ref[pl.ds(k0, SEL_TK), vcol].astype(bf16)
            kpos = k0 + lax.broadcasted_iota(jnp.int32, (1, SEL_TK), 1)
            dist = pos - kpos
            expand = jnp.where(jnp.right_shift(kpos, 6) == j_row, 1.0, 0.0).astype(bf16)
            picked = jnp.dot(sel, expand, preferred_element_type=f32)
            keep = jnp.where(dist >= 0, picked, 0.0) > 0.5
            dist_f = dist.astype(f32)
            for h, hh in enumerate(heads):
                qh = q_ref[:, hh * HEAD_DIM:(hh + 1) * HEAD_DIM].astype(bf16)
                s = lax.dot_general(qh, k, _NT, preferred_element_type=f32) * scale - slopes[h] * dist_f
                _flash_update(jnp.where(keep, s, NEG_INF), v, m_ref, l_ref, acc_ref, h)
            return carry

        lax.fori_loop(0, (t0 + tq - 1) // SEL_TK + 1, sel_body, 0)
        for h, hh in enumerate(heads):
            hs = slice(hh * HEAD_DIM, (hh + 1) * HEAD_DIM)
            o_ref[:, hs] = o_ref[:, hs] + gate_col(1, hh) * (acc_ref[h] / l_ref[h])

        reset()

        def win_body(kt, carry):
            k0 = pl.multiple_of(kt * WIN_TK, WIN_TK)
            k = win_ref[pl.ds(k0, WIN_TK), kcol].astype(bf16)
            v = win_ref[pl.ds(k0, WIN_TK), vcol].astype(bf16)
            dist = pos - (k0 + lax.broadcasted_iota(jnp.int32, (1, WIN_TK), 1))
            keep = (dist >= 0) & (dist < WINDOW)
            dist_f = dist.astype(f32)
            for h, hh in enumerate(heads):
                qh = q_ref[:, hh * HEAD_DIM:(hh + 1) * HEAD_DIM].astype(bf16)
                s = lax.dot_general(qh, k, _NT, preferred_element_type=f32) * scale - slopes[h] * dist_f
                _flash_update(jnp.where(keep, s, NEG_INF), v, m_ref, l_ref, acc_ref, h)
            return carry

        lax.fori_loop(jnp.maximum(t0 - (WINDOW - 1), 0) // WIN_TK, (t0 + tq - 1) // WIN_TK + 1, win_body, 0)
        for h, hh in enumerate(heads):
            hs = slice(hh * HEAD_DIM, (hh + 1) * HEAD_DIM)
            o_ref[:, hs] = o_ref[:, hs] + gate_col(2, hh) * (acc_ref[h] / l_ref[h])


def _nsa_prompt(proj2d, kvc, bsz, seq, tq=256):
    nq = seq // tq
    kvw = 2 * KV_WIDTH
    return pl.pallas_call(
        functools.partial(_nsa_prompt_kernel, tq=tq, seq=seq),
        grid=(bsz, nq),
        in_specs=[pl.BlockSpec((tq, NSA_WIDTH), lambda b, i: (b * nq + i, 0)),
                  pl.BlockSpec((seq, kvw), lambda b, i: (b, SLC_OFF // kvw)),
                  pl.BlockSpec((seq, kvw), lambda b, i: (b, WIN_OFF // kvw)),
                  pl.BlockSpec((tq, HEAD_DIM), lambda b, i: (b * nq + i, GATE_OFF // HEAD_DIM)),
                  pl.BlockSpec((1, seq // CMP_STRIDE, kvw), lambda b, i: (b, 0, 0))],
        out_specs=pl.BlockSpec((tq, NSA_WIDTH), lambda b, i: (b * nq + i, 0)),
        out_shape=jax.ShapeDtypeStruct((bsz * seq, NSA_WIDTH), jnp.float32),
        scratch_shapes=[pltpu.VMEM((HPG, tq, HEAD_DIM), jnp.float32),
                        pltpu.VMEM((HPG, tq, HEAD_DIM), jnp.float32),
                        pltpu.VMEM((HPG, tq, HEAD_DIM), jnp.float32)],
        compiler_params=pltpu.CompilerParams(
            dimension_semantics=("arbitrary", "arbitrary"), vmem_limit_bytes=VMEM_LIMIT_BYTES),
        name="nsa_prompt",
    )(proj2d, proj2d, proj2d, proj2d, kvc)


CMP_PAGES_PER_STEP = 16
TOPK_LANES = 384
IDX_LANES = 128


def _cmp_partial_paged_kernel(pt_ref, *refs, n_src, rows_per_src):
    del pt_ref
    _cmp_partial_kernel(*refs, n_src=n_src, rows_per_src=rows_per_src)


def _compress_paged(pool, page_table, cw):
    bsz, n_pages = page_table.shape
    nps = CMP_PAGES_PER_STEP
    n_tiles = n_pages // nps
    nch_tile = nps * PAGE_SIZE // CMP_STRIDE

    def page_spec(s):
        return pl.BlockSpec((PAGE_SIZE * 4, HEAD_DIM), lambda b, i, pt: (pt[b * n_pages + i * nps + s], 0))

    x_specs = [page_spec(s) for s in range(nps)]
    w_spec = pl.BlockSpec(cw['w1k_cat'].shape, lambda b, i, pt: (0, 0, 0))
    out_spec = pl.BlockSpec((1, nch_tile, 4 * HEAD_DIM), lambda b, i, pt: (b, i, 0))
    shp = jax.ShapeDtypeStruct((bsz, n_tiles * nch_tile, 4 * HEAD_DIM), jnp.float32)
    a, b = pl.pallas_call(
        functools.partial(_cmp_partial_paged_kernel, n_src=nps, rows_per_src=PAGE_SIZE),
        grid_spec=pltpu.PrefetchScalarGridSpec(
            num_scalar_prefetch=1, grid=(bsz, n_tiles),
            in_specs=x_specs + [w_spec, w_spec], out_specs=[out_spec, out_spec]),
        out_shape=[shp, shp],
        compiler_params=pltpu.CompilerParams(
            dimension_semantics=("arbitrary", "arbitrary"), vmem_limit_bytes=VMEM_LIMIT_BYTES),
        name="cmp_partial_paged",
    )(page_table.reshape(-1), *([pool] * nps), cw['w1k_cat'], cw['w1v_cat'])
    return _cmp_finish(a, b, cw)


def _nsa_decode_a_kernel(proj_ref, kvc_ref, win_ref, o_ref, gsel_ref, idx_ref, *, t_new, n_past):
    f32, bf16 = jnp.float32, jnp.bfloat16
    scale = HEAD_DIM ** -0.5
    nch = kvc_ref.shape[1]
    n_win = win_ref.shape[1] // KV_PARTS
    rows = HPG * t_new
    r_iota = lax.broadcasted_iota(jnp.int32, (rows, 1), 0)
    t_row = r_iota % t_new
    h_row = r_iota // t_new
    pos_row = n_past + t_row
    gates = jax.nn.sigmoid(proj_ref[0, :, GATE_OFF:IN_WIDTH])
    c_lane = lax.broadcasted_iota(jnp.int32, (1, nch), 1)
    dist_c = pos_row - (c_lane * CMP_STRIDE + (CMP_BLOCK - 1))
    valid_c = (dist_c >= 0) & (c_lane < nch - 1)
    c_col = lax.broadcasted_iota(jnp.int32, (nch, 1), 0)
    j_lane = lax.broadcasted_iota(jnp.int32, (1, TOPK_LANES), 1)
    overlap = jnp.where((c_col * CMP_STRIDE <= j_lane * SEL_BLOCK + (SEL_BLOCK - 1))
                        & (c_col * CMP_STRIDE + (CMP_BLOCK - 1) >= j_lane * SEL_BLOCK), 1.0, 0.0).astype(bf16)
    pos_t = n_past + lax.broadcasted_iota(jnp.int32, (t_new, 1), 0)
    blk_t = pos_t // SEL_BLOCK
    forced = (j_lane == 0) | (j_lane == blk_t) | (j_lane == blk_t - 1)
    j_f = j_lane.astype(f32)
    k_lane = lax.broadcasted_iota(jnp.int32, (1, IDX_LANES), 1)
    i_win = lax.broadcasted_iota(jnp.int32, (1, n_win), 1)
    dist_w = pos_row - (n_past - n_win + i_win)
    keep_w = (dist_w >= 0) & (dist_w < WINDOW)
    j_new = lax.broadcasted_iota(jnp.int32, (1, 8), 1)
    dist_n = t_row - j_new
    keep_n = (dist_n >= 0) & (j_new < t_new)
    zpad = jnp.zeros((8 - t_new, HEAD_DIM), f32)

    for g in range(NSA_KV_HEADS):
        kcol = slice(g * HEAD_DIM, (g + 1) * HEAD_DIM)
        vcol = slice((NSA_KV_HEADS + g) * HEAD_DIM, (NSA_KV_HEADS + g + 1) * HEAD_DIM)
        heads = [g * HPG + h for h in range(HPG)]
        slope_row = jnp.zeros((rows, 1), f32)
        for h, hh in enumerate(heads):
            slope_row = jnp.where(h_row == h, 2.0 ** -(hh + 1), slope_row)
        q = jnp.concatenate([proj_ref[0, :, hh * HEAD_DIM:(hh + 1) * HEAD_DIM] for hh in heads], axis=0).astype(bf16)

        kc = kvc_ref[0, :, kcol].astype(bf16)
        vc = kvc_ref[0, :, vcol].astype(bf16)
        s = lax.dot_general(q, kc, _NT, preferred_element_type=f32) * scale - slope_row * dist_c.astype(f32)
        s = jnp.where(valid_c, s, NEG_INF)
        e = jnp.exp(s - jnp.max(s, axis=-1, keepdims=True))
        p = e / jnp.sum(e, axis=-1, keepdims=True)
        p = jnp.where(valid_c, p, 0.0)
        o_cmp = jnp.dot(p.astype(bf16), vc, preferred_element_type=f32)
        psum = p[0:t_new]
        for h in range(1, HPG):
            psum = psum + p[h * t_new:(h + 1) * t_new]

        p_hi = psum.astype(bf16)
        p_lo = (psum - p_hi.astype(f32)).astype(bf16)
        imp = (jnp.dot(p_hi, overlap, preferred_element_type=f32)
               + jnp.dot(p_lo, overlap, preferred_element_type=f32))
        imp = jnp.where(forced, imp + FORCE_BONUS, imp)
        imp = jnp.where(j_lane <= blk_t, imp, NEG_INF)
        picked = jnp.full((t_new, IDX_LANES), -1.0, f32)
        for k in range(N_SELECT):
            best = jnp.max(imp, axis=-1, keepdims=True)
            first = jnp.min(jnp.where(imp == best, j_f, 1e9), axis=-1, keepdims=True)
            picked = jnp.where(k_lane == k, jnp.where(best > 0.5 * NEG_INF, first, -1.0), picked)
            imp = jnp.where(j_f == first, -3e38, imp)
        idx_ref[0, g * t_new:(g + 1) * t_new, :] = picked.astype(jnp.int32)

        kw = win_ref[0, pl.ds(g, n_win, stride=KV_PARTS), :].astype(bf16)
        vw = win_ref[0, pl.ds(NSA_KV_HEADS + g, n_win, stride=KV_PARTS), :].astype(bf16)
        kn = jnp.concatenate([proj_ref[0, :, WIN_OFF + g * HEAD_DIM:WIN_OFF + (g + 1) * HEAD_DIM], zpad], axis=0)
        vn = jnp.concatenate([proj_ref[0, :, WIN_OFF + KV_WIDTH + g * HEAD_DIM:
                                       WIN_OFF + KV_WIDTH + (g + 1) * HEAD_DIM], zpad], axis=0)
        s_w = lax.dot_general(q, kw, _NT, preferred_element_type=f32) * scale - slope_row * dist_w.astype(f32)
        s_n = (lax.dot_general(q, kn.astype(bf16), _NT, preferred_element_type=f32) * scale
               - slope_row * dist_n.astype(f32))
        s_w = jnp.where(keep_w, s_w, NEG_INF)
        s_n = jnp.where(keep_n, s_n, NEG_INF)
        m = jnp.maximum(jnp.max(s_w, axis=-1, keepdims=True), jnp.max(s_n, axis=-1, keepdims=True))
        e_w = jnp.exp(s_w - m)
        e_n = jnp.exp(s_n - m)
        den = jnp.sum(e_w, axis=-1, keepdims=True) + jnp.sum(e_n, axis=-1, keepdims=True)
        o_win = (jnp.dot(e_w.astype(bf16), vw, preferred_element_type=f32)
                 + jnp.dot(e_n.astype(bf16), vn.astype(bf16), preferred_element_type=f32)) / den

        for h, hh in enumerate(heads):
            rs = slice(h * t_new, (h + 1) * t_new)
            hs = slice(hh * HEAD_DIM, (hh + 1) * HEAD_DIM)
            o_ref[0, :, hs] = (gates[:, hh:hh + 1] * o_cmp[rs]
                               + gates[:, 2 * NSA_HEADS + hh:2 * NSA_HEADS + hh + 1] * o_win[rs])
            gsel_ref[0, :, hs] = jnp.broadcast_to(gates[:, NSA_HEADS + hh:NSA_HEADS + hh + 1], (t_new, HEAD_DIM))


def _nsa_decode_b_kernel(idx_ref, pt_ref, q_ref, part_ref, gsel_ref, new_ref, *refs, t_new, n_past, n_pages):
    del pt_ref
    f32, bf16 = jnp.float32, jnp.bfloat16
    k_refs, v_refs, o_ref = refs[:N_SELECT], refs[N_SELECT:2 * N_SELECT], refs[2 * N_SELECT]
    b, g, t = pl.program_id(0), pl.program_id(1), pl.program_id(2)
    scale = HEAD_DIM ** -0.5
    n_past_blk = n_past // SEL_BLOCK
    base = ((b * NSA_KV_HEADS + g) * t_new + t) * N_SELECT
    n_keys = N_SELECT * SEL_BLOCK
    lane = lax.broadcasted_iota(jnp.int32, (1, n_keys), 1)
    slot = lane // SEL_BLOCK
    blk_of_lane = jnp.full((1, n_keys), -1, jnp.int32)
    ks, vs = [], []
    for k in range(N_SELECT):
        blk = idx_ref[base + k]
        blk_of_lane = jnp.where(slot == k, blk, blk_of_lane)
        is_new = blk >= n_past_blk
        ks.append(jnp.where(is_new, new_ref[:, 0:HEAD_DIM], k_refs[k][...]).astype(bf16))
        vs.append(jnp.where(is_new, new_ref[:, HEAD_DIM:2 * HEAD_DIM], v_refs[k][...]).astype(bf16))
    k_all = jnp.concatenate(ks, axis=0)
    v_all = jnp.concatenate(vs, axis=0)
    dist = (n_past + t) - (blk_of_lane * SEL_BLOCK + lane % SEL_BLOCK)
    keep = (dist >= 0) & (blk_of_lane >= 0)
    q = jnp.concatenate([q_ref[:, h * HEAD_DIM:(h + 1) * HEAD_DIM] for h in range(HPG)]
                        + [jnp.zeros((8 - HPG, HEAD_DIM), f32)], axis=0).astype(bf16)
    h_row = lax.broadcasted_iota(jnp.int32, (8, 1), 0)
    slope_row = jnp.zeros((8, 1), f32)
    for h in range(HPG):
        slope_row = jnp.where(h_row == h, jnp.where(g == 0, 2.0 ** -(h + 1), 2.0 ** -(HPG + h + 1)), slope_row)
    s = lax.dot_general(q, k_all, _NT, preferred_element_type=f32) * scale - slope_row * dist.astype(f32)
    s = jnp.where(keep, s, NEG_INF)
    e = jnp.exp(s - jnp.max(s, axis=-1, keepdims=True))
    p = e / jnp.sum(e, axis=-1, keepdims=True)
    o_sel = jnp.dot(p.astype(bf16), v_all, preferred_element_type=f32)
    o_row = jnp.concatenate([o_sel[h:h + 1] for h in range(HPG)], axis=1)
    o_ref[...] = part_ref[...] + gsel_ref[...] * o_row


def _nsa_decode(proj3, kvc, win_buf, pool_slc, page_table, n_past):
    bsz, t_new, _ = proj3.shape
    n_pages = page_table.shape[1]
    gw = HPG * HEAD_DIM
    grp = jax.ShapeDtypeStruct((bsz, NSA_KV_HEADS, t_new, gw), jnp.float32)
    part, gsel, idx = pl.pallas_call(
        functools.partial(_nsa_decode_a_kernel, t_new=t_new, n_past=n_past),
        grid=(bsz,),
        in_specs=[pl.BlockSpec((1, t_new, IN_WIDTH), lambda b: (b, 0, 0)),
                  pl.BlockSpec((1,) + kvc.shape[1:], lambda b: (b, 0, 0)),
                  pl.BlockSpec((1,) + win_buf.shape[1:], lambda b: (b, 0, 0))],
        out_specs=[pl.BlockSpec((1, NSA_KV_HEADS, t_new, gw), lambda b: (b, 0, 0, 0)),
                   pl.BlockSpec((1, NSA_KV_HEADS, t_new, gw), lambda b: (b, 0, 0, 0)),
                   pl.BlockSpec((1, NSA_KV_HEADS * t_new, IDX_LANES), lambda b: (b, 0, 0))],
        out_shape=[grp, grp, jax.ShapeDtypeStruct((bsz, NSA_KV_HEADS * t_new, IDX_LANES), jnp.int32)],
        compiler_params=pltpu.CompilerParams(dimension_semantics=("arbitrary",), vmem_limit_bytes=VMEM_LIMIT_BYTES),
        name="nsa_decode_a",
    )(proj3, kvc, win_buf)

    n_rows = bsz * NSA_KV_HEADS * t_new
    n_past_blk = n_past // SEL_BLOCK
    sub = PAGE_SIZE // SEL_BLOCK
    pool2 = pool_slc.reshape(-1, SEL_BLOCK, 2 * KV_WIDTH)
    new_rows = jnp.pad(proj3[:, :, SLC_OFF:WIN_OFF], ((0, 0), (0, SEL_BLOCK - t_new), (0, 0)))
    new_rows = new_rows.reshape(bsz, SEL_BLOCK, 2, NSA_KV_HEADS, HEAD_DIM).transpose(0, 3, 1, 2, 4)
    new_rows = new_rows.reshape(bsz * NSA_KV_HEADS, SEL_BLOCK, 2 * HEAD_DIM)

    def row_map(b, g, t, idx, pt):
        return (b * NSA_KV_HEADS + g) * t_new + t

    def pool_spec(k, is_v):
        def index(b, g, t, idx, pt):
            past = jnp.clip(idx[row_map(b, g, t, idx, pt) * N_SELECT + k], 0, n_past_blk - 1)
            return (pt[b * n_pages + past // sub] * sub + past % sub, 0, is_v * NSA_KV_HEADS + g)
        return pl.BlockSpec((None, SEL_BLOCK, HEAD_DIM), index)

    row_spec = pl.BlockSpec((None, 1, gw), lambda b, g, t, idx, pt: (row_map(b, g, t, idx, pt), 0, 0))
    out = pl.pallas_call(
        functools.partial(_nsa_decode_b_kernel, t_new=t_new, n_past=n_past, n_pages=n_pages),
        grid_spec=pltpu.PrefetchScalarGridSpec(
            num_scalar_prefetch=2, grid=(bsz, NSA_KV_HEADS, t_new),
            in_specs=[pl.BlockSpec((None, 1, gw), lambda b, g, t, idx, pt: (b * t_new + t, 0, g)),
                      row_spec, row_spec,
                      pl.BlockSpec((None, SEL_BLOCK, 2 * HEAD_DIM), lambda b, g, t, idx, pt: (b * NSA_KV_HEADS + g, 0, 0))]
            + [pool_spec(k, 0) for k in range(N_SELECT)] + [pool_spec(k, 1) for k in range(N_SELECT)],
            out_specs=row_spec),
        out_shape=jax.ShapeDtypeStruct((n_rows, 1, gw), jnp.float32),
        compiler_params=pltpu.CompilerParams(
            dimension_semantics=("arbitrary", "arbitrary", "arbitrary"), vmem_limit_bytes=VMEM_LIMIT_BYTES),
        name="nsa_decode_b",
    )(idx[:, :, :N_SELECT].reshape(-1), page_table.reshape(-1),
      proj3.reshape(bsz * t_new, 1, IN_WIDTH), part.reshape(n_rows, 1, gw), gsel.reshape(n_rows, 1, gw),
      new_rows, *([pool2] * (2 * N_SELECT)))
    return out.reshape(bsz, NSA_KV_HEADS, t_new, gw).transpose(0, 2, 1, 3).reshape(bsz, t_new, NSA_WIDTH)


def _nsa_decode_sel_kernel(idx_ref, pt_ref, q_ref, part_ref, gsel_ref, new_ref, *refs, t_new, n_past):
    del pt_ref
    f32, bf16 = jnp.float32, jnp.bfloat16
    o_ref = refs[NSA_KV_HEADS * N_SELECT]
    b, t = pl.program_id(0), pl.program_id(1)
    scale = HEAD_DIM ** -0.5
    n_past_blk = n_past // SEL_BLOCK
    n_keys = N_SELECT * SEL_BLOCK
    lane = lax.broadcasted_iota(jnp.int32, (1, n_keys), 1)
    slot = lane // SEL_BLOCK
    h_row = lax.broadcasted_iota(jnp.int32, (8, 1), 0)
    for g in range(NSA_KV_HEADS):
        base = ((b * NSA_KV_HEADS + g) * t_new + t) * N_SELECT
        blk_of_lane = jnp.full((1, n_keys), -1, jnp.int32)
        ks, vs = [], []
        for k in range(N_SELECT):
            blk = idx_ref[base + k]
            blk_of_lane = jnp.where(slot == k, blk, blk_of_lane)
            is_new = blk >= n_past_blk
            src = refs[g * N_SELECT + k]
            k_old = src[pl.ds(g, SEL_BLOCK, stride=KV_PARTS), :]
            v_old = src[pl.ds(NSA_KV_HEADS + g, SEL_BLOCK, stride=KV_PARTS), :]
            k_new = new_ref[:, g * HEAD_DIM:(g + 1) * HEAD_DIM]
            v_new = new_ref[:, KV_WIDTH + g * HEAD_DIM:KV_WIDTH + (g + 1) * HEAD_DIM]
            ks.append(jnp.where(is_new, k_new, k_old).astype(bf16))
            vs.append(jnp.where(is_new, v_new, v_old).astype(bf16))
        k_all = jnp.concatenate(ks, axis=0)
        v_all = jnp.concatenate(vs, axis=0)
        dist = (n_past + t) - (blk_of_lane * SEL_BLOCK + lane % SEL_BLOCK)
        keep = (dist >= 0) & (blk_of_lane >= 0)
        q = jnp.concatenate([q_ref[:, (g * HPG + h) * HEAD_DIM:(g * HPG + h + 1) * HEAD_DIM] for h in range(HPG)]
                            + [jnp.zeros((8 - HPG, HEAD_DIM), f32)], axis=0).astype(bf16)
        slope_row = jnp.zeros((8, 1), f32)
        for h in range(HPG):
            slope_row = jnp.where(h_row == h, 2.0 ** -(g * HPG + h + 1), slope_row)
        s = lax.dot_general(q, k_all, _NT, preferred_element_type=f32) * scale - slope_row * dist.astype(f32)
        s = jnp.where(keep, s, NEG_INF)
        e = jnp.exp(s - jnp.max(s, axis=-1, keepdims=True))
        p = e / jnp.sum(e, axis=-1, keepdims=True)
        o_sel = jnp.dot(p.astype(bf16), v_all, preferred_element_type=f32)
        for h in range(HPG):
            hs = slice((g * HPG + h) * HEAD_DIM, (g * HPG + h + 1) * HEAD_DIM)
            o_ref[:, hs] = part_ref[:, hs] + gsel_ref[:, hs] * o_sel[h:h + 1]


def _nsa_decode_rows(proj3, kvc, win_rows, pool_rows, page_table, n_past):
    bsz, t_new, _ = proj3.shape
    n_pages = page_table.shape[1]
    full = jax.ShapeDtypeStruct((bsz, t_new, NSA_WIDTH), jnp.float32)
    part, gsel, idx = pl.pallas_call(
        functools.partial(_nsa_decode_a_kernel, t_new=t_new, n_past=n_past),
        grid=(bsz,),
        in_specs=[pl.BlockSpec((1, t_new, IN_WIDTH), lambda b: (b, 0, 0)),
                  pl.BlockSpec((1,) + kvc.shape[1:], lambda b: (b, 0, 0)),
                  pl.BlockSpec((1,) + win_rows.shape[1:], lambda b: (b, 0, 0))],
        out_specs=[pl.BlockSpec((1, t_new, NSA_WIDTH), lambda b: (b, 0, 0)),
                   pl.BlockSpec((1, t_new, NSA_WIDTH), lambda b: (b, 0, 0)),
                   pl.BlockSpec((1, NSA_KV_HEADS * t_new, IDX_LANES), lambda b: (b, 0, 0))],
        out_shape=[full, full, jax.ShapeDtypeStruct((bsz, NSA_KV_HEADS * t_new, IDX_LANES), jnp.int32)],
        compiler_params=pltpu.CompilerParams(dimension_semantics=("arbitrary",), vmem_limit_bytes=VMEM_LIMIT_BYTES),
        name="nsa_decode_a",
    )(proj3, kvc, win_rows)

    n_past_blk = n_past // SEL_BLOCK
    sub = PAGE_SIZE // SEL_BLOCK
    new_rows = jnp.pad(proj3[:, :, SLC_OFF:WIN_OFF], ((0, 0), (0, SEL_BLOCK - t_new), (0, 0)))

    def pool_spec(g, k):
        def index(b, t, idx, pt):
            past = jnp.clip(idx[((b * NSA_KV_HEADS + g) * t_new + t) * N_SELECT + k], 0, n_past_blk - 1)
            return (pt[b * n_pages + past // sub] * sub + past % sub, 0)
        return pl.BlockSpec((SEL_BLOCK * KV_PARTS, HEAD_DIM), index)

    row_spec = pl.BlockSpec((None, 1, NSA_WIDTH), lambda b, t, idx, pt: (b * t_new + t, 0, 0))
    out = pl.pallas_call(
        functools.partial(_nsa_decode_sel_kernel, t_new=t_new, n_past=n_past),
        grid_spec=pltpu.PrefetchScalarGridSpec(
            num_scalar_prefetch=2, grid=(bsz, t_new),
            in_specs=[row_spec, row_spec, row_spec,
                      pl.BlockSpec((None, SEL_BLOCK, 2 * KV_WIDTH), lambda b, t, idx, pt: (b, 0, 0))]
            + [pool_spec(g, k) for g in range(NSA_KV_HEADS) for k in range(N_SELECT)],
            out_specs=row_spec),
        out_shape=jax.ShapeDtypeStruct((bsz * t_new, 1, NSA_WIDTH), jnp.float32),
        compiler_params=pltpu.CompilerParams(
            dimension_semantics=("arbitrary", "arbitrary"), vmem_limit_bytes=VMEM_LIMIT_BYTES),
        name="nsa_decode_sel",
    )(idx[:, :, :N_SELECT].reshape(-1), page_table.reshape(-1),
      proj3[:, :, Q_OFF:Q_OFF + NSA_WIDTH].reshape(bsz * t_new, 1, NSA_WIDTH),
      part.reshape(bsz * t_new, 1, NSA_WIDTH), gsel.reshape(bsz * t_new, 1, NSA_WIDTH),
      new_rows, *([pool_rows] * (NSA_KV_HEADS * N_SELECT)))
    return out.reshape(bsz, t_new, NSA_WIDTH)


WKV_LANES = 2 * RWKV_HEAD_DIM
WKV_PAIRS = RWKV_HEADS // 2
WKV_STACK = 4
WKV_BB = 4
WKV_CHUNK = 128
WKV_MIN_CHUNK = 16
WKV_FLUSH = RWKV_HEAD_DIM


def _wkv_kernel(r_ref, w_ref, k_ref, v_ref, kk_ref, kka_ref, s0_ref, y_ref, st_ref,
                s_scr, y_scr, *, tc):
    f32, bf16 = jnp.float32, jnp.bfloat16
    ti = pl.program_id(1)
    hd = RWKV_HEAD_DIM
    n_tiles = WKV_BB * WKV_PAIRS
    n_stacks = n_tiles // WKV_STACK
    tile = lambda q: (q // WKV_PAIRS, q % WKV_PAIRS)

    @pl.when(ti == 0)
    def _():
        for q in range(n_tiles):
            b, p = tile(q)
            s_scr[q] = jnp.concatenate([s0_ref[b, 2 * p], s0_ref[b, 2 * p + 1]], axis=1)

    lane = lax.broadcasted_iota(jnp.int32, (1, WKV_LANES), 1)
    r2 = lax.broadcasted_iota(jnp.int32, (2 * WKV_LANES, 1), 0)
    c2 = lax.broadcasted_iota(jnp.int32, (1, 2 * WKV_LANES), 1)
    same_head2 = jnp.where(r2 // hd == c2 // hd, 1.0, 0.0).astype(bf16)
    on_diag = lax.broadcasted_iota(jnp.int32, (hd, 1), 0) == lane % hd
    n_flush = min(tc, WKV_FLUSH)
    y_scr[...] = jnp.zeros(y_scr.shape, f32)
    pairs = [(2 * i, 2 * i + 1) for i in range(n_stacks // 2)]
    stack_tiles = lambda st: range(st * WKV_STACK, (st + 1) * WKV_STACK)

    def row_sums(per_tile):
        out = [None] * n_tiles
        for s0, s1 in pairs:
            lhs = jnp.concatenate([jnp.concatenate([per_tile[q] for q in stack_tiles(st)], axis=0)
                                   for st in (s0, s1)], axis=1)
            res = jnp.dot(lhs.astype(bf16), same_head2, preferred_element_type=f32)
            for half, st in enumerate((s0, s1)):
                for n, q in enumerate(stack_tiles(st)):
                    out[q] = res[n * hd:(n + 1) * hd, half * WKV_LANES:(half + 1) * WKV_LANES]
        return out

    def step(t, carry):
        here = (lane % hd) == (t % n_flush)
        get = lambda ref, q: ref[tile(q)[0], tile(q)[1], pl.ds(t, 1), :]
        s_old = [s_scr[q] for q in range(n_tiles)]
        sa = row_sums([s_old[q] * get(kk_ref, q) for q in range(n_tiles)])
        v_col = row_sums([jnp.where(on_diag, get(v_ref, q), 0.0) for q in range(n_tiles)])
        s_new = []
        for q in range(n_tiles):
            s = s_old[q] * get(w_ref, q) - sa[q] * get(kka_ref, q) + v_col[q] * get(k_ref, q)
            s_scr[q] = s
            s_new.append(s)
        y_col = row_sums([s_new[q] * get(r_ref, q) for q in range(n_tiles)])
        for q in range(n_tiles):
            y_scr[q] = jnp.where(here, y_col[q], y_scr[q])
        return carry

    for sub in range(tc // n_flush):
        lax.fori_loop(sub * n_flush, (sub + 1) * n_flush, step, 0)
        for q in range(n_tiles):
            b, p = tile(q)
            yt = y_scr[q].T
            y_ref[b, p, sub * n_flush:(sub + 1) * n_flush, :] = jnp.concatenate(
                [yt[:n_flush], yt[hd:hd + n_flush]], axis=1)

    @pl.when(ti == pl.num_programs(1) - 1)
    def _():
        for q in range(n_tiles):
            b, p = tile(q)
            st_ref[b, 2 * p] = s_scr[q][:, :hd]
            st_ref[b, 2 * p + 1] = s_scr[q][:, hd:]


def _wkv_scan(r, w, k, v, kk, kka, s0, tc):
    bsz, n_pairs, seq, _ = r.shape
    assert n_pairs == WKV_PAIRS and bsz % WKV_BB == 0 and seq % tc == 0
    n_tiles = WKV_BB * WKV_PAIRS
    x_spec = pl.BlockSpec((WKV_BB, WKV_PAIRS, tc, WKV_LANES), lambda b, i: (b, 0, i, 0))
    s_spec = pl.BlockSpec((WKV_BB, RWKV_HEADS, RWKV_HEAD_DIM, RWKV_HEAD_DIM), lambda b, i: (b, 0, 0, 0))
    return pl.pallas_call(
        functools.partial(_wkv_kernel, tc=tc),
        grid=(bsz // WKV_BB, seq // tc),
        in_specs=[x_spec] * 6 + [s_spec],
        out_specs=[x_spec, s_spec],
        out_shape=[jax.ShapeDtypeStruct(r.shape, jnp.float32),
                   jax.ShapeDtypeStruct(s0.shape, jnp.float32)],
        scratch_shapes=[pltpu.VMEM((n_tiles, RWKV_HEAD_DIM, WKV_LANES), jnp.float32),
                        pltpu.VMEM((n_tiles, RWKV_HEAD_DIM, WKV_LANES), jnp.float32)],
        compiler_params=pltpu.CompilerParams(
            dimension_semantics=("arbitrary", "arbitrary"), vmem_limit_bytes=VMEM_LIMIT_BYTES),
        name="wkv_scan",
    )(r, w, k, v, kk, kka, s0)


RW_PACKS = RWKV_WIDTH // WKV_LANES


def _head_sums(x, ones2):
    f32, bf16 = jnp.float32, jnp.bfloat16
    hi = x.astype(bf16)
    lo = (x - hi.astype(f32)).astype(bf16)
    w = 2 * WKV_LANES
    out = []
    for c in range(RWKV_WIDTH // w):
        sl = slice(c * w, (c + 1) * w)
        out.append(jnp.dot(hi[:, sl], ones2, preferred_element_type=f32)
                   + jnp.dot(lo[:, sl], ones2, preferred_element_type=f32))
    return jnp.concatenate(out, axis=1)


def _head_ones():
    w = 2 * WKV_LANES
    r = lax.broadcasted_iota(jnp.int32, (w, 1), 0)
    c = lax.broadcasted_iota(jnp.int32, (1, w), 1)
    return jnp.where(r // RWKV_HEAD_DIM == c // RWKV_HEAD_DIM, 1.0, 0.0).astype(jnp.bfloat16)


def _store_rw(ref, val, pack_major):
    if pack_major:
        for p in range(RW_PACKS):
            ref[0, p] = val[:, p * WKV_LANES:(p + 1) * WKV_LANES]
    else:
        ref[0] = val


def _rwkv_prep_kernel(h_ref, hprev_ref, h0_ref, pr_ref, pk_ref, pv_ref, pprev_r, pprev_k, pprev_v, p0_ref,
                      mu_rkv_ref, mu_wag_ref, dw0_ref, dw1_ref, dw2_ref, a0_ref, a1_ref, a2_ref, g1_ref, g2_ref,
                      kk_w_ref, ka_w_ref, rk_w_ref,
                      r_out, w_out, k_out, v_out, kk_out, kka_out, g_out, bonus_out, *, period, pack_major):
    f32, bf16 = jnp.float32, jnp.bfloat16
    i = pl.program_id(1)
    tm = h_ref.shape[1]
    row = lax.broadcasted_iota(jnp.int32, (tm, 1), 0)
    per_row_first = h0_ref.shape[1] != 1
    first = (row % period == 0) if per_row_first else None

    def shifted(cur, prev_blk, first_rows):
        rolled = pltpu.roll(cur, 1, 0)
        if per_row_first:
            return jnp.where(first, first_rows, rolled)
        row0 = jnp.where(i == 0, first_rows, prev_blk[7:8])
        return jnp.where(row == 0, row0, rolled)

    h = h_ref[0]
    xx = shifted(h, hprev_ref[0], h0_ref[0]) - h
    xw = (h + xx * mu_wag_ref[0:1]).astype(bf16)
    xa = (h + xx * mu_wag_ref[1:2]).astype(bf16)
    xg = (h + xx * mu_wag_ref[2:3]).astype(bf16)
    dmid = jnp.tanh(jnp.dot(xw, dw1_ref[...], preferred_element_type=f32))
    dlin = dw0_ref[...] + jnp.dot(dmid.astype(bf16), dw2_ref[...], preferred_element_type=f32)
    z = -dlin
    w_log = -(jnp.maximum(z, 0.0) + jnp.log(1.0 + jnp.exp(-jnp.abs(z)))) - 0.5
    decay = jnp.exp(-jnp.exp(w_log))
    amid = jnp.dot(xa, a1_ref[...], preferred_element_type=f32)
    a = jax.nn.sigmoid(a0_ref[...] + jnp.dot(amid.astype(bf16), a2_ref[...], preferred_element_type=f32))
    gmid = jax.nn.sigmoid(jnp.dot(xg, g1_ref[...], preferred_element_type=f32))
    g = jnp.dot(gmid.astype(bf16), g2_ref[...], preferred_element_type=f32)

    def mixed(cur_ref, prev_ref, n):
        cur = cur_ref[0]
        cs = slice(n * RWKV_WIDTH, (n + 1) * RWKV_WIDTH)
        prev = shifted(cur, prev_ref[0], p0_ref[0][:, cs])
        return cur + mu_rkv_ref[:, cs] * (prev - cur)

    r = mixed(pr_ref, pprev_r, 0)
    k = mixed(pk_ref, pprev_k, 1)
    v = mixed(pv_ref, pprev_v, 2)
    ones2 = _head_ones()
    kk = k * kk_w_ref[...]
    kk = kk / jnp.maximum(jnp.sqrt(_head_sums(kk * kk, ones2)), 1e-12)
    k = k * (1.0 + (a - 1.0) * ka_w_ref[...])
    bonus = _head_sums(r * k * rk_w_ref[...], ones2) * v
    _store_rw(r_out, r, pack_major)
    _store_rw(w_out, decay, pack_major)
    _store_rw(k_out, k, pack_major)
    _store_rw(v_out, v, pack_major)
    _store_rw(kk_out, kk, pack_major)
    _store_rw(kka_out, kk * a, pack_major)
    g_out[0] = g
    bonus_out[0] = bonus


def _rwkv_prep(h, proj, h0, p0, lp, tm, period, pack_major):
    f32, bf16 = jnp.float32, jnp.bfloat16
    g, t, d = h.shape
    rw = RWKV_WIDTH
    nb = tm // 8
    cur = lambda w, c: pl.BlockSpec((1, tm, w), lambda b, i: (b, i, c))
    prev = lambda w, c: pl.BlockSpec((1, 8, w), lambda b, i: (b, jnp.maximum(i * nb - 1, 0), c))
    per_row = h0.shape[1] != 1
    carry = lambda w: pl.BlockSpec((1, tm if per_row else 1, w), (lambda b, i: (b, i, 0)) if per_row else (lambda b, i: (b, 0, 0)))
    full = lambda a: pl.BlockSpec(a.shape, lambda b, i: (0,) * a.ndim)
    c0 = RKV_OFF // rw
    ws = [lp['mu_rkv'].reshape(1, 3 * rw), lp['mu_wag'], lp['decay_w0'].reshape(1, rw), lp['decay_w1'].astype(bf16),
          lp['decay_w2'].astype(bf16), lp['iclr_a0'].reshape(1, rw), lp['iclr_a1'].astype(bf16),
          lp['iclr_a2'].astype(bf16), lp['gate_g1'].astype(bf16), lp['gate_g2'].astype(bf16),
          lp['k_k'].reshape(1, rw), lp['k_a'].reshape(1, rw), lp['r_k'].reshape(1, rw)]
    if pack_major:
        seq_shape = jax.ShapeDtypeStruct((g, RW_PACKS, t, WKV_LANES), f32)
        seq_spec = pl.BlockSpec((1, RW_PACKS, tm, WKV_LANES), lambda b, i: (b, 0, i, 0))
    else:
        seq_shape = jax.ShapeDtypeStruct((g, t, rw), f32)
        seq_spec = cur(rw, 0)
    flat_shape = jax.ShapeDtypeStruct((g, t, rw), f32)
    return pl.pallas_call(
        functools.partial(_rwkv_prep_kernel, period=period, pack_major=pack_major),
        grid=(g, t // tm),
        in_specs=[cur(d, 0), prev(d, 0), carry(d), cur(rw, c0), cur(rw, c0 + 1), cur(rw, c0 + 2),
                  prev(rw, c0), prev(rw, c0 + 1), prev(rw, c0 + 2), carry(3 * rw)] + [full(w) for w in ws],
        out_specs=[seq_spec] * 6 + [cur(rw, 0), cur(rw, 0)],
        out_shape=[seq_shape] * 6 + [flat_shape, flat_shape],
        compiler_params=pltpu.CompilerParams(
            dimension_semantics=("arbitrary", "arbitrary"), vmem_limit_bytes=VMEM_LIMIT_BYTES),
        name="rwkv_prep",
    )(h, h, h0, proj, proj, proj, proj, proj, proj, p0, *ws)


def _rwkv_post_kernel(y_ref, g_ref, bonus_ref, lnw_ref, lnb_ref, o_ref, *, pack_major):
    if pack_major:
        y = jnp.concatenate([y_ref[0, p] for p in range(RW_PACKS)], axis=1)
    else:
        y = y_ref[0]
    ones2 = _head_ones()
    inv = 1.0 / RWKV_HEAD_DIM
    mu = _head_sums(y, ones2) * inv
    dev = y - mu
    var = _head_sums(dev * dev, ones2) * inv
    yn = dev * lax.rsqrt(var + GN_EPS) * lnw_ref[...] + lnb_ref[...]
    o_ref[0] = (yn + bonus_ref[0]) * g_ref[0]


def _rwkv_post(y, g, bonus, ln_w, ln_b, tm, pack_major):
    gsz, t, rw = g.shape
    flat = pl.BlockSpec((1, tm, rw), lambda b, i: (b, i, 0))
    y_spec = pl.BlockSpec((1, RW_PACKS, tm, WKV_LANES), lambda b, i: (b, 0, i, 0)) if pack_major else flat
    vec = pl.BlockSpec((1, rw), lambda b, i: (0, 0))
    return pl.pallas_call(
        functools.partial(_rwkv_post_kernel, pack_major=pack_major),
        grid=(gsz, t // tm),
        in_specs=[y_spec, flat, flat, vec, vec],
        out_specs=flat,
        out_shape=jax.ShapeDtypeStruct((gsz, t, rw), jnp.float32),
        compiler_params=pltpu.CompilerParams(
            dimension_semantics=("arbitrary", "arbitrary"), vmem_limit_bytes=VMEM_LIMIT_BYTES),
        name="rwkv_post",
    )(y, g, bonus, ln_w.reshape(1, rw), ln_b.reshape(1, rw))


def _rmsnorm(x, g):
    xf = x.astype(jnp.float32)
    y = xf * lax.rsqrt(jnp.mean(xf * xf, axis=-1, keepdims=True) + NORM_EPS)
    return (y * g.astype(jnp.float32)).astype(x.dtype)


def _alibi_slopes():
    n = jnp.arange(1, NSA_HEADS + 1, dtype=jnp.float32)
    return jnp.exp2(-8.0 * n / NSA_HEADS).reshape(NSA_KV_HEADS, HPG)


def _compress_blocks(rows, w1, pe, w2):
    B, L = rows.shape[:2]
    n_chunk = L // CMP_STRIDE
    r = CMP_BLOCK // CMP_STRIDE
    nc = n_chunk - r + 1
    ch = rows.reshape(B, n_chunk, CMP_STRIDE, NSA_KV_HEADS, HEAD_DIM)
    pre = jnp.einsum('pd,pde->e', pe, w1)
    for i in range(r):
        pre = pre + jnp.einsum('bcpgd,pde->bcge', ch[:, i:i + nc], w1[i * CMP_STRIDE:(i + 1) * CMP_STRIDE])
    return jnp.einsum('bcge,ef->bcgf', jax.nn.silu(pre), w2)


def _overlap_matrix(nc, ns):
    c_start = jnp.arange(nc) * CMP_STRIDE
    c_end = c_start + CMP_BLOCK - 1
    s_start = jnp.arange(ns) * SEL_BLOCK
    s_end = s_start + SEL_BLOCK - 1
    return ((c_start[:, None] <= s_end[None]) & (c_end[:, None] >= s_start[None])).astype(jnp.float32)


def _compressed_branch(qg, pos_q, kc, vc, n_sel, slopes):
    nc = kc.shape[1]
    end_c = jnp.arange(nc) * CMP_STRIDE + CMP_BLOCK - 1
    dist = pos_q[:, None] - end_c[None, :]
    valid = (dist >= 0)[None, :, None, None, :]
    s = jnp.einsum('btghd,bcgd->btghc', qg, kc).astype(jnp.float32) * HEAD_DIM ** -0.5
    s = s - slopes[None, None, :, :, None] * dist.astype(jnp.float32)[None, :, None, None, :]
    p = jax.nn.softmax(jnp.where(valid, s, NEG_INF), axis=-1) * valid
    o = jnp.einsum('btghc,bcgd->btghd', p.astype(vc.dtype), vc)
    imp = jnp.einsum('btghc,cj->btgj', p, _overlap_matrix(nc, n_sel))
    blk_t = (pos_q // SEL_BLOCK)[:, None]
    j = jnp.arange(n_sel)[None, :]
    forced = (j == 0) | (j == blk_t) | (j == blk_t - 1)
    imp = jnp.where(forced[None, :, None, :], imp + FORCE_BONUS, imp)
    imp = jnp.where((j <= blk_t)[None, :, None, :], imp, NEG_INF)
    top_val, top_idx = lax.top_k(imp, min(N_SELECT, n_sel))
    return o, top_idx, top_val > 0.5 * NEG_INF


def _gather_selected(idx, new_blk, n_past_blk, pool, layer, page_table):
    b_idx = jnp.arange(idx.shape[0])[:, None, None, None]
    g_idx = jnp.arange(NSA_KV_HEADS)[None, None, :, None]
    local = jnp.clip(idx - n_past_blk, 0, new_blk.shape[1] - 1)
    blk = new_blk[b_idx, local, :, :, g_idx]
    if pool is not None:
        sub_per_page = PAGE_SIZE // SEL_BLOCK
        past = jnp.clip(idx, 0, n_past_blk - 1)
        phys = page_table[b_idx, past // sub_per_page]
        pool_r = pool.reshape(pool.shape[0], pool.shape[1], sub_per_page, SEL_BLOCK, 2, NSA_KV_HEADS, HEAD_DIM)
        l_idx = jnp.full_like(phys, layer)
        pblk = pool_r[l_idx, phys, past % sub_per_page, :, :, g_idx]
        blk = jnp.where((idx < n_past_blk)[..., None, None, None], pblk, blk)
    return blk[..., 0, :], blk[..., 1, :]


def _selected_branch(qg, pos_q, kb, vb, idx, sel_valid, slopes):
    B, Tq, G, H, _ = qg.shape
    s = jnp.einsum('btghd,btgkpd->btghkp', qg, kb).astype(jnp.float32) * HEAD_DIM ** -0.5
    key_pos = idx[..., None] * SEL_BLOCK + jnp.arange(SEL_BLOCK)
    dist = pos_q[None, :, None, None, None] - key_pos
    mask = ((dist >= 0) & sel_valid[..., None])[:, :, :, None]
    s = s - slopes[None, None, :, :, None, None] * dist.astype(jnp.float32)[:, :, :, None]
    s = jnp.where(mask, s, NEG_INF)
    p = jax.nn.softmax(s.reshape(B, Tq, G, H, -1), axis=-1).reshape(s.shape)
    return jnp.einsum('btghkp,btgkpd->btghd', p.astype(vb.dtype), vb)


def _window_attend(qg, pos_q, k, v, key_pos, slopes):
    s = jnp.einsum('btghd,bsgd->btghs', qg, k).astype(jnp.float32) * HEAD_DIM ** -0.5
    dist = pos_q[:, None] - key_pos[None, :]
    mask = ((dist >= 0) & (dist < WINDOW) & (key_pos[None, :] >= 0))[None, :, None, None, :]
    s = s - slopes[None, None, :, :, None] * dist.astype(jnp.float32)[None, :, None, None, :]
    p = jax.nn.softmax(jnp.where(mask, s, NEG_INF), axis=-1)
    return jnp.einsum('btghs,bsgd->btghd', p.astype(v.dtype), v)


def _nsa_group(proj, pos_q, lp, layer, pool_cmp, pool_slc, page_table, win_buf, past_len):
    B, T = proj.shape[:2]
    G, dh = NSA_KV_HEADS, HEAD_DIM
    slopes = _alibi_slopes()
    qg = proj[..., Q_OFF:CMP_OFF].reshape(B, T, G, HPG, dh)
    cmp_new = proj[..., CMP_OFF:SLC_OFF].reshape(B, T, 2, G, dh)
    slc_new = proj[..., SLC_OFF:WIN_OFF].reshape(B, T, 2, G, dh)
    win_new = proj[..., WIN_OFF:RKV_OFF].reshape(B, T, 2, G, dh)
    gates = jax.nn.sigmoid(proj[..., GATE_OFF:IN_WIDTH].astype(jnp.float32)).reshape(B, T, 3, G, HPG, 1)

    if pool_cmp is None:
        proj2d = proj.reshape(B * T, IN_WIDTH)
        kvc = _compress_prompt(proj2d, B, T, _cmp_weights(lp))
        o = _nsa_prompt(proj2d, kvc, B, T).reshape(B, T, NSA_WIDTH)
        pad_rows = jnp.zeros((B, min(WINDOW, past_len), 2, G, dh), win_new.dtype)
        win_state = jnp.concatenate([pad_rows, win_new], axis=1)[:, -min(WINDOW, past_len):]
        return o, cmp_new, slc_new, win_state

    n_past = page_table.shape[1] * PAGE_SIZE
    assert n_past % CMP_STRIDE == 0 and T < CMP_STRIDE and T <= 8
    assert -(-(n_past + T) // SEL_BLOCK) <= TOPK_LANES and n_past % SEL_BLOCK == 0 and T <= SEL_BLOCK
    assert win_buf.shape[1] == WINDOW
    kvw = 2 * KV_WIDTH
    pages = page_table + layer * pool_cmp.shape[1]
    kvc = _compress_paged(pool_cmp.reshape(-1, PAGE_SIZE, kvw), pages, _cmp_weights(lp))
    o = _nsa_decode(proj, kvc, win_buf.reshape(B, WINDOW, kvw), pool_slc, pages, n_past)
    win_state = jnp.concatenate([win_buf, win_new], axis=1)[:, -win_buf.shape[1]:]
    return o, cmp_new, slc_new, win_state


def _rwkv_group(h, h_shift, p_rkv, p_rkv_prev, wkv0, lp):
    B, T, _ = h.shape
    f32 = jnp.float32
    rkv = (p_rkv + lp['mu_rkv'] * (p_rkv_prev - p_rkv)).astype(f32)
    r, k, v = jnp.split(rkv, 3, axis=-1)
    xx = h_shift - h
    xw = h + xx * lp['mu_wag'][0]
    xa = h + xx * lp['mu_wag'][1]
    xg = h + xx * lp['mu_wag'][2]
    w_log = -jax.nn.softplus(-(lp['decay_w0'] + jnp.tanh(xw @ lp['decay_w1']) @ lp['decay_w2']).astype(f32)) - 0.5
    decay = jnp.exp(-jnp.exp(w_log))
    a = jax.nn.sigmoid((lp['iclr_a0'] + (xa @ lp['iclr_a1']) @ lp['iclr_a2']).astype(f32))
    g = (jax.nn.sigmoid(xg @ lp['gate_g1']) @ lp['gate_g2']).astype(f32)

    def heads(t):
        return t.reshape(B, T, RWKV_HEADS, RWKV_HEAD_DIM)

    kk = heads(k * lp['k_k'])
    kk = kk / jnp.maximum(jnp.sqrt(jnp.sum(kk * kk, axis=-1, keepdims=True)), 1e-12)
    k = k * (1.0 + (a - 1.0) * lp['k_a'])
    r_h, k_h, v_h, w_h, a_h = heads(r), heads(k), heads(v), heads(decay), heads(a)

    t_pad = -(-T // WKV_MIN_CHUNK) * WKV_MIN_CHUNK
    tc = WKV_CHUNK if t_pad % WKV_CHUNK == 0 else WKV_MIN_CHUNK

    def pairs(t, fill):
        t = t.reshape(B, T, RWKV_HEADS // 2, WKV_LANES)
        if t_pad != T:
            t = jnp.pad(t, ((0, 0), (0, t_pad - T), (0, 0), (0, 0)), constant_values=fill)
        return t.transpose(0, 2, 1, 3)

    y, S_T = _wkv_scan(pairs(r, 0.0), pairs(decay, 1.0), pairs(k, 0.0), pairs(v, 0.0),
                       pairs(kk, 0.0), pairs(kk * a_h, 0.0), wkv0.astype(f32), tc)
    y = y.transpose(0, 2, 1, 3)[:, :T].reshape(B, T, RWKV_HEADS, RWKV_HEAD_DIM)
    mu = jnp.mean(y, axis=-1, keepdims=True)
    var = jnp.mean(jnp.square(y - mu), axis=-1, keepdims=True)
    y = ((y - mu) * lax.rsqrt(var + GN_EPS)).reshape(B, T, RWKV_WIDTH) * lp['ln_x_w'] + lp['ln_x_b']
    bonus = (jnp.sum(r_h * k_h * lp['r_k'], axis=-1, keepdims=True) * v_h).reshape(B, T, RWKV_WIDTH)
    return (y + bonus) * g, S_T


def _hier_route(logits):
    n = logits.shape[0]
    pg = jax.nn.softmax(logits[:, :N_GROUPS], axis=-1)
    g_val, g_sel = lax.top_k(pg, 1)
    le = logits[:, N_GROUPS:N_GROUPS + N_EXPERTS].reshape(n, N_GROUPS, EXPERTS_PER_GROUP)
    le_g = jnp.take_along_axis(le, g_sel[:, :, None], axis=1)[:, 0]
    e_val, e_sel = lax.top_k(le_g, EXPERT_TOP_K)
    weights = jax.nn.softmax(e_val, axis=-1) * g_val
    return g_sel * EXPERTS_PER_GROUP + e_sel, weights


def _moe_ffn(h_pad, eid, w_gate, w_up, w_down):
    n, d = h_pad.shape[0] - 1, h_pad.shape[1]
    a_tot = n * EXPERT_TOP_K
    flat_e = eid.reshape(-1)
    onehot = (flat_e[:, None] == jnp.arange(N_EXPERTS)[None, :]).astype(jnp.int32)
    csum = jnp.cumsum(onehot, axis=0)
    rank = jnp.take_along_axis(csum, flat_e[:, None], axis=1)[:, 0] - 1
    counts = csum[-1]
    padded = (counts + MOE_TM - 1) // MOE_TM * MOE_TM
    pad_end = jnp.cumsum(padded)
    dest = (pad_end - padded)[flat_e] + rank
    n_blk = (a_tot + N_EXPERTS * (MOE_TM - 1)) // MOE_TM
    tok_buf = jnp.full((n_blk * MOE_TM,), n, jnp.int32).at[dest].set(jnp.arange(a_tot, dtype=jnp.int32) // EXPERT_TOP_K)
    xb = h_pad[tok_buf].reshape(n_blk, MOE_TM, d)
    blk_e = jnp.clip(jnp.searchsorted(pad_end, jnp.arange(n_blk) * MOE_TM, side='right'), 0, N_EXPERTS - 1)
    meta = jnp.concatenate([blk_e, pad_end[-1:] // MOE_TM]).astype(jnp.int32)
    yb = _moe_blocks(xb, meta, w_gate, w_up, w_down).reshape(n_blk * MOE_TM, d)
    return yb[dest].reshape(n, EXPERT_TOP_K * d)


def _layer_front(x, mod, pos_q, lp, layer, pool_cmp, pool_slc, page_table, win_buf, wkv0, shift0, past_len, rows):
    B, T, D = x.shape
    groups = B * T // rows
    per_token = rows > T
    mods = jnp.repeat(mod, T, axis=0).reshape(groups, rows, 6 * D) if per_token else mod[:, None, :]
    sh1, sc1, gt1, sh2, sc2, gt2 = jnp.split(mods, 6, axis=-1)
    xg = x.reshape(groups, rows, D)
    tm_in = min(rows, 512)
    proj, h, kv_rows = _norm_in(xg, lp['norm1'], sc1, sh1, lp['w_in_b'], tm_in)
    kv_shape = (B, T, 2, NSA_KV_HEADS, HEAD_DIM)
    cmp_rows, slc_rows, win_rows = (kv_rows[n].reshape(B, T * KV_PARTS, HEAD_DIM) for n in range(KV_TILES))
    cmp_new, slc_new = cmp_rows.reshape(kv_shape), slc_rows.reshape(kv_shape)

    if pool_cmp is None:
        kvc = _compress_prompt(cmp_rows, _cmp_weights(lp))
        o_nsa = _nsa_prompt(proj.reshape(B * T, IN_WIDTH), kvc, B, T)
        win_len = min(WINDOW, past_len)
        assert T >= win_len
        win_state = win_rows[:, (T - win_len) * KV_PARTS:].reshape((B, win_len) + kv_shape[2:])
    else:
        n_past = page_table.shape[1] * PAGE_SIZE
        assert n_past % CMP_STRIDE == 0 and T < CMP_STRIDE and T <= 8
        assert -(-(n_past + T) // SEL_BLOCK) <= TOPK_LANES and n_past % SEL_BLOCK == 0 and T <= SEL_BLOCK
        assert win_buf.shape[1] == WINDOW
        pages = page_table + layer * pool_cmp.shape[1]
        kvc = _compress_paged(pool_cmp.reshape(-1, HEAD_DIM), pages, _cmp_weights(lp))
        win_buf_rows = win_buf.reshape(B, WINDOW * KV_PARTS, HEAD_DIM)
        o_nsa = _nsa_decode_rows(proj.reshape(B, T, IN_WIDTH), kvc, win_buf_rows, pool_slc.reshape(-1, HEAD_DIM),
                                 pages, n_past)
        win_state = jnp.concatenate([win_buf_rows[:, T * KV_PARTS:], win_rows], axis=1).reshape(
            (B, WINDOW) + kv_shape[2:])

    shift0 = shift0.astype(h.dtype)
    p0 = _matmul(shift0, lp['w_in_b'][:, RKV_OFF:CMP_OFF], B, RWKV_WIDTH)
    if per_token:
        h0 = jnp.repeat(shift0, T, axis=0).reshape(groups, rows, D)
        p0 = jnp.repeat(p0, T, axis=0).reshape(groups, rows, 3 * RWKV_WIDTH)
    else:
        h0, p0 = shift0[:, None], p0[:, None]
    tm_rw = min(rows, 256)
    r, w, k, v, kk, kka, gate, bonus = _rwkv_prep(h, proj, h0, p0, lp, tm_rw, T, pack_major=not per_token)
    if per_token:
        t_pad = -(-T // WKV_MIN_CHUNK) * WKV_MIN_CHUNK

        def pairs(a, fill):
            a = jnp.pad(a.reshape(B, T, RW_PACKS, WKV_LANES), ((0, 0), (0, t_pad - T), (0, 0), (0, 0)),
                        constant_values=fill)
            return a.transpose(0, 2, 1, 3)

        y, wkv_T = _wkv_scan(pairs(r, 0.0), pairs(w, 1.0), pairs(k, 0.0), pairs(v, 0.0), pairs(kk, 0.0),
                             pairs(kka, 0.0), wkv0.astype(jnp.float32), WKV_MIN_CHUNK)
        y = y.transpose(0, 2, 1, 3)[:, :T].reshape(groups, rows, RWKV_WIDTH)
    else:
        y, wkv_T = _wkv_scan(r, w, k, v, kk, kka, wkv0.astype(jnp.float32), WKV_CHUNK)
    o_rwkv = _rwkv_post(y, gate, bonus, lp['ln_x_w'], lp['ln_x_b'], tm_rw, pack_major=not per_token)

    x1, h2, logits = _mix_out(o_nsa.reshape(groups, rows, NSA_WIDTH), o_rwkv, lp['w_out_b'], xg, gt1, sc2, sh2,
                              lp['norm2'], lp['wr_hi'], lp['wr_lo'], lp['br'], min(rows, 256))
    return (x1, h2, logits, gt2), (cmp_new, slc_new, win_state, wkv_T, h.reshape(B, T, D)[:, -1])


def kernel(x_prompt, x_sample, c_prompt, c_sample, cache_cmp_kv, cache_slc_kv, page_table, state_win_kv, state_wkv, state_shift, w_ada, b_ada, norm1, w_in, cmp_k_w1, cmp_k_pe, cmp_k_w2, cmp_v_w1, cmp_v_pe, cmp_v_w2, mu_rkv, mu_wag, decay_w0, decay_w1, decay_w2, iclr_a0, iclr_a1, iclr_a2, gate_g1, gate_g2, k_k, k_a, r_k, ln_x_w, ln_x_b, w_out, norm2, w_router_group, b_router_group, w_router_expert, b_router_expert, w_gate, w_up, w_down, norm_f):
    bp, tp = x_prompt.shape[:2]
    ts = x_sample.shape[1]
    past_len = page_table.shape[1] * PAGE_SIZE
    pos_p = jnp.arange(tp)
    pos_s = past_len + jnp.arange(ts)
    assert DEPTH == 1 and w_in.shape[0] == 1
    l = 0
    bs = x_sample.shape[0]
    f32, bf16 = jnp.float32, jnp.bfloat16
    lyr = lambda a: a.reshape(a.shape[1:])
    wr = jnp.concatenate([lyr(w_router_group), lyr(w_router_expert),
                          jnp.zeros((D_MODEL, ROUTER_LANES - N_GROUPS - N_EXPERTS), f32)], axis=1)
    wr_hi = wr.astype(bf16)
    br = jnp.concatenate([lyr(b_router_group), lyr(b_router_expert),
                          jnp.zeros((ROUTER_LANES - N_GROUPS - N_EXPERTS,), f32)]).reshape(1, ROUTER_LANES)
    lp = dict(norm1=lyr(norm1), w_in_b=_permute_w_in(lyr(w_in)).astype(bf16),
              cmp_k_w1=lyr(cmp_k_w1), cmp_k_pe=lyr(cmp_k_pe), cmp_k_w2=lyr(cmp_k_w2),
              cmp_v_w1=lyr(cmp_v_w1), cmp_v_pe=lyr(cmp_v_pe), cmp_v_w2=lyr(cmp_v_w2),
              mu_rkv=lyr(mu_rkv), mu_wag=lyr(mu_wag), decay_w0=lyr(decay_w0), decay_w1=lyr(decay_w1),
              decay_w2=lyr(decay_w2), iclr_a0=lyr(iclr_a0), iclr_a1=lyr(iclr_a1), iclr_a2=lyr(iclr_a2),
              gate_g1=lyr(gate_g1), gate_g2=lyr(gate_g2), k_k=lyr(k_k), k_a=lyr(k_a), r_k=lyr(r_k),
              ln_x_w=lyr(ln_x_w), ln_x_b=lyr(ln_x_b), w_out_b=lyr(w_out).astype(bf16), norm2=lyr(norm2),
              wr_hi=wr_hi, wr_lo=(wr - wr_hi.astype(f32)).astype(bf16), br=br)

    c_all = jnp.concatenate([c_prompt, c_sample], axis=0)
    mod_all = _matmul(jax.nn.silu(c_all), lyr(w_ada), c_all.shape[0], 1024) + lyr(b_ada)
    mod_p, mod_s = mod_all[:bp], mod_all[bp:]

    wkv_zero = jnp.zeros((bp, RWKV_HEADS, RWKV_HEAD_DIM, RWKV_HEAD_DIM), f32)
    shift_zero = jnp.zeros((bp, D_MODEL), x_prompt.dtype)
    (x1p, h2p, lgp, gt2p), (a1, a2, a3, a4, a5) = _layer_front(
        x_prompt, mod_p, pos_p, lp, l, None, None, None, None, wkv_zero, shift_zero, past_len, tp)
    (x1s, h2s, lgs, gt2s), (b1, b2, b3, b4, b5) = _layer_front(
        x_sample, mod_s, pos_s, lp, l, cache_cmp_kv, cache_slc_kv, page_table, lyr(state_win_kv), lyr(state_wkv),
        lyr(state_shift), past_len, bs * ts)

    n_p, n_s = bp * tp, bs * ts
    h2 = jnp.concatenate([h2p.reshape(n_p, D_MODEL), h2s.reshape(n_s, D_MODEL), jnp.zeros((1, D_MODEL), bf16)], axis=0)
    eid, ew = _hier_route(jnp.concatenate([lgp.reshape(n_p, ROUTER_LANES), lgs.reshape(n_s, ROUTER_LANES)], axis=0))
    y2 = _moe_ffn(h2, eid, lyr(w_gate), lyr(w_up), lyr(w_down))
    y_prompt = _final(x1p, y2[:n_p].reshape(x1p.shape[:2] + (-1,)), ew[:n_p].reshape(x1p.shape[:2] + (-1,)),
                      gt2p, norm_f, 256).reshape(x_prompt.shape)
    y_sample = _final(x1s, y2[n_p:].reshape(x1s.shape[:2] + (-1,)), ew[n_p:].reshape(x1s.shape[:2] + (-1,)),
                      gt2s, norm_f, x1s.shape[1]).reshape(x_sample.shape)
    st = lambda a: a[None]
    return (y_prompt, y_sample, st(a1), st(b1), st(a2), st(b2), st(a3), st(b3), st(a4), st(b4), st(a5), st(b5))
```

```python
import functools

import jax
import jax.numpy as jnp
from jax import lax
from jax.experimental import pallas as pl
from jax.experimental.pallas import tpu as pltpu

D_MODEL = 2048
DEPTH = 1
PAGE_SIZE = 128
HEAD_DIM = 128
NSA_WIDTH = D_MODEL // 2
NSA_HEADS = NSA_WIDTH // HEAD_DIM
NSA_KV_HEADS = 2
HPG = NSA_HEADS // NSA_KV_HEADS
KV_WIDTH = NSA_KV_HEADS * HEAD_DIM
CMP_BLOCK = 32
CMP_STRIDE = 16
SEL_BLOCK = 64
N_SELECT = 16
WINDOW = 512
FORCE_BONUS = 1e4
SEL_QBLOCK = 32
WIN_QBLOCK = 128
RWKV_WIDTH = D_MODEL - NSA_WIDTH
RWKV_HEAD_DIM = 64
RWKV_HEADS = RWKV_WIDTH // RWKV_HEAD_DIM
GN_EPS = 64e-5
N_GROUPS = 4
EXPERTS_PER_GROUP = 8
N_EXPERTS = N_GROUPS * EXPERTS_PER_GROUP
EXPERT_TOP_K = 2
D_EXPERT = 512
MOE_BLOCK = 64
NORM_EPS = 1e-6
NEG_INF = -1e30
W_Q_OFF = 0
W_CMP_OFF = W_Q_OFF + NSA_WIDTH
W_RKV_OFF = W_CMP_OFF + 6 * KV_WIDTH
W_GATE_OFF = W_RKV_OFF + 3 * RWKV_WIDTH
IN_WIDTH = W_GATE_OFF + 3 * NSA_HEADS
Q_OFF = 0
RKV_OFF = Q_OFF + NSA_WIDTH
CMP_OFF = RKV_OFF + 3 * RWKV_WIDTH
SLC_OFF = CMP_OFF + 2 * KV_WIDTH
WIN_OFF = SLC_OFF + 2 * KV_WIDTH
GATE_OFF = WIN_OFF + 2 * KV_WIDTH
assert GATE_OFF == W_GATE_OFF


def _permute_w_in(w):
    return jnp.concatenate([w[:, W_Q_OFF:W_CMP_OFF], w[:, W_RKV_OFF:W_GATE_OFF], w[:, W_CMP_OFF:W_RKV_OFF],
                            w[:, W_GATE_OFF:]], axis=1)

VMEM_LIMIT_BYTES = 48 * 1024 * 1024


def _mm_kernel(x_ref, w_ref, o_ref):
    o_ref[...] = jnp.dot(x_ref[...].astype(jnp.bfloat16), w_ref[...].astype(jnp.bfloat16),
                         preferred_element_type=jnp.float32)


def _matmul(x, w, tm, tn):
    m, k = x.shape
    n = w.shape[1]
    return pl.pallas_call(
        _mm_kernel,
        grid=(pl.cdiv(m, tm), pl.cdiv(n, tn)),
        in_specs=[pl.BlockSpec((tm, k), lambda i, j: (i, 0)),
                  pl.BlockSpec((k, tn), lambda i, j: (0, j))],
        out_specs=pl.BlockSpec((tm, tn), lambda i, j: (i, j)),
        out_shape=jax.ShapeDtypeStruct((m, n), jnp.float32),
        compiler_params=pltpu.CompilerParams(
            dimension_semantics=("arbitrary", "arbitrary"), vmem_limit_bytes=VMEM_LIMIT_BYTES),
        name="matmul",
    )(x, w)


MOE_TM = 256


def _moe_block_kernel(meta_ref, x_ref, wg_ref, wu_ref, wd_ref, o_ref, wg_b, wu_b, wd_b):
    bf16 = jnp.bfloat16
    i = pl.program_id(0)
    used = i < meta_ref[pl.num_programs(0)]
    new_expert = (i == 0) | (meta_ref[i] != meta_ref[jnp.maximum(i - 1, 0)])

    @pl.when(used & new_expert)
    def _():
        wg_b[...] = wg_ref[0].astype(bf16)
        wu_b[...] = wu_ref[0].astype(bf16)
        wd_b[...] = wd_ref[0].astype(bf16)

    @pl.when(used)
    def _():
        x = x_ref[0]
        g = jnp.dot(x, wg_b[...], preferred_element_type=jnp.float32)
        u = jnp.dot(x, wu_b[...], preferred_element_type=jnp.float32)
        hmid = (g * jax.nn.sigmoid(g)) * u
        o_ref[0] = jnp.dot(hmid.astype(bf16), wd_b[...], preferred_element_type=jnp.float32)

    @pl.when(jnp.logical_not(used))
    def _():
        o_ref[...] = jnp.zeros(o_ref.shape, o_ref.dtype)


def _moe_blocks(xb, meta, w_gate, w_up, w_down):
    n_blk, mb, d = xb.shape
    de = w_gate.shape[2]
    grid_spec = pltpu.PrefetchScalarGridSpec(
        num_scalar_prefetch=1,
        grid=(n_blk,),
        in_specs=[pl.BlockSpec((1, mb, d), lambda i, e: (i, 0, 0)),
                  pl.BlockSpec((1, d, de), lambda i, e: (e[i], 0, 0)),
                  pl.BlockSpec((1, d, de), lambda i, e: (e[i], 0, 0)),
                  pl.BlockSpec((1, de, d), lambda i, e: (e[i], 0, 0))],
        out_specs=pl.BlockSpec((1, mb, d), lambda i, e: (i, 0, 0)),
        scratch_shapes=[pltpu.VMEM((d, de), jnp.bfloat16), pltpu.VMEM((d, de), jnp.bfloat16),
                        pltpu.VMEM((de, d), jnp.bfloat16)],
    )
    return pl.pallas_call(
        _moe_block_kernel,
        grid_spec=grid_spec,
        out_shape=jax.ShapeDtypeStruct((n_blk, mb, d), jnp.float32),
        compiler_params=pltpu.CompilerParams(
            dimension_semantics=("arbitrary",), vmem_limit_bytes=VMEM_LIMIT_BYTES),
        name="moe_blocks",
    )(meta, xb, w_gate, w_up, w_down)


ROUTER_LANES = 128


KV_TILES = 3
KV_PARTS = 2 * NSA_KV_HEADS


def _norm_in_kernel(x_ref, n1_ref, sc_ref, sh_ref, w_ref, proj_ref, h_ref, kv_ref, hb_scr, *, kv_tile0):
    j = pl.program_id(2)

    @pl.when(j == 0)
    def _():
        x = x_ref[0]
        y = x * lax.rsqrt(jnp.mean(x * x, axis=-1, keepdims=True) + NORM_EPS) * n1_ref[...]
        h = y * (1.0 + sc_ref[0]) + sh_ref[0]
        h_ref[0] = h
        hb_scr[...] = h.astype(jnp.bfloat16)

    proj_ref[0] = jnp.dot(hb_scr[...], w_ref[...], preferred_element_type=jnp.float32)

    @pl.when((j >= kv_tile0) & (j < kv_tile0 + KV_TILES))
    def _():
        tm = proj_ref.shape[1]
        for part in range(KV_PARTS):
            kv_ref[0, 0, pl.ds(part, tm, stride=KV_PARTS), :] = proj_ref[0, :, part * HEAD_DIM:(part + 1) * HEAD_DIM]


def _norm_in(x, norm1, sc, sh, w_in_b, tm):
    g, t, d = x.shape
    n = w_in_b.shape[1]
    tn = 2 * KV_WIDTH
    kv_tile0 = CMP_OFF // tn
    assert CMP_OFF % tn == 0 and GATE_OFF == CMP_OFF + KV_TILES * tn
    mrows = sc.shape[1]
    mod_spec = pl.BlockSpec((1, mrows if mrows == 1 else tm, d),
                            (lambda b, i, j: (b, 0, 0)) if mrows == 1 else (lambda b, i, j: (b, i, 0)))
    return pl.pallas_call(
        functools.partial(_norm_in_kernel, kv_tile0=kv_tile0),
        grid=(g, t // tm, pl.cdiv(n, tn)),
        in_specs=[pl.BlockSpec((1, tm, d), lambda b, i, j: (b, i, 0)),
                  pl.BlockSpec((1, d), lambda b, i, j: (0, 0)),
                  mod_spec, mod_spec,
                  pl.BlockSpec((d, tn), lambda b, i, j: (0, j))],
        out_specs=[pl.BlockSpec((1, tm, tn), lambda b, i, j: (b, i, j)),
                   pl.BlockSpec((1, tm, d), lambda b, i, j: (b, i, 0)),
                   pl.BlockSpec((1, 1, tm * KV_PARTS, HEAD_DIM),
                                lambda b, i, j: (jnp.clip(j - kv_tile0, 0, KV_TILES - 1), b, i, 0))],
        out_shape=[jax.ShapeDtypeStruct((g, t, n), jnp.float32), jax.ShapeDtypeStruct((g, t, d), jnp.float32),
                   jax.ShapeDtypeStruct((KV_TILES, g, t * KV_PARTS, HEAD_DIM), jnp.float32)],
        scratch_shapes=[pltpu.VMEM((tm, d), jnp.bfloat16)],
        compiler_params=pltpu.CompilerParams(
            dimension_semantics=("arbitrary", "arbitrary", "arbitrary"), vmem_limit_bytes=VMEM_LIMIT_BYTES),
        name="norm_in",
    )(x, norm1.reshape(1, d), sc, sh, w_in_b)


def _mix_out_kernel(on_ref, orw_ref, w_ref, x_ref, gt_ref, sc_ref, sh_ref, n2_ref, wr_hi_ref, wr_lo_ref, br_ref,
                    x1_ref, h2_ref, lg_ref):
    f32, bf16 = jnp.float32, jnp.bfloat16
    half = on_ref.shape[2]
    mixed = (jnp.dot(on_ref[0].astype(bf16), w_ref[0:half, :], preferred_element_type=f32)
             + jnp.dot(orw_ref[0].astype(bf16), w_ref[half:, :], preferred_element_type=f32))
    x1 = x_ref[0] + gt_ref[0] * mixed
    x1_ref[0] = x1
    y = x1 * lax.rsqrt(jnp.mean(x1 * x1, axis=-1, keepdims=True) + NORM_EPS) * n2_ref[...]
    h2 = y * (1.0 + sc_ref[0]) + sh_ref[0]
    hi = h2.astype(bf16)
    h2_ref[0] = hi
    lo = (h2 - hi.astype(f32)).astype(bf16)
    lg_ref[0] = (jnp.dot(hi, wr_hi_ref[...], preferred_element_type=f32)
                 + jnp.dot(hi, wr_lo_ref[...], preferred_element_type=f32)
                 + jnp.dot(lo, wr_hi_ref[...], preferred_element_type=f32) + br_ref[...])


def _mix_out(o_nsa, o_rwkv, w_out_b, x, gt, sc, sh, norm2, wr_hi, wr_lo, br, tm):
    g, t, d = x.shape
    half = o_nsa.shape[2]
    mrows = sc.shape[1]
    mod_spec = pl.BlockSpec((1, mrows if mrows == 1 else tm, d),
                            (lambda b, i: (b, 0, 0)) if mrows == 1 else (lambda b, i: (b, i, 0)))
    row = lambda w: pl.BlockSpec((1, tm, w), lambda b, i: (b, i, 0))
    full = lambda a: pl.BlockSpec(a.shape, lambda b, i: (0,) * a.ndim)
    n2 = norm2.reshape(1, d)
    return pl.pallas_call(
        _mix_out_kernel,
        grid=(g, t // tm),
        in_specs=[row(half), row(half), full(w_out_b), row(d), mod_spec, mod_spec, mod_spec, full(n2),
                  full(wr_hi), full(wr_lo), full(br)],
        out_specs=[row(d), row(d), row(ROUTER_LANES)],
        out_shape=[jax.ShapeDtypeStruct((g, t, d), jnp.float32), jax.ShapeDtypeStruct((g, t, d), jnp.bfloat16),
                   jax.ShapeDtypeStruct((g, t, ROUTER_LANES), jnp.float32)],
        compiler_params=pltpu.CompilerParams(
            dimension_semantics=("arbitrary", "arbitrary"), vmem_limit_bytes=VMEM_LIMIT_BYTES),
        name="mix_out",
    )(o_nsa, o_rwkv, w_out_b, x, gt, sc, sh, n2, wr_hi, wr_lo, br)


def _final_kernel(x_ref, y0_ref, y1_ref, ew_ref, gt_ref, nf_ref, o_ref):
    ew = ew_ref[...]
    ffn = y0_ref[...] * ew[:, 0:1] + y1_ref[...] * ew[:, 1:2]
    x2 = x_ref[0] + gt_ref[0] * ffn
    o_ref[0] = x2 * lax.rsqrt(jnp.mean(x2 * x2, axis=-1, keepdims=True) + NORM_EPS) * nf_ref[...]


def _final(x1, y0, y1, ew, row_off, gt, norm_f, tm):
    g, t, d = x1.shape
    assert row_off % tm == 0
    mrows = gt.shape[1]
    mod_spec = pl.BlockSpec((1, mrows if mrows == 1 else tm, d),
                            (lambda b, i: (b, 0, 0)) if mrows == 1 else (lambda b, i: (b, i, 0)))
    row = pl.BlockSpec((1, tm, d), lambda b, i: (b, i, 0))
    flat = lambda w: pl.BlockSpec((tm, w), lambda b, i: (row_off // tm + b * (t // tm) + i, 0))
    return pl.pallas_call(
        _final_kernel,
        grid=(g, t // tm),
        in_specs=[row, flat(d), flat(d), flat(EXPERT_TOP_K), mod_spec, pl.BlockSpec((1, d), lambda b, i: (0, 0))],
        out_specs=row,
        out_shape=jax.ShapeDtypeStruct((g, t, d), jnp.float32),
        compiler_params=pltpu.CompilerParams(
            dimension_semantics=("arbitrary", "arbitrary"), vmem_limit_bytes=VMEM_LIMIT_BYTES),
        name="final_norm",
    )(x1, y0, y1, ew, gt, norm_f.reshape(1, d))


def _cmp_partial_kernel(*refs, n_src, rows_per_src):
    x_refs = refs[:n_src]
    w1k_ref, w1v_ref, a_ref, b_ref = refs[n_src:]
    nch_src = rows_per_src // CMP_STRIDE
    for kvg in range(4):
        w_ref = w1k_ref if kvg < 2 else w1v_ref
        acc = None
        for p in range(CMP_STRIDE):
            parts = [x_refs[s][pl.ds(4 * p + kvg, nch_src, stride=4 * CMP_STRIDE), :] for s in range(n_src)]
            xp = parts[0] if n_src == 1 else jnp.concatenate(parts, axis=0)
            d = jnp.dot(xp.astype(jnp.bfloat16), w_ref[p], preferred_element_type=jnp.float32)
            acc = d if acc is None else acc + d
        a_ref[0, :, kvg * HEAD_DIM:(kvg + 1) * HEAD_DIM] = acc[:, :HEAD_DIM]
        b_ref[0, :, kvg * HEAD_DIM:(kvg + 1) * HEAD_DIM] = acc[:, HEAD_DIM:]


def _cmp_finish_kernel(a_ref, b_ref, pek_ref, pev_ref, w1k_ref, w1v_ref, w2k_ref, w2v_ref, o_ref):
    nch = a_ref.shape[1]
    for kv, (pe_ref, w1_ref, w2_ref) in enumerate(((pek_ref, w1k_ref, w2k_ref), (pev_ref, w1v_ref, w2v_ref))):
        pe8 = jnp.broadcast_to(pe_ref[...], (8, pe_ref.shape[1])).astype(jnp.bfloat16)
        pterm = jnp.dot(pe8, w1_ref[...], preferred_element_type=jnp.float32)[0:1]
        w2 = w2_ref[...]
        for g in range(NSA_KV_HEADS):
            lo = (kv * NSA_KV_HEADS + g) * HEAD_DIM
            nxt = pltpu.roll(b_ref[0, :, lo:lo + HEAD_DIM], nch - 1, 0)
            pre = a_ref[0, :, lo:lo + HEAD_DIM] + nxt + pterm
            act = pre * jax.nn.sigmoid(pre)
            o_ref[0, :, lo:lo + HEAD_DIM] = jnp.dot(act.astype(jnp.bfloat16), w2, preferred_element_type=jnp.float32)


def _cmp_weights(lp):
    bf = jnp.bfloat16
    half = CMP_BLOCK // 2
    cat = lambda w: jnp.concatenate([w[:half], w[half:]], axis=-1).astype(bf)
    flat = lambda w: w.reshape(CMP_BLOCK * HEAD_DIM, HEAD_DIM).astype(bf)
    return dict(w1k_cat=cat(lp['cmp_k_w1']), w1v_cat=cat(lp['cmp_v_w1']),
                w1k_flat=flat(lp['cmp_k_w1']), w1v_flat=flat(lp['cmp_v_w1']),
                pek=lp['cmp_k_pe'].reshape(1, -1), pev=lp['cmp_v_pe'].reshape(1, -1),
                w2k=lp['cmp_k_w2'].astype(bf), w2v=lp['cmp_v_w2'].astype(bf))


def _cmp_finish(a, b, cw):
    bsz, nch, _ = a.shape
    full = lambda arr: pl.BlockSpec(arr.shape, lambda i: (0,) * arr.ndim)
    blk = pl.BlockSpec((1, nch, 4 * HEAD_DIM), lambda i: (i, 0, 0))
    ws = [cw['pek'], cw['pev'], cw['w1k_flat'], cw['w1v_flat'], cw['w2k'], cw['w2v']]
    return pl.pallas_call(
        _cmp_finish_kernel,
        grid=(bsz,),
        in_specs=[blk, blk] + [full(w) for w in ws],
        out_specs=blk,
        out_shape=jax.ShapeDtypeStruct((bsz, nch, 4 * HEAD_DIM), jnp.float32),
        compiler_params=pltpu.CompilerParams(dimension_semantics=("arbitrary",), vmem_limit_bytes=VMEM_LIMIT_BYTES),
        name="cmp_finish",
    )(a, b, *ws)


def _compress_prompt(rows4, cw):
    bsz, seq = rows4.shape[0], rows4.shape[1] // 4
    nch = seq // CMP_STRIDE
    x_specs = [pl.BlockSpec((None, seq * 4, HEAD_DIM), lambda i: (i, 0, 0))]
    w_spec = pl.BlockSpec(cw['w1k_cat'].shape, lambda i: (0, 0, 0))
    out_spec = pl.BlockSpec((1, nch, 4 * HEAD_DIM), lambda i: (i, 0, 0))
    shp = jax.ShapeDtypeStruct((bsz, nch, 4 * HEAD_DIM), jnp.float32)
    a, b = pl.pallas_call(
        functools.partial(_cmp_partial_kernel, n_src=1, rows_per_src=seq),
        grid=(bsz,),
        in_specs=x_specs + [w_spec, w_spec],
        out_specs=[out_spec, out_spec],
        out_shape=[shp, shp],
        compiler_params=pltpu.CompilerParams(dimension_semantics=("arbitrary",), vmem_limit_bytes=VMEM_LIMIT_BYTES),
        name="cmp_partial_prompt",
    )(rows4, cw['w1k_cat'], cw['w1v_cat'])
    return _cmp_finish(a, b, cw)


_NT = (((1,), (1,)), ((), ()))
SEL_TK = 512
WIN_TK = 256


def _flash_update(s, v, m_ref, l_ref, acc_ref, h):
    tk = s.shape[1]
    m_prev = m_ref[h]
    m_new = jnp.maximum(m_prev, jnp.max(s, axis=-1, keepdims=True))
    alpha = jnp.exp(m_prev - m_new)
    p = jnp.exp(s - jnp.concatenate([m_new] * (tk // HEAD_DIM), axis=1))
    l_ref[h] = alpha * l_ref[h] + jnp.sum(p, axis=-1, keepdims=True)
    acc_ref[h] = alpha * acc_ref[h] + jnp.dot(p.astype(jnp.bfloat16), v, preferred_element_type=jnp.float32)
    m_ref[h] = m_new


def _nsa_prompt_kernel(q_ref, slc_ref, win_ref, gate_ref, kvc_ref, o_ref, m_ref, l_ref, acc_ref, *, tq, seq):
    f32, bf16 = jnp.float32, jnp.bfloat16
    qi = pl.program_id(1)
    t0 = qi * tq
    scale = HEAD_DIM ** -0.5
    nc_valid = seq // CMP_STRIDE - CMP_BLOCK // CMP_STRIDE + 1
    n_sel = seq // SEL_BLOCK
    pos = t0 + lax.broadcasted_iota(jnp.int32, (tq, 1), 0)
    lane = lax.broadcasted_iota(jnp.int32, (1, HEAD_DIM), 1)
    blk_t = jnp.right_shift(pos, 6)
    gates = jax.nn.sigmoid(gate_ref[...])
    dist_c = pos - (lane * CMP_STRIDE + (CMP_BLOCK - 1))
    valid_c = (dist_c >= 0) & (lane < nc_valid)
    dist_cf = dist_c.astype(f32)
    c_row = lax.broadcasted_iota(jnp.int32, (HEAD_DIM, 1), 0)
    overlap = jnp.where((c_row * CMP_STRIDE <= lane * SEL_BLOCK + (SEL_BLOCK - 1))
                        & (c_row * CMP_STRIDE + (CMP_BLOCK - 1) >= lane * SEL_BLOCK), 1.0, 0.0).astype(bf16)
    forced = (lane == 0) | (lane == blk_t) | (lane == blk_t - 1)

    def gate_col(branch, hh):
        c = branch * NSA_HEADS + hh
        return gates[:, c:c + 1]

    def reset():
        m_ref[...] = jnp.full(m_ref.shape, NEG_INF, f32)
        l_ref[...] = jnp.zeros(l_ref.shape, f32)
        acc_ref[...] = jnp.zeros(acc_ref.shape, f32)

    for g in range(NSA_KV_HEADS):
        kcol = slice(g * HEAD_DIM, (g + 1) * HEAD_DIM)
        vcol = slice((NSA_KV_HEADS + g) * HEAD_DIM, (NSA_KV_HEADS + g + 1) * HEAD_DIM)
        heads = [g * HPG + h for h in range(HPG)]
        slopes = [2.0 ** -(hh + 1) for hh in heads]

        kc = kvc_ref[0, :, kcol].astype(bf16)
        vc = kvc_ref[0, :, vcol].astype(bf16)
        psum = jnp.zeros((tq, HEAD_DIM), f32)
        for h, hh in enumerate(heads):
            qh = q_ref[:, hh * HEAD_DIM:(hh + 1) * HEAD_DIM].astype(bf16)
            s = lax.dot_general(qh, kc, _NT, preferred_element_type=f32) * scale - slopes[h] * dist_cf
            s = jnp.where(valid_c, s, NEG_INF)
            e = jnp.exp(s - jnp.max(s, axis=-1, keepdims=True))
            p = e / jnp.sum(e, axis=-1, keepdims=True)
            p = jnp.where(valid_c, p, 0.0)
            o_cmp = jnp.dot(p.astype(bf16), vc, preferred_element_type=f32)
            o_ref[:, hh * HEAD_DIM:(hh + 1) * HEAD_DIM] = gate_col(0, hh) * o_cmp
            psum = psum + p
        p_hi = psum.astype(bf16)
        p_lo = (psum - p_hi.astype(f32)).astype(bf16)
        imp = (jnp.dot(p_hi, overlap, preferred_element_type=f32)
               + jnp.dot(p_lo, overlap, preferred_element_type=f32))
        imp = jnp.where(forced, imp + FORCE_BONUS, imp)
        imp = jnp.where(lane <= blk_t, imp, NEG_INF)
        beaten = jnp.zeros((tq, HEAD_DIM), f32)
        for jp in range(n_sel):
            col = imp[:, jp:jp + 1]
            tie = jnp.where(lane > jp, 1.0, 0.0)
            beaten = beaten + jnp.where(col > imp, 1.0, jnp.where(col == imp, tie, 0.0))
        sel = jnp.where(beaten < N_SELECT, jnp.where(imp > 0.5 * NEG_INF, 1.0, 0.0), 0.0).astype(bf16)

        reset()
        j_row = lax.broadcasted_iota(jnp.int32, (HEAD_DIM, 1), 0)

        def sel_body(kt, carry):
            k0 = pl.multiple_of(kt * SEL_TK, SEL_TK)
            k = slc_ref[pl.ds(k0, SEL_TK), kcol].astype(bf16)
            v = slc_ref[pl.ds(k0, SEL_TK), vcol].astype(bf16)
            kpos = k0 + lax.broadcasted_iota(jnp.int32, (1, SEL_TK), 1)
            dist = pos - kpos
            expand = jnp.where(jnp.right_shift(kpos, 6) == j_row, 1.0, 0.0).astype(bf16)
            picked = jnp.dot(sel, expand, preferred_element_type=f32)
            keep = jnp.where(dist >= 0, picked, 0.0) > 0.5
            dist_f = dist.astype(f32)
            for h, hh in enumerate(heads):
                qh = q_ref[:, hh * HEAD_DIM:(hh + 1) * HEAD_DIM].astype(bf16)
                s = lax.dot_general(qh, k, _NT, preferred_element_type=f32) * scale - slopes[h] * dist_f
                _flash_update(jnp.where(keep, s, NEG_INF), v, m_ref, l_ref, acc_ref, h)
            return carry

        lax.fori_loop(0, (t0 + tq - 1) // SEL_TK + 1, sel_body, 0)
        for h, hh in enumerate(heads):
            hs = slice(hh * HEAD_DIM, (hh + 1) * HEAD_DIM)
            o_ref[:, hs] = o_ref[:, hs] + gate_col(1, hh) * (acc_ref[h] / l_ref[h])

        reset()

        def win_body(kt, carry):
            k0 = pl.multiple_of(kt * WIN_TK, WIN_TK)
            k = win_ref[pl.ds(k0, WIN_TK), kcol].astype(bf16)
            v = win_ref[pl.ds(k0, WIN_TK), vcol].astype(bf16)
            dist = pos - (k0 + lax.broadcasted_iota(jnp.int32, (1, WIN_TK), 1))
            keep = (dist >= 0) & (dist < WINDOW)
            dist_f = dist.astype(f32)
            for h, hh in enumerate(heads):
                qh = q_ref[:, hh * HEAD_DIM:(hh + 1) * HEAD_DIM].astype(bf16)
                s = lax.dot_general(qh, k, _NT, preferred_element_type=f32) * scale - slopes[h] * dist_f
                _flash_update(jnp.where(keep, s, NEG_INF), v, m_ref, l_ref, acc_ref, h)
            return carry

        lax.fori_loop(jnp.maximum(t0 - (WINDOW - 1), 0) // WIN_TK, (t0 + tq - 1) // WIN_TK + 1, win_body, 0)
        for h, hh in enumerate(heads):
            hs = slice(hh * HEAD_DIM, (hh + 1) * HEAD_DIM)
            o_ref[:, hs] = o_ref[:, hs] + gate_col(2, hh) * (acc_ref[h] / l_ref[h])


def _nsa_prompt(proj2d, kvc, bsz, seq, tq=256):
    nq = seq // tq
    kvw = 2 * KV_WIDTH
    return pl.pallas_call(
        functools.partial(_nsa_prompt_kernel, tq=tq, seq=seq),
        grid=(bsz, nq),
        in_specs=[pl.BlockSpec((tq, NSA_WIDTH), lambda b, i: (b * nq + i, 0)),
                  pl.BlockSpec((seq, kvw), lambda b, i: (b, SLC_OFF // kvw)),
                  pl.BlockSpec((seq, kvw), lambda b, i: (b, WIN_OFF // kvw)),
                  pl.BlockSpec((tq, HEAD_DIM), lambda b, i: (b * nq + i, GATE_OFF // HEAD_DIM)),
                  pl.BlockSpec((1, seq // CMP_STRIDE, kvw), lambda b, i: (b, 0, 0))],
        out_specs=pl.BlockSpec((tq, NSA_WIDTH), lambda b, i: (b * nq + i, 0)),
        out_shape=jax.ShapeDtypeStruct((bsz * seq, NSA_WIDTH), jnp.float32),
        scratch_shapes=[pltpu.VMEM((HPG, tq, HEAD_DIM), jnp.float32),
                        pltpu.VMEM((HPG, tq, HEAD_DIM), jnp.float32),
                        pltpu.VMEM((HPG, tq, HEAD_DIM), jnp.float32)],
        compiler_params=pltpu.CompilerParams(
            dimension_semantics=("arbitrary", "arbitrary"), vmem_limit_bytes=VMEM_LIMIT_BYTES),
        name="nsa_prompt",
    )(proj2d, proj2d, proj2d, proj2d, kvc)


CMP_PAGES_PER_STEP = 32
TOPK_LANES = 384
IDX_LANES = 128


def _cmp_partial_paged_kernel(pt_ref, *refs, n_src, rows_per_src):
    del pt_ref
    _cmp_partial_kernel(*refs, n_src=n_src, rows_per_src=rows_per_src)


def _compress_paged(pool, page_table, cw):
    bsz, n_pages = page_table.shape
    nps = CMP_PAGES_PER_STEP
    n_tiles = n_pages // nps
    nch_tile = nps * PAGE_SIZE // CMP_STRIDE

    def page_spec(s):
        return pl.BlockSpec((PAGE_SIZE * 4, HEAD_DIM), lambda b, i, pt: (pt[b * n_pages + i * nps + s], 0))

    x_specs = [page_spec(s) for s in range(nps)]
    w_spec = pl.BlockSpec(cw['w1k_cat'].shape, lambda b, i, pt: (0, 0, 0))
    out_spec = pl.BlockSpec((1, nch_tile, 4 * HEAD_DIM), lambda b, i, pt: (b, i, 0))
    shp = jax.ShapeDtypeStruct((bsz, n_tiles * nch_tile, 4 * HEAD_DIM), jnp.float32)
    a, b = pl.pallas_call(
        functools.partial(_cmp_partial_paged_kernel, n_src=nps, rows_per_src=PAGE_SIZE),
        grid_spec=pltpu.PrefetchScalarGridSpec(
            num_scalar_prefetch=1, grid=(bsz, n_tiles),
            in_specs=x_specs + [w_spec, w_spec], out_specs=[out_spec, out_spec]),
        out_shape=[shp, shp],
        compiler_params=pltpu.CompilerParams(
            dimension_semantics=("arbitrary", "arbitrary"), vmem_limit_bytes=VMEM_LIMIT_BYTES),
        name="cmp_partial_paged",
    )(page_table.reshape(-1), *([pool] * nps), cw['w1k_cat'], cw['w1v_cat'])
    return _cmp_finish(a, b, cw)


def _nsa_decode_a_kernel(proj_ref, kvc_ref, win_ref, o_ref, gsel_ref, idx_ref, *, t_new, n_past):
    f32, bf16 = jnp.float32, jnp.bfloat16
    scale = HEAD_DIM ** -0.5
    nch = kvc_ref.shape[1]
    n_win = win_ref.shape[1] // KV_PARTS
    rows = HPG * t_new
    r_iota = lax.broadcasted_iota(jnp.int32, (rows, 1), 0)
    t_row = r_iota % t_new
    h_row = r_iota // t_new
    pos_row = n_past + t_row
    gates = jax.nn.sigmoid(proj_ref[0, :, GATE_OFF:IN_WIDTH])
    c_lane = lax.broadcasted_iota(jnp.int32, (1, nch), 1)
    dist_c = pos_row - (c_lane * CMP_STRIDE + (CMP_BLOCK - 1))
    valid_c = (dist_c >= 0) & (c_lane < nch - 1)
    c_col = lax.broadcasted_iota(jnp.int32, (nch, 1), 0)
    j_lane = lax.broadcasted_iota(jnp.int32, (1, TOPK_LANES), 1)
    overlap = jnp.where((c_col * CMP_STRIDE <= j_lane * SEL_BLOCK + (SEL_BLOCK - 1))
                        & (c_col * CMP_STRIDE + (CMP_BLOCK - 1) >= j_lane * SEL_BLOCK), 1.0, 0.0).astype(bf16)
    pos_t = n_past + lax.broadcasted_iota(jnp.int32, (t_new, 1), 0)
    blk_t = pos_t // SEL_BLOCK
    forced = (j_lane == 0) | (j_lane == blk_t) | (j_lane == blk_t - 1)
    j_f = j_lane.astype(f32)
    k_lane = lax.broadcasted_iota(jnp.int32, (1, IDX_LANES), 1)
    i_win = lax.broadcasted_iota(jnp.int32, (1, n_win), 1)
    dist_w = pos_row - (n_past - n_win + i_win)
    keep_w = (dist_w >= 0) & (dist_w < WINDOW)
    j_new = lax.broadcasted_iota(jnp.int32, (1, 8), 1)
    dist_n = t_row - j_new
    keep_n = (dist_n >= 0) & (j_new < t_new)
    zpad = jnp.zeros((8 - t_new, HEAD_DIM), f32)

    for g in range(NSA_KV_HEADS):
        kcol = slice(g * HEAD_DIM, (g + 1) * HEAD_DIM)
        vcol = slice((NSA_KV_HEADS + g) * HEAD_DIM, (NSA_KV_HEADS + g + 1) * HEAD_DIM)
        heads = [g * HPG + h for h in range(HPG)]
        slope_row = jnp.zeros((rows, 1), f32)
        for h, hh in enumerate(heads):
            slope_row = jnp.where(h_row == h, 2.0 ** -(hh + 1), slope_row)
        q = jnp.concatenate([proj_ref[0, :, hh * HEAD_DIM:(hh + 1) * HEAD_DIM] for hh in heads], axis=0).astype(bf16)

        kc = kvc_ref[0, :, kcol].astype(bf16)
        vc = kvc_ref[0, :, vcol].astype(bf16)
        s = lax.dot_general(q, kc, _NT, preferred_element_type=f32) * scale - slope_row * dist_c.astype(f32)
        s = jnp.where(valid_c, s, NEG_INF)
        e = jnp.exp(s - jnp.max(s, axis=-1, keepdims=True))
        p = e / jnp.sum(e, axis=-1, keepdims=True)
        p = jnp.where(valid_c, p, 0.0)
        o_cmp = jnp.dot(p.astype(bf16), vc, preferred_element_type=f32)
        psum = p[0:t_new]
        for h in range(1, HPG):
            psum = psum + p[h * t_new:(h + 1) * t_new]

        p_hi = psum.astype(bf16)
        p_lo = (psum - p_hi.astype(f32)).astype(bf16)
        imp = (jnp.dot(p_hi, overlap, preferred_element_type=f32)
               + jnp.dot(p_lo, overlap, preferred_element_type=f32))
        imp = jnp.where(forced, imp + FORCE_BONUS, imp)
        imp = jnp.where(j_lane <= blk_t, imp, NEG_INF)
        picked = jnp.full((t_new, IDX_LANES), -1.0, f32)
        for k in range(N_SELECT):
            best = jnp.max(imp, axis=-1, keepdims=True)
            first = jnp.min(jnp.where(imp == best, j_f, 1e9), axis=-1, keepdims=True)
            picked = jnp.where(k_lane == k, jnp.where(best > 0.5 * NEG_INF, first, -1.0), picked)
            imp = jnp.where(j_f == first, -3e38, imp)
        idx_ref[0, g * t_new:(g + 1) * t_new, :] = picked.astype(jnp.int32)

        kw = win_ref[0, pl.ds(g, n_win, stride=KV_PARTS), :].astype(bf16)
        vw = win_ref[0, pl.ds(NSA_KV_HEADS + g, n_win, stride=KV_PARTS), :].astype(bf16)
        kn = jnp.concatenate([proj_ref[0, :, WIN_OFF + g * HEAD_DIM:WIN_OFF + (g + 1) * HEAD_DIM], zpad], axis=0)
        vn = jnp.concatenate([proj_ref[0, :, WIN_OFF + KV_WIDTH + g * HEAD_DIM:
                                       WIN_OFF + KV_WIDTH + (g + 1) * HEAD_DIM], zpad], axis=0)
        s_w = lax.dot_general(q, kw, _NT, preferred_element_type=f32) * scale - slope_row * dist_w.astype(f32)
        s_n = (lax.dot_general(q, kn.astype(bf16), _NT, preferred_element_type=f32) * scale
               - slope_row * dist_n.astype(f32))
        s_w = jnp.where(keep_w, s_w, NEG_INF)
        s_n = jnp.where(keep_n, s_n, NEG_INF)
        m = jnp.maximum(jnp.max(s_w, axis=-1, keepdims=True), jnp.max(s_n, axis=-1, keepdims=True))
        e_w = jnp.exp(s_w - m)
        e_n = jnp.exp(s_n - m)
        den = jnp.sum(e_w, axis=-1, keepdims=True) + jnp.sum(e_n, axis=-1, keepdims=True)
        o_win = (jnp.dot(e_w.astype(bf16), vw, preferred_element_type=f32)
                 + jnp.dot(e_n.astype(bf16), vn.astype(bf16), preferred_element_type=f32)) / den

        for h, hh in enumerate(heads):
            rs = slice(h * t_new, (h + 1) * t_new)
            hs = slice(hh * HEAD_DIM, (hh + 1) * HEAD_DIM)
            o_ref[0, :, hs] = (gates[:, hh:hh + 1] * o_cmp[rs]
                               + gates[:, 2 * NSA_HEADS + hh:2 * NSA_HEADS + hh + 1] * o_win[rs])
            gsel_ref[0, :, hs] = jnp.broadcast_to(gates[:, NSA_HEADS + hh:NSA_HEADS + hh + 1], (t_new, HEAD_DIM))


def _nsa_decode_b_kernel(idx_ref, pt_ref, q_ref, part_ref, gsel_ref, new_ref, *refs, t_new, n_past, n_pages):
    del pt_ref
    f32, bf16 = jnp.float32, jnp.bfloat16
    k_refs, v_refs, o_ref = refs[:N_SELECT], refs[N_SELECT:2 * N_SELECT], refs[2 * N_SELECT]
    b, g, t = pl.program_id(0), pl.program_id(1), pl.program_id(2)
    scale = HEAD_DIM ** -0.5
    n_past_blk = n_past // SEL_BLOCK
    base = ((b * NSA_KV_HEADS + g) * t_new + t) * N_SELECT
    n_keys = N_SELECT * SEL_BLOCK
    lane = lax.broadcasted_iota(jnp.int32, (1, n_keys), 1)
    slot = lane // SEL_BLOCK
    blk_of_lane = jnp.full((1, n_keys), -1, jnp.int32)
    ks, vs = [], []
    for k in range(N_SELECT):
        blk = idx_ref[base + k]
        blk_of_lane = jnp.where(slot == k, blk, blk_of_lane)
        is_new = blk >= n_past_blk
        ks.append(jnp.where(is_new, new_ref[:, 0:HEAD_DIM], k_refs[k][...]).astype(bf16))
        vs.append(jnp.where(is_new, new_ref[:, HEAD_DIM:2 * HEAD_DIM], v_refs[k][...]).astype(bf16))
    k_all = jnp.concatenate(ks, axis=0)
    v_all = jnp.concatenate(vs, axis=0)
    dist = (n_past + t) - (blk_of_lane * SEL_BLOCK + lane % SEL_BLOCK)
    keep = (dist >= 0) & (blk_of_lane >= 0)
    q = jnp.concatenate([q_ref[:, h * HEAD_DIM:(h + 1) * HEAD_DIM] for h in range(HPG)]
                        + [jnp.zeros((8 - HPG, HEAD_DIM), f32)], axis=0).astype(bf16)
    h_row = lax.broadcasted_iota(jnp.int32, (8, 1), 0)
    slope_row = jnp.zeros((8, 1), f32)
    for h in range(HPG):
        slope_row = jnp.where(h_row == h, jnp.where(g == 0, 2.0 ** -(h + 1), 2.0 ** -(HPG + h + 1)), slope_row)
    s = lax.dot_general(q, k_all, _NT, preferred_element_type=f32) * scale - slope_row * dist.astype(f32)
    s = jnp.where(keep, s, NEG_INF)
    e = jnp.exp(s - jnp.max(s, axis=-1, keepdims=True))
    p = e / jnp.sum(e, axis=-1, keepdims=True)
    o_sel = jnp.dot(p.astype(bf16), v_all, preferred_element_type=f32)
    o_row = jnp.concatenate([o_sel[h:h + 1] for h in range(HPG)], axis=1)
    o_ref[...] = part_ref[...] + gsel_ref[...] * o_row


def _nsa_decode(proj3, kvc, win_buf, pool_slc, page_table, n_past):
    bsz, t_new, _ = proj3.shape
    n_pages = page_table.shape[1]
    gw = HPG * HEAD_DIM
    grp = jax.ShapeDtypeStruct((bsz, NSA_KV_HEADS, t_new, gw), jnp.float32)
    part, gsel, idx = pl.pallas_call(
        functools.partial(_nsa_decode_a_kernel, t_new=t_new, n_past=n_past),
        grid=(bsz,),
        in_specs=[pl.BlockSpec((1, t_new, IN_WIDTH), lambda b: (b, 0, 0)),
                  pl.BlockSpec((1,) + kvc.shape[1:], lambda b: (b, 0, 0)),
                  pl.BlockSpec((1,) + win_buf.shape[1:], lambda b: (b, 0, 0))],
        out_specs=[pl.BlockSpec((1, NSA_KV_HEADS, t_new, gw), lambda b: (b, 0, 0, 0)),
                   pl.BlockSpec((1, NSA_KV_HEADS, t_new, gw), lambda b: (b, 0, 0, 0)),
                   pl.BlockSpec((1, NSA_KV_HEADS * t_new, IDX_LANES), lambda b: (b, 0, 0))],
        out_shape=[grp, grp, jax.ShapeDtypeStruct((bsz, NSA_KV_HEADS * t_new, IDX_LANES), jnp.int32)],
        compiler_params=pltpu.CompilerParams(dimension_semantics=("arbitrary",), vmem_limit_bytes=VMEM_LIMIT_BYTES),
        name="nsa_decode_a",
    )(proj3, kvc, win_buf)

    n_rows = bsz * NSA_KV_HEADS * t_new
    n_past_blk = n_past // SEL_BLOCK
    sub = PAGE_SIZE // SEL_BLOCK
    pool2 = pool_slc.reshape(-1, SEL_BLOCK, 2 * KV_WIDTH)
    new_rows = jnp.pad(proj3[:, :, SLC_OFF:WIN_OFF], ((0, 0), (0, SEL_BLOCK - t_new), (0, 0)))
    new_rows = new_rows.reshape(bsz, SEL_BLOCK, 2, NSA_KV_HEADS, HEAD_DIM).transpose(0, 3, 1, 2, 4)
    new_rows = new_rows.reshape(bsz * NSA_KV_HEADS, SEL_BLOCK, 2 * HEAD_DIM)

    def row_map(b, g, t, idx, pt):
        return (b * NSA_KV_HEADS + g) * t_new + t

    def pool_spec(k, is_v):
        def index(b, g, t, idx, pt):
            past = jnp.clip(idx[row_map(b, g, t, idx, pt) * N_SELECT + k], 0, n_past_blk - 1)
            return (pt[b * n_pages + past // sub] * sub + past % sub, 0, is_v * NSA_KV_HEADS + g)
        return pl.BlockSpec((None, SEL_BLOCK, HEAD_DIM), index)

    row_spec = pl.BlockSpec((None, 1, gw), lambda b, g, t, idx, pt: (row_map(b, g, t, idx, pt), 0, 0))
    out = pl.pallas_call(
        functools.partial(_nsa_decode_b_kernel, t_new=t_new, n_past=n_past, n_pages=n_pages),
        grid_spec=pltpu.PrefetchScalarGridSpec(
            num_scalar_prefetch=2, grid=(bsz, NSA_KV_HEADS, t_new),
            in_specs=[pl.BlockSpec((None, 1, gw), lambda b, g, t, idx, pt: (b * t_new + t, 0, g)),
                      row_spec, row_spec,
                      pl.BlockSpec((None, SEL_BLOCK, 2 * HEAD_DIM), lambda b, g, t, idx, pt: (b * NSA_KV_HEADS + g, 0, 0))]
            + [pool_spec(k, 0) for k in range(N_SELECT)] + [pool_spec(k, 1) for k in range(N_SELECT)],
            out_specs=row_spec),
        out_shape=jax.ShapeDtypeStruct((n_rows, 1, gw), jnp.float32),
        compiler_params=pltpu.CompilerParams(
            dimension_semantics=("arbitrary", "arbitrary", "arbitrary"), vmem_limit_bytes=VMEM_LIMIT_BYTES),
        name="nsa_decode_b",
    )(idx[:, :, :N_SELECT].reshape(-1), page_table.reshape(-1),
      proj3.reshape(bsz * t_new, 1, IN_WIDTH), part.reshape(n_rows, 1, gw), gsel.reshape(n_rows, 1, gw),
      new_rows, *([pool2] * (2 * N_SELECT)))
    return out.reshape(bsz, NSA_KV_HEADS, t_new, gw).transpose(0, 2, 1, 3).reshape(bsz, t_new, NSA_WIDTH)


def _nsa_decode_sel_kernel(idx_ref, pt_ref, q_ref, part_ref, gsel_ref, new_ref, *refs, t_new, n_past):
    del pt_ref
    f32, bf16 = jnp.float32, jnp.bfloat16
    o_ref = refs[NSA_KV_HEADS * N_SELECT]
    b, t = pl.program_id(0), pl.program_id(1)
    scale = HEAD_DIM ** -0.5
    n_past_blk = n_past // SEL_BLOCK
    n_keys = N_SELECT * SEL_BLOCK
    lane = lax.broadcasted_iota(jnp.int32, (1, n_keys), 1)
    slot = lane // SEL_BLOCK
    h_row = lax.broadcasted_iota(jnp.int32, (8, 1), 0)
    for g in range(NSA_KV_HEADS):
        base = ((b * NSA_KV_HEADS + g) * t_new + t) * N_SELECT
        blk_of_lane = jnp.full((1, n_keys), -1, jnp.int32)
        ks, vs = [], []
        for k in range(N_SELECT):
            blk = idx_ref[base + k]
            blk_of_lane = jnp.where(slot == k, blk, blk_of_lane)
            is_new = blk >= n_past_blk
            src = refs[g * N_SELECT + k]
            k_old = src[pl.ds(g, SEL_BLOCK, stride=KV_PARTS), :]
            v_old = src[pl.ds(NSA_KV_HEADS + g, SEL_BLOCK, stride=KV_PARTS), :]
            k_new = new_ref[:, g * HEAD_DIM:(g + 1) * HEAD_DIM]
            v_new = new_ref[:, KV_WIDTH + g * HEAD_DIM:KV_WIDTH + (g + 1) * HEAD_DIM]
            ks.append(jnp.where(is_new, k_new, k_old).astype(bf16))
            vs.append(jnp.where(is_new, v_new, v_old).astype(bf16))
        k_all = jnp.concatenate(ks, axis=0)
        v_all = jnp.concatenate(vs, axis=0)
        dist = (n_past + t) - (blk_of_lane * SEL_BLOCK + lane % SEL_BLOCK)
        keep = (dist >= 0) & (blk_of_lane >= 0)
        q = jnp.concatenate([q_ref[:, (g * HPG + h) * HEAD_DIM:(g * HPG + h + 1) * HEAD_DIM] for h in range(HPG)]
                            + [jnp.zeros((8 - HPG, HEAD_DIM), f32)], axis=0).astype(bf16)
        slope_row = jnp.zeros((8, 1), f32)
        for h in range(HPG):
            slope_row = jnp.where(h_row == h, 2.0 ** -(g * HPG + h + 1), slope_row)
        s = lax.dot_general(q, k_all, _NT, preferred_element_type=f32) * scale - slope_row * dist.astype(f32)
        s = jnp.where(keep, s, NEG_INF)
        e = jnp.exp(s - jnp.max(s, axis=-1, keepdims=True))
        p = e / jnp.sum(e, axis=-1, keepdims=True)
        o_sel = jnp.dot(p.astype(bf16), v_all, preferred_element_type=f32)
        for h in range(HPG):
            hs = slice((g * HPG + h) * HEAD_DIM, (g * HPG + h + 1) * HEAD_DIM)
            o_ref[:, hs] = part_ref[:, hs] + gsel_ref[:, hs] * o_sel[h:h + 1]


def _nsa_decode_rows(proj3, kvc, win_rows, pool_rows, page_table, n_past):
    bsz, t_new, _ = proj3.shape
    n_pages = page_table.shape[1]
    full = jax.ShapeDtypeStruct((bsz, t_new, NSA_WIDTH), jnp.float32)
    part, gsel, idx = pl.pallas_call(
        functools.partial(_nsa_decode_a_kernel, t_new=t_new, n_past=n_past),
        grid=(bsz,),
        in_specs=[pl.BlockSpec((1, t_new, IN_WIDTH), lambda b: (b, 0, 0)),
                  pl.BlockSpec((1,) + kvc.shape[1:], lambda b: (b, 0, 0)),
                  pl.BlockSpec((1,) + win_rows.shape[1:], lambda b: (b, 0, 0))],
        out_specs=[pl.BlockSpec((1, t_new, NSA_WIDTH), lambda b: (b, 0, 0)),
                   pl.BlockSpec((1, t_new, NSA_WIDTH), lambda b: (b, 0, 0)),
                   pl.BlockSpec((1, NSA_KV_HEADS * t_new, IDX_LANES), lambda b: (b, 0, 0))],
        out_shape=[full, full, jax.ShapeDtypeStruct((bsz, NSA_KV_HEADS * t_new, IDX_LANES), jnp.int32)],
        compiler_params=pltpu.CompilerParams(dimension_semantics=("arbitrary",), vmem_limit_bytes=VMEM_LIMIT_BYTES),
        name="nsa_decode_a",
    )(proj3, kvc, win_rows)

    n_past_blk = n_past // SEL_BLOCK
    sub = PAGE_SIZE // SEL_BLOCK
    new_rows = jnp.pad(proj3[:, :, SLC_OFF:WIN_OFF], ((0, 0), (0, SEL_BLOCK - t_new), (0, 0)))

    def pool_spec(g, k):
        def index(b, t, idx, pt):
            past = jnp.clip(idx[((b * NSA_KV_HEADS + g) * t_new + t) * N_SELECT + k], 0, n_past_blk - 1)
            return (pt[b * n_pages + past // sub] * sub + past % sub, 0)
        return pl.BlockSpec((SEL_BLOCK * KV_PARTS, HEAD_DIM), index)

    row_spec = pl.BlockSpec((None, 1, NSA_WIDTH), lambda b, t, idx, pt: (b * t_new + t, 0, 0))
    out = pl.pallas_call(
        functools.partial(_nsa_decode_sel_kernel, t_new=t_new, n_past=n_past),
        grid_spec=pltpu.PrefetchScalarGridSpec(
            num_scalar_prefetch=2, grid=(bsz, t_new),
            in_specs=[row_spec, row_spec, row_spec,
                      pl.BlockSpec((None, SEL_BLOCK, 2 * KV_WIDTH), lambda b, t, idx, pt: (b, 0, 0))]
            + [pool_spec(g, k) for g in range(NSA_KV_HEADS) for k in range(N_SELECT)],
            out_specs=row_spec),
        out_shape=jax.ShapeDtypeStruct((bsz * t_new, 1, NSA_WIDTH), jnp.float32),
        compiler_params=pltpu.CompilerParams(
            dimension_semantics=("arbitrary", "arbitrary"), vmem_limit_bytes=VMEM_LIMIT_BYTES),
        name="nsa_decode_sel",
    )(idx[:, :, :N_SELECT].reshape(-1), page_table.reshape(-1),
      proj3[:, :, Q_OFF:Q_OFF + NSA_WIDTH].reshape(bsz * t_new, 1, NSA_WIDTH),
      part.reshape(bsz * t_new, 1, NSA_WIDTH), gsel.reshape(bsz * t_new, 1, NSA_WIDTH),
      new_rows, *([pool_rows] * (NSA_KV_HEADS * N_SELECT)))
    return out.reshape(bsz, t_new, NSA_WIDTH)


WKV_LANES = 2 * RWKV_HEAD_DIM
WKV_PAIRS = RWKV_HEADS // 2
WKV_STACK = 4
WKV_BB = 4
WKV_CHUNK = 64
WKV_MIN_CHUNK = 16
WKV_FLUSH = RWKV_HEAD_DIM


def _wkv_kernel(wr_ref, w_ref, k_ref, v_ref, kk_ref, kka_ref, c1_ref, c2_ref, s0_ref, y_ref, st_ref,
                s_scr, y_scr, *, tc):
    f32, bf16 = jnp.float32, jnp.bfloat16
    ti = pl.program_id(1)
    hd = RWKV_HEAD_DIM
    n_tiles = WKV_BB * WKV_PAIRS
    n_stacks = n_tiles // WKV_STACK
    tile = lambda q: (q // WKV_PAIRS, q % WKV_PAIRS)

    @pl.when(ti == 0)
    def _():
        for q in range(n_tiles):
            b, p = tile(q)
            s_scr[q] = jnp.concatenate([s0_ref[b, 2 * p], s0_ref[b, 2 * p + 1]], axis=1)

    lane = lax.broadcasted_iota(jnp.int32, (1, WKV_LANES), 1)
    r2 = lax.broadcasted_iota(jnp.int32, (2 * WKV_LANES, 1), 0)
    c2 = lax.broadcasted_iota(jnp.int32, (1, 2 * WKV_LANES), 1)
    same_head2 = jnp.where(r2 // hd == c2 // hd, 1.0, 0.0).astype(bf16)
    on_diag = lax.broadcasted_iota(jnp.int32, (hd, 1), 0) == lane % hd
    n_flush = min(tc, WKV_FLUSH)
    y_scr[...] = jnp.zeros(y_scr.shape, f32)
    pairs = [(2 * i, 2 * i + 1) for i in range(n_stacks // 2)]
    stack_tiles = lambda st: range(st * WKV_STACK, (st + 1) * WKV_STACK)

    def row_sums(per_tile, s0, s1):
        lhs = jnp.concatenate([jnp.concatenate([per_tile[q] for q in stack_tiles(st)], axis=0)
                               for st in (s0, s1)], axis=1)
        res = jnp.dot(lhs.astype(bf16), same_head2, preferred_element_type=f32)
        return {q: res[n * hd:(n + 1) * hd, half * WKV_LANES:(half + 1) * WKV_LANES]
                for half, st in enumerate((s0, s1)) for n, q in enumerate(stack_tiles(st))}

    def step(t, carry):
        here = (lane % hd) == (t % n_flush)
        get = lambda ref, q: ref[tile(q)[0], tile(q)[1], pl.ds(t, 1), :]
        for s0, s1 in pairs:
            tiles = list(stack_tiles(s0)) + list(stack_tiles(s1))
            s_old = {q: s_scr[q] for q in tiles}
            sa = row_sums({q: s_old[q] * get(kk_ref, q) for q in tiles}, s0, s1)
            y_old = row_sums({q: s_old[q] * get(wr_ref, q) for q in tiles}, s0, s1)
            v_col = row_sums({q: jnp.where(on_diag, get(v_ref, q), 0.0) for q in tiles}, s0, s1)
            for q in tiles:
                s_scr[q] = s_old[q] * get(w_ref, q) - sa[q] * get(kka_ref, q) + v_col[q] * get(k_ref, q)
                y_col = y_old[q] - sa[q] * get(c1_ref, q) + v_col[q] * get(c2_ref, q)
                y_scr[q] = jnp.where(here, y_col, y_scr[q])
        return carry

    for sub in range(tc // n_flush):
        lax.fori_loop(sub * n_flush, (sub + 1) * n_flush, step, 0)
        for q in range(n_tiles):
            b, p = tile(q)
            yt = y_scr[q].T
            y_ref[b, p, sub * n_flush:(sub + 1) * n_flush, :] = jnp.concatenate(
                [yt[:n_flush], yt[hd:hd + n_flush]], axis=1)

    @pl.when(ti == pl.num_programs(1) - 1)
    def _():
        for q in range(n_tiles):
            b, p = tile(q)
            st_ref[b, 2 * p] = s_scr[q][:, :hd]
            st_ref[b, 2 * p + 1] = s_scr[q][:, hd:]


def _wkv_scan(wr, w, k, v, kk, kka, c1, c2, s0, tc):
    bsz, n_pairs, seq, _ = wr.shape
    assert n_pairs == WKV_PAIRS and bsz % WKV_BB == 0 and seq % tc == 0
    n_tiles = WKV_BB * WKV_PAIRS
    x_spec = pl.BlockSpec((WKV_BB, WKV_PAIRS, tc, WKV_LANES), lambda b, i: (b, 0, i, 0))
    s_spec = pl.BlockSpec((WKV_BB, RWKV_HEADS, RWKV_HEAD_DIM, RWKV_HEAD_DIM), lambda b, i: (b, 0, 0, 0))
    return pl.pallas_call(
        functools.partial(_wkv_kernel, tc=tc),
        grid=(bsz // WKV_BB, seq // tc),
        in_specs=[x_spec] * 8 + [s_spec],
        out_specs=[x_spec, s_spec],
        out_shape=[jax.ShapeDtypeStruct(wr.shape, jnp.float32),
                   jax.ShapeDtypeStruct(s0.shape, jnp.float32)],
        scratch_shapes=[pltpu.VMEM((n_tiles, RWKV_HEAD_DIM, WKV_LANES), jnp.float32),
                        pltpu.VMEM((n_tiles, RWKV_HEAD_DIM, WKV_LANES), jnp.float32)],
        compiler_params=pltpu.CompilerParams(
            dimension_semantics=("arbitrary", "arbitrary"), vmem_limit_bytes=VMEM_LIMIT_BYTES),
        name="wkv_scan",
    )(wr, w, k, v, kk, kka, c1, c2, s0)


RW_PACKS = RWKV_WIDTH // WKV_LANES


def _head_sums(x, ones2):
    f32, bf16 = jnp.float32, jnp.bfloat16
    hi = x.astype(bf16)
    lo = (x - hi.astype(f32)).astype(bf16)
    w = 2 * WKV_LANES
    out = []
    for c in range(RWKV_WIDTH // w):
        sl = slice(c * w, (c + 1) * w)
        out.append(jnp.dot(hi[:, sl], ones2, preferred_element_type=f32)
                   + jnp.dot(lo[:, sl], ones2, preferred_element_type=f32))
    return jnp.concatenate(out, axis=1)


def _head_ones():
    w = 2 * WKV_LANES
    r = lax.broadcasted_iota(jnp.int32, (w, 1), 0)
    c = lax.broadcasted_iota(jnp.int32, (1, w), 1)
    return jnp.where(r // RWKV_HEAD_DIM == c // RWKV_HEAD_DIM, 1.0, 0.0).astype(jnp.bfloat16)


def _store_rw(ref, val, pack_major):
    if pack_major:
        for p in range(RW_PACKS):
            ref[0, p] = val[:, p * WKV_LANES:(p + 1) * WKV_LANES]
    else:
        ref[0] = val


def _rwkv_prep_kernel(h_ref, hprev_ref, h0_ref, pr_ref, pk_ref, pv_ref, pprev_r, pprev_k, pprev_v, p0_ref,
                      mu_rkv_ref, mu_wag_ref, dw0_ref, dw1_ref, dw2_ref, a0_ref, a1_ref, a2_ref, g1_ref, g2_ref,
                      kk_w_ref, ka_w_ref, rk_w_ref,
                      wr_out, w_out, k_out, v_out, kk_out, kka_out, c1_out, c2_out, g_out, bonus_out,
                      *, period, pack_major):
    f32, bf16 = jnp.float32, jnp.bfloat16
    i = pl.program_id(1)
    tm = h_ref.shape[1]
    row = lax.broadcasted_iota(jnp.int32, (tm, 1), 0)
    per_row_first = h0_ref.shape[1] != 1
    first = (row % period == 0) if per_row_first else None

    def shifted(cur, prev_blk, first_rows):
        rolled = pltpu.roll(cur, 1, 0)
        if per_row_first:
            return jnp.where(first, first_rows, rolled)
        row0 = jnp.where(i == 0, first_rows, prev_blk[7:8])
        return jnp.where(row == 0, row0, rolled)

    h = h_ref[0]
    xx = shifted(h, hprev_ref[0], h0_ref[0]) - h
    xw = (h + xx * mu_wag_ref[0:1]).astype(bf16)
    xa = (h + xx * mu_wag_ref[1:2]).astype(bf16)
    xg = (h + xx * mu_wag_ref[2:3]).astype(bf16)
    dmid = jnp.tanh(jnp.dot(xw, dw1_ref[...], preferred_element_type=f32))
    dlin = dw0_ref[...] + jnp.dot(dmid.astype(bf16), dw2_ref[...], preferred_element_type=f32)
    z = -dlin
    w_log = -(jnp.maximum(z, 0.0) + jnp.log(1.0 + jnp.exp(-jnp.abs(z)))) - 0.5
    decay = jnp.exp(-jnp.exp(w_log))
    amid = jnp.dot(xa, a1_ref[...], preferred_element_type=f32)
    a = jax.nn.sigmoid(a0_ref[...] + jnp.dot(amid.astype(bf16), a2_ref[...], preferred_element_type=f32))
    gmid = jax.nn.sigmoid(jnp.dot(xg, g1_ref[...], preferred_element_type=f32))
    g = jnp.dot(gmid.astype(bf16), g2_ref[...], preferred_element_type=f32)

    def mixed(cur_ref, prev_ref, n):
        cur = cur_ref[0]
        cs = slice(n * RWKV_WIDTH, (n + 1) * RWKV_WIDTH)
        prev = shifted(cur, prev_ref[0], p0_ref[0][:, cs])
        return cur + mu_rkv_ref[:, cs] * (prev - cur)

    r = mixed(pr_ref, pprev_r, 0)
    k = mixed(pk_ref, pprev_k, 1)
    v = mixed(pv_ref, pprev_v, 2)
    ones2 = _head_ones()
    kk = k * kk_w_ref[...]
    kk = kk / jnp.maximum(jnp.sqrt(_head_sums(kk * kk, ones2)), 1e-12)
    k = k * (1.0 + (a - 1.0) * ka_w_ref[...])
    bonus = _head_sums(r * k * rk_w_ref[...], ones2) * v
    _store_rw(wr_out, decay * r, pack_major)
    _store_rw(c1_out, _head_sums(kk * a * r, ones2), pack_major)
    _store_rw(c2_out, _head_sums(k * r, ones2), pack_major)
    _store_rw(w_out, decay, pack_major)
    _store_rw(k_out, k, pack_major)
    _store_rw(v_out, v, pack_major)
    _store_rw(kk_out, kk, pack_major)
    _store_rw(kka_out, kk * a, pack_major)
    g_out[0] = g
    bonus_out[0] = bonus


def _rwkv_prep(h, proj, h0, p0, lp, tm, period, pack_major):
    f32, bf16 = jnp.float32, jnp.bfloat16
    g, t, d = h.shape
    rw = RWKV_WIDTH
    nb = tm // 8
    cur = lambda w, c: pl.BlockSpec((1, tm, w), lambda b, i: (b, i, c))
    prev = lambda w, c: pl.BlockSpec((1, 8, w), lambda b, i: (b, jnp.maximum(i * nb - 1, 0), c))
    per_row = h0.shape[1] != 1
    carry = lambda w: pl.BlockSpec((1, tm if per_row else 1, w), (lambda b, i: (b, i, 0)) if per_row else (lambda b, i: (b, 0, 0)))
    full = lambda a: pl.BlockSpec(a.shape, lambda b, i: (0,) * a.ndim)
    c0 = RKV_OFF // rw
    ws = [lp['mu_rkv'].reshape(1, 3 * rw), lp['mu_wag'], lp['decay_w0'].reshape(1, rw), lp['decay_w1'].astype(bf16),
          lp['decay_w2'].astype(bf16), lp['iclr_a0'].reshape(1, rw), lp['iclr_a1'].astype(bf16),
          lp['iclr_a2'].astype(bf16), lp['gate_g1'].astype(bf16), lp['gate_g2'].astype(bf16),
          lp['k_k'].reshape(1, rw), lp['k_a'].reshape(1, rw), lp['r_k'].reshape(1, rw)]
    if pack_major:
        seq_shape = jax.ShapeDtypeStruct((g, RW_PACKS, t, WKV_LANES), f32)
        seq_spec = pl.BlockSpec((1, RW_PACKS, tm, WKV_LANES), lambda b, i: (b, 0, i, 0))
    else:
        seq_shape = jax.ShapeDtypeStruct((g, t, rw), f32)
        seq_spec = cur(rw, 0)
    flat_shape = jax.ShapeDtypeStruct((g, t, rw), f32)
    return pl.pallas_call(
        functools.partial(_rwkv_prep_kernel, period=period, pack_major=pack_major),
        grid=(g, t // tm),
        in_specs=[cur(d, 0), prev(d, 0), carry(d), cur(rw, c0), cur(rw, c0 + 1), cur(rw, c0 + 2),
                  prev(rw, c0), prev(rw, c0 + 1), prev(rw, c0 + 2), carry(3 * rw)] + [full(w) for w in ws],
        out_specs=[seq_spec] * 8 + [cur(rw, 0), cur(rw, 0)],
        out_shape=[seq_shape] * 8 + [flat_shape, flat_shape],
        compiler_params=pltpu.CompilerParams(
            dimension_semantics=("arbitrary", "arbitrary"), vmem_limit_bytes=VMEM_LIMIT_BYTES),
        name="rwkv_prep",
    )(h, h, h0, proj, proj, proj, proj, proj, proj, p0, *ws)


def _rwkv_post_kernel(y_ref, g_ref, bonus_ref, lnw_ref, lnb_ref, o_ref, *, pack_major):
    if pack_major:
        y = jnp.concatenate([y_ref[0, p] for p in range(RW_PACKS)], axis=1)
    else:
        y = y_ref[0]
    ones2 = _head_ones()
    inv = 1.0 / RWKV_HEAD_DIM
    mu = _head_sums(y, ones2) * inv
    dev = y - mu
    var = _head_sums(dev * dev, ones2) * inv
    yn = dev * lax.rsqrt(var + GN_EPS) * lnw_ref[...] + lnb_ref[...]
    o_ref[0] = (yn + bonus_ref[0]) * g_ref[0]


def _rwkv_post(y, g, bonus, ln_w, ln_b, tm, pack_major):
    gsz, t, rw = g.shape
    flat = pl.BlockSpec((1, tm, rw), lambda b, i: (b, i, 0))
    y_spec = pl.BlockSpec((1, RW_PACKS, tm, WKV_LANES), lambda b, i: (b, 0, i, 0)) if pack_major else flat
    vec = pl.BlockSpec((1, rw), lambda b, i: (0, 0))
    return pl.pallas_call(
        functools.partial(_rwkv_post_kernel, pack_major=pack_major),
        grid=(gsz, t // tm),
        in_specs=[y_spec, flat, flat, vec, vec],
        out_specs=flat,
        out_shape=jax.ShapeDtypeStruct((gsz, t, rw), jnp.float32),
        compiler_params=pltpu.CompilerParams(
            dimension_semantics=("arbitrary", "arbitrary"), vmem_limit_bytes=VMEM_LIMIT_BYTES),
        name="rwkv_post",
    )(y, g, bonus, ln_w.reshape(1, rw), ln_b.reshape(1, rw))


def _rmsnorm(x, g):
    xf = x.astype(jnp.float32)
    y = xf * lax.rsqrt(jnp.mean(xf * xf, axis=-1, keepdims=True) + NORM_EPS)
    return (y * g.astype(jnp.float32)).astype(x.dtype)


def _alibi_slopes():
    n = jnp.arange(1, NSA_HEADS + 1, dtype=jnp.float32)
    return jnp.exp2(-8.0 * n / NSA_HEADS).reshape(NSA_KV_HEADS, HPG)


def _compress_blocks(rows, w1, pe, w2):
    B, L = rows.shape[:2]
    n_chunk = L // CMP_STRIDE
    r = CMP_BLOCK // CMP_STRIDE
    nc = n_chunk - r + 1
    ch = rows.reshape(B, n_chunk, CMP_STRIDE, NSA_KV_HEADS, HEAD_DIM)
    pre = jnp.einsum('pd,pde->e', pe, w1)
    for i in range(r):
        pre = pre + jnp.einsum('bcpgd,pde->bcge', ch[:, i:i + nc], w1[i * CMP_STRIDE:(i + 1) * CMP_STRIDE])
    return jnp.einsum('bcge,ef->bcgf', jax.nn.silu(pre), w2)


def _overlap_matrix(nc, ns):
    c_start = jnp.arange(nc) * CMP_STRIDE
    c_end = c_start + CMP_BLOCK - 1
    s_start = jnp.arange(ns) * SEL_BLOCK
    s_end = s_start + SEL_BLOCK - 1
    return ((c_start[:, None] <= s_end[None]) & (c_end[:, None] >= s_start[None])).astype(jnp.float32)


def _compressed_branch(qg, pos_q, kc, vc, n_sel, slopes):
    nc = kc.shape[1]
    end_c = jnp.arange(nc) * CMP_STRIDE + CMP_BLOCK - 1
    dist = pos_q[:, None] - end_c[None, :]
    valid = (dist >= 0)[None, :, None, None, :]
    s = jnp.einsum('btghd,bcgd->btghc', qg, kc).astype(jnp.float32) * HEAD_DIM ** -0.5
    s = s - slopes[None, None, :, :, None] * dist.astype(jnp.float32)[None, :, None, None, :]
    p = jax.nn.softmax(jnp.where(valid, s, NEG_INF), axis=-1) * valid
    o = jnp.einsum('btghc,bcgd->btghd', p.astype(vc.dtype), vc)
    imp = jnp.einsum('btghc,cj->btgj', p, _overlap_matrix(nc, n_sel))
    blk_t = (pos_q // SEL_BLOCK)[:, None]
    j = jnp.arange(n_sel)[None, :]
    forced = (j == 0) | (j == blk_t) | (j == blk_t - 1)
    imp = jnp.where(forced[None, :, None, :], imp + FORCE_BONUS, imp)
    imp = jnp.where((j <= blk_t)[None, :, None, :], imp, NEG_INF)
    top_val, top_idx = lax.top_k(imp, min(N_SELECT, n_sel))
    return o, top_idx, top_val > 0.5 * NEG_INF


def _gather_selected(idx, new_blk, n_past_blk, pool, layer, page_table):
    b_idx = jnp.arange(idx.shape[0])[:, None, None, None]
    g_idx = jnp.arange(NSA_KV_HEADS)[None, None, :, None]
    local = jnp.clip(idx - n_past_blk, 0, new_blk.shape[1] - 1)
    blk = new_blk[b_idx, local, :, :, g_idx]
    if pool is not None:
        sub_per_page = PAGE_SIZE // SEL_BLOCK
        past = jnp.clip(idx, 0, n_past_blk - 1)
        phys = page_table[b_idx, past // sub_per_page]
        pool_r = pool.reshape(pool.shape[0], pool.shape[1], sub_per_page, SEL_BLOCK, 2, NSA_KV_HEADS, HEAD_DIM)
        l_idx = jnp.full_like(phys, layer)
        pblk = pool_r[l_idx, phys, past % sub_per_page, :, :, g_idx]
        blk = jnp.where((idx < n_past_blk)[..., None, None, None], pblk, blk)
    return blk[..., 0, :], blk[..., 1, :]


def _selected_branch(qg, pos_q, kb, vb, idx, sel_valid, slopes):
    B, Tq, G, H, _ = qg.shape
    s = jnp.einsum('btghd,btgkpd->btghkp', qg, kb).astype(jnp.float32) * HEAD_DIM ** -0.5
    key_pos = idx[..., None] * SEL_BLOCK + jnp.arange(SEL_BLOCK)
    dist = pos_q[None, :, None, None, None] - key_pos
    mask = ((dist >= 0) & sel_valid[..., None])[:, :, :, None]
    s = s - slopes[None, None, :, :, None, None] * dist.astype(jnp.float32)[:, :, :, None]
    s = jnp.where(mask, s, NEG_INF)
    p = jax.nn.softmax(s.reshape(B, Tq, G, H, -1), axis=-1).reshape(s.shape)
    return jnp.einsum('btghkp,btgkpd->btghd', p.astype(vb.dtype), vb)


def _window_attend(qg, pos_q, k, v, key_pos, slopes):
    s = jnp.einsum('btghd,bsgd->btghs', qg, k).astype(jnp.float32) * HEAD_DIM ** -0.5
    dist = pos_q[:, None] - key_pos[None, :]
    mask = ((dist >= 0) & (dist < WINDOW) & (key_pos[None, :] >= 0))[None, :, None, None, :]
    s = s - slopes[None, None, :, :, None] * dist.astype(jnp.float32)[None, :, None, None, :]
    p = jax.nn.softmax(jnp.where(mask, s, NEG_INF), axis=-1)
    return jnp.einsum('btghs,bsgd->btghd', p.astype(v.dtype), v)


def _nsa_group(proj, pos_q, lp, layer, pool_cmp, pool_slc, page_table, win_buf, past_len):
    B, T = proj.shape[:2]
    G, dh = NSA_KV_HEADS, HEAD_DIM
    slopes = _alibi_slopes()
    qg = proj[..., Q_OFF:CMP_OFF].reshape(B, T, G, HPG, dh)
    cmp_new = proj[..., CMP_OFF:SLC_OFF].reshape(B, T, 2, G, dh)
    slc_new = proj[..., SLC_OFF:WIN_OFF].reshape(B, T, 2, G, dh)
    win_new = proj[..., WIN_OFF:RKV_OFF].reshape(B, T, 2, G, dh)
    gates = jax.nn.sigmoid(proj[..., GATE_OFF:IN_WIDTH].astype(jnp.float32)).reshape(B, T, 3, G, HPG, 1)

    if pool_cmp is None:
        proj2d = proj.reshape(B * T, IN_WIDTH)
        kvc = _compress_prompt(proj2d, B, T, _cmp_weights(lp))
        o = _nsa_prompt(proj2d, kvc, B, T).reshape(B, T, NSA_WIDTH)
        pad_rows = jnp.zeros((B, min(WINDOW, past_len), 2, G, dh), win_new.dtype)
        win_state = jnp.concatenate([pad_rows, win_new], axis=1)[:, -min(WINDOW, past_len):]
        return o, cmp_new, slc_new, win_state

    n_past = page_table.shape[1] * PAGE_SIZE
    assert n_past % CMP_STRIDE == 0 and T < CMP_STRIDE and T <= 8
    assert -(-(n_past + T) // SEL_BLOCK) <= TOPK_LANES and n_past % SEL_BLOCK == 0 and T <= SEL_BLOCK
    assert win_buf.shape[1] == WINDOW
    kvw = 2 * KV_WIDTH
    pages = page_table + layer * pool_cmp.shape[1]
    kvc = _compress_paged(pool_cmp.reshape(-1, PAGE_SIZE, kvw), pages, _cmp_weights(lp))
    o = _nsa_decode(proj, kvc, win_buf.reshape(B, WINDOW, kvw), pool_slc, pages, n_past)
    win_state = jnp.concatenate([win_buf, win_new], axis=1)[:, -win_buf.shape[1]:]
    return o, cmp_new, slc_new, win_state


def _rwkv_group(h, h_shift, p_rkv, p_rkv_prev, wkv0, lp):
    B, T, _ = h.shape
    f32 = jnp.float32
    rkv = (p_rkv + lp['mu_rkv'] * (p_rkv_prev - p_rkv)).astype(f32)
    r, k, v = jnp.split(rkv, 3, axis=-1)
    xx = h_shift - h
    xw = h + xx * lp['mu_wag'][0]
    xa = h + xx * lp['mu_wag'][1]
    xg = h + xx * lp['mu_wag'][2]
    w_log = -jax.nn.softplus(-(lp['decay_w0'] + jnp.tanh(xw @ lp['decay_w1']) @ lp['decay_w2']).astype(f32)) - 0.5
    decay = jnp.exp(-jnp.exp(w_log))
    a = jax.nn.sigmoid((lp['iclr_a0'] + (xa @ lp['iclr_a1']) @ lp['iclr_a2']).astype(f32))
    g = (jax.nn.sigmoid(xg @ lp['gate_g1']) @ lp['gate_g2']).astype(f32)

    def heads(t):
        return t.reshape(B, T, RWKV_HEADS, RWKV_HEAD_DIM)

    kk = heads(k * lp['k_k'])
    kk = kk / jnp.maximum(jnp.sqrt(jnp.sum(kk * kk, axis=-1, keepdims=True)), 1e-12)
    k = k * (1.0 + (a - 1.0) * lp['k_a'])
    r_h, k_h, v_h, w_h, a_h = heads(r), heads(k), heads(v), heads(decay), heads(a)

    t_pad = -(-T // WKV_MIN_CHUNK) * WKV_MIN_CHUNK
    tc = WKV_CHUNK if t_pad % WKV_CHUNK == 0 else WKV_MIN_CHUNK

    def pairs(t, fill):
        t = t.reshape(B, T, RWKV_HEADS // 2, WKV_LANES)
        if t_pad != T:
            t = jnp.pad(t, ((0, 0), (0, t_pad - T), (0, 0), (0, 0)), constant_values=fill)
        return t.transpose(0, 2, 1, 3)

    y, S_T = _wkv_scan(pairs(r, 0.0), pairs(decay, 1.0), pairs(k, 0.0), pairs(v, 0.0),
                       pairs(kk, 0.0), pairs(kk * a_h, 0.0), wkv0.astype(f32), tc)
    y = y.transpose(0, 2, 1, 3)[:, :T].reshape(B, T, RWKV_HEADS, RWKV_HEAD_DIM)
    mu = jnp.mean(y, axis=-1, keepdims=True)
    var = jnp.mean(jnp.square(y - mu), axis=-1, keepdims=True)
    y = ((y - mu) * lax.rsqrt(var + GN_EPS)).reshape(B, T, RWKV_WIDTH) * lp['ln_x_w'] + lp['ln_x_b']
    bonus = (jnp.sum(r_h * k_h * lp['r_k'], axis=-1, keepdims=True) * v_h).reshape(B, T, RWKV_WIDTH)
    return (y + bonus) * g, S_T


def _hier_route(logits):
    n = logits.shape[0]
    pg = jax.nn.softmax(logits[:, :N_GROUPS], axis=-1)
    g_val, g_sel = lax.top_k(pg, 1)
    le = logits[:, N_GROUPS:N_GROUPS + N_EXPERTS].reshape(n, N_GROUPS, EXPERTS_PER_GROUP)
    le_g = jnp.take_along_axis(le, g_sel[:, :, None], axis=1)[:, 0]
    e_val, e_sel = lax.top_k(le_g, EXPERT_TOP_K)
    weights = jax.nn.softmax(e_val, axis=-1) * g_val
    return g_sel * EXPERTS_PER_GROUP + e_sel, weights


def _moe_ffn(h_pad, eid, w_gate, w_up, w_down):
    n, d = h_pad.shape[0] - 1, h_pad.shape[1]
    a_tot = n * EXPERT_TOP_K
    flat_e = eid.reshape(-1)
    onehot = (flat_e[:, None] == jnp.arange(N_EXPERTS)[None, :]).astype(jnp.int32)
    csum = jnp.cumsum(onehot, axis=0)
    rank = jnp.take_along_axis(csum, flat_e[:, None], axis=1)[:, 0] - 1
    counts = csum[-1]
    padded = (counts + MOE_TM - 1) // MOE_TM * MOE_TM
    pad_end = jnp.cumsum(padded)
    dest = (pad_end - padded)[flat_e] + rank
    n_blk = (a_tot + N_EXPERTS * (MOE_TM - 1)) // MOE_TM
    tok_buf = jnp.full((n_blk * MOE_TM,), n, jnp.int32).at[dest].set(jnp.arange(a_tot, dtype=jnp.int32) // EXPERT_TOP_K)
    xb = h_pad[tok_buf].reshape(n_blk, MOE_TM, d)
    blk_start = jnp.arange(n_blk, dtype=jnp.int32) * MOE_TM
    blk_e = jnp.minimum(jnp.sum((pad_end[None, :] <= blk_start[:, None]).astype(jnp.int32), axis=1), N_EXPERTS - 1)
    meta = jnp.concatenate([blk_e, pad_end[-1:] // MOE_TM]).astype(jnp.int32)
    yb = _moe_blocks(xb, meta, w_gate, w_up, w_down).reshape(n_blk * MOE_TM, d)
    dest2 = dest.reshape(n, EXPERT_TOP_K)
    return yb[dest2[:, 0]], yb[dest2[:, 1]]


def _layer_front(x, mod, pos_q, lp, layer, pool_cmp, pool_slc, page_table, win_buf, wkv0, shift0, past_len, rows):
    B, T, D = x.shape
    groups = B * T // rows
    per_token = rows > T
    mods = jnp.repeat(mod, T, axis=0).reshape(groups, rows, 6 * D) if per_token else mod[:, None, :]
    sh1, sc1, gt1, sh2, sc2, gt2 = jnp.split(mods, 6, axis=-1)
    xg = x.reshape(groups, rows, D)
    tm_in = min(rows, 512)
    proj, h, kv_rows = _norm_in(xg, lp['norm1'], sc1, sh1, lp['w_in_b'], tm_in)
    kv_shape = (B, T, 2, NSA_KV_HEADS, HEAD_DIM)
    cmp_rows, slc_rows, win_rows = (kv_rows[n].reshape(B, T * KV_PARTS, HEAD_DIM) for n in range(KV_TILES))
    cmp_new, slc_new = cmp_rows.reshape(kv_shape), slc_rows.reshape(kv_shape)

    if pool_cmp is None:
        kvc = _compress_prompt(cmp_rows, _cmp_weights(lp))
        o_nsa = _nsa_prompt(proj.reshape(B * T, IN_WIDTH), kvc, B, T)
        win_len = min(WINDOW, past_len)
        assert T >= win_len
        win_state = win_rows[:, (T - win_len) * KV_PARTS:].reshape((B, win_len) + kv_shape[2:])
    else:
        n_past = page_table.shape[1] * PAGE_SIZE
        assert n_past % CMP_STRIDE == 0 and T < CMP_STRIDE and T <= 8
        assert -(-(n_past + T) // SEL_BLOCK) <= TOPK_LANES and n_past % SEL_BLOCK == 0 and T <= SEL_BLOCK
        assert win_buf.shape[1] == WINDOW
        pages = page_table + layer * pool_cmp.shape[1]
        kvc = _compress_paged(pool_cmp.reshape(-1, HEAD_DIM), pages, _cmp_weights(lp))
        win_buf_rows = win_buf.reshape(B, WINDOW * KV_PARTS, HEAD_DIM)
        o_nsa = _nsa_decode_rows(proj.reshape(B, T, IN_WIDTH), kvc, win_buf_rows, pool_slc.reshape(-1, HEAD_DIM),
                                 pages, n_past)
        win_state = jnp.concatenate([win_buf_rows[:, T * KV_PARTS:], win_rows], axis=1).reshape(
            (B, WINDOW) + kv_shape[2:])

    shift0 = shift0.astype(h.dtype)
    p0 = _matmul(shift0, lp['w_in_b'][:, RKV_OFF:CMP_OFF], B, RWKV_WIDTH)
    if per_token:
        h0 = jnp.repeat(shift0, T, axis=0).reshape(groups, rows, D)
        p0 = jnp.repeat(p0, T, axis=0).reshape(groups, rows, 3 * RWKV_WIDTH)
    else:
        h0, p0 = shift0[:, None], p0[:, None]
    tm_rw = min(rows, 256)
    seqs = _rwkv_prep(h, proj, h0, p0, lp, tm_rw, T, pack_major=not per_token)
    seqs, (gate, bonus) = seqs[:8], seqs[8:]
    if per_token:
        t_pad = -(-T // WKV_MIN_CHUNK) * WKV_MIN_CHUNK

        def pairs(a, fill):
            a = jnp.pad(a.reshape(B, T, RW_PACKS, WKV_LANES), ((0, 0), (0, t_pad - T), (0, 0), (0, 0)),
                        constant_values=fill)
            return a.transpose(0, 2, 1, 3)

        fills = (0.0, 1.0) + (0.0,) * 6
        y, wkv_T = _wkv_scan(*[pairs(a, f) for a, f in zip(seqs, fills)], wkv0.astype(jnp.float32), WKV_MIN_CHUNK)
        y = y.transpose(0, 2, 1, 3)[:, :T].reshape(groups, rows, RWKV_WIDTH)
    else:
        y, wkv_T = _wkv_scan(*seqs, wkv0.astype(jnp.float32), WKV_CHUNK)
    o_rwkv = _rwkv_post(y, gate, bonus, lp['ln_x_w'], lp['ln_x_b'], tm_rw, pack_major=not per_token)

    x1, h2, logits = _mix_out(o_nsa.reshape(groups, rows, NSA_WIDTH), o_rwkv, lp['w_out_b'], xg, gt1, sc2, sh2,
                              lp['norm2'], lp['wr_hi'], lp['wr_lo'], lp['br'], min(rows, 256))
    return (x1, h2, logits, gt2), (cmp_new, slc_new, win_state, wkv_T, h.reshape(B, T, D)[:, -1])


def kernel(x_prompt, x_sample, c_prompt, c_sample, cache_cmp_kv, cache_slc_kv, page_table, state_win_kv, state_wkv, state_shift, w_ada, b_ada, norm1, w_in, cmp_k_w1, cmp_k_pe, cmp_k_w2, cmp_v_w1, cmp_v_pe, cmp_v_w2, mu_rkv, mu_wag, decay_w0, decay_w1, decay_w2, iclr_a0, iclr_a1, iclr_a2, gate_g1, gate_g2, k_k, k_a, r_k, ln_x_w, ln_x_b, w_out, norm2, w_router_group, b_router_group, w_router_expert, b_router_expert, w_gate, w_up, w_down, norm_f):
    bp, tp = x_prompt.shape[:2]
    ts = x_sample.shape[1]
    past_len = page_table.shape[1] * PAGE_SIZE
    pos_p = jnp.arange(tp)
    pos_s = past_len + jnp.arange(ts)
    assert DEPTH == 1 and w_in.shape[0] == 1
    l = 0
    bs = x_sample.shape[0]
    f32, bf16 = jnp.float32, jnp.bfloat16
    lyr = lambda a: a.reshape(a.shape[1:])
    wr = jnp.concatenate([lyr(w_router_group), lyr(w_router_expert),
                          jnp.zeros((D_MODEL, ROUTER_LANES - N_GROUPS - N_EXPERTS), f32)], axis=1)
    wr_hi = wr.astype(bf16)
    br = jnp.concatenate([lyr(b_router_group), lyr(b_router_expert),
                          jnp.zeros((ROUTER_LANES - N_GROUPS - N_EXPERTS,), f32)]).reshape(1, ROUTER_LANES)
    lp = dict(norm1=lyr(norm1), w_in_b=_permute_w_in(lyr(w_in)).astype(bf16),
              cmp_k_w1=lyr(cmp_k_w1), cmp_k_pe=lyr(cmp_k_pe), cmp_k_w2=lyr(cmp_k_w2),
              cmp_v_w1=lyr(cmp_v_w1), cmp_v_pe=lyr(cmp_v_pe), cmp_v_w2=lyr(cmp_v_w2),
              mu_rkv=lyr(mu_rkv), mu_wag=lyr(mu_wag), decay_w0=lyr(decay_w0), decay_w1=lyr(decay_w1),
              decay_w2=lyr(decay_w2), iclr_a0=lyr(iclr_a0), iclr_a1=lyr(iclr_a1), iclr_a2=lyr(iclr_a2),
              gate_g1=lyr(gate_g1), gate_g2=lyr(gate_g2), k_k=lyr(k_k), k_a=lyr(k_a), r_k=lyr(r_k),
              ln_x_w=lyr(ln_x_w), ln_x_b=lyr(ln_x_b), w_out_b=lyr(w_out).astype(bf16), norm2=lyr(norm2),
              wr_hi=wr_hi, wr_lo=(wr - wr_hi.astype(f32)).astype(bf16), br=br)

    c_all = jnp.concatenate([c_prompt, c_sample], axis=0)
    mod_all = _matmul(jax.nn.silu(c_all), lyr(w_ada), c_all.shape[0], 1024) + lyr(b_ada)
    mod_p, mod_s = mod_all[:bp], mod_all[bp:]

    wkv_zero = jnp.zeros((bp, RWKV_HEADS, RWKV_HEAD_DIM, RWKV_HEAD_DIM), f32)
    shift_zero = jnp.zeros((bp, D_MODEL), x_prompt.dtype)
    (x1p, h2p, lgp, gt2p), (a1, a2, a3, a4, a5) = _layer_front(
        x_prompt, mod_p, pos_p, lp, l, None, None, None, None, wkv_zero, shift_zero, past_len, tp)
    (x1s, h2s, lgs, gt2s), (b1, b2, b3, b4, b5) = _layer_front(
        x_sample, mod_s, pos_s, lp, l, cache_cmp_kv, cache_slc_kv, page_table, lyr(state_win_kv), lyr(state_wkv),
        lyr(state_shift), past_len, bs * ts)

    n_p, n_s = bp * tp, bs * ts
    h2 = jnp.concatenate([h2p.reshape(n_p, D_MODEL), h2s.reshape(n_s, D_MODEL), jnp.zeros((1, D_MODEL), bf16)], axis=0)
    eid, ew = _hier_route(jnp.concatenate([lgp.reshape(n_p, ROUTER_LANES), lgs.reshape(n_s, ROUTER_LANES)], axis=0))
    y0, y1 = _moe_ffn(h2, eid, lyr(w_gate), lyr(w_up), lyr(w_down))
    y_prompt = _final(x1p, y0, y1, ew, 0, gt2p, norm_f, 256).reshape(x_prompt.shape)
    y_sample = _final(x1s, y0, y1, ew, n_p, gt2s, norm_f, x1s.shape[1]).reshape(x_sample.shape)
    st = lambda a: a[None]
    return (y_prompt, y_sample, st(a1), st(b1), st(a2), st(b2), st(a3), st(b3), st(a4), st(b4), st(a5), st(b5))
```

```python
import functools

import jax
import jax.numpy as jnp
from jax import lax
from jax.experimental import pallas as pl
from jax.experimental.pallas import tpu as pltpu

D_MODEL = 2048
DEPTH = 1
PAGE_SIZE = 128
HEAD_DIM = 128
NSA_WIDTH = D_MODEL // 2
NSA_HEADS = NSA_WIDTH // HEAD_DIM
NSA_KV_HEADS = 2
HPG = NSA_HEADS // NSA_KV_HEADS
KV_WIDTH = NSA_KV_HEADS * HEAD_DIM
CMP_BLOCK = 32
CMP_STRIDE = 16
SEL_BLOCK = 64
N_SELECT = 16
WINDOW = 512
FORCE_BONUS = 1e4
RWKV_WIDTH = D_MODEL - NSA_WIDTH
RWKV_HEAD_DIM = 64
RWKV_HEADS = RWKV_WIDTH // RWKV_HEAD_DIM
GN_EPS = 64e-5
N_GROUPS = 4
EXPERTS_PER_GROUP = 8
N_EXPERTS = N_GROUPS * EXPERTS_PER_GROUP
EXPERT_TOP_K = 2
NORM_EPS = 1e-6
NEG_INF = -1e30
W_Q_OFF = 0
W_CMP_OFF = W_Q_OFF + NSA_WIDTH
W_RKV_OFF = W_CMP_OFF + 6 * KV_WIDTH
W_GATE_OFF = W_RKV_OFF + 3 * RWKV_WIDTH
IN_WIDTH = W_GATE_OFF + 3 * NSA_HEADS
Q_OFF = 0
RKV_OFF = Q_OFF + NSA_WIDTH
CMP_OFF = RKV_OFF + 3 * RWKV_WIDTH
SLC_OFF = CMP_OFF + 2 * KV_WIDTH
WIN_OFF = SLC_OFF + 2 * KV_WIDTH
GATE_OFF = WIN_OFF + 2 * KV_WIDTH
assert GATE_OFF == W_GATE_OFF


def _permute_w_in(w):
    return jnp.concatenate([w[:, W_Q_OFF:W_CMP_OFF], w[:, W_RKV_OFF:W_GATE_OFF], w[:, W_CMP_OFF:W_RKV_OFF],
                            w[:, W_GATE_OFF:]], axis=1)

VMEM_LIMIT_BYTES = 48 * 1024 * 1024


def _mm_kernel(x_ref, w_ref, o_ref):
    o_ref[...] = jnp.dot(x_ref[...].astype(jnp.bfloat16), w_ref[...].astype(jnp.bfloat16),
                         preferred_element_type=jnp.float32)


def _matmul(x, w, tm, tn):
    m, k = x.shape
    n = w.shape[1]
    return pl.pallas_call(
        _mm_kernel,
        grid=(pl.cdiv(m, tm), pl.cdiv(n, tn)),
        in_specs=[pl.BlockSpec((tm, k), lambda i, j: (i, 0)),
                  pl.BlockSpec((k, tn), lambda i, j: (0, j))],
        out_specs=pl.BlockSpec((tm, tn), lambda i, j: (i, j)),
        out_shape=jax.ShapeDtypeStruct((m, n), jnp.float32),
        compiler_params=pltpu.CompilerParams(
            dimension_semantics=("arbitrary", "arbitrary"), vmem_limit_bytes=VMEM_LIMIT_BYTES),
        name="matmul",
    )(x, w)


MOE_TM = 256


def _moe_block_kernel(meta_ref, x_ref, wg_ref, wu_ref, wd_ref, o_ref, wg_b, wu_b, wd_b):
    bf16 = jnp.bfloat16
    i = pl.program_id(0)
    used = i < meta_ref[pl.num_programs(0)]
    new_expert = (i == 0) | (meta_ref[i] != meta_ref[jnp.maximum(i - 1, 0)])

    @pl.when(used & new_expert)
    def _():
        wg_b[...] = wg_ref[0].astype(bf16)
        wu_b[...] = wu_ref[0].astype(bf16)
        wd_b[...] = wd_ref[0].astype(bf16)

    @pl.when(used)
    def _():
        x = x_ref[0]
        g = jnp.dot(x, wg_b[...], preferred_element_type=jnp.float32)
        u = jnp.dot(x, wu_b[...], preferred_element_type=jnp.float32)
        hmid = (g * jax.nn.sigmoid(g)) * u
        o_ref[0] = jnp.dot(hmid.astype(bf16), wd_b[...], preferred_element_type=jnp.float32)

    @pl.when(jnp.logical_not(used))
    def _():
        o_ref[...] = jnp.zeros(o_ref.shape, o_ref.dtype)


def _moe_blocks(xb, meta, w_gate, w_up, w_down):
    n_blk, mb, d = xb.shape
    de = w_gate.shape[2]
    grid_spec = pltpu.PrefetchScalarGridSpec(
        num_scalar_prefetch=1,
        grid=(n_blk,),
        in_specs=[pl.BlockSpec((1, mb, d), lambda i, e: (i, 0, 0)),
                  pl.BlockSpec((1, d, de), lambda i, e: (e[i], 0, 0)),
                  pl.BlockSpec((1, d, de), lambda i, e: (e[i], 0, 0)),
                  pl.BlockSpec((1, de, d), lambda i, e: (e[i], 0, 0))],
        out_specs=pl.BlockSpec((1, mb, d), lambda i, e: (i, 0, 0)),
        scratch_shapes=[pltpu.VMEM((d, de), jnp.bfloat16), pltpu.VMEM((d, de), jnp.bfloat16),
                        pltpu.VMEM((de, d), jnp.bfloat16)],
    )
    return pl.pallas_call(
        _moe_block_kernel,
        grid_spec=grid_spec,
        out_shape=jax.ShapeDtypeStruct((n_blk, mb, d), jnp.float32),
        compiler_params=pltpu.CompilerParams(
            dimension_semantics=("arbitrary",), vmem_limit_bytes=VMEM_LIMIT_BYTES),
        name="moe_blocks",
    )(meta, xb, w_gate, w_up, w_down)


ROUTER_LANES = 128


KV_TILES = 3
KV_PARTS = 2 * NSA_KV_HEADS


def _norm_in_kernel(x_ref, n1_ref, sc_ref, sh_ref, w_ref, proj_ref, h_ref, kv_ref, hb_scr, *, kv_tile0):
    j = pl.program_id(2)

    @pl.when(j == 0)
    def _():
        x = x_ref[0]
        y = x * lax.rsqrt(jnp.mean(x * x, axis=-1, keepdims=True) + NORM_EPS) * n1_ref[...]
        h = y * (1.0 + sc_ref[0]) + sh_ref[0]
        h_ref[0] = h
        hb_scr[...] = h.astype(jnp.bfloat16)

    proj_ref[0] = jnp.dot(hb_scr[...], w_ref[...], preferred_element_type=jnp.float32)

    @pl.when((j >= kv_tile0) & (j < kv_tile0 + KV_TILES))
    def _():
        tm = proj_ref.shape[1]
        for part in range(KV_PARTS):
            kv_ref[0, 0, pl.ds(part, tm, stride=KV_PARTS), :] = proj_ref[0, :, part * HEAD_DIM:(part + 1) * HEAD_DIM]


def _norm_in(x, norm1, sc, sh, w_in_b, tm):
    g, t, d = x.shape
    n = w_in_b.shape[1]
    tn = 2 * KV_WIDTH
    kv_tile0 = CMP_OFF // tn
    assert CMP_OFF % tn == 0 and GATE_OFF == CMP_OFF + KV_TILES * tn
    mrows = sc.shape[1]
    mod_spec = pl.BlockSpec((1, mrows if mrows == 1 else tm, d),
                            (lambda b, i, j: (b, 0, 0)) if mrows == 1 else (lambda b, i, j: (b, i, 0)))
    return pl.pallas_call(
        functools.partial(_norm_in_kernel, kv_tile0=kv_tile0),
        grid=(g, t // tm, pl.cdiv(n, tn)),
        in_specs=[pl.BlockSpec((1, tm, d), lambda b, i, j: (b, i, 0)),
                  pl.BlockSpec((1, d), lambda b, i, j: (0, 0)),
                  mod_spec, mod_spec,
                  pl.BlockSpec((d, tn), lambda b, i, j: (0, j))],
        out_specs=[pl.BlockSpec((1, tm, tn), lambda b, i, j: (b, i, j)),
                   pl.BlockSpec((1, tm, d), lambda b, i, j: (b, i, 0)),
                   pl.BlockSpec((1, 1, tm * KV_PARTS, HEAD_DIM),
                                lambda b, i, j: (jnp.clip(j - kv_tile0, 0, KV_TILES - 1), b, i, 0))],
        out_shape=[jax.ShapeDtypeStruct((g, t, n), jnp.float32), jax.ShapeDtypeStruct((g, t, d), jnp.float32),
                   jax.ShapeDtypeStruct((KV_TILES, g, t * KV_PARTS, HEAD_DIM), jnp.float32)],
        scratch_shapes=[pltpu.VMEM((tm, d), jnp.bfloat16)],
        compiler_params=pltpu.CompilerParams(
            dimension_semantics=("arbitrary", "arbitrary", "arbitrary"), vmem_limit_bytes=VMEM_LIMIT_BYTES),
        name="norm_in",
    )(x, norm1.reshape(1, d), sc, sh, w_in_b)


def _mix_out_kernel(on_ref, orw_ref, w_ref, x_ref, gt_ref, sc_ref, sh_ref, n2_ref, wr_hi_ref, wr_lo_ref, br_ref,
                    x1_ref, h2_ref, lg_ref):
    f32, bf16 = jnp.float32, jnp.bfloat16
    half = on_ref.shape[2]
    mixed = (jnp.dot(on_ref[0].astype(bf16), w_ref[0:half, :], preferred_element_type=f32)
             + jnp.dot(orw_ref[0].astype(bf16), w_ref[half:, :], preferred_element_type=f32))
    x1 = x_ref[0] + gt_ref[0] * mixed
    x1_ref[0] = x1
    y = x1 * lax.rsqrt(jnp.mean(x1 * x1, axis=-1, keepdims=True) + NORM_EPS) * n2_ref[...]
    h2 = y * (1.0 + sc_ref[0]) + sh_ref[0]
    hi = h2.astype(bf16)
    h2_ref[0] = hi
    lo = (h2 - hi.astype(f32)).astype(bf16)
    lg_ref[0] = (jnp.dot(hi, wr_hi_ref[...], preferred_element_type=f32)
                 + jnp.dot(hi, wr_lo_ref[...], preferred_element_type=f32)
                 + jnp.dot(lo, wr_hi_ref[...], preferred_element_type=f32) + br_ref[...])


def _mix_out(o_nsa, o_rwkv, w_out_b, x, gt, sc, sh, norm2, wr_hi, wr_lo, br, tm):
    g, t, d = x.shape
    half = o_nsa.shape[2]
    mrows = sc.shape[1]
    mod_spec = pl.BlockSpec((1, mrows if mrows == 1 else tm, d),
                            (lambda b, i: (b, 0, 0)) if mrows == 1 else (lambda b, i: (b, i, 0)))
    row = lambda w: pl.BlockSpec((1, tm, w), lambda b, i: (b, i, 0))
    full = lambda a: pl.BlockSpec(a.shape, lambda b, i: (0,) * a.ndim)
    n2 = norm2.reshape(1, d)
    return pl.pallas_call(
        _mix_out_kernel,
        grid=(g, t // tm),
        in_specs=[row(half), row(half), full(w_out_b), row(d), mod_spec, mod_spec, mod_spec, full(n2),
                  full(wr_hi), full(wr_lo), full(br)],
        out_specs=[row(d), row(d), row(ROUTER_LANES)],
        out_shape=[jax.ShapeDtypeStruct((g, t, d), jnp.float32), jax.ShapeDtypeStruct((g, t, d), jnp.bfloat16),
                   jax.ShapeDtypeStruct((g, t, ROUTER_LANES), jnp.float32)],
        compiler_params=pltpu.CompilerParams(
            dimension_semantics=("arbitrary", "arbitrary"), vmem_limit_bytes=VMEM_LIMIT_BYTES),
        name="mix_out",
    )(o_nsa, o_rwkv, w_out_b, x, gt, sc, sh, n2, wr_hi, wr_lo, br)


def _final_kernel(x_ref, y0_ref, y1_ref, ew_ref, gt_ref, nf_ref, o_ref):
    ew = ew_ref[...]
    ffn = y0_ref[...] * ew[:, 0:1] + y1_ref[...] * ew[:, 1:2]
    x2 = x_ref[0] + gt_ref[0] * ffn
    o_ref[0] = x2 * lax.rsqrt(jnp.mean(x2 * x2, axis=-1, keepdims=True) + NORM_EPS) * nf_ref[...]


def _final(x1, y0, y1, ew, row_off, gt, norm_f, tm):
    g, t, d = x1.shape
    assert row_off % tm == 0
    mrows = gt.shape[1]
    mod_spec = pl.BlockSpec((1, mrows if mrows == 1 else tm, d),
                            (lambda b, i: (b, 0, 0)) if mrows == 1 else (lambda b, i: (b, i, 0)))
    row = pl.BlockSpec((1, tm, d), lambda b, i: (b, i, 0))
    flat = lambda w: pl.BlockSpec((tm, w), lambda b, i: (row_off // tm + b * (t // tm) + i, 0))
    return pl.pallas_call(
        _final_kernel,
        grid=(g, t // tm),
        in_specs=[row, flat(d), flat(d), flat(EXPERT_TOP_K), mod_spec, pl.BlockSpec((1, d), lambda b, i: (0, 0))],
        out_specs=row,
        out_shape=jax.ShapeDtypeStruct((g, t, d), jnp.float32),
        compiler_params=pltpu.CompilerParams(
            dimension_semantics=("arbitrary", "arbitrary"), vmem_limit_bytes=VMEM_LIMIT_BYTES),
        name="final_norm",
    )(x1, y0, y1, ew, gt, norm_f.reshape(1, d))


def _cmp_partial_kernel(*refs, n_src, rows_per_src):
    x_refs = refs[:n_src]
    w1k_ref, w1v_ref, a_ref, b_ref = refs[n_src:]
    nch_src = rows_per_src // CMP_STRIDE
    for kvg in range(4):
        w_ref = w1k_ref if kvg < 2 else w1v_ref
        acc = None
        for p in range(CMP_STRIDE):
            parts = [x_refs[s][pl.ds(4 * p + kvg, nch_src, stride=4 * CMP_STRIDE), :] for s in range(n_src)]
            xp = parts[0] if n_src == 1 else jnp.concatenate(parts, axis=0)
            d = jnp.dot(xp.astype(jnp.bfloat16), w_ref[p], preferred_element_type=jnp.float32)
            acc = d if acc is None else acc + d
        a_ref[0, :, kvg * HEAD_DIM:(kvg + 1) * HEAD_DIM] = acc[:, :HEAD_DIM]
        b_ref[0, :, kvg * HEAD_DIM:(kvg + 1) * HEAD_DIM] = acc[:, HEAD_DIM:]


def _cmp_finish_kernel(a_ref, b_ref, pek_ref, pev_ref, w1k_ref, w1v_ref, w2k_ref, w2v_ref, o_ref):
    nch = a_ref.shape[1]
    for kv, (pe_ref, w1_ref, w2_ref) in enumerate(((pek_ref, w1k_ref, w2k_ref), (pev_ref, w1v_ref, w2v_ref))):
        pe8 = jnp.broadcast_to(pe_ref[...], (8, pe_ref.shape[1])).astype(jnp.bfloat16)
        pterm = jnp.dot(pe8, w1_ref[...], preferred_element_type=jnp.float32)[0:1]
        w2 = w2_ref[...]
        for g in range(NSA_KV_HEADS):
            lo = (kv * NSA_KV_HEADS + g) * HEAD_DIM
            nxt = pltpu.roll(b_ref[0, :, lo:lo + HEAD_DIM], nch - 1, 0)
            pre = a_ref[0, :, lo:lo + HEAD_DIM] + nxt + pterm
            act = pre * jax.nn.sigmoid(pre)
            o_ref[0, :, lo:lo + HEAD_DIM] = jnp.dot(act.astype(jnp.bfloat16), w2, preferred_element_type=jnp.float32)


def _cmp_weights(lp):
    bf = jnp.bfloat16
    half = CMP_BLOCK // 2
    cat = lambda w: jnp.concatenate([w[:half], w[half:]], axis=-1).astype(bf)
    flat = lambda w: w.reshape(CMP_BLOCK * HEAD_DIM, HEAD_DIM).astype(bf)
    return dict(w1k_cat=cat(lp['cmp_k_w1']), w1v_cat=cat(lp['cmp_v_w1']),
                w1k_flat=flat(lp['cmp_k_w1']), w1v_flat=flat(lp['cmp_v_w1']),
                pek=lp['cmp_k_pe'].reshape(1, -1), pev=lp['cmp_v_pe'].reshape(1, -1),
                w2k=lp['cmp_k_w2'].astype(bf), w2v=lp['cmp_v_w2'].astype(bf))


def _cmp_finish(a, b, cw):
    bsz, nch, _ = a.shape
    full = lambda arr: pl.BlockSpec(arr.shape, lambda i: (0,) * arr.ndim)
    blk = pl.BlockSpec((1, nch, 4 * HEAD_DIM), lambda i: (i, 0, 0))
    ws = [cw['pek'], cw['pev'], cw['w1k_flat'], cw['w1v_flat'], cw['w2k'], cw['w2v']]
    return pl.pallas_call(
        _cmp_finish_kernel,
        grid=(bsz,),
        in_specs=[blk, blk] + [full(w) for w in ws],
        out_specs=blk,
        out_shape=jax.ShapeDtypeStruct((bsz, nch, 4 * HEAD_DIM), jnp.float32),
        compiler_params=pltpu.CompilerParams(dimension_semantics=("arbitrary",), vmem_limit_bytes=VMEM_LIMIT_BYTES),
        name="cmp_finish",
    )(a, b, *ws)


def _compress_prompt(rows4, cw):
    bsz, seq = rows4.shape[0], rows4.shape[1] // 4
    nch = seq // CMP_STRIDE
    x_specs = [pl.BlockSpec((None, seq * 4, HEAD_DIM), lambda i: (i, 0, 0))]
    w_spec = pl.BlockSpec(cw['w1k_cat'].shape, lambda i: (0, 0, 0))
    out_spec = pl.BlockSpec((1, nch, 4 * HEAD_DIM), lambda i: (i, 0, 0))
    shp = jax.ShapeDtypeStruct((bsz, nch, 4 * HEAD_DIM), jnp.float32)
    a, b = pl.pallas_call(
        functools.partial(_cmp_partial_kernel, n_src=1, rows_per_src=seq),
        grid=(bsz,),
        in_specs=x_specs + [w_spec, w_spec],
        out_specs=[out_spec, out_spec],
        out_shape=[shp, shp],
        compiler_params=pltpu.CompilerParams(dimension_semantics=("arbitrary",), vmem_limit_bytes=VMEM_LIMIT_BYTES),
        name="cmp_partial_prompt",
    )(rows4, cw['w1k_cat'], cw['w1v_cat'])
    return _cmp_finish(a, b, cw)


_NT = (((1,), (1,)), ((), ()))
LOG2E = 1.4426950408889634
SEL_TK = 512
WIN_TK = 256


def _flash_update(s, v, m_ref, l_ref, acc_ref, h):
    tk = s.shape[1]
    m_prev = m_ref[h]
    m_new = jnp.maximum(m_prev, jnp.max(s, axis=-1, keepdims=True))
    alpha = jnp.exp2(m_prev - m_new)
    p = jnp.exp2(s - jnp.concatenate([m_new] * (tk // HEAD_DIM), axis=1))
    l_ref[h] = alpha * l_ref[h] + jnp.sum(p, axis=-1, keepdims=True)
    acc_ref[h] = alpha * acc_ref[h] + jnp.dot(p.astype(jnp.bfloat16), v, preferred_element_type=jnp.float32)
    m_ref[h] = m_new


def _nsa_prompt_kernel(q_ref, slc_ref, win_ref, gate_ref, kvc_ref, o_ref, m_ref, l_ref, acc_ref, *, tq, seq):
    f32, bf16 = jnp.float32, jnp.bfloat16
    qi = pl.program_id(1)
    t0 = qi * tq
    scale = HEAD_DIM ** -0.5
    nc_valid = seq // CMP_STRIDE - CMP_BLOCK // CMP_STRIDE + 1
    n_sel = seq // SEL_BLOCK
    pos = t0 + lax.broadcasted_iota(jnp.int32, (tq, 1), 0)
    lane = lax.broadcasted_iota(jnp.int32, (1, HEAD_DIM), 1)
    blk_t = jnp.right_shift(pos, 6)
    gates = jax.nn.sigmoid(gate_ref[...])
    dist_c = pos - (lane * CMP_STRIDE + (CMP_BLOCK - 1))
    valid_c = (dist_c >= 0) & (lane < nc_valid)
    dist_cf = dist_c.astype(f32)
    c_row = lax.broadcasted_iota(jnp.int32, (HEAD_DIM, 1), 0)
    overlap = jnp.where((c_row * CMP_STRIDE <= lane * SEL_BLOCK + (SEL_BLOCK - 1))
                        & (c_row * CMP_STRIDE + (CMP_BLOCK - 1) >= lane * SEL_BLOCK), 1.0, 0.0).astype(bf16)
    forced = (lane == 0) | (lane == blk_t) | (lane == blk_t - 1)

    def gate_col(branch, hh):
        c = branch * NSA_HEADS + hh
        return gates[:, c:c + 1]

    def reset():
        m_ref[...] = jnp.full(m_ref.shape, NEG_INF, f32)
        l_ref[...] = jnp.zeros(l_ref.shape, f32)
        acc_ref[...] = jnp.zeros(acc_ref.shape, f32)

    for g in range(NSA_KV_HEADS):
        kcol = slice(g * HEAD_DIM, (g + 1) * HEAD_DIM)
        vcol = slice((NSA_KV_HEADS + g) * HEAD_DIM, (NSA_KV_HEADS + g + 1) * HEAD_DIM)
        heads = [g * HPG + h for h in range(HPG)]
        slopes = [2.0 ** -(hh + 1) for hh in heads]

        kc = kvc_ref[0, :, kcol].astype(bf16)
        vc = kvc_ref[0, :, vcol].astype(bf16)
        psum = jnp.zeros((tq, HEAD_DIM), f32)
        for h, hh in enumerate(heads):
            qh = q_ref[:, hh * HEAD_DIM:(hh + 1) * HEAD_DIM].astype(bf16)
            s = lax.dot_general(qh, kc, _NT, preferred_element_type=f32) * scale - slopes[h] * dist_cf
            s = jnp.where(valid_c, s, NEG_INF)
            e = jnp.exp(s - jnp.max(s, axis=-1, keepdims=True))
            p = e / jnp.sum(e, axis=-1, keepdims=True)
            p = jnp.where(valid_c, p, 0.0)
            o_cmp = jnp.dot(p.astype(bf16), vc, preferred_element_type=f32)
            o_ref[:, hh * HEAD_DIM:(hh + 1) * HEAD_DIM] = gate_col(0, hh) * o_cmp
            psum = psum + p
        p_hi = psum.astype(bf16)
        p_lo = (psum - p_hi.astype(f32)).astype(bf16)
        imp = (jnp.dot(p_hi, overlap, preferred_element_type=f32)
               + jnp.dot(p_lo, overlap, preferred_element_type=f32))
        imp = jnp.where(forced, imp + FORCE_BONUS, imp)
        imp = jnp.where(lane <= blk_t, imp, NEG_INF)
        beaten = jnp.zeros((tq, HEAD_DIM), f32)
        for jp in range(n_sel):
            col = imp[:, jp:jp + 1]
            tie = jnp.where(lane > jp, 1.0, 0.0)
            beaten = beaten + jnp.where(col > imp, 1.0, jnp.where(col == imp, tie, 0.0))
        sel = jnp.where(beaten < N_SELECT, jnp.where(imp > 0.5 * NEG_INF, 1.0, 0.0), 0.0).astype(bf16)

        reset()
        j_row = lax.broadcasted_iota(jnp.int32, (HEAD_DIM, 1), 0)

        def sel_body(kt, carry):
            k0 = pl.multiple_of(kt * SEL_TK, SEL_TK)
            k = slc_ref[pl.ds(k0, SEL_TK), kcol].astype(bf16)
            v = slc_ref[pl.ds(k0, SEL_TK), vcol].astype(bf16)
            kpos = k0 + lax.broadcasted_iota(jnp.int32, (1, SEL_TK), 1)
            dist = pos - kpos
            expand = jnp.where(jnp.right_shift(kpos, 6) == j_row, 1.0, 0.0).astype(bf16)
            picked = jnp.dot(sel, expand, preferred_element_type=f32)
            keep = jnp.where(dist >= 0, picked, 0.0) > 0.5
            kpos_f = (pos[0:1] - dist[0:1]).astype(f32)
            for h, hh in enumerate(heads):
                qh = q_ref[:, hh * HEAD_DIM:(hh + 1) * HEAD_DIM].astype(bf16)
                s = (lax.dot_general(qh, k, _NT, preferred_element_type=f32) * (scale * LOG2E)
                     + (slopes[h] * LOG2E) * kpos_f)
                _flash_update(jnp.where(keep, s, NEG_INF), v, m_ref, l_ref, acc_ref, h)
            return carry

        lax.fori_loop(0, (t0 + tq - 1) // SEL_TK + 1, sel_body, 0)
        for h, hh in enumerate(heads):
            hs = slice(hh * HEAD_DIM, (hh + 1) * HEAD_DIM)
            o_ref[:, hs] = o_ref[:, hs] + gate_col(1, hh) * (acc_ref[h] / l_ref[h])

        reset()

        def win_body(kt, carry):
            k0 = pl.multiple_of(kt * WIN_TK, WIN_TK)
            k = win_ref[pl.ds(k0, WIN_TK), kcol].astype(bf16)
            v = win_ref[pl.ds(k0, WIN_TK), vcol].astype(bf16)
            dist = pos - (k0 + lax.broadcasted_iota(jnp.int32, (1, WIN_TK), 1))
            keep = (dist >= 0) & (dist < WINDOW)
            kpos_f = (pos[0:1] - dist[0:1]).astype(f32)
            for h, hh in enumerate(heads):
                qh = q_ref[:, hh * HEAD_DIM:(hh + 1) * HEAD_DIM].astype(bf16)
                s = (lax.dot_general(qh, k, _NT, preferred_element_type=f32) * (scale * LOG2E)
                     + (slopes[h] * LOG2E) * kpos_f)
                _flash_update(jnp.where(keep, s, NEG_INF), v, m_ref, l_ref, acc_ref, h)
            return carry

        lax.fori_loop(jnp.maximum(t0 - (WINDOW - 1), 0) // WIN_TK, (t0 + tq - 1) // WIN_TK + 1, win_body, 0)
        for h, hh in enumerate(heads):
            hs = slice(hh * HEAD_DIM, (hh + 1) * HEAD_DIM)
            o_ref[:, hs] = o_ref[:, hs] + gate_col(2, hh) * (acc_ref[h] / l_ref[h])


def _nsa_prompt(proj2d, kvc, bsz, seq, tq=256):
    nq = seq // tq
    kvw = 2 * KV_WIDTH
    return pl.pallas_call(
        functools.partial(_nsa_prompt_kernel, tq=tq, seq=seq),
        grid=(bsz, nq),
        in_specs=[pl.BlockSpec((tq, NSA_WIDTH), lambda b, i: (b * nq + i, 0)),
                  pl.BlockSpec((seq, kvw), lambda b, i: (b, SLC_OFF // kvw)),
                  pl.BlockSpec((seq, kvw), lambda b, i: (b, WIN_OFF // kvw)),
                  pl.BlockSpec((tq, HEAD_DIM), lambda b, i: (b * nq + i, GATE_OFF // HEAD_DIM)),
                  pl.BlockSpec((1, seq // CMP_STRIDE, kvw), lambda b, i: (b, 0, 0))],
        out_specs=pl.BlockSpec((tq, NSA_WIDTH), lambda b, i: (b * nq + i, 0)),
        out_shape=jax.ShapeDtypeStruct((bsz * seq, NSA_WIDTH), jnp.float32),
        scratch_shapes=[pltpu.VMEM((HPG, tq, HEAD_DIM), jnp.float32),
                        pltpu.VMEM((HPG, tq, HEAD_DIM), jnp.float32),
                        pltpu.VMEM((HPG, tq, HEAD_DIM), jnp.float32)],
        compiler_params=pltpu.CompilerParams(
            dimension_semantics=("arbitrary", "arbitrary"), vmem_limit_bytes=VMEM_LIMIT_BYTES),
        name="nsa_prompt",
    )(proj2d, proj2d, proj2d, proj2d, kvc)


CMP_PAGES_PER_STEP = 32
TOPK_LANES = 384
IDX_LANES = 128


def _cmp_partial_paged_kernel(pt_ref, *refs, n_src, rows_per_src):
    del pt_ref
    _cmp_partial_kernel(*refs, n_src=n_src, rows_per_src=rows_per_src)


def _compress_paged(pool, page_table, cw):
    bsz, n_pages = page_table.shape
    nps = CMP_PAGES_PER_STEP
    n_tiles = n_pages // nps
    nch_tile = nps * PAGE_SIZE // CMP_STRIDE

    def page_spec(s):
        return pl.BlockSpec((PAGE_SIZE * 4, HEAD_DIM), lambda b, i, pt: (pt[b * n_pages + i * nps + s], 0))

    x_specs = [page_spec(s) for s in range(nps)]
    w_spec = pl.BlockSpec(cw['w1k_cat'].shape, lambda b, i, pt: (0, 0, 0))
    out_spec = pl.BlockSpec((1, nch_tile, 4 * HEAD_DIM), lambda b, i, pt: (b, i, 0))
    shp = jax.ShapeDtypeStruct((bsz, n_tiles * nch_tile, 4 * HEAD_DIM), jnp.float32)
    a, b = pl.pallas_call(
        functools.partial(_cmp_partial_paged_kernel, n_src=nps, rows_per_src=PAGE_SIZE),
        grid_spec=pltpu.PrefetchScalarGridSpec(
            num_scalar_prefetch=1, grid=(bsz, n_tiles),
            in_specs=x_specs + [w_spec, w_spec], out_specs=[out_spec, out_spec]),
        out_shape=[shp, shp],
        compiler_params=pltpu.CompilerParams(
            dimension_semantics=("arbitrary", "arbitrary"), vmem_limit_bytes=VMEM_LIMIT_BYTES),
        name="cmp_partial_paged",
    )(page_table.reshape(-1), *([pool] * nps), cw['w1k_cat'], cw['w1v_cat'])
    return _cmp_finish(a, b, cw)


def _nsa_decode_a_kernel(proj_ref, kvc_ref, win_ref, o_ref, gsel_ref, idx_ref, *, t_new, n_past):
    f32, bf16 = jnp.float32, jnp.bfloat16
    scale = HEAD_DIM ** -0.5
    nch = kvc_ref.shape[1]
    n_win = win_ref.shape[1] // KV_PARTS
    rows = HPG * t_new
    r_iota = lax.broadcasted_iota(jnp.int32, (rows, 1), 0)
    t_row = r_iota % t_new
    h_row = r_iota // t_new
    pos_row = n_past + t_row
    gates = jax.nn.sigmoid(proj_ref[0, :, GATE_OFF:IN_WIDTH])
    c_lane = lax.broadcasted_iota(jnp.int32, (1, nch), 1)
    dist_c = pos_row - (c_lane * CMP_STRIDE + (CMP_BLOCK - 1))
    valid_c = (dist_c >= 0) & (c_lane < nch - 1)
    c_col = lax.broadcasted_iota(jnp.int32, (nch, 1), 0)
    j_lane = lax.broadcasted_iota(jnp.int32, (1, TOPK_LANES), 1)
    overlap = jnp.where((c_col * CMP_STRIDE <= j_lane * SEL_BLOCK + (SEL_BLOCK - 1))
                        & (c_col * CMP_STRIDE + (CMP_BLOCK - 1) >= j_lane * SEL_BLOCK), 1.0, 0.0).astype(bf16)
    pos_t = n_past + lax.broadcasted_iota(jnp.int32, (t_new, 1), 0)
    blk_t = pos_t // SEL_BLOCK
    forced = (j_lane == 0) | (j_lane == blk_t) | (j_lane == blk_t - 1)
    j_f = j_lane.astype(f32)
    k_lane = lax.broadcasted_iota(jnp.int32, (1, IDX_LANES), 1)
    i_win = lax.broadcasted_iota(jnp.int32, (1, n_win), 1)
    dist_w = pos_row - (n_past - n_win + i_win)
    keep_w = (dist_w >= 0) & (dist_w < WINDOW)
    j_new = lax.broadcasted_iota(jnp.int32, (1, 8), 1)
    dist_n = t_row - j_new
    keep_n = (dist_n >= 0) & (j_new < t_new)
    zpad = jnp.zeros((8 - t_new, HEAD_DIM), f32)

    for g in range(NSA_KV_HEADS):
        kcol = slice(g * HEAD_DIM, (g + 1) * HEAD_DIM)
        vcol = slice((NSA_KV_HEADS + g) * HEAD_DIM, (NSA_KV_HEADS + g + 1) * HEAD_DIM)
        heads = [g * HPG + h for h in range(HPG)]
        slope_row = jnp.zeros((rows, 1), f32)
        for h, hh in enumerate(heads):
            slope_row = jnp.where(h_row == h, 2.0 ** -(hh + 1), slope_row)
        q = jnp.concatenate([proj_ref[0, :, hh * HEAD_DIM:(hh + 1) * HEAD_DIM] for hh in heads], axis=0).astype(bf16)

        kc = kvc_ref[0, :, kcol].astype(bf16)
        vc = kvc_ref[0, :, vcol].astype(bf16)
        s = lax.dot_general(q, kc, _NT, preferred_element_type=f32) * scale - slope_row * dist_c.astype(f32)
        s = jnp.where(valid_c, s, NEG_INF)
        e = jnp.exp(s - jnp.max(s, axis=-1, keepdims=True))
        p = e / jnp.sum(e, axis=-1, keepdims=True)
        p = jnp.where(valid_c, p, 0.0)
        o_cmp = jnp.dot(p.astype(bf16), vc, preferred_element_type=f32)
        psum = p[0:t_new]
        for h in range(1, HPG):
            psum = psum + p[h * t_new:(h + 1) * t_new]

        p_hi = psum.astype(bf16)
        p_lo = (psum - p_hi.astype(f32)).astype(bf16)
        imp = (jnp.dot(p_hi, overlap, preferred_element_type=f32)
               + jnp.dot(p_lo, overlap, preferred_element_type=f32))
        imp = jnp.where(forced, imp + FORCE_BONUS, imp)
        imp = jnp.where(j_lane <= blk_t, imp, NEG_INF)
        picked = jnp.full((t_new, IDX_LANES), -1.0, f32)
        for k in range(N_SELECT):
            best = jnp.max(imp, axis=-1, keepdims=True)
            first = jnp.min(jnp.where(imp == best, j_f, 1e9), axis=-1, keepdims=True)
            picked = jnp.where(k_lane == k, jnp.where(best > 0.5 * NEG_INF, first, -1.0), picked)
            imp = jnp.where(j_f == first, -3e38, imp)
        idx_ref[0, g * t_new:(g + 1) * t_new, :] = picked.astype(jnp.int32)

        kw = win_ref[0, pl.ds(g, n_win, stride=KV_PARTS), :].astype(bf16)
        vw = win_ref[0, pl.ds(NSA_KV_HEADS + g, n_win, stride=KV_PARTS), :].astype(bf16)
        kn = jnp.concatenate([proj_ref[0, :, WIN_OFF + g * HEAD_DIM:WIN_OFF + (g + 1) * HEAD_DIM], zpad], axis=0)
        vn = jnp.concatenate([proj_ref[0, :, WIN_OFF + KV_WIDTH + g * HEAD_DIM:
                                       WIN_OFF + KV_WIDTH + (g + 1) * HEAD_DIM], zpad], axis=0)
        s_w = lax.dot_general(q, kw, _NT, preferred_element_type=f32) * scale - slope_row * dist_w.astype(f32)
        s_n = (lax.dot_general(q, kn.astype(bf16), _NT, preferred_element_type=f32) * scale
               - slope_row * dist_n.astype(f32))
        s_w = jnp.where(keep_w, s_w, NEG_INF)
        s_n = jnp.where(keep_n, s_n, NEG_INF)
        m = jnp.maximum(jnp.max(s_w, axis=-1, keepdims=True), jnp.max(s_n, axis=-1, keepdims=True))
        e_w = jnp.exp(s_w - m)
        e_n = jnp.exp(s_n - m)
        den = jnp.sum(e_w, axis=-1, keepdims=True) + jnp.sum(e_n, axis=-1, keepdims=True)
        o_win = (jnp.dot(e_w.astype(bf16), vw, preferred_element_type=f32)
                 + jnp.dot(e_n.astype(bf16), vn.astype(bf16), preferred_element_type=f32)) / den

        for h, hh in enumerate(heads):
            rs = slice(h * t_new, (h + 1) * t_new)
            hs = slice(hh * HEAD_DIM, (hh + 1) * HEAD_DIM)
            o_ref[0, :, hs] = (gates[:, hh:hh + 1] * o_cmp[rs]
                               + gates[:, 2 * NSA_HEADS + hh:2 * NSA_HEADS + hh + 1] * o_win[rs])
            gsel_ref[0, :, hs] = jnp.broadcast_to(gates[:, NSA_HEADS + hh:NSA_HEADS + hh + 1], (t_new, HEAD_DIM))


def _nsa_decode_sel_kernel(idx_ref, pt_ref, q_ref, part_ref, gsel_ref, new_ref, *refs, t_new, n_past):
    del pt_ref
    f32, bf16 = jnp.float32, jnp.bfloat16
    o_ref = refs[NSA_KV_HEADS * N_SELECT]
    b, t = pl.program_id(0), pl.program_id(1)
    scale = HEAD_DIM ** -0.5
    n_past_blk = n_past // SEL_BLOCK
    n_keys = N_SELECT * SEL_BLOCK
    lane = lax.broadcasted_iota(jnp.int32, (1, n_keys), 1)
    slot = lane // SEL_BLOCK
    h_row = lax.broadcasted_iota(jnp.int32, (8, 1), 0)
    for g in range(NSA_KV_HEADS):
        base = ((b * NSA_KV_HEADS + g) * t_new + t) * N_SELECT
        blk_of_lane = jnp.full((1, n_keys), -1, jnp.int32)
        ks, vs = [], []
        for k in range(N_SELECT):
            blk = idx_ref[base + k]
            blk_of_lane = jnp.where(slot == k, blk, blk_of_lane)
            is_new = blk >= n_past_blk
            src = refs[g * N_SELECT + k]
            k_old = src[pl.ds(g, SEL_BLOCK, stride=KV_PARTS), :]
            v_old = src[pl.ds(NSA_KV_HEADS + g, SEL_BLOCK, stride=KV_PARTS), :]
            k_new = new_ref[:, g * HEAD_DIM:(g + 1) * HEAD_DIM]
            v_new = new_ref[:, KV_WIDTH + g * HEAD_DIM:KV_WIDTH + (g + 1) * HEAD_DIM]
            ks.append(jnp.where(is_new, k_new, k_old).astype(bf16))
            vs.append(jnp.where(is_new, v_new, v_old).astype(bf16))
        k_all = jnp.concatenate(ks, axis=0)
        v_all = jnp.concatenate(vs, axis=0)
        dist = (n_past + t) - (blk_of_lane * SEL_BLOCK + lane % SEL_BLOCK)
        keep = (dist >= 0) & (blk_of_lane >= 0)
        q = jnp.concatenate([q_ref[:, (g * HPG + h) * HEAD_DIM:(g * HPG + h + 1) * HEAD_DIM] for h in range(HPG)]
                            + [jnp.zeros((8 - HPG, HEAD_DIM), f32)], axis=0).astype(bf16)
        slope_row = jnp.zeros((8, 1), f32)
        for h in range(HPG):
            slope_row = jnp.where(h_row == h, 2.0 ** -(g * HPG + h + 1), slope_row)
        s = lax.dot_general(q, k_all, _NT, preferred_element_type=f32) * scale - slope_row * dist.astype(f32)
        s = jnp.where(keep, s, NEG_INF)
        e = jnp.exp(s - jnp.max(s, axis=-1, keepdims=True))
        p = e / jnp.sum(e, axis=-1, keepdims=True)
        o_sel = jnp.dot(p.astype(bf16), v_all, preferred_element_type=f32)
        for h in range(HPG):
            hs = slice((g * HPG + h) * HEAD_DIM, (g * HPG + h + 1) * HEAD_DIM)
            o_ref[:, hs] = part_ref[:, hs] + gsel_ref[:, hs] * o_sel[h:h + 1]


def _nsa_decode_rows(proj3, kvc, win_rows, pool_rows, page_table, n_past):
    bsz, t_new, _ = proj3.shape
    n_pages = page_table.shape[1]
    full = jax.ShapeDtypeStruct((bsz, t_new, NSA_WIDTH), jnp.float32)
    part, gsel, idx = pl.pallas_call(
        functools.partial(_nsa_decode_a_kernel, t_new=t_new, n_past=n_past),
        grid=(bsz,),
        in_specs=[pl.BlockSpec((1, t_new, IN_WIDTH), lambda b: (b, 0, 0)),
                  pl.BlockSpec((1,) + kvc.shape[1:], lambda b: (b, 0, 0)),
                  pl.BlockSpec((1,) + win_rows.shape[1:], lambda b: (b, 0, 0))],
        out_specs=[pl.BlockSpec((1, t_new, NSA_WIDTH), lambda b: (b, 0, 0)),
                   pl.BlockSpec((1, t_new, NSA_WIDTH), lambda b: (b, 0, 0)),
                   pl.BlockSpec((1, NSA_KV_HEADS * t_new, IDX_LANES), lambda b: (b, 0, 0))],
        out_shape=[full, full, jax.ShapeDtypeStruct((bsz, NSA_KV_HEADS * t_new, IDX_LANES), jnp.int32)],
        compiler_params=pltpu.CompilerParams(dimension_semantics=("arbitrary",), vmem_limit_bytes=VMEM_LIMIT_BYTES),
        name="nsa_decode_a",
    )(proj3, kvc, win_rows)

    n_past_blk = n_past // SEL_BLOCK
    sub = PAGE_SIZE // SEL_BLOCK
    new_rows = jnp.pad(proj3[:, :, SLC_OFF:WIN_OFF], ((0, 0), (0, SEL_BLOCK - t_new), (0, 0)))

    picked = idx[:, :, :N_SELECT]
    past = jnp.clip(picked, 0, n_past_blk - 1)
    pool_blk = jnp.take_along_axis(page_table[:, None, :], past // sub, axis=2) * sub + past % sub

    def pool_spec(g, k):
        def index(b, t, idx, blk):
            return (blk[((b * NSA_KV_HEADS + g) * t_new + t) * N_SELECT + k], 0)
        return pl.BlockSpec((SEL_BLOCK * KV_PARTS, HEAD_DIM), index)

    row_spec = pl.BlockSpec((None, 1, NSA_WIDTH), lambda b, t, idx, pt: (b * t_new + t, 0, 0))
    out = pl.pallas_call(
        functools.partial(_nsa_decode_sel_kernel, t_new=t_new, n_past=n_past),
        grid_spec=pltpu.PrefetchScalarGridSpec(
            num_scalar_prefetch=2, grid=(bsz, t_new),
            in_specs=[row_spec, row_spec, row_spec,
                      pl.BlockSpec((None, SEL_BLOCK, 2 * KV_WIDTH), lambda b, t, idx, pt: (b, 0, 0))]
            + [pool_spec(g, k) for g in range(NSA_KV_HEADS) for k in range(N_SELECT)],
            out_specs=row_spec),
        out_shape=jax.ShapeDtypeStruct((bsz * t_new, 1, NSA_WIDTH), jnp.float32),
        compiler_params=pltpu.CompilerParams(
            dimension_semantics=("arbitrary", "arbitrary"), vmem_limit_bytes=VMEM_LIMIT_BYTES),
        name="nsa_decode_sel",
    )(picked.reshape(-1), pool_blk.reshape(-1).astype(jnp.int32),
      proj3[:, :, Q_OFF:Q_OFF + NSA_WIDTH].reshape(bsz * t_new, 1, NSA_WIDTH),
      part.reshape(bsz * t_new, 1, NSA_WIDTH), gsel.reshape(bsz * t_new, 1, NSA_WIDTH),
      new_rows, *([pool_rows] * (NSA_KV_HEADS * N_SELECT)))
    return out.reshape(bsz, t_new, NSA_WIDTH)


WKV_LANES = 2 * RWKV_HEAD_DIM
WKV_PAIRS = RWKV_HEADS // 2
WKV_STACK = 4
WKV_BB = 4
WKV_CHUNK = 64
WKV_MIN_CHUNK = 16
WKV_FLUSH = RWKV_HEAD_DIM


def _wkv_kernel(wr_ref, w_ref, k_ref, v_ref, kk_ref, kka_ref, c1_ref, c2_ref, s0_ref, y_ref, st_ref,
                s_scr, y_scr, *, tc):
    f32, bf16 = jnp.float32, jnp.bfloat16
    ti = pl.program_id(1)
    hd = RWKV_HEAD_DIM
    n_tiles = WKV_BB * WKV_PAIRS
    n_stacks = n_tiles // WKV_STACK
    tile = lambda q: (q // WKV_PAIRS, q % WKV_PAIRS)

    @pl.when(ti == 0)
    def _():
        for q in range(n_tiles):
            b, p = tile(q)
            s_scr[q] = jnp.concatenate([s0_ref[b, 2 * p], s0_ref[b, 2 * p + 1]], axis=1)

    lane = lax.broadcasted_iota(jnp.int32, (1, WKV_LANES), 1)
    r2 = lax.broadcasted_iota(jnp.int32, (2 * WKV_LANES, 1), 0)
    c2 = lax.broadcasted_iota(jnp.int32, (1, 2 * WKV_LANES), 1)
    same_head2 = jnp.where(r2 // hd == c2 // hd, 1.0, 0.0).astype(bf16)
    on_diag = lax.broadcasted_iota(jnp.int32, (hd, 1), 0) == lane % hd
    n_flush = min(tc, WKV_FLUSH)
    y_scr[...] = jnp.zeros(y_scr.shape, f32)
    pairs = [(2 * i, 2 * i + 1) for i in range(n_stacks // 2)]
    stack_tiles = lambda st: range(st * WKV_STACK, (st + 1) * WKV_STACK)

    def row_sums(per_tile, s0, s1):
        lhs = jnp.concatenate([jnp.concatenate([per_tile[q] for q in stack_tiles(st)], axis=0)
                               for st in (s0, s1)], axis=1)
        res = jnp.dot(lhs.astype(bf16), same_head2, preferred_element_type=f32)
        return {q: res[n * hd:(n + 1) * hd, half * WKV_LANES:(half + 1) * WKV_LANES]
                for half, st in enumerate((s0, s1)) for n, q in enumerate(stack_tiles(st))}

    def step(t, carry):
        here = (lane % hd) == (t % n_flush)
        get = lambda ref, q: ref[tile(q)[0], tile(q)[1], pl.ds(t, 1), :]
        for s0, s1 in pairs:
            tiles = list(stack_tiles(s0)) + list(stack_tiles(s1))
            s_old = {q: s_scr[q] for q in tiles}
            sa = row_sums({q: s_old[q] * get(kk_ref, q) for q in tiles}, s0, s1)
            y_old = row_sums({q: s_old[q] * get(wr_ref, q) for q in tiles}, s0, s1)
            v_col = row_sums({q: jnp.where(on_diag, get(v_ref, q), 0.0) for q in tiles}, s0, s1)
            for q in tiles:
                s_scr[q] = s_old[q] * get(w_ref, q) - sa[q] * get(kka_ref, q) + v_col[q] * get(k_ref, q)
                y_col = y_old[q] - sa[q] * get(c1_ref, q) + v_col[q] * get(c2_ref, q)
                y_scr[q] = jnp.where(here, y_col, y_scr[q])
        return carry

    for sub in range(tc // n_flush):
        lax.fori_loop(sub * n_flush, (sub + 1) * n_flush, step, 0)
        for q in range(n_tiles):
            b, p = tile(q)
            yt = y_scr[q].T
            y_ref[b, p, sub * n_flush:(sub + 1) * n_flush, :] = jnp.concatenate(
                [yt[:n_flush], yt[hd:hd + n_flush]], axis=1)

    @pl.when(ti == pl.num_programs(1) - 1)
    def _():
        for q in range(n_tiles):
            b, p = tile(q)
            st_ref[b, 2 * p] = s_scr[q][:, :hd]
            st_ref[b, 2 * p + 1] = s_scr[q][:, hd:]


def _wkv_scan(wr, w, k, v, kk, kka, c1, c2, s0, tc):
    bsz, n_pairs, seq, _ = wr.shape
    assert n_pairs == WKV_PAIRS and bsz % WKV_BB == 0 and seq % tc == 0
    n_tiles = WKV_BB * WKV_PAIRS
    x_spec = pl.BlockSpec((WKV_BB, WKV_PAIRS, tc, WKV_LANES), lambda b, i: (b, 0, i, 0))
    s_spec = pl.BlockSpec((WKV_BB, RWKV_HEADS, RWKV_HEAD_DIM, RWKV_HEAD_DIM), lambda b, i: (b, 0, 0, 0))
    return pl.pallas_call(
        functools.partial(_wkv_kernel, tc=tc),
        grid=(bsz // WKV_BB, seq // tc),
        in_specs=[x_spec] * 8 + [s_spec],
        out_specs=[x_spec, s_spec],
        out_shape=[jax.ShapeDtypeStruct(wr.shape, jnp.float32),
                   jax.ShapeDtypeStruct(s0.shape, jnp.float32)],
        scratch_shapes=[pltpu.VMEM((n_tiles, RWKV_HEAD_DIM, WKV_LANES), jnp.float32),
                        pltpu.VMEM((n_tiles, RWKV_HEAD_DIM, WKV_LANES), jnp.float32)],
        compiler_params=pltpu.CompilerParams(
            dimension_semantics=("arbitrary", "arbitrary"), vmem_limit_bytes=VMEM_LIMIT_BYTES),
        name="wkv_scan",
    )(wr, w, k, v, kk, kka, c1, c2, s0)


RW_PACKS = RWKV_WIDTH // WKV_LANES


def _head_sums(x, ones2):
    f32, bf16 = jnp.float32, jnp.bfloat16
    hi = x.astype(bf16)
    lo = (x - hi.astype(f32)).astype(bf16)
    w = 2 * WKV_LANES
    out = []
    for c in range(RWKV_WIDTH // w):
        sl = slice(c * w, (c + 1) * w)
        out.append(jnp.dot(hi[:, sl], ones2, preferred_element_type=f32)
                   + jnp.dot(lo[:, sl], ones2, preferred_element_type=f32))
    return jnp.concatenate(out, axis=1)


def _head_ones():
    w = 2 * WKV_LANES
    r = lax.broadcasted_iota(jnp.int32, (w, 1), 0)
    c = lax.broadcasted_iota(jnp.int32, (1, w), 1)
    return jnp.where(r // RWKV_HEAD_DIM == c // RWKV_HEAD_DIM, 1.0, 0.0).astype(jnp.bfloat16)


def _store_rw(ref, val, pack_major):
    if pack_major:
        for p in range(RW_PACKS):
            ref[0, p] = val[:, p * WKV_LANES:(p + 1) * WKV_LANES]
    else:
        ref[0] = val


def _rwkv_prep_kernel(h_ref, hprev_ref, h0_ref, pr_ref, pk_ref, pv_ref, pprev_r, pprev_k, pprev_v, p0_ref,
                      mu_rkv_ref, mu_wag_ref, dw0_ref, dw1_ref, dw2_ref, a0_ref, a1_ref, a2_ref, g1_ref, g2_ref,
                      kk_w_ref, ka_w_ref, rk_w_ref,
                      wr_out, w_out, k_out, v_out, kk_out, kka_out, c1_out, c2_out, g_out, bonus_out,
                      *, period, pack_major):
    f32, bf16 = jnp.float32, jnp.bfloat16
    i = pl.program_id(1)
    tm = h_ref.shape[1]
    row = lax.broadcasted_iota(jnp.int32, (tm, 1), 0)
    per_row_first = h0_ref.shape[1] != 1
    first = (row % period == 0) if per_row_first else None

    def shifted(cur, prev_blk, first_rows):
        rolled = pltpu.roll(cur, 1, 0)
        if per_row_first:
            return jnp.where(first, first_rows, rolled)
        row0 = jnp.where(i == 0, first_rows, prev_blk[7:8])
        return jnp.where(row == 0, row0, rolled)

    h = h_ref[0]
    xx = shifted(h, hprev_ref[0], h0_ref[0]) - h
    xw = (h + xx * mu_wag_ref[0:1]).astype(bf16)
    xa = (h + xx * mu_wag_ref[1:2]).astype(bf16)
    xg = (h + xx * mu_wag_ref[2:3]).astype(bf16)
    dmid = jnp.tanh(jnp.dot(xw, dw1_ref[...], preferred_element_type=f32))
    dlin = dw0_ref[...] + jnp.dot(dmid.astype(bf16), dw2_ref[...], preferred_element_type=f32)
    z = -dlin
    w_log = -(jnp.maximum(z, 0.0) + jnp.log(1.0 + jnp.exp(-jnp.abs(z)))) - 0.5
    decay = jnp.exp(-jnp.exp(w_log))
    amid = jnp.dot(xa, a1_ref[...], preferred_element_type=f32)
    a = jax.nn.sigmoid(a0_ref[...] + jnp.dot(amid.astype(bf16), a2_ref[...], preferred_element_type=f32))
    gmid = jax.nn.sigmoid(jnp.dot(xg, g1_ref[...], preferred_element_type=f32))
    g = jnp.dot(gmid.astype(bf16), g2_ref[...], preferred_element_type=f32)

    def mixed(cur_ref, prev_ref, n):
        cur = cur_ref[0]
        cs = slice(n * RWKV_WIDTH, (n + 1) * RWKV_WIDTH)
        prev = shifted(cur, prev_ref[0], p0_ref[0][:, cs])
        return cur + mu_rkv_ref[:, cs] * (prev - cur)

    r = mixed(pr_ref, pprev_r, 0)
    k = mixed(pk_ref, pprev_k, 1)
    v = mixed(pv_ref, pprev_v, 2)
    ones2 = _head_ones()
    kk = k * kk_w_ref[...]
    kk = kk / jnp.maximum(jnp.sqrt(_head_sums(kk * kk, ones2)), 1e-12)
    k = k * (1.0 + (a - 1.0) * ka_w_ref[...])
    bonus = _head_sums(r * k * rk_w_ref[...], ones2) * v
    _store_rw(wr_out, decay * r, pack_major)
    _store_rw(c1_out, _head_sums(kk * a * r, ones2), pack_major)
    _store_rw(c2_out, _head_sums(k * r, ones2), pack_major)
    _store_rw(w_out, decay, pack_major)
    _store_rw(k_out, k, pack_major)
    _store_rw(v_out, v, pack_major)
    _store_rw(kk_out, kk, pack_major)
    _store_rw(kka_out, kk * a, pack_major)
    g_out[0] = g
    bonus_out[0] = bonus


def _rwkv_prep(h, proj, h0, p0, lp, tm, period, pack_major):
    f32, bf16 = jnp.float32, jnp.bfloat16
    g, t, d = h.shape
    rw = RWKV_WIDTH
    nb = tm // 8
    cur = lambda w, c: pl.BlockSpec((1, tm, w), lambda b, i: (b, i, c))
    prev = lambda w, c: pl.BlockSpec((1, 8, w), lambda b, i: (b, jnp.maximum(i * nb - 1, 0), c))
    per_row = h0.shape[1] != 1
    carry = lambda w: pl.BlockSpec((1, tm if per_row else 1, w), (lambda b, i: (b, i, 0)) if per_row else (lambda b, i: (b, 0, 0)))
    full = lambda a: pl.BlockSpec(a.shape, lambda b, i: (0,) * a.ndim)
    c0 = RKV_OFF // rw
    ws = [lp['mu_rkv'].reshape(1, 3 * rw), lp['mu_wag'], lp['decay_w0'].reshape(1, rw), lp['decay_w1'].astype(bf16),
          lp['decay_w2'].astype(bf16), lp['iclr_a0'].reshape(1, rw), lp['iclr_a1'].astype(bf16),
          lp['iclr_a2'].astype(bf16), lp['gate_g1'].astype(bf16), lp['gate_g2'].astype(bf16),
          lp['k_k'].reshape(1, rw), lp['k_a'].reshape(1, rw), lp['r_k'].reshape(1, rw)]
    if pack_major:
        seq_shape = jax.ShapeDtypeStruct((g, RW_PACKS, t, WKV_LANES), f32)
        seq_spec = pl.BlockSpec((1, RW_PACKS, tm, WKV_LANES), lambda b, i: (b, 0, i, 0))
    else:
        seq_shape = jax.ShapeDtypeStruct((g, t, rw), f32)
        seq_spec = cur(rw, 0)
    flat_shape = jax.ShapeDtypeStruct((g, t, rw), f32)
    return pl.pallas_call(
        functools.partial(_rwkv_prep_kernel, period=period, pack_major=pack_major),
        grid=(g, t // tm),
        in_specs=[cur(d, 0), prev(d, 0), carry(d), cur(rw, c0), cur(rw, c0 + 1), cur(rw, c0 + 2),
                  prev(rw, c0), prev(rw, c0 + 1), prev(rw, c0 + 2), carry(3 * rw)] + [full(w) for w in ws],
        out_specs=[seq_spec] * 8 + [cur(rw, 0), cur(rw, 0)],
        out_shape=[seq_shape] * 8 + [flat_shape, flat_shape],
        compiler_params=pltpu.CompilerParams(
            dimension_semantics=("arbitrary", "arbitrary"), vmem_limit_bytes=VMEM_LIMIT_BYTES),
        name="rwkv_prep",
    )(h, h, h0, proj, proj, proj, proj, proj, proj, p0, *ws)


def _rwkv_post_kernel(y_ref, g_ref, bonus_ref, lnw_ref, lnb_ref, o_ref, *, pack_major):
    if pack_major:
        y = jnp.concatenate([y_ref[0, p] for p in range(RW_PACKS)], axis=1)
    else:
        y = y_ref[0]
    ones2 = _head_ones()
    inv = 1.0 / RWKV_HEAD_DIM
    mu = _head_sums(y, ones2) * inv
    dev = y - mu
    var = _head_sums(dev * dev, ones2) * inv
    yn = dev * lax.rsqrt(var + GN_EPS) * lnw_ref[...] + lnb_ref[...]
    o_ref[0] = (yn + bonus_ref[0]) * g_ref[0]


def _rwkv_post(y, g, bonus, ln_w, ln_b, tm, pack_major):
    gsz, t, rw = g.shape
    flat = pl.BlockSpec((1, tm, rw), lambda b, i: (b, i, 0))
    y_spec = pl.BlockSpec((1, RW_PACKS, tm, WKV_LANES), lambda b, i: (b, 0, i, 0)) if pack_major else flat
    vec = pl.BlockSpec((1, rw), lambda b, i: (0, 0))
    return pl.pallas_call(
        functools.partial(_rwkv_post_kernel, pack_major=pack_major),
        grid=(gsz, t // tm),
        in_specs=[y_spec, flat, flat, vec, vec],
        out_specs=flat,
        out_shape=jax.ShapeDtypeStruct((gsz, t, rw), jnp.float32),
        compiler_params=pltpu.CompilerParams(
            dimension_semantics=("arbitrary", "arbitrary"), vmem_limit_bytes=VMEM_LIMIT_BYTES),
        name="rwkv_post",
    )(y, g, bonus, ln_w.reshape(1, rw), ln_b.reshape(1, rw))


def _hier_route(logits):
    n = logits.shape[0]
    pg = jax.nn.softmax(logits[:, :N_GROUPS], axis=-1)
    g_val, g_sel = lax.top_k(pg, 1)
    le = logits[:, N_GROUPS:N_GROUPS + N_EXPERTS].reshape(n, N_GROUPS, EXPERTS_PER_GROUP)
    le_g = jnp.take_along_axis(le, g_sel[:, :, None], axis=1)[:, 0]
    e_val, e_sel = lax.top_k(le_g, EXPERT_TOP_K)
    weights = jax.nn.softmax(e_val, axis=-1) * g_val
    return g_sel * EXPERTS_PER_GROUP + e_sel, weights


def _moe_ffn(h_pad, eid, w_gate, w_up, w_down):
    n, d = h_pad.shape[0] - 1, h_pad.shape[1]
    a_tot = n * EXPERT_TOP_K
    flat_e = eid.reshape(-1)
    onehot = (flat_e[:, None] == jnp.arange(N_EXPERTS)[None, :]).astype(jnp.int32)
    csum = jnp.cumsum(onehot, axis=0)
    rank = jnp.take_along_axis(csum, flat_e[:, None], axis=1)[:, 0] - 1
    counts = csum[-1]
    padded = (counts + MOE_TM - 1) // MOE_TM * MOE_TM
    pad_end = jnp.cumsum(padded)
    dest = (pad_end - padded)[flat_e] + rank
    n_blk = (a_tot + N_EXPERTS * (MOE_TM - 1)) // MOE_TM
    tok_buf = jnp.full((n_blk * MOE_TM,), n, jnp.int32).at[dest].set(jnp.arange(a_tot, dtype=jnp.int32) // EXPERT_TOP_K)
    xb = h_pad[tok_buf].reshape(n_blk, MOE_TM, d)
    blk_start = jnp.arange(n_blk, dtype=jnp.int32) * MOE_TM
    blk_e = jnp.minimum(jnp.sum((pad_end[None, :] <= blk_start[:, None]).astype(jnp.int32), axis=1), N_EXPERTS - 1)
    meta = jnp.concatenate([blk_e, pad_end[-1:] // MOE_TM]).astype(jnp.int32)
    yb = _moe_blocks(xb, meta, w_gate, w_up, w_down).reshape(n_blk * MOE_TM, d)
    dest2 = dest.reshape(n, EXPERT_TOP_K)
    return yb[dest2[:, 0]], yb[dest2[:, 1]]


def _layer_front(x, mod, lp, layer, pool_cmp, pool_slc, page_table, win_buf, wkv0, shift0, past_len, rows):
    B, T, D = x.shape
    groups = B * T // rows
    per_token = rows > T
    mods = jnp.repeat(mod, T, axis=0).reshape(groups, rows, 6 * D) if per_token else mod[:, None, :]
    sh1, sc1, gt1, sh2, sc2, gt2 = jnp.split(mods, 6, axis=-1)
    xg = x.reshape(groups, rows, D)
    tm_in = min(rows, 512)
    proj, h, kv_rows = _norm_in(xg, lp['norm1'], sc1, sh1, lp['w_in_b'], tm_in)
    kv_shape = (B, T, 2, NSA_KV_HEADS, HEAD_DIM)
    cmp_rows, slc_rows, win_rows = (kv_rows[n].reshape(B, T * KV_PARTS, HEAD_DIM) for n in range(KV_TILES))
    cmp_new, slc_new = cmp_rows.reshape(kv_shape), slc_rows.reshape(kv_shape)

    if pool_cmp is None:
        kvc = _compress_prompt(cmp_rows, _cmp_weights(lp))
        o_nsa = _nsa_prompt(proj.reshape(B * T, IN_WIDTH), kvc, B, T)
        win_len = min(WINDOW, past_len)
        assert T >= win_len
        win_state = win_rows[:, (T - win_len) * KV_PARTS:].reshape((B, win_len) + kv_shape[2:])
    else:
        n_past = page_table.shape[1] * PAGE_SIZE
        assert n_past % CMP_STRIDE == 0 and T < CMP_STRIDE and T <= 8
        assert -(-(n_past + T) // SEL_BLOCK) <= TOPK_LANES and n_past % SEL_BLOCK == 0 and T <= SEL_BLOCK
        assert win_buf.shape[1] == WINDOW
        pages = page_table + layer * pool_cmp.shape[1]
        kvc = _compress_paged(pool_cmp.reshape(-1, HEAD_DIM), pages, _cmp_weights(lp))
        win_buf_rows = win_buf.reshape(B, WINDOW * KV_PARTS, HEAD_DIM)
        o_nsa = _nsa_decode_rows(proj.reshape(B, T, IN_WIDTH), kvc, win_buf_rows, pool_slc.reshape(-1, HEAD_DIM),
                                 pages, n_past)
        win_state = jnp.concatenate([win_buf_rows[:, T * KV_PARTS:], win_rows], axis=1).reshape(
            (B, WINDOW) + kv_shape[2:])

    shift0 = shift0.astype(h.dtype)
    p0 = _matmul(shift0, lp['w_in_b'][:, RKV_OFF:CMP_OFF], B, RWKV_WIDTH)
    if per_token:
        h0 = jnp.repeat(shift0, T, axis=0).reshape(groups, rows, D)
        p0 = jnp.repeat(p0, T, axis=0).reshape(groups, rows, 3 * RWKV_WIDTH)
    else:
        h0, p0 = shift0[:, None], p0[:, None]
    tm_rw = min(rows, 256)
    seqs = _rwkv_prep(h, proj, h0, p0, lp, tm_rw, T, pack_major=not per_token)
    seqs, (gate, bonus) = seqs[:8], seqs[8:]
    if per_token:
        t_pad = -(-T // WKV_MIN_CHUNK) * WKV_MIN_CHUNK

        def pairs(a, fill):
            a = jnp.pad(a.reshape(B, T, RW_PACKS, WKV_LANES), ((0, 0), (0, t_pad - T), (0, 0), (0, 0)),
                        constant_values=fill)
            return a.transpose(0, 2, 1, 3)

        fills = (0.0, 1.0) + (0.0,) * 6
        y, wkv_T = _wkv_scan(*[pairs(a, f) for a, f in zip(seqs, fills)], wkv0.astype(jnp.float32), WKV_MIN_CHUNK)
        y = y.transpose(0, 2, 1, 3)[:, :T].reshape(groups, rows, RWKV_WIDTH)
    else:
        y, wkv_T = _wkv_scan(*seqs, wkv0.astype(jnp.float32), WKV_CHUNK)
    o_rwkv = _rwkv_post(y, gate, bonus, lp['ln_x_w'], lp['ln_x_b'], tm_rw, pack_major=not per_token)

    x1, h2, logits = _mix_out(o_nsa.reshape(groups, rows, NSA_WIDTH), o_rwkv, lp['w_out_b'], xg, gt1, sc2, sh2,
                              lp['norm2'], lp['wr_hi'], lp['wr_lo'], lp['br'], min(rows, 256))
    return (x1, h2, logits, gt2), (cmp_new, slc_new, win_state, wkv_T, h.reshape(B, T, D)[:, -1])


def kernel(x_prompt, x_sample, c_prompt, c_sample, cache_cmp_kv, cache_slc_kv, page_table, state_win_kv, state_wkv, state_shift, w_ada, b_ada, norm1, w_in, cmp_k_w1, cmp_k_pe, cmp_k_w2, cmp_v_w1, cmp_v_pe, cmp_v_w2, mu_rkv, mu_wag, decay_w0, decay_w1, decay_w2, iclr_a0, iclr_a1, iclr_a2, gate_g1, gate_g2, k_k, k_a, r_k, ln_x_w, ln_x_b, w_out, norm2, w_router_group, b_router_group, w_router_expert, b_router_expert, w_gate, w_up, w_down, norm_f):
    bp, tp = x_prompt.shape[:2]
    ts = x_sample.shape[1]
    past_len = page_table.shape[1] * PAGE_SIZE
    assert DEPTH == 1 and w_in.shape[0] == 1
    l = 0
    bs = x_sample.shape[0]
    f32, bf16 = jnp.float32, jnp.bfloat16
    lyr = lambda a: a.reshape(a.shape[1:])
    wr = jnp.concatenate([lyr(w_router_group), lyr(w_router_expert),
                          jnp.zeros((D_MODEL, ROUTER_LANES - N_GROUPS - N_EXPERTS), f32)], axis=1)
    wr_hi = wr.astype(bf16)
    br = jnp.concatenate([lyr(b_router_group), lyr(b_router_expert),
                          jnp.zeros((ROUTER_LANES - N_GROUPS - N_EXPERTS,), f32)]).reshape(1, ROUTER_LANES)
    lp = dict(norm1=lyr(norm1), w_in_b=_permute_w_in(lyr(w_in)).astype(bf16),
              cmp_k_w1=lyr(cmp_k_w1), cmp_k_pe=lyr(cmp_k_pe), cmp_k_w2=lyr(cmp_k_w2),
              cmp_v_w1=lyr(cmp_v_w1), cmp_v_pe=lyr(cmp_v_pe), cmp_v_w2=lyr(cmp_v_w2),
              mu_rkv=lyr(mu_rkv), mu_wag=lyr(mu_wag), decay_w0=lyr(decay_w0), decay_w1=lyr(decay_w1),
              decay_w2=lyr(decay_w2), iclr_a0=lyr(iclr_a0), iclr_a1=lyr(iclr_a1), iclr_a2=lyr(iclr_a2),
              gate_g1=lyr(gate_g1), gate_g2=lyr(gate_g2), k_k=lyr(k_k), k_a=lyr(k_a), r_k=lyr(r_k),
              ln_x_w=lyr(ln_x_w), ln_x_b=lyr(ln_x_b), w_out_b=lyr(w_out).astype(bf16), norm2=lyr(norm2),
              wr_hi=wr_hi, wr_lo=(wr - wr_hi.astype(f32)).astype(bf16), br=br)

    c_all = jnp.concatenate([c_prompt, c_sample], axis=0)
    mod_all = _matmul(jax.nn.silu(c_all), lyr(w_ada), c_all.shape[0], 1024) + lyr(b_ada)
    mod_p, mod_s = mod_all[:bp], mod_all[bp:]

    wkv_zero = jnp.zeros((bp, RWKV_HEADS, RWKV_HEAD_DIM, RWKV_HEAD_DIM), f32)
    shift_zero = jnp.zeros((bp, D_MODEL), x_prompt.dtype)
    (x1p, h2p, lgp, gt2p), (a1, a2, a3, a4, a5) = _layer_front(
        x_prompt, mod_p, lp, l, None, None, None, None, wkv_zero, shift_zero, past_len, tp)
    (x1s, h2s, lgs, gt2s), (b1, b2, b3, b4, b5) = _layer_front(
        x_sample, mod_s, lp, l, cache_cmp_kv, cache_slc_kv, page_table, lyr(state_win_kv), lyr(state_wkv),
        lyr(state_shift), past_len, bs * ts)

    n_p, n_s = bp * tp, bs * ts
    h2 = jnp.concatenate([h2p.reshape(n_p, D_MODEL), h2s.reshape(n_s, D_MODEL), jnp.zeros((1, D_MODEL), bf16)], axis=0)
    eid, ew = _hier_route(jnp.concatenate([lgp.reshape(n_p, ROUTER_LANES), lgs.reshape(n_s, ROUTER_LANES)], axis=0))
    y0, y1 = _moe_ffn(h2, eid, lyr(w_gate), lyr(w_up), lyr(w_down))
    y_prompt = _final(x1p, y0, y1, ew, 0, gt2p, norm_f, 256).reshape(x_prompt.shape)
    y_sample = _final(x1s, y0, y1, ew, n_p, gt2s, norm_f, x1s.shape[1]).reshape(x_sample.shape)
    st = lambda a: a[None]
    return (y_prompt, y_sample, st(a1), st(b1), st(a2), st(b2), st(a3), st(b3), st(a4), st(b4), st(a5), st(b5))
```

```python
import functools

import jax
import jax.numpy as jnp
from jax import lax
from jax.experimental import pallas as pl
from jax.experimental.pallas import tpu as pltpu

D_MODEL = 2048
DEPTH = 1
PAGE_SIZE = 128
HEAD_DIM = 128
NSA_WIDTH = D_MODEL // 2
NSA_HEADS = NSA_WIDTH // HEAD_DIM
NSA_KV_HEADS = 2
HPG = NSA_HEADS // NSA_KV_HEADS
KV_WIDTH = NSA_KV_HEADS * HEAD_DIM
CMP_BLOCK = 32
CMP_STRIDE = 16
SEL_BLOCK = 64
N_SELECT = 16
WINDOW = 512
FORCE_BONUS = 1e4
RWKV_WIDTH = D_MODEL - NSA_WIDTH
RWKV_HEAD_DIM = 64
RWKV_HEADS = RWKV_WIDTH // RWKV_HEAD_DIM
GN_EPS = 64e-5
N_GROUPS = 4
EXPERTS_PER_GROUP = 8
N_EXPERTS = N_GROUPS * EXPERTS_PER_GROUP
EXPERT_TOP_K = 2
NORM_EPS = 1e-6
NEG_INF = -1e30
W_Q_OFF = 0
W_CMP_OFF = W_Q_OFF + NSA_WIDTH
W_RKV_OFF = W_CMP_OFF + 6 * KV_WIDTH
W_GATE_OFF = W_RKV_OFF + 3 * RWKV_WIDTH
IN_WIDTH = W_GATE_OFF + 3 * NSA_HEADS
Q_OFF = 0
RKV_OFF = Q_OFF + NSA_WIDTH
CMP_OFF = RKV_OFF + 3 * RWKV_WIDTH
SLC_OFF = CMP_OFF + 2 * KV_WIDTH
WIN_OFF = SLC_OFF + 2 * KV_WIDTH
GATE_OFF = WIN_OFF + 2 * KV_WIDTH
assert GATE_OFF == W_GATE_OFF


def _permute_w_in(w):
    return jnp.concatenate([w[:, W_Q_OFF:W_CMP_OFF], w[:, W_RKV_OFF:W_GATE_OFF], w[:, W_CMP_OFF:W_RKV_OFF],
                            w[:, W_GATE_OFF:]], axis=1)

VMEM_LIMIT_BYTES = 48 * 1024 * 1024


def _mm_kernel(x_ref, w_ref, o_ref):
    o_ref[...] = jnp.dot(x_ref[...].astype(jnp.bfloat16), w_ref[...].astype(jnp.bfloat16),
                         preferred_element_type=jnp.float32)


def _matmul(x, w, tm, tn):
    m, k = x.shape
    n = w.shape[1]
    return pl.pallas_call(
        _mm_kernel,
        grid=(pl.cdiv(m, tm), pl.cdiv(n, tn)),
        in_specs=[pl.BlockSpec((tm, k), lambda i, j: (i, 0)),
                  pl.BlockSpec((k, tn), lambda i, j: (0, j))],
        out_specs=pl.BlockSpec((tm, tn), lambda i, j: (i, j)),
        out_shape=jax.ShapeDtypeStruct((m, n), jnp.float32),
        compiler_params=pltpu.CompilerParams(
            dimension_semantics=("arbitrary", "arbitrary"), vmem_limit_bytes=VMEM_LIMIT_BYTES),
        name="matmul",
    )(x, w)


MOE_TM = 256


def _moe_block_kernel(meta_ref, x_ref, wg_ref, wu_ref, wd_ref, o_ref, wg_b, wu_b, wd_b):
    bf16 = jnp.bfloat16
    i = pl.program_id(0)
    used = i < meta_ref[pl.num_programs(0)]
    new_expert = (i == 0) | (meta_ref[i] != meta_ref[jnp.maximum(i - 1, 0)])

    @pl.when(used & new_expert)
    def _():
        wg_b[...] = wg_ref[0].astype(bf16)
        wu_b[...] = wu_ref[0].astype(bf16)
        wd_b[...] = wd_ref[0].astype(bf16)

    @pl.when(used)
    def _():
        x = x_ref[0]
        g = jnp.dot(x, wg_b[...], preferred_element_type=jnp.float32)
        u = jnp.dot(x, wu_b[...], preferred_element_type=jnp.float32)
        hmid = (g * jax.nn.sigmoid(g)) * u
        o_ref[0] = jnp.dot(hmid.astype(bf16), wd_b[...], preferred_element_type=jnp.float32)

    @pl.when(jnp.logical_not(used))
    def _():
        o_ref[...] = jnp.zeros(o_ref.shape, o_ref.dtype)


def _moe_blocks(xb, meta, w_gate, w_up, w_down):
    n_blk, mb, d = xb.shape
    de = w_gate.shape[2]
    grid_spec = pltpu.PrefetchScalarGridSpec(
        num_scalar_prefetch=1,
        grid=(n_blk,),
        in_specs=[pl.BlockSpec((1, mb, d), lambda i, e: (i, 0, 0)),
                  pl.BlockSpec((1, d, de), lambda i, e: (e[i], 0, 0)),
                  pl.BlockSpec((1, d, de), lambda i, e: (e[i], 0, 0)),
                  pl.BlockSpec((1, de, d), lambda i, e: (e[i], 0, 0))],
        out_specs=pl.BlockSpec((1, mb, d), lambda i, e: (i, 0, 0)),
        scratch_shapes=[pltpu.VMEM((d, de), jnp.bfloat16), pltpu.VMEM((d, de), jnp.bfloat16),
                        pltpu.VMEM((de, d), jnp.bfloat16)],
    )
    return pl.pallas_call(
        _moe_block_kernel,
        grid_spec=grid_spec,
        out_shape=jax.ShapeDtypeStruct((n_blk, mb, d), jnp.float32),
        compiler_params=pltpu.CompilerParams(
            dimension_semantics=("arbitrary",), vmem_limit_bytes=VMEM_LIMIT_BYTES),
        name="moe_blocks",
    )(meta, xb, w_gate, w_up, w_down)


ROUTER_LANES = 128


KV_TILES = 3
KV_PARTS = 2 * NSA_KV_HEADS


def _norm_in_kernel(x_ref, n1_ref, sc_ref, sh_ref, w_ref, proj_ref, h_ref, kv_ref, hb_scr, *, kv_tile0):
    j = pl.program_id(2)

    @pl.when(j == 0)
    def _():
        x = x_ref[0]
        y = x * lax.rsqrt(jnp.mean(x * x, axis=-1, keepdims=True) + NORM_EPS) * n1_ref[...]
        h = y * (1.0 + sc_ref[0]) + sh_ref[0]
        h_ref[0] = h
        hb_scr[...] = h.astype(jnp.bfloat16)

    proj_ref[0] = jnp.dot(hb_scr[...], w_ref[...], preferred_element_type=jnp.float32)

    @pl.when((j >= kv_tile0) & (j < kv_tile0 + KV_TILES))
    def _():
        tm = proj_ref.shape[1]
        for part in range(KV_PARTS):
            kv_ref[0, 0, pl.ds(part, tm, stride=KV_PARTS), :] = proj_ref[0, :, part * HEAD_DIM:(part + 1) * HEAD_DIM]


def _norm_in(x, norm1, sc, sh, w_in_b, tm):
    g, t, d = x.shape
    n = w_in_b.shape[1]
    tn = 2 * KV_WIDTH
    kv_tile0 = CMP_OFF // tn
    assert CMP_OFF % tn == 0 and GATE_OFF == CMP_OFF + KV_TILES * tn
    mrows = sc.shape[1]
    mod_spec = pl.BlockSpec((1, mrows if mrows == 1 else tm, d),
                            (lambda b, i, j: (b, 0, 0)) if mrows == 1 else (lambda b, i, j: (b, i, 0)))
    return pl.pallas_call(
        functools.partial(_norm_in_kernel, kv_tile0=kv_tile0),
        grid=(g, t // tm, pl.cdiv(n, tn)),
        in_specs=[pl.BlockSpec((1, tm, d), lambda b, i, j: (b, i, 0)),
                  pl.BlockSpec((1, d), lambda b, i, j: (0, 0)),
                  mod_spec, mod_spec,
                  pl.BlockSpec((d, tn), lambda b, i, j: (0, j))],
        out_specs=[pl.BlockSpec((1, tm, tn), lambda b, i, j: (b, i, j)),
                   pl.BlockSpec((1, tm, d), lambda b, i, j: (b, i, 0)),
                   pl.BlockSpec((1, 1, tm * KV_PARTS, HEAD_DIM),
                                lambda b, i, j: (jnp.clip(j - kv_tile0, 0, KV_TILES - 1), b, i, 0))],
        out_shape=[jax.ShapeDtypeStruct((g, t, n), jnp.float32), jax.ShapeDtypeStruct((g, t, d), jnp.float32),
                   jax.ShapeDtypeStruct((KV_TILES, g, t * KV_PARTS, HEAD_DIM), jnp.float32)],
        scratch_shapes=[pltpu.VMEM((tm, d), jnp.bfloat16)],
        compiler_params=pltpu.CompilerParams(
            dimension_semantics=("arbitrary", "arbitrary", "arbitrary"), vmem_limit_bytes=VMEM_LIMIT_BYTES),
        name="norm_in",
    )(x, norm1.reshape(1, d), sc, sh, w_in_b)


def _mix_out_kernel(on_ref, orw_ref, w_ref, x_ref, gt_ref, sc_ref, sh_ref, n2_ref, wr_hi_ref, wr_lo_ref, br_ref,
                    x1_ref, h2_ref, lg_ref):
    f32, bf16 = jnp.float32, jnp.bfloat16
    half = on_ref.shape[2]
    mixed = (jnp.dot(on_ref[0].astype(bf16), w_ref[0:half, :], preferred_element_type=f32)
             + jnp.dot(orw_ref[0].astype(bf16), w_ref[half:, :], preferred_element_type=f32))
    x1 = x_ref[0] + gt_ref[0] * mixed
    x1_ref[0] = x1
    y = x1 * lax.rsqrt(jnp.mean(x1 * x1, axis=-1, keepdims=True) + NORM_EPS) * n2_ref[...]
    h2 = y * (1.0 + sc_ref[0]) + sh_ref[0]
    hi = h2.astype(bf16)
    h2_ref[0] = hi
    lo = (h2 - hi.astype(f32)).astype(bf16)
    lg_ref[0] = (jnp.dot(hi, wr_hi_ref[...], preferred_element_type=f32)
                 + jnp.dot(hi, wr_lo_ref[...], preferred_element_type=f32)
                 + jnp.dot(lo, wr_hi_ref[...], preferred_element_type=f32) + br_ref[...])


def _mix_out(o_nsa, o_rwkv, w_out_b, x, gt, sc, sh, norm2, wr_hi, wr_lo, br, tm):
    g, t, d = x.shape
    half = o_nsa.shape[2]
    mrows = sc.shape[1]
    mod_spec = pl.BlockSpec((1, mrows if mrows == 1 else tm, d),
                            (lambda b, i: (b, 0, 0)) if mrows == 1 else (lambda b, i: (b, i, 0)))
    row = lambda w: pl.BlockSpec((1, tm, w), lambda b, i: (b, i, 0))
    full = lambda a: pl.BlockSpec(a.shape, lambda b, i: (0,) * a.ndim)
    n2 = norm2.reshape(1, d)
    return pl.pallas_call(
        _mix_out_kernel,
        grid=(g, t // tm),
        in_specs=[row(half), row(half), full(w_out_b), row(d), mod_spec, mod_spec, mod_spec, full(n2),
                  full(wr_hi), full(wr_lo), full(br)],
        out_specs=[row(d), row(d), row(ROUTER_LANES)],
        out_shape=[jax.ShapeDtypeStruct((g, t, d), jnp.float32), jax.ShapeDtypeStruct((g, t, d), jnp.bfloat16),
                   jax.ShapeDtypeStruct((g, t, ROUTER_LANES), jnp.float32)],
        compiler_params=pltpu.CompilerParams(
            dimension_semantics=("arbitrary", "arbitrary"), vmem_limit_bytes=VMEM_LIMIT_BYTES),
        name="mix_out",
    )(o_nsa, o_rwkv, w_out_b, x, gt, sc, sh, n2, wr_hi, wr_lo, br)


def _final_kernel(x_ref, y0_ref, y1_ref, ew_ref, gt_ref, nf_ref, o_ref):
    ew = ew_ref[...]
    ffn = y0_ref[...] * ew[:, 0:1] + y1_ref[...] * ew[:, 1:2]
    x2 = x_ref[0] + gt_ref[0] * ffn
    o_ref[0] = x2 * lax.rsqrt(jnp.mean(x2 * x2, axis=-1, keepdims=True) + NORM_EPS) * nf_ref[...]


def _final(x1, y0, y1, ew, row_off, gt, norm_f, tm):
    g, t, d = x1.shape
    assert row_off % tm == 0
    mrows = gt.shape[1]
    mod_spec = pl.BlockSpec((1, mrows if mrows == 1 else tm, d),
                            (lambda b, i: (b, 0, 0)) if mrows == 1 else (lambda b, i: (b, i, 0)))
    row = pl.BlockSpec((1, tm, d), lambda b, i: (b, i, 0))
    flat = lambda w: pl.BlockSpec((tm, w), lambda b, i: (row_off // tm + b * (t // tm) + i, 0))
    return pl.pallas_call(
        _final_kernel,
        grid=(g, t // tm),
        in_specs=[row, flat(d), flat(d), flat(EXPERT_TOP_K), mod_spec, pl.BlockSpec((1, d), lambda b, i: (0, 0))],
        out_specs=row,
        out_shape=jax.ShapeDtypeStruct((g, t, d), jnp.float32),
        compiler_params=pltpu.CompilerParams(
            dimension_semantics=("arbitrary", "arbitrary"), vmem_limit_bytes=VMEM_LIMIT_BYTES),
        name="final_norm",
    )(x1, y0, y1, ew, gt, norm_f.reshape(1, d))


def _cmp_partial_kernel(*refs, n_src, rows_per_src):
    x_refs = refs[:n_src]
    w1k_ref, w1v_ref, a_ref, b_ref = refs[n_src:]
    nch_src = rows_per_src // CMP_STRIDE
    for kvg in range(4):
        w_ref = w1k_ref if kvg < 2 else w1v_ref
        acc = None
        for p in range(CMP_STRIDE):
            parts = [x_refs[s][pl.ds(4 * p + kvg, nch_src, stride=4 * CMP_STRIDE), :] for s in range(n_src)]
            xp = parts[0] if n_src == 1 else jnp.concatenate(parts, axis=0)
            d = jnp.dot(xp.astype(jnp.bfloat16), w_ref[p], preferred_element_type=jnp.float32)
            acc = d if acc is None else acc + d
        a_ref[0, :, kvg * HEAD_DIM:(kvg + 1) * HEAD_DIM] = acc[:, :HEAD_DIM]
        b_ref[0, :, kvg * HEAD_DIM:(kvg + 1) * HEAD_DIM] = acc[:, HEAD_DIM:]


def _cmp_finish_kernel(a_ref, b_ref, pek_ref, pev_ref, w1k_ref, w1v_ref, w2k_ref, w2v_ref, o_ref):
    nch = a_ref.shape[1]
    for kv, (pe_ref, w1_ref, w2_ref) in enumerate(((pek_ref, w1k_ref, w2k_ref), (pev_ref, w1v_ref, w2v_ref))):
        pe8 = jnp.broadcast_to(pe_ref[...], (8, pe_ref.shape[1])).astype(jnp.bfloat16)
        pterm = jnp.dot(pe8, w1_ref[...], preferred_element_type=jnp.float32)[0:1]
        w2 = w2_ref[...]
        for g in range(NSA_KV_HEADS):
            lo = (kv * NSA_KV_HEADS + g) * HEAD_DIM
            nxt = pltpu.roll(b_ref[0, :, lo:lo + HEAD_DIM], nch - 1, 0)
            pre = a_ref[0, :, lo:lo + HEAD_DIM] + nxt + pterm
            act = pre * jax.nn.sigmoid(pre)
            o_ref[0, :, lo:lo + HEAD_DIM] = jnp.dot(act.astype(jnp.bfloat16), w2, preferred_element_type=jnp.float32)


def _cmp_weights(lp):
    bf = jnp.bfloat16
    half = CMP_BLOCK // 2
    cat = lambda w: jnp.concatenate([w[:half], w[half:]], axis=-1).astype(bf)
    flat = lambda w: w.reshape(CMP_BLOCK * HEAD_DIM, HEAD_DIM).astype(bf)
    return dict(w1k_cat=cat(lp['cmp_k_w1']), w1v_cat=cat(lp['cmp_v_w1']),
                w1k_flat=flat(lp['cmp_k_w1']), w1v_flat=flat(lp['cmp_v_w1']),
                pek=lp['cmp_k_pe'].reshape(1, -1), pev=lp['cmp_v_pe'].reshape(1, -1),
                w2k=lp['cmp_k_w2'].astype(bf), w2v=lp['cmp_v_w2'].astype(bf))


def _cmp_finish(a, b, cw):
    bsz, nch, _ = a.shape
    full = lambda arr: pl.BlockSpec(arr.shape, lambda i: (0,) * arr.ndim)
    blk = pl.BlockSpec((1, nch, 4 * HEAD_DIM), lambda i: (i, 0, 0))
    ws = [cw['pek'], cw['pev'], cw['w1k_flat'], cw['w1v_flat'], cw['w2k'], cw['w2v']]
    return pl.pallas_call(
        _cmp_finish_kernel,
        grid=(bsz,),
        in_specs=[blk, blk] + [full(w) for w in ws],
        out_specs=blk,
        out_shape=jax.ShapeDtypeStruct((bsz, nch, 4 * HEAD_DIM), jnp.float32),
        compiler_params=pltpu.CompilerParams(dimension_semantics=("arbitrary",), vmem_limit_bytes=VMEM_LIMIT_BYTES),
        name="cmp_finish",
    )(a, b, *ws)


def _compress_prompt(rows4, cw):
    bsz, seq = rows4.shape[0], rows4.shape[1] // 4
    nch = seq // CMP_STRIDE
    x_specs = [pl.BlockSpec((None, seq * 4, HEAD_DIM), lambda i: (i, 0, 0))]
    w_spec = pl.BlockSpec(cw['w1k_cat'].shape, lambda i: (0, 0, 0))
    out_spec = pl.BlockSpec((1, nch, 4 * HEAD_DIM), lambda i: (i, 0, 0))
    shp = jax.ShapeDtypeStruct((bsz, nch, 4 * HEAD_DIM), jnp.float32)
    a, b = pl.pallas_call(
        functools.partial(_cmp_partial_kernel, n_src=1, rows_per_src=seq),
        grid=(bsz,),
        in_specs=x_specs + [w_spec, w_spec],
        out_specs=[out_spec, out_spec],
        out_shape=[shp, shp],
        compiler_params=pltpu.CompilerParams(dimension_semantics=("arbitrary",), vmem_limit_bytes=VMEM_LIMIT_BYTES),
        name="cmp_partial_prompt",
    )(rows4, cw['w1k_cat'], cw['w1v_cat'])
    return _cmp_finish(a, b, cw)


_NT = (((1,), (1,)), ((), ()))
LOG2E = 1.4426950408889634
SEL_TK = 512
WIN_TK = 256


def _flash_update(s, v, m_ref, l_ref, acc_ref, h):
    tk = s.shape[1]
    m_prev = m_ref[h]
    m_new = jnp.maximum(m_prev, jnp.max(s, axis=-1, keepdims=True))
    alpha = jnp.exp2(m_prev - m_new)
    p = jnp.exp2(s - jnp.concatenate([m_new] * (tk // HEAD_DIM), axis=1))
    l_ref[h] = alpha * l_ref[h] + jnp.sum(p, axis=-1, keepdims=True)
    acc_ref[h] = alpha * acc_ref[h] + jnp.dot(p.astype(jnp.bfloat16), v, preferred_element_type=jnp.float32)
    m_ref[h] = m_new


def _nsa_prompt_kernel(q_ref, slc_ref, win_ref, gate_ref, kvc_ref, o_ref, m_ref, l_ref, acc_ref, *, tq, seq):
    f32, bf16 = jnp.float32, jnp.bfloat16
    qi = pl.program_id(1)
    t0 = qi * tq
    scale = HEAD_DIM ** -0.5
    nc_valid = seq // CMP_STRIDE - CMP_BLOCK // CMP_STRIDE + 1
    n_sel = seq // SEL_BLOCK
    pos = t0 + lax.broadcasted_iota(jnp.int32, (tq, 1), 0)
    lane = lax.broadcasted_iota(jnp.int32, (1, HEAD_DIM), 1)
    gates = jax.nn.sigmoid(gate_ref[...])
    dist_c = pos - (lane * CMP_STRIDE + (CMP_BLOCK - 1))
    valid_c = (dist_c >= 0) & (lane < nc_valid)
    dist_cf = dist_c.astype(f32)
    c_row = lax.broadcasted_iota(jnp.int32, (HEAD_DIM, 1), 0)
    overlap = jnp.where((c_row * CMP_STRIDE <= lane * SEL_BLOCK + (SEL_BLOCK - 1))
                        & (c_row * CMP_STRIDE + (CMP_BLOCK - 1) >= lane * SEL_BLOCK), 1.0, 0.0).astype(bf16)
    j_row = lax.broadcasted_iota(jnp.int32, (n_sel, 1), 0)
    blk_lane = jnp.right_shift(t0 + lax.broadcasted_iota(jnp.int32, (1, tq), 1), 6)
    forced_t = (j_row == 0) | (j_row == blk_lane) | (j_row == blk_lane - 1)

    def gate_col(branch, hh):
        c = branch * NSA_HEADS + hh
        return gates[:, c:c + 1]

    def reset():
        m_ref[...] = jnp.full(m_ref.shape, NEG_INF, f32)
        l_ref[...] = jnp.zeros(l_ref.shape, f32)
        acc_ref[...] = jnp.zeros(acc_ref.shape, f32)

    for g in range(NSA_KV_HEADS):
        kcol = slice(g * HEAD_DIM, (g + 1) * HEAD_DIM)
        vcol = slice((NSA_KV_HEADS + g) * HEAD_DIM, (NSA_KV_HEADS + g + 1) * HEAD_DIM)
        heads = [g * HPG + h for h in range(HPG)]
        slopes = [2.0 ** -(hh + 1) for hh in heads]

        kc = kvc_ref[0, :, kcol].astype(bf16)
        vc = kvc_ref[0, :, vcol].astype(bf16)
        psum = jnp.zeros((tq, HEAD_DIM), f32)
        for h, hh in enumerate(heads):
            qh = q_ref[:, hh * HEAD_DIM:(hh + 1) * HEAD_DIM].astype(bf16)
            s = lax.dot_general(qh, kc, _NT, preferred_element_type=f32) * scale - slopes[h] * dist_cf
            s = jnp.where(valid_c, s, NEG_INF)
            e = jnp.exp(s - jnp.max(s, axis=-1, keepdims=True))
            p = e / jnp.sum(e, axis=-1, keepdims=True)
            p = jnp.where(valid_c, p, 0.0)
            o_cmp = jnp.dot(p.astype(bf16), vc, preferred_element_type=f32)
            o_ref[:, hh * HEAD_DIM:(hh + 1) * HEAD_DIM] = gate_col(0, hh) * o_cmp
            psum = psum + p
        p_hi = psum.astype(bf16)
        p_lo = (psum - p_hi.astype(f32)).astype(bf16)
        imp = (jnp.dot(p_hi, overlap, preferred_element_type=f32)
               + jnp.dot(p_lo, overlap, preferred_element_type=f32))
        imp = imp.T[:n_sel]
        imp = jnp.where(forced_t, imp + FORCE_BONUS, imp)
        imp = jnp.where(j_row <= blk_lane, imp, NEG_INF)
        beaten = jnp.zeros((n_sel, tq), f32)
        for jp in range(n_sel):
            other = imp[jp:jp + 1]
            tie = jnp.where(j_row > jp, 1.0, 0.0)
            beaten = beaten + jnp.where(other > imp, 1.0, jnp.where(other == imp, tie, 0.0))
        sel_t = jnp.where(beaten < N_SELECT, jnp.where(imp > 0.5 * NEG_INF, 1.0, 0.0), 0.0).astype(bf16)

        reset()

        def sel_body(kt, carry):
            k0 = pl.multiple_of(kt * SEL_TK, SEL_TK)
            k = slc_ref[pl.ds(k0, SEL_TK), kcol].astype(bf16)
            v = slc_ref[pl.ds(k0, SEL_TK), vcol].astype(bf16)
            kpos = k0 + lax.broadcasted_iota(jnp.int32, (1, SEL_TK), 1)
            dist = pos - kpos
            expand = jnp.where(jnp.right_shift(kpos, 6) == j_row, 1.0, 0.0).astype(bf16)
            picked = lax.dot_general(sel_t, expand, (((0,), (0,)), ((), ())), preferred_element_type=f32)
            keep = jnp.where(dist >= 0, picked, 0.0) > 0.5
            kpos_f = (pos[0:1] - dist[0:1]).astype(f32)
            for h, hh in enumerate(heads):
                qh = q_ref[:, hh * HEAD_DIM:(hh + 1) * HEAD_DIM].astype(bf16)
                s = (lax.dot_general(qh, k, _NT, preferred_element_type=f32) * (scale * LOG2E)
                     + (slopes[h] * LOG2E) * kpos_f)
                _flash_update(jnp.where(keep, s, NEG_INF), v, m_ref, l_ref, acc_ref, h)
            return carry

        lax.fori_loop(0, (t0 + tq - 1) // SEL_TK + 1, sel_body, 0)
        for h, hh in enumerate(heads):
            hs = slice(hh * HEAD_DIM, (hh + 1) * HEAD_DIM)
            o_ref[:, hs] = o_ref[:, hs] + gate_col(1, hh) * (acc_ref[h] / l_ref[h])

        reset()

        def win_body(kt, carry):
            k0 = pl.multiple_of(kt * WIN_TK, WIN_TK)
            k = win_ref[pl.ds(k0, WIN_TK), kcol].astype(bf16)
            v = win_ref[pl.ds(k0, WIN_TK), vcol].astype(bf16)
            dist = pos - (k0 + lax.broadcasted_iota(jnp.int32, (1, WIN_TK), 1))
            keep = (dist >= 0) & (dist < WINDOW)
            kpos_f = (pos[0:1] - dist[0:1]).astype(f32)
            for h, hh in enumerate(heads):
                qh = q_ref[:, hh * HEAD_DIM:(hh + 1) * HEAD_DIM].astype(bf16)
                s = (lax.dot_general(qh, k, _NT, preferred_element_type=f32) * (scale * LOG2E)
                     + (slopes[h] * LOG2E) * kpos_f)
                _flash_update(jnp.where(keep, s, NEG_INF), v, m_ref, l_ref, acc_ref, h)
            return carry

        lax.fori_loop(jnp.maximum(t0 - (WINDOW - 1), 0) // WIN_TK, (t0 + tq - 1) // WIN_TK + 1, win_body, 0)
        for h, hh in enumerate(heads):
            hs = slice(hh * HEAD_DIM, (hh + 1) * HEAD_DIM)
            o_ref[:, hs] = o_ref[:, hs] + gate_col(2, hh) * (acc_ref[h] / l_ref[h])


def _nsa_prompt(proj2d, kvc, bsz, seq, tq=256):
    nq = seq // tq
    kvw = 2 * KV_WIDTH
    return pl.pallas_call(
        functools.partial(_nsa_prompt_kernel, tq=tq, seq=seq),
        grid=(bsz, nq),
        in_specs=[pl.BlockSpec((tq, NSA_WIDTH), lambda b, i: (b * nq + i, 0)),
                  pl.BlockSpec((seq, kvw), lambda b, i: (b, SLC_OFF // kvw)),
                  pl.BlockSpec((seq, kvw), lambda b, i: (b, WIN_OFF // kvw)),
                  pl.BlockSpec((tq, HEAD_DIM), lambda b, i: (b * nq + i, GATE_OFF // HEAD_DIM)),
                  pl.BlockSpec((1, seq // CMP_STRIDE, kvw), lambda b, i: (b, 0, 0))],
        out_specs=pl.BlockSpec((tq, NSA_WIDTH), lambda b, i: (b * nq + i, 0)),
        out_shape=jax.ShapeDtypeStruct((bsz * seq, NSA_WIDTH), jnp.float32),
        scratch_shapes=[pltpu.VMEM((HPG, tq, HEAD_DIM), jnp.float32),
                        pltpu.VMEM((HPG, tq, HEAD_DIM), jnp.float32),
                        pltpu.VMEM((HPG, tq, HEAD_DIM), jnp.float32)],
        compiler_params=pltpu.CompilerParams(
            dimension_semantics=("arbitrary", "arbitrary"), vmem_limit_bytes=VMEM_LIMIT_BYTES),
        name="nsa_prompt",
    )(proj2d, proj2d, proj2d, proj2d, kvc)


CMP_PAGES_PER_STEP = 32
TOPK_LANES = 384
IDX_LANES = 128


def _cmp_partial_paged_kernel(pt_ref, *refs, n_src, rows_per_src):
    del pt_ref
    _cmp_partial_kernel(*refs, n_src=n_src, rows_per_src=rows_per_src)


def _compress_paged(pool, page_table, cw):
    bsz, n_pages = page_table.shape
    nps = CMP_PAGES_PER_STEP
    n_tiles = n_pages // nps
    nch_tile = nps * PAGE_SIZE // CMP_STRIDE

    def page_spec(s):
        return pl.BlockSpec((PAGE_SIZE * 4, HEAD_DIM), lambda b, i, pt: (pt[b * n_pages + i * nps + s], 0))

    x_specs = [page_spec(s) for s in range(nps)]
    w_spec = pl.BlockSpec(cw['w1k_cat'].shape, lambda b, i, pt: (0, 0, 0))
    out_spec = pl.BlockSpec((1, nch_tile, 4 * HEAD_DIM), lambda b, i, pt: (b, i, 0))
    shp = jax.ShapeDtypeStruct((bsz, n_tiles * nch_tile, 4 * HEAD_DIM), jnp.float32)
    a, b = pl.pallas_call(
        functools.partial(_cmp_partial_paged_kernel, n_src=nps, rows_per_src=PAGE_SIZE),
        grid_spec=pltpu.PrefetchScalarGridSpec(
            num_scalar_prefetch=1, grid=(bsz, n_tiles),
            in_specs=x_specs + [w_spec, w_spec], out_specs=[out_spec, out_spec]),
        out_shape=[shp, shp],
        compiler_params=pltpu.CompilerParams(
            dimension_semantics=("arbitrary", "arbitrary"), vmem_limit_bytes=VMEM_LIMIT_BYTES),
        name="cmp_partial_paged",
    )(page_table.reshape(-1), *([pool] * nps), cw['w1k_cat'], cw['w1v_cat'])
    return _cmp_finish(a, b, cw)


def _nsa_decode_a_kernel(proj_ref, kvc_ref, win_ref, o_ref, gsel_ref, idx_ref, *, t_new, n_past):
    f32, bf16 = jnp.float32, jnp.bfloat16
    scale = HEAD_DIM ** -0.5
    nch = kvc_ref.shape[1]
    n_win = win_ref.shape[1] // KV_PARTS
    rows = HPG * t_new
    r_iota = lax.broadcasted_iota(jnp.int32, (rows, 1), 0)
    t_row = r_iota % t_new
    h_row = r_iota // t_new
    pos_row = n_past + t_row
    gates = jax.nn.sigmoid(proj_ref[0, :, GATE_OFF:IN_WIDTH])
    c_lane = lax.broadcasted_iota(jnp.int32, (1, nch), 1)
    dist_c = pos_row - (c_lane * CMP_STRIDE + (CMP_BLOCK - 1))
    valid_c = (dist_c >= 0) & (c_lane < nch - 1)
    c_col = lax.broadcasted_iota(jnp.int32, (nch, 1), 0)
    j_lane = lax.broadcasted_iota(jnp.int32, (1, TOPK_LANES), 1)
    overlap = jnp.where((c_col * CMP_STRIDE <= j_lane * SEL_BLOCK + (SEL_BLOCK - 1))
                        & (c_col * CMP_STRIDE + (CMP_BLOCK - 1) >= j_lane * SEL_BLOCK), 1.0, 0.0).astype(bf16)
    pos_t = n_past + lax.broadcasted_iota(jnp.int32, (t_new, 1), 0)
    blk_t = pos_t // SEL_BLOCK
    forced = (j_lane == 0) | (j_lane == blk_t) | (j_lane == blk_t - 1)
    j_f = j_lane.astype(f32)
    k_lane = lax.broadcasted_iota(jnp.int32, (1, IDX_LANES), 1)
    i_win = lax.broadcasted_iota(jnp.int32, (1, n_win), 1)
    dist_w = pos_row - (n_past - n_win + i_win)
    keep_w = (dist_w >= 0) & (dist_w < WINDOW)
    j_new = lax.broadcasted_iota(jnp.int32, (1, 8), 1)
    dist_n = t_row - j_new
    keep_n = (dist_n >= 0) & (j_new < t_new)
    zpad = jnp.zeros((8 - t_new, HEAD_DIM), f32)

    for g in range(NSA_KV_HEADS):
        kcol = slice(g * HEAD_DIM, (g + 1) * HEAD_DIM)
        vcol = slice((NSA_KV_HEADS + g) * HEAD_DIM, (NSA_KV_HEADS + g + 1) * HEAD_DIM)
        heads = [g * HPG + h for h in range(HPG)]
        slope_row = jnp.zeros((rows, 1), f32)
        for h, hh in enumerate(heads):
            slope_row = jnp.where(h_row == h, 2.0 ** -(hh + 1), slope_row)
        q = jnp.concatenate([proj_ref[0, :, hh * HEAD_DIM:(hh + 1) * HEAD_DIM] for hh in heads], axis=0).astype(bf16)

        kc = kvc_ref[0, :, kcol].astype(bf16)
        vc = kvc_ref[0, :, vcol].astype(bf16)
        s = lax.dot_general(q, kc, _NT, preferred_element_type=f32) * scale - slope_row * dist_c.astype(f32)
        s = jnp.where(valid_c, s, NEG_INF)
        e = jnp.exp(s - jnp.max(s, axis=-1, keepdims=True))
        p = e / jnp.sum(e, axis=-1, keepdims=True)
        p = jnp.where(valid_c, p, 0.0)
        o_cmp = jnp.dot(p.astype(bf16), vc, preferred_element_type=f32)
        psum = p[0:t_new]
        for h in range(1, HPG):
            psum = psum + p[h * t_new:(h + 1) * t_new]

        p_hi = psum.astype(bf16)
        p_lo = (psum - p_hi.astype(f32)).astype(bf16)
        imp = (jnp.dot(p_hi, overlap, preferred_element_type=f32)
               + jnp.dot(p_lo, overlap, preferred_element_type=f32))
        imp = jnp.where(forced, imp + FORCE_BONUS, imp)
        imp = jnp.where(j_lane <= blk_t, imp, NEG_INF)
        picked = jnp.full((t_new, IDX_LANES), -1.0, f32)
        for k in range(N_SELECT):
            best = jnp.max(imp, axis=-1, keepdims=True)
            first = jnp.min(jnp.where(imp == best, j_f, 1e9), axis=-1, keepdims=True)
            picked = jnp.where(k_lane == k, jnp.where(best > 0.5 * NEG_INF, first, -1.0), picked)
            imp = jnp.where(j_f == first, -3e38, imp)
        idx_ref[0, g * t_new:(g + 1) * t_new, :] = picked.astype(jnp.int32)

        kw = win_ref[0, pl.ds(g, n_win, stride=KV_PARTS), :].astype(bf16)
        vw = win_ref[0, pl.ds(NSA_KV_HEADS + g, n_win, stride=KV_PARTS), :].astype(bf16)
        kn = jnp.concatenate([proj_ref[0, :, WIN_OFF + g * HEAD_DIM:WIN_OFF + (g + 1) * HEAD_DIM], zpad], axis=0)
        vn = jnp.concatenate([proj_ref[0, :, WIN_OFF + KV_WIDTH + g * HEAD_DIM:
                                       WIN_OFF + KV_WIDTH + (g + 1) * HEAD_DIM], zpad], axis=0)
        s_w = lax.dot_general(q, kw, _NT, preferred_element_type=f32) * scale - slope_row * dist_w.astype(f32)
        s_n = (lax.dot_general(q, kn.astype(bf16), _NT, preferred_element_type=f32) * scale
               - slope_row * dist_n.astype(f32))
        s_w = jnp.where(keep_w, s_w, NEG_INF)
        s_n = jnp.where(keep_n, s_n, NEG_INF)
        m = jnp.maximum(jnp.max(s_w, axis=-1, keepdims=True), jnp.max(s_n, axis=-1, keepdims=True))
        e_w = jnp.exp(s_w - m)
        e_n = jnp.exp(s_n - m)
        den = jnp.sum(e_w, axis=-1, keepdims=True) + jnp.sum(e_n, axis=-1, keepdims=True)
        o_win = (jnp.dot(e_w.astype(bf16), vw, preferred_element_type=f32)
                 + jnp.dot(e_n.astype(bf16), vn.astype(bf16), preferred_element_type=f32)) / den

        for h, hh in enumerate(heads):
            rs = slice(h * t_new, (h + 1) * t_new)
            hs = slice(hh * HEAD_DIM, (hh + 1) * HEAD_DIM)
            o_ref[0, :, hs] = (gates[:, hh:hh + 1] * o_cmp[rs]
                               + gates[:, 2 * NSA_HEADS + hh:2 * NSA_HEADS + hh + 1] * o_win[rs])
            gsel_ref[0, :, hs] = jnp.broadcast_to(gates[:, NSA_HEADS + hh:NSA_HEADS + hh + 1], (t_new, HEAD_DIM))


def _nsa_decode_sel_kernel(idx_ref, pt_ref, q_ref, part_ref, gsel_ref, new_ref, *refs, t_new, n_past):
    del pt_ref
    f32, bf16 = jnp.float32, jnp.bfloat16
    o_ref = refs[NSA_KV_HEADS * N_SELECT]
    b, t = pl.program_id(0), pl.program_id(1)
    scale = HEAD_DIM ** -0.5
    n_past_blk = n_past // SEL_BLOCK
    n_keys = N_SELECT * SEL_BLOCK
    lane = lax.broadcasted_iota(jnp.int32, (1, n_keys), 1)
    slot = lane // SEL_BLOCK
    h_row = lax.broadcasted_iota(jnp.int32, (8, 1), 0)
    for g in range(NSA_KV_HEADS):
        base = ((b * NSA_KV_HEADS + g) * t_new + t) * N_SELECT
        blk_of_lane = jnp.full((1, n_keys), -1, jnp.int32)
        ks, vs = [], []
        for k in range(N_SELECT):
            blk = idx_ref[base + k]
            blk_of_lane = jnp.where(slot == k, blk, blk_of_lane)
            is_new = blk >= n_past_blk
            src = refs[g * N_SELECT + k]
            k_old = src[pl.ds(g, SEL_BLOCK, stride=KV_PARTS), :]
            v_old = src[pl.ds(NSA_KV_HEADS + g, SEL_BLOCK, stride=KV_PARTS), :]
            k_new = new_ref[:, g * HEAD_DIM:(g + 1) * HEAD_DIM]
            v_new = new_ref[:, KV_WIDTH + g * HEAD_DIM:KV_WIDTH + (g + 1) * HEAD_DIM]
            ks.append(jnp.where(is_new, k_new, k_old).astype(bf16))
            vs.append(jnp.where(is_new, v_new, v_old).astype(bf16))
        k_all = jnp.concatenate(ks, axis=0)
        v_all = jnp.concatenate(vs, axis=0)
        dist = (n_past + t) - (blk_of_lane * SEL_BLOCK + lane % SEL_BLOCK)
        keep = (dist >= 0) & (blk_of_lane >= 0)
        q = jnp.concatenate([q_ref[:, (g * HPG + h) * HEAD_DIM:(g * HPG + h + 1) * HEAD_DIM] for h in range(HPG)]
                            + [jnp.zeros((8 - HPG, HEAD_DIM), f32)], axis=0).astype(bf16)
        slope_row = jnp.zeros((8, 1), f32)
        for h in range(HPG):
            slope_row = jnp.where(h_row == h, 2.0 ** -(g * HPG + h + 1), slope_row)
        s = lax.dot_general(q, k_all, _NT, preferred_element_type=f32) * scale - slope_row * dist.astype(f32)
        s = jnp.where(keep, s, NEG_INF)
        e = jnp.exp(s - jnp.max(s, axis=-1, keepdims=True))
        p = e / jnp.sum(e, axis=-1, keepdims=True)
        o_sel = jnp.dot(p.astype(bf16), v_all, preferred_element_type=f32)
        for h in range(HPG):
            hs = slice((g * HPG + h) * HEAD_DIM, (g * HPG + h + 1) * HEAD_DIM)
            o_ref[:, hs] = part_ref[:, hs] + gsel_ref[:, hs] * o_sel[h:h + 1]


def _nsa_decode_rows(proj3, kvc, win_rows, pool_rows, page_table, n_past):
    bsz, t_new, _ = proj3.shape
    n_pages = page_table.shape[1]
    full = jax.ShapeDtypeStruct((bsz, t_new, NSA_WIDTH), jnp.float32)
    part, gsel, idx = pl.pallas_call(
        functools.partial(_nsa_decode_a_kernel, t_new=t_new, n_past=n_past),
        grid=(bsz,),
        in_specs=[pl.BlockSpec((1, t_new, IN_WIDTH), lambda b: (b, 0, 0)),
                  pl.BlockSpec((1,) + kvc.shape[1:], lambda b: (b, 0, 0)),
                  pl.BlockSpec((1,) + win_rows.shape[1:], lambda b: (b, 0, 0))],
        out_specs=[pl.BlockSpec((1, t_new, NSA_WIDTH), lambda b: (b, 0, 0)),
                   pl.BlockSpec((1, t_new, NSA_WIDTH), lambda b: (b, 0, 0)),
                   pl.BlockSpec((1, NSA_KV_HEADS * t_new, IDX_LANES), lambda b: (b, 0, 0))],
        out_shape=[full, full, jax.ShapeDtypeStruct((bsz, NSA_KV_HEADS * t_new, IDX_LANES), jnp.int32)],
        compiler_params=pltpu.CompilerParams(dimension_semantics=("arbitrary",), vmem_limit_bytes=VMEM_LIMIT_BYTES),
        name="nsa_decode_a",
    )(proj3, kvc, win_rows)

    n_past_blk = n_past // SEL_BLOCK
    sub = PAGE_SIZE // SEL_BLOCK
    new_rows = jnp.pad(proj3[:, :, SLC_OFF:WIN_OFF], ((0, 0), (0, SEL_BLOCK - t_new), (0, 0)))

    picked = idx[:, :, :N_SELECT]
    past = jnp.clip(picked, 0, n_past_blk - 1)
    pool_blk = jnp.take_along_axis(page_table[:, None, :], past // sub, axis=2) * sub + past % sub

    def pool_spec(g, k):
        def index(b, t, idx, blk):
            return (blk[((b * NSA_KV_HEADS + g) * t_new + t) * N_SELECT + k], 0)
        return pl.BlockSpec((SEL_BLOCK * KV_PARTS, HEAD_DIM), index)

    row_spec = pl.BlockSpec((None, 1, NSA_WIDTH), lambda b, t, idx, pt: (b * t_new + t, 0, 0))
    out = pl.pallas_call(
        functools.partial(_nsa_decode_sel_kernel, t_new=t_new, n_past=n_past),
        grid_spec=pltpu.PrefetchScalarGridSpec(
            num_scalar_prefetch=2, grid=(bsz, t_new),
            in_specs=[row_spec, row_spec, row_spec,
                      pl.BlockSpec((None, SEL_BLOCK, 2 * KV_WIDTH), lambda b, t, idx, pt: (b, 0, 0))]
            + [pool_spec(g, k) for g in range(NSA_KV_HEADS) for k in range(N_SELECT)],
            out_specs=row_spec),
        out_shape=jax.ShapeDtypeStruct((bsz * t_new, 1, NSA_WIDTH), jnp.float32),
        compiler_params=pltpu.CompilerParams(
            dimension_semantics=("arbitrary", "arbitrary"), vmem_limit_bytes=VMEM_LIMIT_BYTES),
        name="nsa_decode_sel",
    )(picked.reshape(-1), pool_blk.reshape(-1).astype(jnp.int32),
      proj3[:, :, Q_OFF:Q_OFF + NSA_WIDTH].reshape(bsz * t_new, 1, NSA_WIDTH),
      part.reshape(bsz * t_new, 1, NSA_WIDTH), gsel.reshape(bsz * t_new, 1, NSA_WIDTH),
      new_rows, *([pool_rows] * (NSA_KV_HEADS * N_SELECT)))
    return out.reshape(bsz, t_new, NSA_WIDTH)


WKV_LANES = 2 * RWKV_HEAD_DIM
WKV_PAIRS = RWKV_HEADS // 2
WKV_STACK = 4
WKV_BB = 4
WKV_CHUNK = 64
WKV_MIN_CHUNK = 16
WKV_FLUSH = RWKV_HEAD_DIM


def _wkv_kernel(wr_ref, w_ref, k_ref, v_ref, kk_ref, kka_ref, c1_ref, c2_ref, s0_ref, y_ref, st_ref,
                s_scr, y_scr, *, tc):
    f32, bf16 = jnp.float32, jnp.bfloat16
    ti = pl.program_id(1)
    hd = RWKV_HEAD_DIM
    n_tiles = WKV_BB * WKV_PAIRS
    n_stacks = n_tiles // WKV_STACK
    tile = lambda q: (q // WKV_PAIRS, q % WKV_PAIRS)

    @pl.when(ti == 0)
    def _():
        for q in range(n_tiles):
            b, p = tile(q)
            s_scr[q] = jnp.concatenate([s0_ref[b, 2 * p], s0_ref[b, 2 * p + 1]], axis=1)

    lane = lax.broadcasted_iota(jnp.int32, (1, WKV_LANES), 1)
    r2 = lax.broadcasted_iota(jnp.int32, (2 * WKV_LANES, 1), 0)
    c2 = lax.broadcasted_iota(jnp.int32, (1, 2 * WKV_LANES), 1)
    same_head2 = jnp.where(r2 // hd == c2 // hd, 1.0, 0.0).astype(bf16)
    on_diag = lax.broadcasted_iota(jnp.int32, (hd, 1), 0) == lane % hd
    n_flush = min(tc, WKV_FLUSH)
    y_scr[...] = jnp.zeros(y_scr.shape, f32)
    pairs = [(2 * i, 2 * i + 1) for i in range(n_stacks // 2)]
    stack_tiles = lambda st: range(st * WKV_STACK, (st + 1) * WKV_STACK)

    def row_sums(per_tile, s0, s1):
        lhs = jnp.concatenate([jnp.concatenate([per_tile[q] for q in stack_tiles(st)], axis=0)
                               for st in (s0, s1)], axis=1)
        res = jnp.dot(lhs.astype(bf16), same_head2, preferred_element_type=f32)
        return {q: res[n * hd:(n + 1) * hd, half * WKV_LANES:(half + 1) * WKV_LANES]
                for half, st in enumerate((s0, s1)) for n, q in enumerate(stack_tiles(st))}

    def step(t, carry):
        here = (lane % hd) == (t % n_flush)
        get = lambda ref, q: ref[tile(q)[0], tile(q)[1], pl.ds(t, 1), :]
        for s0, s1 in pairs:
            tiles = list(stack_tiles(s0)) + list(stack_tiles(s1))
            s_old = {q: s_scr[q] for q in tiles}
            sa = row_sums({q: s_old[q] * get(kk_ref, q) for q in tiles}, s0, s1)
            y_old = row_sums({q: s_old[q] * get(wr_ref, q) for q in tiles}, s0, s1)
            v_col = row_sums({q: jnp.where(on_diag, get(v_ref, q), 0.0) for q in tiles}, s0, s1)
            for q in tiles:
                s_scr[q] = s_old[q] * get(w_ref, q) - sa[q] * get(kka_ref, q) + v_col[q] * get(k_ref, q)
                y_col = y_old[q] - sa[q] * get(c1_ref, q) + v_col[q] * get(c2_ref, q)
                y_scr[q] = jnp.where(here, y_col, y_scr[q])
        return carry

    for sub in range(tc // n_flush):
        lax.fori_loop(sub * n_flush, (sub + 1) * n_flush, step, 0)
        for q in range(n_tiles):
            b, p = tile(q)
            yt = y_scr[q].T
            y_ref[b, p, sub * n_flush:(sub + 1) * n_flush, :] = jnp.concatenate(
                [yt[:n_flush], yt[hd:hd + n_flush]], axis=1)

    @pl.when(ti == pl.num_programs(1) - 1)
    def _():
        for q in range(n_tiles):
            b, p = tile(q)
            st_ref[b, 2 * p] = s_scr[q][:, :hd]
            st_ref[b, 2 * p + 1] = s_scr[q][:, hd:]


def _wkv_scan(wr, w, k, v, kk, kka, c1, c2, s0, tc):
    bsz, n_pairs, seq, _ = wr.shape
    assert n_pairs == WKV_PAIRS and bsz % WKV_BB == 0 and seq % tc == 0
    n_tiles = WKV_BB * WKV_PAIRS
    x_spec = pl.BlockSpec((WKV_BB, WKV_PAIRS, tc, WKV_LANES), lambda b, i: (b, 0, i, 0))
    s_spec = pl.BlockSpec((WKV_BB, RWKV_HEADS, RWKV_HEAD_DIM, RWKV_HEAD_DIM), lambda b, i: (b, 0, 0, 0))
    return pl.pallas_call(
        functools.partial(_wkv_kernel, tc=tc),
        grid=(bsz // WKV_BB, seq // tc),
        in_specs=[x_spec] * 8 + [s_spec],
        out_specs=[x_spec, s_spec],
        out_shape=[jax.ShapeDtypeStruct(wr.shape, jnp.float32),
                   jax.ShapeDtypeStruct(s0.shape, jnp.float32)],
        scratch_shapes=[pltpu.VMEM((n_tiles, RWKV_HEAD_DIM, WKV_LANES), jnp.float32),
                        pltpu.VMEM((n_tiles, RWKV_HEAD_DIM, WKV_LANES), jnp.float32)],
        compiler_params=pltpu.CompilerParams(
            dimension_semantics=("arbitrary", "arbitrary"), vmem_limit_bytes=VMEM_LIMIT_BYTES),
        name="wkv_scan",
    )(wr, w, k, v, kk, kka, c1, c2, s0)


RW_PACKS = RWKV_WIDTH // WKV_LANES


def _head_sums(x, ones2):
    f32, bf16 = jnp.float32, jnp.bfloat16
    hi = x.astype(bf16)
    lo = (x - hi.astype(f32)).astype(bf16)
    w = 2 * WKV_LANES
    out = []
    for c in range(RWKV_WIDTH // w):
        sl = slice(c * w, (c + 1) * w)
        out.append(jnp.dot(hi[:, sl], ones2, preferred_element_type=f32)
                   + jnp.dot(lo[:, sl], ones2, preferred_element_type=f32))
    return jnp.concatenate(out, axis=1)


def _head_ones():
    w = 2 * WKV_LANES
    r = lax.broadcasted_iota(jnp.int32, (w, 1), 0)
    c = lax.broadcasted_iota(jnp.int32, (1, w), 1)
    return jnp.where(r // RWKV_HEAD_DIM == c // RWKV_HEAD_DIM, 1.0, 0.0).astype(jnp.bfloat16)


def _store_rw(ref, val, pack_major):
    if pack_major:
        for p in range(RW_PACKS):
            ref[0, p] = val[:, p * WKV_LANES:(p + 1) * WKV_LANES]
    else:
        ref[0] = val


def _rwkv_prep_kernel(h_ref, hprev_ref, h0_ref, pr_ref, pk_ref, pv_ref, pprev_r, pprev_k, pprev_v, p0_ref,
                      mu_rkv_ref, mu_wag_ref, dw0_ref, dw1_ref, dw2_ref, a0_ref, a1_ref, a2_ref, g1_ref, g2_ref,
                      kk_w_ref, ka_w_ref, rk_w_ref,
                      wr_out, w_out, k_out, v_out, kk_out, kka_out, c1_out, c2_out, g_out, bonus_out,
                      *, period, pack_major):
    f32, bf16 = jnp.float32, jnp.bfloat16
    i = pl.program_id(1)
    tm = h_ref.shape[1]
    row = lax.broadcasted_iota(jnp.int32, (tm, 1), 0)
    per_row_first = h0_ref.shape[1] != 1
    first = (row % period == 0) if per_row_first else None

    def shifted(cur, prev_blk, first_rows):
        rolled = pltpu.roll(cur, 1, 0)
        if per_row_first:
            return jnp.where(first, first_rows, rolled)
        row0 = jnp.where(i == 0, first_rows, prev_blk[7:8])
        return jnp.where(row == 0, row0, rolled)

    h = h_ref[0]
    xx = shifted(h, hprev_ref[0], h0_ref[0]) - h
    xw = (h + xx * mu_wag_ref[0:1]).astype(bf16)
    xa = (h + xx * mu_wag_ref[1:2]).astype(bf16)
    xg = (h + xx * mu_wag_ref[2:3]).astype(bf16)
    dmid = jnp.tanh(jnp.dot(xw, dw1_ref[...], preferred_element_type=f32))
    dlin = dw0_ref[...] + jnp.dot(dmid.astype(bf16), dw2_ref[...], preferred_element_type=f32)
    z = -dlin
    w_log = -(jnp.maximum(z, 0.0) + jnp.log(1.0 + jnp.exp(-jnp.abs(z)))) - 0.5
    decay = jnp.exp(-jnp.exp(w_log))
    amid = jnp.dot(xa, a1_ref[...], preferred_element_type=f32)
    a = jax.nn.sigmoid(a0_ref[...] + jnp.dot(amid.astype(bf16), a2_ref[...], preferred_element_type=f32))
    gmid = jax.nn.sigmoid(jnp.dot(xg, g1_ref[...], preferred_element_type=f32))
    g = jnp.dot(gmid.astype(bf16), g2_ref[...], preferred_element_type=f32)

    def mixed(cur_ref, prev_ref, n):
        cur = cur_ref[0]
        cs = slice(n * RWKV_WIDTH, (n + 1) * RWKV_WIDTH)
        prev = shifted(cur, prev_ref[0], p0_ref[0][:, cs])
        return cur + mu_rkv_ref[:, cs] * (prev - cur)

    r = mixed(pr_ref, pprev_r, 0)
    k = mixed(pk_ref, pprev_k, 1)
    v = mixed(pv_ref, pprev_v, 2)
    ones2 = _head_ones()
    kk = k * kk_w_ref[...]
    kk = kk / jnp.maximum(jnp.sqrt(_head_sums(kk * kk, ones2)), 1e-12)
    k = k * (1.0 + (a - 1.0) * ka_w_ref[...])
    bonus = _head_sums(r * k * rk_w_ref[...], ones2) * v
    _store_rw(wr_out, decay * r, pack_major)
    _store_rw(c1_out, _head_sums(kk * a * r, ones2), pack_major)
    _store_rw(c2_out, _head_sums(k * r, ones2), pack_major)
    _store_rw(w_out, decay, pack_major)
    _store_rw(k_out, k, pack_major)
    _store_rw(v_out, v, pack_major)
    _store_rw(kk_out, kk, pack_major)
    _store_rw(kka_out, kk * a, pack_major)
    g_out[0] = g
    bonus_out[0] = bonus


def _rwkv_prep(h, proj, h0, p0, lp, tm, period, pack_major):
    f32, bf16 = jnp.float32, jnp.bfloat16
    g, t, d = h.shape
    rw = RWKV_WIDTH
    nb = tm // 8
    cur = lambda w, c: pl.BlockSpec((1, tm, w), lambda b, i: (b, i, c))
    prev = lambda w, c: pl.BlockSpec((1, 8, w), lambda b, i: (b, jnp.maximum(i * nb - 1, 0), c))
    per_row = h0.shape[1] != 1
    carry = lambda w: pl.BlockSpec((1, tm if per_row else 1, w), (lambda b, i: (b, i, 0)) if per_row else (lambda b, i: (b, 0, 0)))
    full = lambda a: pl.BlockSpec(a.shape, lambda b, i: (0,) * a.ndim)
    c0 = RKV_OFF // rw
    ws = [lp['mu_rkv'].reshape(1, 3 * rw), lp['mu_wag'], lp['decay_w0'].reshape(1, rw), lp['decay_w1'].astype(bf16),
          lp['decay_w2'].astype(bf16), lp['iclr_a0'].reshape(1, rw), lp['iclr_a1'].astype(bf16),
          lp['iclr_a2'].astype(bf16), lp['gate_g1'].astype(bf16), lp['gate_g2'].astype(bf16),
          lp['k_k'].reshape(1, rw), lp['k_a'].reshape(1, rw), lp['r_k'].reshape(1, rw)]
    if pack_major:
        seq_shape = jax.ShapeDtypeStruct((g, RW_PACKS, t, WKV_LANES), f32)
        seq_spec = pl.BlockSpec((1, RW_PACKS, tm, WKV_LANES), lambda b, i: (b, 0, i, 0))
    else:
        seq_shape = jax.ShapeDtypeStruct((g, t, rw), f32)
        seq_spec = cur(rw, 0)
    flat_shape = jax.ShapeDtypeStruct((g, t, rw), f32)
    return pl.pallas_call(
        functools.partial(_rwkv_prep_kernel, period=period, pack_major=pack_major),
        grid=(g, t // tm),
        in_specs=[cur(d, 0), prev(d, 0), carry(d), cur(rw, c0), cur(rw, c0 + 1), cur(rw, c0 + 2),
                  prev(rw, c0), prev(rw, c0 + 1), prev(rw, c0 + 2), carry(3 * rw)] + [full(w) for w in ws],
        out_specs=[seq_spec] * 8 + [cur(rw, 0), cur(rw, 0)],
        out_shape=[seq_shape] * 8 + [flat_shape, flat_shape],
        compiler_params=pltpu.CompilerParams(
            dimension_semantics=("arbitrary", "arbitrary"), vmem_limit_bytes=VMEM_LIMIT_BYTES),
        name="rwkv_prep",
    )(h, h, h0, proj, proj, proj, proj, proj, proj, p0, *ws)


def _rwkv_post_kernel(y_ref, g_ref, bonus_ref, lnw_ref, lnb_ref, o_ref, *, pack_major):
    if pack_major:
        y = jnp.concatenate([y_ref[0, p] for p in range(RW_PACKS)], axis=1)
    else:
        y = y_ref[0]
    ones2 = _head_ones()
    inv = 1.0 / RWKV_HEAD_DIM
    mu = _head_sums(y, ones2) * inv
    dev = y - mu
    var = _head_sums(dev * dev, ones2) * inv
    yn = dev * lax.rsqrt(var + GN_EPS) * lnw_ref[...] + lnb_ref[...]
    o_ref[0] = (yn + bonus_ref[0]) * g_ref[0]


def _rwkv_post(y, g, bonus, ln_w, ln_b, tm, pack_major):
    gsz, t, rw = g.shape
    flat = pl.BlockSpec((1, tm, rw), lambda b, i: (b, i, 0))
    y_spec = pl.BlockSpec((1, RW_PACKS, tm, WKV_LANES), lambda b, i: (b, 0, i, 0)) if pack_major else flat
    vec = pl.BlockSpec((1, rw), lambda b, i: (0, 0))
    return pl.pallas_call(
        functools.partial(_rwkv_post_kernel, pack_major=pack_major),
        grid=(gsz, t // tm),
        in_specs=[y_spec, flat, flat, vec, vec],
        out_specs=flat,
        out_shape=jax.ShapeDtypeStruct((gsz, t, rw), jnp.float32),
        compiler_params=pltpu.CompilerParams(
            dimension_semantics=("arbitrary", "arbitrary"), vmem_limit_bytes=VMEM_LIMIT_BYTES),
        name="rwkv_post",
    )(y, g, bonus, ln_w.reshape(1, rw), ln_b.reshape(1, rw))


def _hier_route(logits):
    assert EXPERT_TOP_K == 2
    n = logits.shape[0]
    pg = jax.nn.softmax(logits[:, :N_GROUPS], axis=-1)

    def key(x):
        bits = lax.bitcast_convert_type(x, jnp.int32)
        return jnp.where(bits < 0, bits ^ jnp.int32(0x7FFFFFFF), bits)

    g_sel = jnp.argmax(key(pg), axis=-1)[:, None]
    g_val = jnp.take_along_axis(pg, g_sel, axis=1)
    le = logits[:, N_GROUPS:N_GROUPS + N_EXPERTS].reshape(n, N_GROUPS, EXPERTS_PER_GROUP)
    le_g = jnp.take_along_axis(le, g_sel[:, :, None], axis=1)[:, 0]
    le_key = key(le_g)
    e1 = jnp.argmax(le_key, axis=-1)[:, None]
    rest = jnp.where(jnp.arange(EXPERTS_PER_GROUP)[None, :] == e1, jnp.iinfo(jnp.int32).min, le_key)
    e2 = jnp.argmax(rest, axis=-1)[:, None]
    e_sel = jnp.concatenate([e1, e2], axis=1)
    e_val = jnp.take_along_axis(le_g, e_sel, axis=1)
    weights = jax.nn.softmax(e_val, axis=-1) * g_val
    return (g_sel * EXPERTS_PER_GROUP + e_sel).astype(jnp.int32), weights


def _moe_ffn(h_pad, eid, w_gate, w_up, w_down):
    n, d = h_pad.shape[0] - 1, h_pad.shape[1]
    a_tot = n * EXPERT_TOP_K
    flat_e = eid.reshape(-1)
    onehot = (flat_e[:, None] == jnp.arange(N_EXPERTS)[None, :]).astype(jnp.int32)
    csum = jnp.cumsum(onehot, axis=0)
    rank = jnp.take_along_axis(csum, flat_e[:, None], axis=1)[:, 0] - 1
    counts = csum[-1]
    padded = (counts + MOE_TM - 1) // MOE_TM * MOE_TM
    pad_end = jnp.cumsum(padded)
    dest = (pad_end - padded)[flat_e] + rank
    n_blk = (a_tot + N_EXPERTS * (MOE_TM - 1)) // MOE_TM
    tok_buf = jnp.full((n_blk * MOE_TM,), n, jnp.int32).at[dest].set(jnp.arange(a_tot, dtype=jnp.int32) // EXPERT_TOP_K)
    xb = h_pad[tok_buf].reshape(n_blk, MOE_TM, d)
    blk_start = jnp.arange(n_blk, dtype=jnp.int32) * MOE_TM
    blk_e = jnp.minimum(jnp.sum((pad_end[None, :] <= blk_start[:, None]).astype(jnp.int32), axis=1), N_EXPERTS - 1)
    meta = jnp.concatenate([blk_e, pad_end[-1:] // MOE_TM]).astype(jnp.int32)
    yb = _moe_blocks(xb, meta, w_gate, w_up, w_down).reshape(n_blk * MOE_TM, d)
    dest2 = dest.reshape(n, EXPERT_TOP_K)
    return yb[dest2[:, 0]], yb[dest2[:, 1]]


def _layer_front(x, mod, lp, layer, pool_cmp, pool_slc, page_table, win_buf, wkv0, shift0, past_len, rows):
    B, T, D = x.shape
    groups = B * T // rows
    per_token = rows > T
    mods = jnp.repeat(mod, T, axis=0).reshape(groups, rows, 6 * D) if per_token else mod[:, None, :]
    sh1, sc1, gt1, sh2, sc2, gt2 = jnp.split(mods, 6, axis=-1)
    xg = x.reshape(groups, rows, D)
    tm_in = min(rows, 512)
    proj, h, kv_rows = _norm_in(xg, lp['norm1'], sc1, sh1, lp['w_in_b'], tm_in)
    kv_shape = (B, T, 2, NSA_KV_HEADS, HEAD_DIM)
    cmp_rows, slc_rows, win_rows = (kv_rows[n].reshape(B, T * KV_PARTS, HEAD_DIM) for n in range(KV_TILES))
    cmp_new, slc_new = cmp_rows.reshape(kv_shape), slc_rows.reshape(kv_shape)

    if pool_cmp is None:
        kvc = _compress_prompt(cmp_rows, _cmp_weights(lp))
        o_nsa = _nsa_prompt(proj.reshape(B * T, IN_WIDTH), kvc, B, T)
        win_len = min(WINDOW, past_len)
        assert T >= win_len
        win_state = win_rows[:, (T - win_len) * KV_PARTS:].reshape((B, win_len) + kv_shape[2:])
    else:
        n_past = page_table.shape[1] * PAGE_SIZE
        assert n_past % CMP_STRIDE == 0 and T < CMP_STRIDE and T <= 8
        assert -(-(n_past + T) // SEL_BLOCK) <= TOPK_LANES and n_past % SEL_BLOCK == 0 and T <= SEL_BLOCK
        assert win_buf.shape[1] == WINDOW
        pages = page_table + layer * pool_cmp.shape[1]
        kvc = _compress_paged(pool_cmp.reshape(-1, HEAD_DIM), pages, _cmp_weights(lp))
        win_buf_rows = win_buf.reshape(B, WINDOW * KV_PARTS, HEAD_DIM)
        o_nsa = _nsa_decode_rows(proj.reshape(B, T, IN_WIDTH), kvc, win_buf_rows, pool_slc.reshape(-1, HEAD_DIM),
                                 pages, n_past)
        win_state = jnp.concatenate([win_buf_rows[:, T * KV_PARTS:], win_rows], axis=1).reshape(
            (B, WINDOW) + kv_shape[2:])

    shift0 = shift0.astype(h.dtype)
    p0 = _matmul(shift0, lp['w_in_b'][:, RKV_OFF:CMP_OFF], B, RWKV_WIDTH)
    if per_token:
        h0 = jnp.repeat(shift0, T, axis=0).reshape(groups, rows, D)
        p0 = jnp.repeat(p0, T, axis=0).reshape(groups, rows, 3 * RWKV_WIDTH)
    else:
        h0, p0 = shift0[:, None], p0[:, None]
    tm_rw = min(rows, 256)
    seqs = _rwkv_prep(h, proj, h0, p0, lp, tm_rw, T, pack_major=not per_token)
    seqs, (gate, bonus) = seqs[:8], seqs[8:]
    if per_token:
        t_pad = -(-T // WKV_MIN_CHUNK) * WKV_MIN_CHUNK

        def pairs(a, fill):
            a = jnp.pad(a.reshape(B, T, RW_PACKS, WKV_LANES), ((0, 0), (0, t_pad - T), (0, 0), (0, 0)),
                        constant_values=fill)
            return a.transpose(0, 2, 1, 3)

        fills = (0.0, 1.0) + (0.0,) * 6
        y, wkv_T = _wkv_scan(*[pairs(a, f) for a, f in zip(seqs, fills)], wkv0.astype(jnp.float32), WKV_MIN_CHUNK)
        y = y.transpose(0, 2, 1, 3)[:, :T].reshape(groups, rows, RWKV_WIDTH)
    else:
        y, wkv_T = _wkv_scan(*seqs, wkv0.astype(jnp.float32), WKV_CHUNK)
    o_rwkv = _rwkv_post(y, gate, bonus, lp['ln_x_w'], lp['ln_x_b'], tm_rw, pack_major=not per_token)

    x1, h2, logits = _mix_out(o_nsa.reshape(groups, rows, NSA_WIDTH), o_rwkv, lp['w_out_b'], xg, gt1, sc2, sh2,
                              lp['norm2'], lp['wr_hi'], lp['wr_lo'], lp['br'], min(rows, 256))
    return (x1, h2, logits, gt2), (cmp_new, slc_new, win_state, wkv_T, h.reshape(B, T, D)[:, -1])


def kernel(x_prompt, x_sample, c_prompt, c_sample, cache_cmp_kv, cache_slc_kv, page_table, state_win_kv, state_wkv, state_shift, w_ada, b_ada, norm1, w_in, cmp_k_w1, cmp_k_pe, cmp_k_w2, cmp_v_w1, cmp_v_pe, cmp_v_w2, mu_rkv, mu_wag, decay_w0, decay_w1, decay_w2, iclr_a0, iclr_a1, iclr_a2, gate_g1, gate_g2, k_k, k_a, r_k, ln_x_w, ln_x_b, w_out, norm2, w_router_group, b_router_group, w_router_expert, b_router_expert, w_gate, w_up, w_down, norm_f):
    bp, tp = x_prompt.shape[:2]
    ts = x_sample.shape[1]
    past_len = page_table.shape[1] * PAGE_SIZE
    assert DEPTH == 1 and w_in.shape[0] == 1
    l = 0
    bs = x_sample.shape[0]
    f32, bf16 = jnp.float32, jnp.bfloat16
    lyr = lambda a: a.reshape(a.shape[1:])
    wr = jnp.concatenate([lyr(w_router_group), lyr(w_router_expert),
                          jnp.zeros((D_MODEL, ROUTER_LANES - N_GROUPS - N_EXPERTS), f32)], axis=1)
    wr_hi = wr.astype(bf16)
    br = jnp.concatenate([lyr(b_router_group), lyr(b_router_expert),
                          jnp.zeros((ROUTER_LANES - N_GROUPS - N_EXPERTS,), f32)]).reshape(1, ROUTER_LANES)
    lp = dict(norm1=lyr(norm1), w_in_b=_permute_w_in(lyr(w_in)).astype(bf16),
              cmp_k_w1=lyr(cmp_k_w1), cmp_k_pe=lyr(cmp_k_pe), cmp_k_w2=lyr(cmp_k_w2),
              cmp_v_w1=lyr(cmp_v_w1), cmp_v_pe=lyr(cmp_v_pe), cmp_v_w2=lyr(cmp_v_w2),
              mu_rkv=lyr(mu_rkv), mu_wag=lyr(mu_wag), decay_w0=lyr(decay_w0), decay_w1=lyr(decay_w1),
              decay_w2=lyr(decay_w2), iclr_a0=lyr(iclr_a0), iclr_a1=lyr(iclr_a1), iclr_a2=lyr(iclr_a2),
              gate_g1=lyr(gate_g1), gate_g2=lyr(gate_g2), k_k=lyr(k_k), k_a=lyr(k_a), r_k=lyr(r_k),
              ln_x_w=lyr(ln_x_w), ln_x_b=lyr(ln_x_b), w_out_b=lyr(w_out).astype(bf16), norm2=lyr(norm2),
              wr_hi=wr_hi, wr_lo=(wr - wr_hi.astype(f32)).astype(bf16), br=br)

    c_all = jnp.concatenate([c_prompt, c_sample], axis=0)
    mod_all = _matmul(jax.nn.silu(c_all), lyr(w_ada), c_all.shape[0], 1024) + lyr(b_ada)
    mod_p, mod_s = mod_all[:bp], mod_all[bp:]

    wkv_zero = jnp.zeros((bp, RWKV_HEADS, RWKV_HEAD_DIM, RWKV_HEAD_DIM), f32)
    shift_zero = jnp.zeros((bp, D_MODEL), x_prompt.dtype)
    (x1p, h2p, lgp, gt2p), (a1, a2, a3, a4, a5) = _layer_front(
        x_prompt, mod_p, lp, l, None, None, None, None, wkv_zero, shift_zero, past_len, tp)
    (x1s, h2s, lgs, gt2s), (b1, b2, b3, b4, b5) = _layer_front(
        x_sample, mod_s, lp, l, cache_cmp_kv, cache_slc_kv, page_table, lyr(state_win_kv), lyr(state_wkv),
        lyr(state_shift), past_len, bs * ts)

    n_p, n_s = bp * tp, bs * ts
    h2 = jnp.concatenate([h2p.reshape(n_p, D_MODEL), h2s.reshape(n_s, D_MODEL), jnp.zeros((1, D_MODEL), bf16)], axis=0)
    eid, ew = _hier_route(jnp.concatenate([lgp.reshape(n_p, ROUTER_LANES), lgs.reshape(n_s, ROUTER_LANES)], axis=0))
    y0, y1 = _moe_ffn(h2, eid, lyr(w_gate), lyr(w_up), lyr(w_down))
    y_prompt = _final(x1p, y0, y1, ew, 0, gt2p, norm_f, 256).reshape(x_prompt.shape)
    y_sample = _final(x1s, y0, y1, ew, n_p, gt2s, norm_f, x1s.shape[1]).reshape(x_sample.shape)
    st = lambda a: a[None]
    return (y_prompt, y_sample, st(a1), st(b1), st(a2), st(b2), st(a3), st(b3), st(a4), st(b4), st(a5), st(b5))
```

```python
import functools

import jax
import jax.numpy as jnp
from jax import lax
from jax.experimental import pallas as pl
from jax.experimental.pallas import tpu as pltpu

D_MODEL = 2048
DEPTH = 1
PAGE_SIZE = 128
HEAD_DIM = 128
NSA_WIDTH = D_MODEL // 2
NSA_HEADS = NSA_WIDTH // HEAD_DIM
NSA_KV_HEADS = 2
HPG = NSA_HEADS // NSA_KV_HEADS
KV_WIDTH = NSA_KV_HEADS * HEAD_DIM
CMP_BLOCK = 32
CMP_STRIDE = 16
SEL_BLOCK = 64
N_SELECT = 16
WINDOW = 512
FORCE_BONUS = 1e4
RWKV_WIDTH = D_MODEL - NSA_WIDTH
RWKV_HEAD_DIM = 64
RWKV_HEADS = RWKV_WIDTH // RWKV_HEAD_DIM
GN_EPS = 64e-5
N_GROUPS = 4
EXPERTS_PER_GROUP = 8
N_EXPERTS = N_GROUPS * EXPERTS_PER_GROUP
EXPERT_TOP_K = 2
NORM_EPS = 1e-6
NEG_INF = -1e30
W_Q_OFF = 0
W_CMP_OFF = W_Q_OFF + NSA_WIDTH
W_RKV_OFF = W_CMP_OFF + 6 * KV_WIDTH
W_GATE_OFF = W_RKV_OFF + 3 * RWKV_WIDTH
IN_WIDTH = W_GATE_OFF + 3 * NSA_HEADS
Q_OFF = 0
RKV_OFF = Q_OFF + NSA_WIDTH
CMP_OFF = RKV_OFF + 3 * RWKV_WIDTH
SLC_OFF = CMP_OFF + 2 * KV_WIDTH
WIN_OFF = SLC_OFF + 2 * KV_WIDTH
GATE_OFF = WIN_OFF + 2 * KV_WIDTH
assert GATE_OFF == W_GATE_OFF


def _permute_w_in(w):
    return jnp.concatenate([w[:, W_Q_OFF:W_CMP_OFF], w[:, W_RKV_OFF:W_GATE_OFF], w[:, W_CMP_OFF:W_RKV_OFF],
                            w[:, W_GATE_OFF:]], axis=1)

VMEM_LIMIT_BYTES = 48 * 1024 * 1024


def _mm_kernel(x_ref, w_ref, o_ref):
    o_ref[...] = jnp.dot(x_ref[...].astype(jnp.bfloat16), w_ref[...].astype(jnp.bfloat16),
                         preferred_element_type=jnp.float32)


def _matmul(x, w, tm, tn):
    m, k = x.shape
    n = w.shape[1]
    return pl.pallas_call(
        _mm_kernel,
        grid=(pl.cdiv(m, tm), pl.cdiv(n, tn)),
        in_specs=[pl.BlockSpec((tm, k), lambda i, j: (i, 0)),
                  pl.BlockSpec((k, tn), lambda i, j: (0, j))],
        out_specs=pl.BlockSpec((tm, tn), lambda i, j: (i, j)),
        out_shape=jax.ShapeDtypeStruct((m, n), jnp.float32),
        compiler_params=pltpu.CompilerParams(
            dimension_semantics=("arbitrary", "arbitrary"), vmem_limit_bytes=VMEM_LIMIT_BYTES),
        name="matmul",
    )(x, w)


MOE_TM = 256
MOE_TM_DECODE = 128


def _moe_block_kernel(meta_ref, x_ref, wg_ref, wu_ref, wd_ref, o_ref, wg_b, wu_b, wd_b):
    bf16 = jnp.bfloat16
    i = pl.program_id(0)
    used = i < meta_ref[pl.num_programs(0)]
    new_expert = (i == 0) | (meta_ref[i] != meta_ref[jnp.maximum(i - 1, 0)])

    @pl.when(used & new_expert)
    def _():
        wg_b[...] = wg_ref[0].astype(bf16)
        wu_b[...] = wu_ref[0].astype(bf16)
        wd_b[...] = wd_ref[0].astype(bf16)

    @pl.when(used)
    def _():
        x = x_ref[0]
        g = jnp.dot(x, wg_b[...], preferred_element_type=jnp.float32)
        u = jnp.dot(x, wu_b[...], preferred_element_type=jnp.float32)
        hmid = (g * jax.nn.sigmoid(g)) * u
        o_ref[0] = jnp.dot(hmid.astype(bf16), wd_b[...], preferred_element_type=jnp.float32)

    @pl.when(jnp.logical_not(used))
    def _():
        o_ref[...] = jnp.zeros(o_ref.shape, o_ref.dtype)


def _moe_blocks(xb, meta, w_gate, w_up, w_down):
    n_blk, mb, d = xb.shape
    de = w_gate.shape[2]
    grid_spec = pltpu.PrefetchScalarGridSpec(
        num_scalar_prefetch=1,
        grid=(n_blk,),
        in_specs=[pl.BlockSpec((1, mb, d), lambda i, e: (i, 0, 0)),
                  pl.BlockSpec((1, d, de), lambda i, e: (e[i], 0, 0)),
                  pl.BlockSpec((1, d, de), lambda i, e: (e[i], 0, 0)),
                  pl.BlockSpec((1, de, d), lambda i, e: (e[i], 0, 0))],
        out_specs=pl.BlockSpec((1, mb, d), lambda i, e: (i, 0, 0)),
        scratch_shapes=[pltpu.VMEM((d, de), jnp.bfloat16), pltpu.VMEM((d, de), jnp.bfloat16),
                        pltpu.VMEM((de, d), jnp.bfloat16)],
    )
    return pl.pallas_call(
        _moe_block_kernel,
        grid_spec=grid_spec,
        out_shape=jax.ShapeDtypeStruct((n_blk, mb, d), jnp.float32),
        compiler_params=pltpu.CompilerParams(
            dimension_semantics=("arbitrary",), vmem_limit_bytes=VMEM_LIMIT_BYTES),
        name="moe_blocks",
    )(meta, xb, w_gate, w_up, w_down)


ROUTER_LANES = 128


KV_TILES = 3
KV_PARTS = 2 * NSA_KV_HEADS


def _norm_in_kernel(x_ref, n1_ref, sc_ref, sh_ref, w_ref, proj_ref, h_ref, kv_ref, hb_scr, *, kv_tile0):
    j = pl.program_id(2)

    @pl.when(j == 0)
    def _():
        x = x_ref[0]
        y = x * lax.rsqrt(jnp.mean(x * x, axis=-1, keepdims=True) + NORM_EPS) * n1_ref[...]
        h = y * (1.0 + sc_ref[0]) + sh_ref[0]
        h_ref[0] = h
        hb_scr[...] = h.astype(jnp.bfloat16)

    proj_ref[0] = jnp.dot(hb_scr[...], w_ref[...], preferred_element_type=jnp.float32)

    @pl.when((j >= kv_tile0) & (j < kv_tile0 + KV_TILES))
    def _():
        tm = proj_ref.shape[1]
        for part in range(KV_PARTS):
            kv_ref[0, 0, pl.ds(part, tm, stride=KV_PARTS), :] = proj_ref[0, :, part * HEAD_DIM:(part + 1) * HEAD_DIM]


def _norm_in(x, norm1, sc, sh, w_in_b, tm):
    g, t, d = x.shape
    n = w_in_b.shape[1]
    tn = 2 * KV_WIDTH
    kv_tile0 = CMP_OFF // tn
    assert CMP_OFF % tn == 0 and GATE_OFF == CMP_OFF + KV_TILES * tn
    mrows = sc.shape[1]
    mod_spec = pl.BlockSpec((1, mrows if mrows == 1 else tm, d),
                            (lambda b, i, j: (b, 0, 0)) if mrows == 1 else (lambda b, i, j: (b, i, 0)))
    return pl.pallas_call(
        functools.partial(_norm_in_kernel, kv_tile0=kv_tile0),
        grid=(g, t // tm, pl.cdiv(n, tn)),
        in_specs=[pl.BlockSpec((1, tm, d), lambda b, i, j: (b, i, 0)),
                  pl.BlockSpec((1, d), lambda b, i, j: (0, 0)),
                  mod_spec, mod_spec,
                  pl.BlockSpec((d, tn), lambda b, i, j: (0, j))],
        out_specs=[pl.BlockSpec((1, tm, tn), lambda b, i, j: (b, i, j)),
                   pl.BlockSpec((1, tm, d), lambda b, i, j: (b, i, 0)),
                   pl.BlockSpec((1, 1, tm * KV_PARTS, HEAD_DIM),
                                lambda b, i, j: (jnp.clip(j - kv_tile0, 0, KV_TILES - 1), b, i, 0))],
        out_shape=[jax.ShapeDtypeStruct((g, t, n), jnp.float32), jax.ShapeDtypeStruct((g, t, d), jnp.float32),
                   jax.ShapeDtypeStruct((KV_TILES, g, t * KV_PARTS, HEAD_DIM), jnp.float32)],
        scratch_shapes=[pltpu.VMEM((tm, d), jnp.bfloat16)],
        compiler_params=pltpu.CompilerParams(
            dimension_semantics=("arbitrary", "arbitrary", "arbitrary"), vmem_limit_bytes=VMEM_LIMIT_BYTES),
        name="norm_in",
    )(x, norm1.reshape(1, d), sc, sh, w_in_b)


def _mix_out_kernel(on_ref, orw_ref, w_ref, x_ref, gt_ref, sc_ref, sh_ref, n2_ref, wr_hi_ref, wr_lo_ref, br_ref,
                    x1_ref, h2_ref, lg_ref):
    f32, bf16 = jnp.float32, jnp.bfloat16
    half = on_ref.shape[2]
    mixed = (jnp.dot(on_ref[0].astype(bf16), w_ref[0:half, :], preferred_element_type=f32)
             + jnp.dot(orw_ref[0].astype(bf16), w_ref[half:, :], preferred_element_type=f32))
    x1 = x_ref[0] + gt_ref[0] * mixed
    x1_ref[0] = x1
    y = x1 * lax.rsqrt(jnp.mean(x1 * x1, axis=-1, keepdims=True) + NORM_EPS) * n2_ref[...]
    h2 = y * (1.0 + sc_ref[0]) + sh_ref[0]
    hi = h2.astype(bf16)
    h2_ref[0] = hi
    lo = (h2 - hi.astype(f32)).astype(bf16)
    lg_ref[0] = (jnp.dot(hi, wr_hi_ref[...], preferred_element_type=f32)
                 + jnp.dot(hi, wr_lo_ref[...], preferred_element_type=f32)
                 + jnp.dot(lo, wr_hi_ref[...], preferred_element_type=f32) + br_ref[...])


def _mix_out(o_nsa, o_rwkv, w_out_b, x, gt, sc, sh, norm2, wr_hi, wr_lo, br, tm):
    g, t, d = x.shape
    half = o_nsa.shape[2]
    mrows = sc.shape[1]
    mod_spec = pl.BlockSpec((1, mrows if mrows == 1 else tm, d),
                            (lambda b, i: (b, 0, 0)) if mrows == 1 else (lambda b, i: (b, i, 0)))
    row = lambda w: pl.BlockSpec((1, tm, w), lambda b, i: (b, i, 0))
    full = lambda a: pl.BlockSpec(a.shape, lambda b, i: (0,) * a.ndim)
    n2 = norm2.reshape(1, d)
    return pl.pallas_call(
        _mix_out_kernel,
        grid=(g, t // tm),
        in_specs=[row(half), row(half), full(w_out_b), row(d), mod_spec, mod_spec, mod_spec, full(n2),
                  full(wr_hi), full(wr_lo), full(br)],
        out_specs=[row(d), row(d), row(ROUTER_LANES)],
        out_shape=[jax.ShapeDtypeStruct((g, t, d), jnp.float32), jax.ShapeDtypeStruct((g, t, d), jnp.bfloat16),
                   jax.ShapeDtypeStruct((g, t, ROUTER_LANES), jnp.float32)],
        compiler_params=pltpu.CompilerParams(
            dimension_semantics=("arbitrary", "arbitrary"), vmem_limit_bytes=VMEM_LIMIT_BYTES),
        name="mix_out",
    )(o_nsa, o_rwkv, w_out_b, x, gt, sc, sh, n2, wr_hi, wr_lo, br)


def _final_kernel(x_ref, y0_ref, y1_ref, ew_ref, gt_ref, nf_ref, o_ref):
    ew = ew_ref[...]
    ffn = y0_ref[...] * ew[:, 0:1] + y1_ref[...] * ew[:, 1:2]
    x2 = x_ref[0] + gt_ref[0] * ffn
    o_ref[0] = x2 * lax.rsqrt(jnp.mean(x2 * x2, axis=-1, keepdims=True) + NORM_EPS) * nf_ref[...]


def _final(x1, y0, y1, ew, row_off, gt, norm_f, tm):
    g, t, d = x1.shape
    assert row_off % tm == 0
    mrows = gt.shape[1]
    mod_spec = pl.BlockSpec((1, mrows if mrows == 1 else tm, d),
                            (lambda b, i: (b, 0, 0)) if mrows == 1 else (lambda b, i: (b, i, 0)))
    row = pl.BlockSpec((1, tm, d), lambda b, i: (b, i, 0))
    flat = lambda w: pl.BlockSpec((tm, w), lambda b, i: (row_off // tm + b * (t // tm) + i, 0))
    return pl.pallas_call(
        _final_kernel,
        grid=(g, t // tm),
        in_specs=[row, flat(d), flat(d), flat(EXPERT_TOP_K), mod_spec, pl.BlockSpec((1, d), lambda b, i: (0, 0))],
        out_specs=row,
        out_shape=jax.ShapeDtypeStruct((g, t, d), jnp.float32),
        compiler_params=pltpu.CompilerParams(
            dimension_semantics=("arbitrary", "arbitrary"), vmem_limit_bytes=VMEM_LIMIT_BYTES),
        name="final_norm",
    )(x1, y0, y1, ew, gt, norm_f.reshape(1, d))


def _cmp_partial_kernel(*refs, n_src, rows_per_src):
    x_refs = refs[:n_src]
    w1k_ref, w1v_ref, a_ref, b_ref = refs[n_src:]
    nch_src = rows_per_src // CMP_STRIDE
    for kvg in range(4):
        w_ref = w1k_ref if kvg < 2 else w1v_ref
        acc = None
        for p in range(CMP_STRIDE):
            parts = [x_refs[s][pl.ds(4 * p + kvg, nch_src, stride=4 * CMP_STRIDE), :] for s in range(n_src)]
            xp = parts[0] if n_src == 1 else jnp.concatenate(parts, axis=0)
            d = jnp.dot(xp.astype(jnp.bfloat16), w_ref[p], preferred_element_type=jnp.float32)
            acc = d if acc is None else acc + d
        a_ref[0, :, kvg * HEAD_DIM:(kvg + 1) * HEAD_DIM] = acc[:, :HEAD_DIM]
        b_ref[0, :, kvg * HEAD_DIM:(kvg + 1) * HEAD_DIM] = acc[:, HEAD_DIM:]


def _cmp_finish_kernel(a_ref, b_ref, pek_ref, pev_ref, w1k_ref, w1v_ref, w2k_ref, w2v_ref, o_ref):
    nch = a_ref.shape[1]
    for kv, (pe_ref, w1_ref, w2_ref) in enumerate(((pek_ref, w1k_ref, w2k_ref), (pev_ref, w1v_ref, w2v_ref))):
        pe8 = jnp.broadcast_to(pe_ref[...], (8, pe_ref.shape[1])).astype(jnp.bfloat16)
        pterm = jnp.dot(pe8, w1_ref[...], preferred_element_type=jnp.float32)[0:1]
        w2 = w2_ref[...]
        for g in range(NSA_KV_HEADS):
            lo = (kv * NSA_KV_HEADS + g) * HEAD_DIM
            nxt = pltpu.roll(b_ref[0, :, lo:lo + HEAD_DIM], nch - 1, 0)
            pre = a_ref[0, :, lo:lo + HEAD_DIM] + nxt + pterm
            act = pre * jax.nn.sigmoid(pre)
            o_ref[0, :, lo:lo + HEAD_DIM] = jnp.dot(act.astype(jnp.bfloat16), w2, preferred_element_type=jnp.float32)


def _cmp_weights(lp):
    bf = jnp.bfloat16
    half = CMP_BLOCK // 2
    cat = lambda w: jnp.concatenate([w[:half], w[half:]], axis=-1).astype(bf)
    flat = lambda w: w.reshape(CMP_BLOCK * HEAD_DIM, HEAD_DIM).astype(bf)
    return dict(w1k_cat=cat(lp['cmp_k_w1']), w1v_cat=cat(lp['cmp_v_w1']),
                w1k_flat=flat(lp['cmp_k_w1']), w1v_flat=flat(lp['cmp_v_w1']),
                pek=lp['cmp_k_pe'].reshape(1, -1), pev=lp['cmp_v_pe'].reshape(1, -1),
                w2k=lp['cmp_k_w2'].astype(bf), w2v=lp['cmp_v_w2'].astype(bf))


def _cmp_finish(a, b, cw):
    bsz, nch, _ = a.shape
    full = lambda arr: pl.BlockSpec(arr.shape, lambda i: (0,) * arr.ndim)
    blk = pl.BlockSpec((1, nch, 4 * HEAD_DIM), lambda i: (i, 0, 0))
    ws = [cw['pek'], cw['pev'], cw['w1k_flat'], cw['w1v_flat'], cw['w2k'], cw['w2v']]
    return pl.pallas_call(
        _cmp_finish_kernel,
        grid=(bsz,),
        in_specs=[blk, blk] + [full(w) for w in ws],
        out_specs=blk,
        out_shape=jax.ShapeDtypeStruct((bsz, nch, 4 * HEAD_DIM), jnp.float32),
        compiler_params=pltpu.CompilerParams(dimension_semantics=("arbitrary",), vmem_limit_bytes=VMEM_LIMIT_BYTES),
        name="cmp_finish",
    )(a, b, *ws)


def _compress_prompt(rows4, cw):
    bsz, seq = rows4.shape[0], rows4.shape[1] // 4
    nch = seq // CMP_STRIDE
    x_specs = [pl.BlockSpec((None, seq * 4, HEAD_DIM), lambda i: (i, 0, 0))]
    w_spec = pl.BlockSpec(cw['w1k_cat'].shape, lambda i: (0, 0, 0))
    out_spec = pl.BlockSpec((1, nch, 4 * HEAD_DIM), lambda i: (i, 0, 0))
    shp = jax.ShapeDtypeStruct((bsz, nch, 4 * HEAD_DIM), jnp.float32)
    a, b = pl.pallas_call(
        functools.partial(_cmp_partial_kernel, n_src=1, rows_per_src=seq),
        grid=(bsz,),
        in_specs=x_specs + [w_spec, w_spec],
        out_specs=[out_spec, out_spec],
        out_shape=[shp, shp],
        compiler_params=pltpu.CompilerParams(dimension_semantics=("arbitrary",), vmem_limit_bytes=VMEM_LIMIT_BYTES),
        name="cmp_partial_prompt",
    )(rows4, cw['w1k_cat'], cw['w1v_cat'])
    return _cmp_finish(a, b, cw)


_NT = (((1,), (1,)), ((), ()))
LOG2E = 1.4426950408889634
SEL_TK = 512
WIN_TK = 256


def _flash_update(s, v, m_ref, l_ref, acc_ref, h):
    tk = s.shape[1]
    m_prev = m_ref[h]
    m_new = jnp.maximum(m_prev, jnp.max(s, axis=-1, keepdims=True))
    alpha = jnp.exp2(m_prev - m_new)
    p = jnp.exp2(s - jnp.concatenate([m_new] * (tk // HEAD_DIM), axis=1))
    l_ref[h] = alpha * l_ref[h] + jnp.sum(p, axis=-1, keepdims=True)
    acc_ref[h] = alpha * acc_ref[h] + jnp.dot(p.astype(jnp.bfloat16), v, preferred_element_type=jnp.float32)
    m_ref[h] = m_new


def _nsa_prompt_kernel(q_ref, slc_ref, win_ref, gate_ref, kvc_ref, o_ref, m_ref, l_ref, acc_ref, *, tq, seq):
    f32, bf16 = jnp.float32, jnp.bfloat16
    qi = pl.program_id(1)
    t0 = qi * tq
    scale = HEAD_DIM ** -0.5
    nc_valid = seq // CMP_STRIDE - CMP_BLOCK // CMP_STRIDE + 1
    n_sel = seq // SEL_BLOCK
    pos = t0 + lax.broadcasted_iota(jnp.int32, (tq, 1), 0)
    lane = lax.broadcasted_iota(jnp.int32, (1, HEAD_DIM), 1)
    gates = jax.nn.sigmoid(gate_ref[...])
    dist_c = pos - (lane * CMP_STRIDE + (CMP_BLOCK - 1))
    valid_c = (dist_c >= 0) & (lane < nc_valid)
    dist_cf = dist_c.astype(f32)
    c_row = lax.broadcasted_iota(jnp.int32, (HEAD_DIM, 1), 0)
    overlap = jnp.where((c_row * CMP_STRIDE <= lane * SEL_BLOCK + (SEL_BLOCK - 1))
                        & (c_row * CMP_STRIDE + (CMP_BLOCK - 1) >= lane * SEL_BLOCK), 1.0, 0.0).astype(bf16)
    j_row = lax.broadcasted_iota(jnp.int32, (n_sel, 1), 0)
    blk_lane = jnp.right_shift(t0 + lax.broadcasted_iota(jnp.int32, (1, tq), 1), 6)
    forced_t = (j_row == 0) | (j_row == blk_lane) | (j_row == blk_lane - 1)

    def gate_col(branch, hh):
        c = branch * NSA_HEADS + hh
        return gates[:, c:c + 1]

    def reset():
        m_ref[...] = jnp.full(m_ref.shape, NEG_INF, f32)
        l_ref[...] = jnp.zeros(l_ref.shape, f32)
        acc_ref[...] = jnp.zeros(acc_ref.shape, f32)

    for g in range(NSA_KV_HEADS):
        kcol = slice(g * HEAD_DIM, (g + 1) * HEAD_DIM)
        vcol = slice((NSA_KV_HEADS + g) * HEAD_DIM, (NSA_KV_HEADS + g + 1) * HEAD_DIM)
        heads = [g * HPG + h for h in range(HPG)]
        slopes = [2.0 ** -(hh + 1) for hh in heads]

        kc = kvc_ref[0, :, kcol].astype(bf16)
        vc = kvc_ref[0, :, vcol].astype(bf16)
        psum = jnp.zeros((tq, HEAD_DIM), f32)
        for h, hh in enumerate(heads):
            qh = q_ref[:, hh * HEAD_DIM:(hh + 1) * HEAD_DIM].astype(bf16)
            s = lax.dot_general(qh, kc, _NT, preferred_element_type=f32) * scale - slopes[h] * dist_cf
            s = jnp.where(valid_c, s, NEG_INF)
            e = jnp.exp(s - jnp.max(s, axis=-1, keepdims=True))
            p = e / jnp.sum(e, axis=-1, keepdims=True)
            p = jnp.where(valid_c, p, 0.0)
            o_cmp = jnp.dot(p.astype(bf16), vc, preferred_element_type=f32)
            o_ref[:, hh * HEAD_DIM:(hh + 1) * HEAD_DIM] = gate_col(0, hh) * o_cmp
            psum = psum + p
        p_hi = psum.astype(bf16)
        p_lo = (psum - p_hi.astype(f32)).astype(bf16)
        imp = (jnp.dot(p_hi, overlap, preferred_element_type=f32)
               + jnp.dot(p_lo, overlap, preferred_element_type=f32))
        imp = imp.T[:n_sel]
        imp = jnp.where(forced_t, imp + FORCE_BONUS, imp)
        imp = jnp.where(j_row <= blk_lane, imp, NEG_INF)
        beaten = jnp.zeros((n_sel, tq), f32)
        for jp in range(n_sel):
            other = imp[jp:jp + 1]
            tie = jnp.where(j_row > jp, 1.0, 0.0)
            beaten = beaten + jnp.where(other > imp, 1.0, jnp.where(other == imp, tie, 0.0))
        sel_t = jnp.where(beaten < N_SELECT, jnp.where(imp > 0.5 * NEG_INF, 1.0, 0.0), 0.0).astype(bf16)

        reset()

        def sel_body(kt, carry):
            k0 = pl.multiple_of(kt * SEL_TK, SEL_TK)
            k = slc_ref[pl.ds(k0, SEL_TK), kcol].astype(bf16)
            v = slc_ref[pl.ds(k0, SEL_TK), vcol].astype(bf16)
            kpos = k0 + lax.broadcasted_iota(jnp.int32, (1, SEL_TK), 1)
            dist = pos - kpos
            expand = jnp.where(jnp.right_shift(kpos, 6) == j_row, 1.0, 0.0).astype(bf16)
            picked = lax.dot_general(sel_t, expand, (((0,), (0,)), ((), ())), preferred_element_type=f32)
            keep = jnp.where(dist >= 0, picked, 0.0) > 0.5
            kpos_f = (pos[0:1] - dist[0:1]).astype(f32)
            for h, hh in enumerate(heads):
                qh = q_ref[:, hh * HEAD_DIM:(hh + 1) * HEAD_DIM].astype(bf16)
                s = (lax.dot_general(qh, k, _NT, preferred_element_type=f32) * (scale * LOG2E)
                     + (slopes[h] * LOG2E) * kpos_f)
                _flash_update(jnp.where(keep, s, NEG_INF), v, m_ref, l_ref, acc_ref, h)
            return carry

        lax.fori_loop(0, (t0 + tq - 1) // SEL_TK + 1, sel_body, 0)
        for h, hh in enumerate(heads):
            hs = slice(hh * HEAD_DIM, (hh + 1) * HEAD_DIM)
            o_ref[:, hs] = o_ref[:, hs] + gate_col(1, hh) * (acc_ref[h] / l_ref[h])

        reset()

        def win_body(kt, carry):
            k0 = pl.multiple_of(kt * WIN_TK, WIN_TK)
            k = win_ref[pl.ds(k0, WIN_TK), kcol].astype(bf16)
            v = win_ref[pl.ds(k0, WIN_TK), vcol].astype(bf16)
            dist = pos - (k0 + lax.broadcasted_iota(jnp.int32, (1, WIN_TK), 1))
            keep = (dist >= 0) & (dist < WINDOW)
            kpos_f = (pos[0:1] - dist[0:1]).astype(f32)
            for h, hh in enumerate(heads):
                qh = q_ref[:, hh * HEAD_DIM:(hh + 1) * HEAD_DIM].astype(bf16)
                s = (lax.dot_general(qh, k, _NT, preferred_element_type=f32) * (scale * LOG2E)
                     + (slopes[h] * LOG2E) * kpos_f)
                _flash_update(jnp.where(keep, s, NEG_INF), v, m_ref, l_ref, acc_ref, h)
            return carry

        lax.fori_loop(jnp.maximum(t0 - (WINDOW - 1), 0) // WIN_TK, (t0 + tq - 1) // WIN_TK + 1, win_body, 0)
        for h, hh in enumerate(heads):
            hs = slice(hh * HEAD_DIM, (hh + 1) * HEAD_DIM)
            o_ref[:, hs] = o_ref[:, hs] + gate_col(2, hh) * (acc_ref[h] / l_ref[h])


def _nsa_prompt(proj2d, kvc, bsz, seq, tq=256):
    nq = seq // tq
    kvw = 2 * KV_WIDTH
    return pl.pallas_call(
        functools.partial(_nsa_prompt_kernel, tq=tq, seq=seq),
        grid=(bsz, nq),
        in_specs=[pl.BlockSpec((tq, NSA_WIDTH), lambda b, i: (b * nq + i, 0)),
                  pl.BlockSpec((seq, kvw), lambda b, i: (b, SLC_OFF // kvw)),
                  pl.BlockSpec((seq, kvw), lambda b, i: (b, WIN_OFF // kvw)),
                  pl.BlockSpec((tq, HEAD_DIM), lambda b, i: (b * nq + i, GATE_OFF // HEAD_DIM)),
                  pl.BlockSpec((1, seq // CMP_STRIDE, kvw), lambda b, i: (b, 0, 0))],
        out_specs=pl.BlockSpec((tq, NSA_WIDTH), lambda b, i: (b * nq + i, 0)),
        out_shape=jax.ShapeDtypeStruct((bsz * seq, NSA_WIDTH), jnp.float32),
        scratch_shapes=[pltpu.VMEM((HPG, tq, HEAD_DIM), jnp.float32),
                        pltpu.VMEM((HPG, tq, HEAD_DIM), jnp.float32),
                        pltpu.VMEM((HPG, tq, HEAD_DIM), jnp.float32)],
        compiler_params=pltpu.CompilerParams(
            dimension_semantics=("arbitrary", "arbitrary"), vmem_limit_bytes=VMEM_LIMIT_BYTES),
        name="nsa_prompt",
    )(proj2d, proj2d, proj2d, proj2d, kvc)


CMP_PAGES_PER_STEP = 32
TOPK_LANES = 384
IDX_LANES = 128


def _cmp_partial_paged_kernel(pt_ref, *refs, n_src, rows_per_src):
    del pt_ref
    _cmp_partial_kernel(*refs, n_src=n_src, rows_per_src=rows_per_src)


def _compress_paged(pool, page_table, cw):
    bsz, n_pages = page_table.shape
    nps = CMP_PAGES_PER_STEP
    n_tiles = n_pages // nps
    nch_tile = nps * PAGE_SIZE // CMP_STRIDE

    def page_spec(s):
        return pl.BlockSpec((PAGE_SIZE * 4, HEAD_DIM), lambda b, i, pt: (pt[b * n_pages + i * nps + s], 0))

    x_specs = [page_spec(s) for s in range(nps)]
    w_spec = pl.BlockSpec(cw['w1k_cat'].shape, lambda b, i, pt: (0, 0, 0))
    out_spec = pl.BlockSpec((1, nch_tile, 4 * HEAD_DIM), lambda b, i, pt: (b, i, 0))
    shp = jax.ShapeDtypeStruct((bsz, n_tiles * nch_tile, 4 * HEAD_DIM), jnp.float32)
    a, b = pl.pallas_call(
        functools.partial(_cmp_partial_paged_kernel, n_src=nps, rows_per_src=PAGE_SIZE),
        grid_spec=pltpu.PrefetchScalarGridSpec(
            num_scalar_prefetch=1, grid=(bsz, n_tiles),
            in_specs=x_specs + [w_spec, w_spec], out_specs=[out_spec, out_spec]),
        out_shape=[shp, shp],
        compiler_params=pltpu.CompilerParams(
            dimension_semantics=("arbitrary", "arbitrary"), vmem_limit_bytes=VMEM_LIMIT_BYTES),
        name="cmp_partial_paged",
    )(page_table.reshape(-1), *([pool] * nps), cw['w1k_cat'], cw['w1v_cat'])
    return _cmp_finish(a, b, cw)


def _nsa_decode_a_kernel(proj_ref, kvc_ref, win_ref, o_ref, gsel_ref, idx_ref, *, t_new, n_past):
    f32, bf16 = jnp.float32, jnp.bfloat16
    scale = HEAD_DIM ** -0.5
    nch = kvc_ref.shape[1]
    n_win = win_ref.shape[1] // KV_PARTS
    rows = HPG * t_new
    r_iota = lax.broadcasted_iota(jnp.int32, (rows, 1), 0)
    t_row = r_iota % t_new
    h_row = r_iota // t_new
    pos_row = n_past + t_row
    gates = jax.nn.sigmoid(proj_ref[0, :, GATE_OFF:IN_WIDTH])
    c_lane = lax.broadcasted_iota(jnp.int32, (1, nch), 1)
    dist_c = pos_row - (c_lane * CMP_STRIDE + (CMP_BLOCK - 1))
    valid_c = (dist_c >= 0) & (c_lane < nch - 1)
    c_col = lax.broadcasted_iota(jnp.int32, (nch, 1), 0)
    j_lane = lax.broadcasted_iota(jnp.int32, (1, TOPK_LANES), 1)
    overlap = jnp.where((c_col * CMP_STRIDE <= j_lane * SEL_BLOCK + (SEL_BLOCK - 1))
                        & (c_col * CMP_STRIDE + (CMP_BLOCK - 1) >= j_lane * SEL_BLOCK), 1.0, 0.0).astype(bf16)
    pos_t = n_past + lax.broadcasted_iota(jnp.int32, (t_new, 1), 0)
    blk_t = pos_t // SEL_BLOCK
    forced = (j_lane == 0) | (j_lane == blk_t) | (j_lane == blk_t - 1)
    j_f = j_lane.astype(f32)
    k_lane = lax.broadcasted_iota(jnp.int32, (1, IDX_LANES), 1)
    i_win = lax.broadcasted_iota(jnp.int32, (1, n_win), 1)
    dist_w = pos_row - (n_past - n_win + i_win)
    keep_w = (dist_w >= 0) & (dist_w < WINDOW)
    j_new = lax.broadcasted_iota(jnp.int32, (1, 8), 1)
    dist_n = t_row - j_new
    keep_n = (dist_n >= 0) & (j_new < t_new)
    zpad = jnp.zeros((8 - t_new, HEAD_DIM), f32)

    for g in range(NSA_KV_HEADS):
        kcol = slice(g * HEAD_DIM, (g + 1) * HEAD_DIM)
        vcol = slice((NSA_KV_HEADS + g) * HEAD_DIM, (NSA_KV_HEADS + g + 1) * HEAD_DIM)
        heads = [g * HPG + h for h in range(HPG)]
        slope_row = jnp.zeros((rows, 1), f32)
        for h, hh in enumerate(heads):
            slope_row = jnp.where(h_row == h, 2.0 ** -(hh + 1), slope_row)
        q = jnp.concatenate([proj_ref[0, :, hh * HEAD_DIM:(hh + 1) * HEAD_DIM] for hh in heads], axis=0).astype(bf16)

        kc = kvc_ref[0, :, kcol].astype(bf16)
        vc = kvc_ref[0, :, vcol].astype(bf16)
        s = lax.dot_general(q, kc, _NT, preferred_element_type=f32) * scale - slope_row * dist_c.astype(f32)
        s = jnp.where(valid_c, s, NEG_INF)
        e = jnp.exp(s - jnp.max(s, axis=-1, keepdims=True))
        p = e / jnp.sum(e, axis=-1, keepdims=True)
        p = jnp.where(valid_c, p, 0.0)
        o_cmp = jnp.dot(p.astype(bf16), vc, preferred_element_type=f32)
        psum = p[0:t_new]
        for h in range(1, HPG):
            psum = psum + p[h * t_new:(h + 1) * t_new]

        p_hi = psum.astype(bf16)
        p_lo = (psum - p_hi.astype(f32)).astype(bf16)
        imp = (jnp.dot(p_hi, overlap, preferred_element_type=f32)
               + jnp.dot(p_lo, overlap, preferred_element_type=f32))
        imp = jnp.where(forced, imp + FORCE_BONUS, imp)
        imp = jnp.where(j_lane <= blk_t, imp, NEG_INF)
        picked = jnp.full((t_new, IDX_LANES), -1.0, f32)
        for k in range(N_SELECT):
            best = jnp.max(imp, axis=-1, keepdims=True)
            first = jnp.min(jnp.where(imp == best, j_f, 1e9), axis=-1, keepdims=True)
            picked = jnp.where(k_lane == k, jnp.where(best > 0.5 * NEG_INF, first, -1.0), picked)
            imp = jnp.where(j_f == first, -3e38, imp)
        idx_ref[0, g * t_new:(g + 1) * t_new, :] = picked.astype(jnp.int32)

        kw = win_ref[0, pl.ds(g, n_win, stride=KV_PARTS), :].astype(bf16)
        vw = win_ref[0, pl.ds(NSA_KV_HEADS + g, n_win, stride=KV_PARTS), :].astype(bf16)
        kn = jnp.concatenate([proj_ref[0, :, WIN_OFF + g * HEAD_DIM:WIN_OFF + (g + 1) * HEAD_DIM], zpad], axis=0)
        vn = jnp.concatenate([proj_ref[0, :, WIN_OFF + KV_WIDTH + g * HEAD_DIM:
                                       WIN_OFF + KV_WIDTH + (g + 1) * HEAD_DIM], zpad], axis=0)
        s_w = lax.dot_general(q, kw, _NT, preferred_element_type=f32) * scale - slope_row * dist_w.astype(f32)
        s_n = (lax.dot_general(q, kn.astype(bf16), _NT, preferred_element_type=f32) * scale
               - slope_row * dist_n.astype(f32))
        s_w = jnp.where(keep_w, s_w, NEG_INF)
        s_n = jnp.where(keep_n, s_n, NEG_INF)
        m = jnp.maximum(jnp.max(s_w, axis=-1, keepdims=True), jnp.max(s_n, axis=-1, keepdims=True))
        e_w = jnp.exp(s_w - m)
        e_n = jnp.exp(s_n - m)
        den = jnp.sum(e_w, axis=-1, keepdims=True) + jnp.sum(e_n, axis=-1, keepdims=True)
        o_win = (jnp.dot(e_w.astype(bf16), vw, preferred_element_type=f32)
                 + jnp.dot(e_n.astype(bf16), vn.astype(bf16), preferred_element_type=f32)) / den

        for h, hh in enumerate(heads):
            rs = slice(h * t_new, (h + 1) * t_new)
            hs = slice(hh * HEAD_DIM, (hh + 1) * HEAD_DIM)
            o_ref[0, :, hs] = (gates[:, hh:hh + 1] * o_cmp[rs]
                               + gates[:, 2 * NSA_HEADS + hh:2 * NSA_HEADS + hh + 1] * o_win[rs])
            gsel_ref[0, :, hs] = jnp.broadcast_to(gates[:, NSA_HEADS + hh:NSA_HEADS + hh + 1], (t_new, HEAD_DIM))


def _nsa_decode_sel_kernel(idx_ref, pt_ref, q_ref, part_ref, gsel_ref, new_ref, *refs, t_new, n_past):
    del pt_ref
    f32, bf16 = jnp.float32, jnp.bfloat16
    o_ref = refs[NSA_KV_HEADS * N_SELECT]
    b, t = pl.program_id(0), pl.program_id(1)
    scale = HEAD_DIM ** -0.5
    n_past_blk = n_past // SEL_BLOCK
    n_keys = N_SELECT * SEL_BLOCK
    lane = lax.broadcasted_iota(jnp.int32, (1, n_keys), 1)
    slot = lane // SEL_BLOCK
    h_row = lax.broadcasted_iota(jnp.int32, (8, 1), 0)
    for g in range(NSA_KV_HEADS):
        base = ((b * NSA_KV_HEADS + g) * t_new + t) * N_SELECT
        blk_of_lane = jnp.full((1, n_keys), -1, jnp.int32)
        ks, vs = [], []
        for k in range(N_SELECT):
            blk = idx_ref[base + k]
            blk_of_lane = jnp.where(slot == k, blk, blk_of_lane)
            is_new = blk >= n_past_blk
            src = refs[g * N_SELECT + k]
            k_old = src[pl.ds(g, SEL_BLOCK, stride=KV_PARTS), :]
            v_old = src[pl.ds(NSA_KV_HEADS + g, SEL_BLOCK, stride=KV_PARTS), :]
            k_new = new_ref[:, g * HEAD_DIM:(g + 1) * HEAD_DIM]
            v_new = new_ref[:, KV_WIDTH + g * HEAD_DIM:KV_WIDTH + (g + 1) * HEAD_DIM]
            ks.append(jnp.where(is_new, k_new, k_old).astype(bf16))
            vs.append(jnp.where(is_new, v_new, v_old).astype(bf16))
        k_all = jnp.concatenate(ks, axis=0)
        v_all = jnp.concatenate(vs, axis=0)
        dist = (n_past + t) - (blk_of_lane * SEL_BLOCK + lane % SEL_BLOCK)
        keep = (dist >= 0) & (blk_of_lane >= 0)
        q = jnp.concatenate([q_ref[:, (g * HPG + h) * HEAD_DIM:(g * HPG + h + 1) * HEAD_DIM] for h in range(HPG)]
                            + [jnp.zeros((8 - HPG, HEAD_DIM), f32)], axis=0).astype(bf16)
        slope_row = jnp.zeros((8, 1), f32)
        for h in range(HPG):
            slope_row = jnp.where(h_row == h, 2.0 ** -(g * HPG + h + 1), slope_row)
        s = lax.dot_general(q, k_all, _NT, preferred_element_type=f32) * scale - slope_row * dist.astype(f32)
        s = jnp.where(keep, s, NEG_INF)
        e = jnp.exp(s - jnp.max(s, axis=-1, keepdims=True))
        p = e / jnp.sum(e, axis=-1, keepdims=True)
        o_sel = jnp.dot(p.astype(bf16), v_all, preferred_element_type=f32)
        for h in range(HPG):
            hs = slice((g * HPG + h) * HEAD_DIM, (g * HPG + h + 1) * HEAD_DIM)
            o_ref[:, hs] = part_ref[:, hs] + gsel_ref[:, hs] * o_sel[h:h + 1]


def _nsa_decode_rows(proj3, kvc, win_rows, pool_rows, page_table, n_past):
    bsz, t_new, _ = proj3.shape
    n_pages = page_table.shape[1]
    full = jax.ShapeDtypeStruct((bsz, t_new, NSA_WIDTH), jnp.float32)
    part, gsel, idx = pl.pallas_call(
        functools.partial(_nsa_decode_a_kernel, t_new=t_new, n_past=n_past),
        grid=(bsz,),
        in_specs=[pl.BlockSpec((1, t_new, IN_WIDTH), lambda b: (b, 0, 0)),
                  pl.BlockSpec((1,) + kvc.shape[1:], lambda b: (b, 0, 0)),
                  pl.BlockSpec((1,) + win_rows.shape[1:], lambda b: (b, 0, 0))],
        out_specs=[pl.BlockSpec((1, t_new, NSA_WIDTH), lambda b: (b, 0, 0)),
                   pl.BlockSpec((1, t_new, NSA_WIDTH), lambda b: (b, 0, 0)),
                   pl.BlockSpec((1, NSA_KV_HEADS * t_new, IDX_LANES), lambda b: (b, 0, 0))],
        out_shape=[full, full, jax.ShapeDtypeStruct((bsz, NSA_KV_HEADS * t_new, IDX_LANES), jnp.int32)],
        compiler_params=pltpu.CompilerParams(dimension_semantics=("arbitrary",), vmem_limit_bytes=VMEM_LIMIT_BYTES),
        name="nsa_decode_a",
    )(proj3, kvc, win_rows)

    n_past_blk = n_past // SEL_BLOCK
    sub = PAGE_SIZE // SEL_BLOCK
    new_rows = jnp.pad(proj3[:, :, SLC_OFF:WIN_OFF], ((0, 0), (0, SEL_BLOCK - t_new), (0, 0)))

    picked = idx[:, :, :N_SELECT]
    past = jnp.clip(picked, 0, n_past_blk - 1)
    pool_blk = jnp.take_along_axis(page_table[:, None, :], past // sub, axis=2) * sub + past % sub

    def pool_spec(g, k):
        def index(b, t, idx, blk):
            return (blk[((b * NSA_KV_HEADS + g) * t_new + t) * N_SELECT + k], 0)
        return pl.BlockSpec((SEL_BLOCK * KV_PARTS, HEAD_DIM), index)

    row_spec = pl.BlockSpec((None, 1, NSA_WIDTH), lambda b, t, idx, pt: (b * t_new + t, 0, 0))
    out = pl.pallas_call(
        functools.partial(_nsa_decode_sel_kernel, t_new=t_new, n_past=n_past),
        grid_spec=pltpu.PrefetchScalarGridSpec(
            num_scalar_prefetch=2, grid=(bsz, t_new),
            in_specs=[row_spec, row_spec, row_spec,
                      pl.BlockSpec((None, SEL_BLOCK, 2 * KV_WIDTH), lambda b, t, idx, pt: (b, 0, 0))]
            + [pool_spec(g, k) for g in range(NSA_KV_HEADS) for k in range(N_SELECT)],
            out_specs=row_spec),
        out_shape=jax.ShapeDtypeStruct((bsz * t_new, 1, NSA_WIDTH), jnp.float32),
        compiler_params=pltpu.CompilerParams(
            dimension_semantics=("arbitrary", "arbitrary"), vmem_limit_bytes=VMEM_LIMIT_BYTES),
        name="nsa_decode_sel",
    )(picked.reshape(-1), pool_blk.reshape(-1).astype(jnp.int32),
      proj3[:, :, Q_OFF:Q_OFF + NSA_WIDTH].reshape(bsz * t_new, 1, NSA_WIDTH),
      part.reshape(bsz * t_new, 1, NSA_WIDTH), gsel.reshape(bsz * t_new, 1, NSA_WIDTH),
      new_rows, *([pool_rows] * (NSA_KV_HEADS * N_SELECT)))
    return out.reshape(bsz, t_new, NSA_WIDTH)


WKV_LANES = 2 * RWKV_HEAD_DIM
WKV_PAIRS = RWKV_HEADS // 2
WKV_STACK = 4
WKV_BB = 4
WKV_CHUNK = 64
WKV_MIN_CHUNK = 16
WKV_FLUSH = RWKV_HEAD_DIM


def _wkv_kernel(wr_ref, w_ref, k_ref, v_ref, kk_ref, kka_ref, c1_ref, c2_ref, s0_ref, y_ref, st_ref,
                s_scr, y_scr, *, tc):
    f32, bf16 = jnp.float32, jnp.bfloat16
    ti = pl.program_id(1)
    hd = RWKV_HEAD_DIM
    n_tiles = WKV_BB * WKV_PAIRS
    n_stacks = n_tiles // WKV_STACK
    tile = lambda q: (q // WKV_PAIRS, q % WKV_PAIRS)

    @pl.when(ti == 0)
    def _():
        for q in range(n_tiles):
            b, p = tile(q)
            s_scr[q] = jnp.concatenate([s0_ref[b, 2 * p], s0_ref[b, 2 * p + 1]], axis=1)

    lane = lax.broadcasted_iota(jnp.int32, (1, WKV_LANES), 1)
    r2 = lax.broadcasted_iota(jnp.int32, (2 * WKV_LANES, 1), 0)
    c2 = lax.broadcasted_iota(jnp.int32, (1, 2 * WKV_LANES), 1)
    same_head2 = jnp.where(r2 // hd == c2 // hd, 1.0, 0.0).astype(bf16)
    on_diag = lax.broadcasted_iota(jnp.int32, (hd, 1), 0) == lane % hd
    n_flush = min(tc, WKV_FLUSH)
    y_scr[...] = jnp.zeros(y_scr.shape, f32)
    pairs = [(2 * i, 2 * i + 1) for i in range(n_stacks // 2)]
    stack_tiles = lambda st: range(st * WKV_STACK, (st + 1) * WKV_STACK)

    def row_sums(per_tile, s0, s1):
        lhs = jnp.concatenate([jnp.concatenate([per_tile[q] for q in stack_tiles(st)], axis=0)
                               for st in (s0, s1)], axis=1)
        res = jnp.dot(lhs.astype(bf16), same_head2, preferred_element_type=f32)
        return {q: res[n * hd:(n + 1) * hd, half * WKV_LANES:(half + 1) * WKV_LANES]
                for half, st in enumerate((s0, s1)) for n, q in enumerate(stack_tiles(st))}

    def step(t, carry):
        here = (lane % hd) == (t % n_flush)
        get = lambda ref, q: ref[tile(q)[0], tile(q)[1], pl.ds(t, 1), :]
        for s0, s1 in pairs:
            tiles = list(stack_tiles(s0)) + list(stack_tiles(s1))
            s_old = {q: s_scr[q] for q in tiles}
            sa = row_sums({q: s_old[q] * get(kk_ref, q) for q in tiles}, s0, s1)
            y_old = row_sums({q: s_old[q] * get(wr_ref, q) for q in tiles}, s0, s1)
            v_col = row_sums({q: jnp.where(on_diag, get(v_ref, q), 0.0) for q in tiles}, s0, s1)
            for q in tiles:
                s_scr[q] = s_old[q] * get(w_ref, q) - sa[q] * get(kka_ref, q) + v_col[q] * get(k_ref, q)
                y_col = y_old[q] - sa[q] * get(c1_ref, q) + v_col[q] * get(c2_ref, q)
                y_scr[q] = jnp.where(here, y_col, y_scr[q])
        return carry

    for sub in range(tc // n_flush):
        lax.fori_loop(sub * n_flush, (sub + 1) * n_flush, step, 0)
        for q in range(n_tiles):
            b, p = tile(q)
            yt = y_scr[q].T
            y_ref[b, p, sub * n_flush:(sub + 1) * n_flush, :] = jnp.concatenate(
                [yt[:n_flush], yt[hd:hd + n_flush]], axis=1)

    @pl.when(ti == pl.num_programs(1) - 1)
    def _():
        for q in range(n_tiles):
            b, p = tile(q)
            st_ref[b, 2 * p] = s_scr[q][:, :hd]
            st_ref[b, 2 * p + 1] = s_scr[q][:, hd:]


def _wkv_scan(wr, w, k, v, kk, kka, c1, c2, s0, tc):
    bsz, n_pairs, seq, _ = wr.shape
    assert n_pairs == WKV_PAIRS and bsz % WKV_BB == 0 and seq % tc == 0
    n_tiles = WKV_BB * WKV_PAIRS
    x_spec = pl.BlockSpec((WKV_BB, WKV_PAIRS, tc, WKV_LANES), lambda b, i: (b, 0, i, 0))
    s_spec = pl.BlockSpec((WKV_BB, RWKV_HEADS, RWKV_HEAD_DIM, RWKV_HEAD_DIM), lambda b, i: (b, 0, 0, 0))
    return pl.pallas_call(
        functools.partial(_wkv_kernel, tc=tc),
        grid=(bsz // WKV_BB, seq // tc),
        in_specs=[x_spec] * 8 + [s_spec],
        out_specs=[x_spec, s_spec],
        out_shape=[jax.ShapeDtypeStruct(wr.shape, jnp.float32),
                   jax.ShapeDtypeStruct(s0.shape, jnp.float32)],
        scratch_shapes=[pltpu.VMEM((n_tiles, RWKV_HEAD_DIM, WKV_LANES), jnp.float32),
                        pltpu.VMEM((n_tiles, RWKV_HEAD_DIM, WKV_LANES), jnp.float32)],
        compiler_params=pltpu.CompilerParams(
            dimension_semantics=("arbitrary", "arbitrary"), vmem_limit_bytes=VMEM_LIMIT_BYTES),
        name="wkv_scan",
    )(wr, w, k, v, kk, kka, c1, c2, s0)


RW_PACKS = RWKV_WIDTH // WKV_LANES


def _head_sums(x, ones2):
    f32, bf16 = jnp.float32, jnp.bfloat16
    hi = x.astype(bf16)
    lo = (x - hi.astype(f32)).astype(bf16)
    w = 2 * WKV_LANES
    out = []
    for c in range(RWKV_WIDTH // w):
        sl = slice(c * w, (c + 1) * w)
        out.append(jnp.dot(hi[:, sl], ones2, preferred_element_type=f32)
                   + jnp.dot(lo[:, sl], ones2, preferred_element_type=f32))
    return jnp.concatenate(out, axis=1)


def _head_ones():
    w = 2 * WKV_LANES
    r = lax.broadcasted_iota(jnp.int32, (w, 1), 0)
    c = lax.broadcasted_iota(jnp.int32, (1, w), 1)
    return jnp.where(r // RWKV_HEAD_DIM == c // RWKV_HEAD_DIM, 1.0, 0.0).astype(jnp.bfloat16)


def _store_rw(ref, val, pack_major):
    if pack_major:
        for p in range(RW_PACKS):
            ref[0, p] = val[:, p * WKV_LANES:(p + 1) * WKV_LANES]
    else:
        ref[0] = val


def _rwkv_prep_kernel(h_ref, hprev_ref, h0_ref, pr_ref, pk_ref, pv_ref, pprev_r, pprev_k, pprev_v, p0_ref,
                      mu_rkv_ref, mu_wag_ref, dw0_ref, dw1_ref, dw2_ref, a0_ref, a1_ref, a2_ref, g1_ref, g2_ref,
                      kk_w_ref, ka_w_ref, rk_w_ref,
                      wr_out, w_out, k_out, v_out, kk_out, kka_out, c1_out, c2_out, g_out, bonus_out,
                      *, period, pack_major):
    f32, bf16 = jnp.float32, jnp.bfloat16
    i = pl.program_id(1)
    tm = h_ref.shape[1]
    row = lax.broadcasted_iota(jnp.int32, (tm, 1), 0)
    per_row_first = h0_ref.shape[1] != 1
    first = (row % period == 0) if per_row_first else None

    def shifted(cur, prev_blk, first_rows):
        rolled = pltpu.roll(cur, 1, 0)
        if per_row_first:
            return jnp.where(first, first_rows, rolled)
        row0 = jnp.where(i == 0, first_rows, prev_blk[7:8])
        return jnp.where(row == 0, row0, rolled)

    h = h_ref[0]
    xx = shifted(h, hprev_ref[0], h0_ref[0]) - h
    xw = (h + xx * mu_wag_ref[0:1]).astype(bf16)
    xa = (h + xx * mu_wag_ref[1:2]).astype(bf16)
    xg = (h + xx * mu_wag_ref[2:3]).astype(bf16)
    dmid = jnp.tanh(jnp.dot(xw, dw1_ref[...], preferred_element_type=f32))
    dlin = dw0_ref[...] + jnp.dot(dmid.astype(bf16), dw2_ref[...], preferred_element_type=f32)
    z = -dlin
    w_log = -(jnp.maximum(z, 0.0) + jnp.log(1.0 + jnp.exp(-jnp.abs(z)))) - 0.5
    decay = jnp.exp(-jnp.exp(w_log))
    amid = jnp.dot(xa, a1_ref[...], preferred_element_type=f32)
    a = jax.nn.sigmoid(a0_ref[...] + jnp.dot(amid.astype(bf16), a2_ref[...], preferred_element_type=f32))
    gmid = jax.nn.sigmoid(jnp.dot(xg, g1_ref[...], preferred_element_type=f32))
    g = jnp.dot(gmid.astype(bf16), g2_ref[...], preferred_element_type=f32)

    def mixed(cur_ref, prev_ref, n):
        cur = cur_ref[0]
        cs = slice(n * RWKV_WIDTH, (n + 1) * RWKV_WIDTH)
        prev = shifted(cur, prev_ref[0], p0_ref[0][:, cs])
        return cur + mu_rkv_ref[:, cs] * (prev - cur)

    r = mixed(pr_ref, pprev_r, 0)
    k = mixed(pk_ref, pprev_k, 1)
    v = mixed(pv_ref, pprev_v, 2)
    ones2 = _head_ones()
    kk = k * kk_w_ref[...]
    kk = kk / jnp.maximum(jnp.sqrt(_head_sums(kk * kk, ones2)), 1e-12)
    k = k * (1.0 + (a - 1.0) * ka_w_ref[...])
    bonus = _head_sums(r * k * rk_w_ref[...], ones2) * v
    _store_rw(wr_out, decay * r, pack_major)
    _store_rw(c1_out, _head_sums(kk * a * r, ones2), pack_major)
    _store_rw(c2_out, _head_sums(k * r, ones2), pack_major)
    _store_rw(w_out, decay, pack_major)
    _store_rw(k_out, k, pack_major)
    _store_rw(v_out, v, pack_major)
    _store_rw(kk_out, kk, pack_major)
    _store_rw(kka_out, kk * a, pack_major)
    g_out[0] = g
    bonus_out[0] = bonus


def _rwkv_prep(h, proj, h0, p0, lp, tm, period, pack_major):
    f32, bf16 = jnp.float32, jnp.bfloat16
    g, t, d = h.shape
    rw = RWKV_WIDTH
    nb = tm // 8
    cur = lambda w, c: pl.BlockSpec((1, tm, w), lambda b, i: (b, i, c))
    prev = lambda w, c: pl.BlockSpec((1, 8, w), lambda b, i: (b, jnp.maximum(i * nb - 1, 0), c))
    per_row = h0.shape[1] != 1
    carry = lambda w: pl.BlockSpec((1, tm if per_row else 1, w), (lambda b, i: (b, i, 0)) if per_row else (lambda b, i: (b, 0, 0)))
    full = lambda a: pl.BlockSpec(a.shape, lambda b, i: (0,) * a.ndim)
    c0 = RKV_OFF // rw
    ws = [lp['mu_rkv'].reshape(1, 3 * rw), lp['mu_wag'], lp['decay_w0'].reshape(1, rw), lp['decay_w1'].astype(bf16),
          lp['decay_w2'].astype(bf16), lp['iclr_a0'].reshape(1, rw), lp['iclr_a1'].astype(bf16),
          lp['iclr_a2'].astype(bf16), lp['gate_g1'].astype(bf16), lp['gate_g2'].astype(bf16),
          lp['k_k'].reshape(1, rw), lp['k_a'].reshape(1, rw), lp['r_k'].reshape(1, rw)]
    if pack_major:
        seq_shape = jax.ShapeDtypeStruct((g, RW_PACKS, t, WKV_LANES), f32)
        seq_spec = pl.BlockSpec((1, RW_PACKS, tm, WKV_LANES), lambda b, i: (b, 0, i, 0))
    else:
        seq_shape = jax.ShapeDtypeStruct((g, t, rw), f32)
        seq_spec = cur(rw, 0)
    flat_shape = jax.ShapeDtypeStruct((g, t, rw), f32)
    return pl.pallas_call(
        functools.partial(_rwkv_prep_kernel, period=period, pack_major=pack_major),
        grid=(g, t // tm),
        in_specs=[cur(d, 0), prev(d, 0), carry(d), cur(rw, c0), cur(rw, c0 + 1), cur(rw, c0 + 2),
                  prev(rw, c0), prev(rw, c0 + 1), prev(rw, c0 + 2), carry(3 * rw)] + [full(w) for w in ws],
        out_specs=[seq_spec] * 8 + [cur(rw, 0), cur(rw, 0)],
        out_shape=[seq_shape] * 8 + [flat_shape, flat_shape],
        compiler_params=pltpu.CompilerParams(
            dimension_semantics=("arbitrary", "arbitrary"), vmem_limit_bytes=VMEM_LIMIT_BYTES),
        name="rwkv_prep",
    )(h, h, h0, proj, proj, proj, proj, proj, proj, p0, *ws)


def _rwkv_post_kernel(y_ref, g_ref, bonus_ref, lnw_ref, lnb_ref, o_ref, *, pack_major):
    if pack_major:
        y = jnp.concatenate([y_ref[0, p] for p in range(RW_PACKS)], axis=1)
    else:
        y = y_ref[0]
    ones2 = _head_ones()
    inv = 1.0 / RWKV_HEAD_DIM
    mu = _head_sums(y, ones2) * inv
    dev = y - mu
    var = _head_sums(dev * dev, ones2) * inv
    yn = dev * lax.rsqrt(var + GN_EPS) * lnw_ref[...] + lnb_ref[...]
    o_ref[0] = (yn + bonus_ref[0]) * g_ref[0]


def _rwkv_post(y, g, bonus, ln_w, ln_b, tm, pack_major):
    gsz, t, rw = g.shape
    flat = pl.BlockSpec((1, tm, rw), lambda b, i: (b, i, 0))
    y_spec = pl.BlockSpec((1, RW_PACKS, tm, WKV_LANES), lambda b, i: (b, 0, i, 0)) if pack_major else flat
    vec = pl.BlockSpec((1, rw), lambda b, i: (0, 0))
    return pl.pallas_call(
        functools.partial(_rwkv_post_kernel, pack_major=pack_major),
        grid=(gsz, t // tm),
        in_specs=[y_spec, flat, flat, vec, vec],
        out_specs=flat,
        out_shape=jax.ShapeDtypeStruct((gsz, t, rw), jnp.float32),
        compiler_params=pltpu.CompilerParams(
            dimension_semantics=("arbitrary", "arbitrary"), vmem_limit_bytes=VMEM_LIMIT_BYTES),
        name="rwkv_post",
    )(y, g, bonus, ln_w.reshape(1, rw), ln_b.reshape(1, rw))


def _hier_route(logits):
    assert EXPERT_TOP_K == 2
    n = logits.shape[0]
    pg = jax.nn.softmax(logits[:, :N_GROUPS], axis=-1)

    def key(x):
        bits = lax.bitcast_convert_type(x, jnp.int32)
        return jnp.where(bits < 0, bits ^ jnp.int32(0x7FFFFFFF), bits)

    g_sel = jnp.argmax(key(pg), axis=-1)[:, None]
    g_val = jnp.take_along_axis(pg, g_sel, axis=1)
    le = logits[:, N_GROUPS:N_GROUPS + N_EXPERTS].reshape(n, N_GROUPS, EXPERTS_PER_GROUP)
    le_g = jnp.take_along_axis(le, g_sel[:, :, None], axis=1)[:, 0]
    le_key = key(le_g)
    e1 = jnp.argmax(le_key, axis=-1)[:, None]
    rest = jnp.where(jnp.arange(EXPERTS_PER_GROUP)[None, :] == e1, jnp.iinfo(jnp.int32).min, le_key)
    e2 = jnp.argmax(rest, axis=-1)[:, None]
    e_sel = jnp.concatenate([e1, e2], axis=1)
    e_val = jnp.take_along_axis(le_g, e_sel, axis=1)
    weights = jax.nn.softmax(e_val, axis=-1) * g_val
    return (g_sel * EXPERTS_PER_GROUP + e_sel).astype(jnp.int32), weights


def _moe_ffn(h_pad, eid, w_gate, w_up, w_down, tm):
    n, d = h_pad.shape[0] - 1, h_pad.shape[1]
    a_tot = n * EXPERT_TOP_K
    flat_e = eid.reshape(-1)
    onehot = (flat_e[:, None] == jnp.arange(N_EXPERTS)[None, :]).astype(jnp.int32)
    csum = jnp.cumsum(onehot, axis=0)
    rank = jnp.take_along_axis(csum, flat_e[:, None], axis=1)[:, 0] - 1
    counts = csum[-1]
    padded = (counts + tm - 1) // tm * tm
    pad_end = jnp.cumsum(padded)
    dest = (pad_end - padded)[flat_e] + rank
    n_blk = (a_tot + N_EXPERTS * (tm - 1)) // tm
    tok_buf = jnp.full((n_blk * tm,), n, jnp.int32).at[dest].set(jnp.arange(a_tot, dtype=jnp.int32) // EXPERT_TOP_K)
    xb = h_pad[tok_buf].reshape(n_blk, tm, d)
    blk_start = jnp.arange(n_blk, dtype=jnp.int32) * tm
    blk_e = jnp.minimum(jnp.sum((pad_end[None, :] <= blk_start[:, None]).astype(jnp.int32), axis=1), N_EXPERTS - 1)
    meta = jnp.concatenate([blk_e, pad_end[-1:] // tm]).astype(jnp.int32)
    yb = _moe_blocks(xb, meta, w_gate, w_up, w_down).reshape(n_blk * tm, d)
    dest2 = dest.reshape(n, EXPERT_TOP_K)
    return yb[dest2[:, 0]], yb[dest2[:, 1]]


def _layer_front(x, mod, lp, layer, pool_cmp, pool_slc, page_table, win_buf, wkv0, shift0, past_len, rows):
    B, T, D = x.shape
    groups = B * T // rows
    per_token = rows > T
    mods = jnp.repeat(mod, T, axis=0).reshape(groups, rows, 6 * D) if per_token else mod[:, None, :]
    sh1, sc1, gt1, sh2, sc2, gt2 = jnp.split(mods, 6, axis=-1)
    xg = x.reshape(groups, rows, D)
    tm_in = min(rows, 512)
    proj, h, kv_rows = _norm_in(xg, lp['norm1'], sc1, sh1, lp['w_in_b'], tm_in)
    kv_shape = (B, T, 2, NSA_KV_HEADS, HEAD_DIM)
    cmp_rows, slc_rows, win_rows = (kv_rows[n].reshape(B, T * KV_PARTS, HEAD_DIM) for n in range(KV_TILES))
    cmp_new, slc_new = cmp_rows.reshape(kv_shape), slc_rows.reshape(kv_shape)

    if pool_cmp is None:
        kvc = _compress_prompt(cmp_rows, _cmp_weights(lp))
        o_nsa = _nsa_prompt(proj.reshape(B * T, IN_WIDTH), kvc, B, T)
        win_len = min(WINDOW, past_len)
        assert T >= win_len
        win_state = win_rows[:, (T - win_len) * KV_PARTS:].reshape((B, win_len) + kv_shape[2:])
    else:
        n_past = page_table.shape[1] * PAGE_SIZE
        assert n_past % CMP_STRIDE == 0 and T < CMP_STRIDE and T <= 8
        assert -(-(n_past + T) // SEL_BLOCK) <= TOPK_LANES and n_past % SEL_BLOCK == 0 and T <= SEL_BLOCK
        assert win_buf.shape[1] == WINDOW
        pages = page_table + layer * pool_cmp.shape[1]
        kvc = _compress_paged(pool_cmp.reshape(-1, HEAD_DIM), pages, _cmp_weights(lp))
        win_buf_rows = win_buf.reshape(B, WINDOW * KV_PARTS, HEAD_DIM)
        o_nsa = _nsa_decode_rows(proj.reshape(B, T, IN_WIDTH), kvc, win_buf_rows, pool_slc.reshape(-1, HEAD_DIM),
                                 pages, n_past)
        win_state = jnp.concatenate([win_buf_rows[:, T * KV_PARTS:], win_rows], axis=1).reshape(
            (B, WINDOW) + kv_shape[2:])

    shift0 = shift0.astype(h.dtype)
    p0 = _matmul(shift0, lp['w_in_b'][:, RKV_OFF:CMP_OFF], B, RWKV_WIDTH)
    if per_token:
        h0 = jnp.repeat(shift0, T, axis=0).reshape(groups, rows, D)
        p0 = jnp.repeat(p0, T, axis=0).reshape(groups, rows, 3 * RWKV_WIDTH)
    else:
        h0, p0 = shift0[:, None], p0[:, None]
    tm_rw = min(rows, 256)
    seqs = _rwkv_prep(h, proj, h0, p0, lp, tm_rw, T, pack_major=not per_token)
    seqs, (gate, bonus) = seqs[:8], seqs[8:]
    if per_token:
        t_pad = -(-T // WKV_MIN_CHUNK) * WKV_MIN_CHUNK

        def pairs(a, fill):
            a = jnp.pad(a.reshape(B, T, RW_PACKS, WKV_LANES), ((0, 0), (0, t_pad - T), (0, 0), (0, 0)),
                        constant_values=fill)
            return a.transpose(0, 2, 1, 3)

        fills = (0.0, 1.0) + (0.0,) * 6
        y, wkv_T = _wkv_scan(*[pairs(a, f) for a, f in zip(seqs, fills)], wkv0.astype(jnp.float32), WKV_MIN_CHUNK)
        y = y.transpose(0, 2, 1, 3)[:, :T].reshape(groups, rows, RWKV_WIDTH)
    else:
        y, wkv_T = _wkv_scan(*seqs, wkv0.astype(jnp.float32), WKV_CHUNK)
    o_rwkv = _rwkv_post(y, gate, bonus, lp['ln_x_w'], lp['ln_x_b'], tm_rw, pack_major=not per_token)

    x1, h2, logits = _mix_out(o_nsa.reshape(groups, rows, NSA_WIDTH), o_rwkv, lp['w_out_b'], xg, gt1, sc2, sh2,
                              lp['norm2'], lp['wr_hi'], lp['wr_lo'], lp['br'], min(rows, 256))
    return (x1, h2, logits, gt2), (cmp_new, slc_new, win_state, wkv_T, h.reshape(B, T, D)[:, -1])


def kernel(x_prompt, x_sample, c_prompt, c_sample, cache_cmp_kv, cache_slc_kv, page_table, state_win_kv, state_wkv, state_shift, w_ada, b_ada, norm1, w_in, cmp_k_w1, cmp_k_pe, cmp_k_w2, cmp_v_w1, cmp_v_pe, cmp_v_w2, mu_rkv, mu_wag, decay_w0, decay_w1, decay_w2, iclr_a0, iclr_a1, iclr_a2, gate_g1, gate_g2, k_k, k_a, r_k, ln_x_w, ln_x_b, w_out, norm2, w_router_group, b_router_group, w_router_expert, b_router_expert, w_gate, w_up, w_down, norm_f):
    bp, tp = x_prompt.shape[:2]
    ts = x_sample.shape[1]
    past_len = page_table.shape[1] * PAGE_SIZE
    assert DEPTH == 1 and w_in.shape[0] == 1
    l = 0
    bs = x_sample.shape[0]
    f32, bf16 = jnp.float32, jnp.bfloat16
    lyr = lambda a: a.reshape(a.shape[1:])
    wr = jnp.concatenate([lyr(w_router_group), lyr(w_router_expert),
                          jnp.zeros((D_MODEL, ROUTER_LANES - N_GROUPS - N_EXPERTS), f32)], axis=1)
    wr_hi = wr.astype(bf16)
    br = jnp.concatenate([lyr(b_router_group), lyr(b_router_expert),
                          jnp.zeros((ROUTER_LANES - N_GROUPS - N_EXPERTS,), f32)]).reshape(1, ROUTER_LANES)
    lp = dict(norm1=lyr(norm1), w_in_b=_permute_w_in(lyr(w_in)).astype(bf16),
              cmp_k_w1=lyr(cmp_k_w1), cmp_k_pe=lyr(cmp_k_pe), cmp_k_w2=lyr(cmp_k_w2),
              cmp_v_w1=lyr(cmp_v_w1), cmp_v_pe=lyr(cmp_v_pe), cmp_v_w2=lyr(cmp_v_w2),
              mu_rkv=lyr(mu_rkv), mu_wag=lyr(mu_wag), decay_w0=lyr(decay_w0), decay_w1=lyr(decay_w1),
              decay_w2=lyr(decay_w2), iclr_a0=lyr(iclr_a0), iclr_a1=lyr(iclr_a1), iclr_a2=lyr(iclr_a2),
              gate_g1=lyr(gate_g1), gate_g2=lyr(gate_g2), k_k=lyr(k_k), k_a=lyr(k_a), r_k=lyr(r_k),
              ln_x_w=lyr(ln_x_w), ln_x_b=lyr(ln_x_b), w_out_b=lyr(w_out).astype(bf16), norm2=lyr(norm2),
              wr_hi=wr_hi, wr_lo=(wr - wr_hi.astype(f32)).astype(bf16), br=br)

    c_all = jnp.concatenate([c_prompt, c_sample], axis=0)
    mod_all = _matmul(jax.nn.silu(c_all), lyr(w_ada), c_all.shape[0], 1024) + lyr(b_ada)
    mod_p, mod_s = mod_all[:bp], mod_all[bp:]

    def experts(h2, logits, tm):
        n = h2.shape[0] * h2.shape[1]
        eid, ew = _hier_route(logits.reshape(n, ROUTER_LANES))
        h_pad = jnp.concatenate([h2.reshape(n, D_MODEL), jnp.zeros((1, D_MODEL), bf16)], axis=0)
        return _moe_ffn(h_pad, eid, lyr(w_gate), lyr(w_up), lyr(w_down), tm) + (ew,)

    wkv_zero = jnp.zeros((bp, RWKV_HEADS, RWKV_HEAD_DIM, RWKV_HEAD_DIM), f32)
    shift_zero = jnp.zeros((bp, D_MODEL), x_prompt.dtype)
    (x1p, h2p, lgp, gt2p), (a1, a2, a3, a4, a5) = _layer_front(
        x_prompt, mod_p, lp, l, None, None, None, None, wkv_zero, shift_zero, past_len, tp)
    y0p, y1p, ewp = experts(h2p, lgp, MOE_TM)
    (x1s, h2s, lgs, gt2s), (b1, b2, b3, b4, b5) = _layer_front(
        x_sample, mod_s, lp, l, cache_cmp_kv, cache_slc_kv, page_table, lyr(state_win_kv), lyr(state_wkv),
        lyr(state_shift), past_len, bs * ts)
    y0s, y1s, ews = experts(h2s, lgs, MOE_TM_DECODE)
    y_prompt = _final(x1p, y0p, y1p, ewp, 0, gt2p, norm_f, 256).reshape(x_prompt.shape)
    y_sample = _final(x1s, y0s, y1s, ews, 0, gt2s, norm_f, x1s.shape[1]).reshape(x_sample.shape)
    st = lambda a: a[None]
    return (y_prompt, y_sample, st(a1), st(b1), st(a2), st(b2), st(a3), st(b3), st(a4), st(b4), st(a5), st(b5))
```

```python
import functools

import jax
import jax.numpy as jnp
from jax import lax
from jax.experimental import pallas as pl
from jax.experimental.pallas import tpu as pltpu

D_MODEL = 2048
DEPTH = 1
PAGE_SIZE = 128
HEAD_DIM = 128
NSA_WIDTH = D_MODEL // 2
NSA_HEADS = NSA_WIDTH // HEAD_DIM
NSA_KV_HEADS = 2
HPG = NSA_HEADS // NSA_KV_HEADS
KV_WIDTH = NSA_KV_HEADS * HEAD_DIM
CMP_BLOCK = 32
CMP_STRIDE = 16
SEL_BLOCK = 64
N_SELECT = 16
WINDOW = 512
FORCE_BONUS = 1e4
RWKV_WIDTH = D_MODEL - NSA_WIDTH
RWKV_HEAD_DIM = 64
RWKV_HEADS = RWKV_WIDTH // RWKV_HEAD_DIM
GN_EPS = 64e-5
N_GROUPS = 4
EXPERTS_PER_GROUP = 8
N_EXPERTS = N_GROUPS * EXPERTS_PER_GROUP
EXPERT_TOP_K = 2
NORM_EPS = 1e-6
NEG_INF = -1e30
W_Q_OFF = 0
W_CMP_OFF = W_Q_OFF + NSA_WIDTH
W_RKV_OFF = W_CMP_OFF + 6 * KV_WIDTH
W_GATE_OFF = W_RKV_OFF + 3 * RWKV_WIDTH
IN_WIDTH = W_GATE_OFF + 3 * NSA_HEADS
Q_OFF = 0
RKV_OFF = Q_OFF + NSA_WIDTH
CMP_OFF = RKV_OFF + 3 * RWKV_WIDTH
SLC_OFF = CMP_OFF + 2 * KV_WIDTH
WIN_OFF = SLC_OFF + 2 * KV_WIDTH
GATE_OFF = WIN_OFF + 2 * KV_WIDTH
assert GATE_OFF == W_GATE_OFF


def _permute_w_in(w):
    return jnp.concatenate([w[:, W_Q_OFF:W_CMP_OFF], w[:, W_RKV_OFF:W_GATE_OFF], w[:, W_CMP_OFF:W_RKV_OFF],
                            w[:, W_GATE_OFF:]], axis=1)

VMEM_LIMIT_BYTES = 48 * 1024 * 1024


def _mm_kernel(x_ref, w_ref, o_ref):
    o_ref[...] = jnp.dot(x_ref[...].astype(jnp.bfloat16), w_ref[...].astype(jnp.bfloat16),
                         preferred_element_type=jnp.float32)


def _matmul(x, w, tm, tn):
    m, k = x.shape
    n = w.shape[1]
    return pl.pallas_call(
        _mm_kernel,
        grid=(pl.cdiv(m, tm), pl.cdiv(n, tn)),
        in_specs=[pl.BlockSpec((tm, k), lambda i, j: (i, 0)),
                  pl.BlockSpec((k, tn), lambda i, j: (0, j))],
        out_specs=pl.BlockSpec((tm, tn), lambda i, j: (i, j)),
        out_shape=jax.ShapeDtypeStruct((m, n), jnp.float32),
        compiler_params=pltpu.CompilerParams(
            dimension_semantics=("arbitrary", "arbitrary"), vmem_limit_bytes=VMEM_LIMIT_BYTES),
        name="matmul",
    )(x, w)


MOE_TM = 256
MOE_TM_DECODE = 128


def _moe_block_kernel(meta_ref, x_ref, wg_ref, wu_ref, wd_ref, o_ref, wg_b, wu_b, wd_b):
    bf16 = jnp.bfloat16
    i = pl.program_id(0)
    used = i < meta_ref[pl.num_programs(0)]
    new_expert = (i == 0) | (meta_ref[i] != meta_ref[jnp.maximum(i - 1, 0)])

    @pl.when(used & new_expert)
    def _():
        wg_b[...] = wg_ref[0].astype(bf16)
        wu_b[...] = wu_ref[0].astype(bf16)
        wd_b[...] = wd_ref[0].astype(bf16)

    @pl.when(used)
    def _():
        x = x_ref[0]
        g = jnp.dot(x, wg_b[...], preferred_element_type=jnp.float32)
        u = jnp.dot(x, wu_b[...], preferred_element_type=jnp.float32)
        hmid = (g * jax.nn.sigmoid(g)) * u
        o_ref[0] = jnp.dot(hmid.astype(bf16), wd_b[...], preferred_element_type=jnp.float32)

    @pl.when(jnp.logical_not(used))
    def _():
        o_ref[...] = jnp.zeros(o_ref.shape, o_ref.dtype)


def _moe_blocks(xb, meta, w_gate, w_up, w_down):
    n_blk, mb, d = xb.shape
    de = w_gate.shape[2]
    grid_spec = pltpu.PrefetchScalarGridSpec(
        num_scalar_prefetch=1,
        grid=(n_blk,),
        in_specs=[pl.BlockSpec((1, mb, d), lambda i, e: (i, 0, 0)),
                  pl.BlockSpec((1, d, de), lambda i, e: (e[i], 0, 0)),
                  pl.BlockSpec((1, d, de), lambda i, e: (e[i], 0, 0)),
                  pl.BlockSpec((1, de, d), lambda i, e: (e[i], 0, 0))],
        out_specs=pl.BlockSpec((1, mb, d), lambda i, e: (i, 0, 0)),
        scratch_shapes=[pltpu.VMEM((d, de), jnp.bfloat16), pltpu.VMEM((d, de), jnp.bfloat16),
                        pltpu.VMEM((de, d), jnp.bfloat16)],
    )
    return pl.pallas_call(
        _moe_block_kernel,
        grid_spec=grid_spec,
        out_shape=jax.ShapeDtypeStruct((n_blk, mb, d), jnp.float32),
        compiler_params=pltpu.CompilerParams(
            dimension_semantics=("arbitrary",), vmem_limit_bytes=VMEM_LIMIT_BYTES),
        name="moe_blocks",
    )(meta, xb, w_gate, w_up, w_down)


ROUTER_LANES = 128


KV_TILES = 3
KV_PARTS = 2 * NSA_KV_HEADS


def _norm_in_kernel(x_ref, n1_ref, sc_ref, sh_ref, w_ref, proj_ref, h_ref, cmp_ref, slc_ref, win_ref, hb_scr,
                    *, kv_tile0):
    j = pl.program_id(2)

    @pl.when(j == 0)
    def _():
        x = x_ref[0]
        y = x * lax.rsqrt(jnp.mean(x * x, axis=-1, keepdims=True) + NORM_EPS) * n1_ref[...]
        h = y * (1.0 + sc_ref[0]) + sh_ref[0]
        h_ref[0] = h
        hb_scr[...] = h.astype(jnp.bfloat16)

    proj_ref[0] = jnp.dot(hb_scr[...], w_ref[...], preferred_element_type=jnp.float32)

    for n, kv_ref in enumerate((cmp_ref, slc_ref, win_ref)):
        @pl.when(j == kv_tile0 + n)
        def _(kv_ref=kv_ref):
            tm = proj_ref.shape[1]
            for part in range(KV_PARTS):
                kv_ref[0, pl.ds(part, tm, stride=KV_PARTS), :] = proj_ref[0, :, part * HEAD_DIM:(part + 1) * HEAD_DIM]


def _norm_in(x, norm1, sc, sh, w_in_b, tm):
    g, t, d = x.shape
    n = w_in_b.shape[1]
    tn = 2 * KV_WIDTH
    kv_tile0 = CMP_OFF // tn
    assert CMP_OFF % tn == 0 and GATE_OFF == CMP_OFF + KV_TILES * tn
    mrows = sc.shape[1]
    mod_spec = pl.BlockSpec((1, mrows if mrows == 1 else tm, d),
                            (lambda b, i, j: (b, 0, 0)) if mrows == 1 else (lambda b, i, j: (b, i, 0)))
    return pl.pallas_call(
        functools.partial(_norm_in_kernel, kv_tile0=kv_tile0),
        grid=(g, t // tm, pl.cdiv(n, tn)),
        in_specs=[pl.BlockSpec((1, tm, d), lambda b, i, j: (b, i, 0)),
                  pl.BlockSpec((1, d), lambda b, i, j: (0, 0)),
                  mod_spec, mod_spec,
                  pl.BlockSpec((d, tn), lambda b, i, j: (0, j))],
        out_specs=[pl.BlockSpec((1, tm, tn), lambda b, i, j: (b, i, j)),
                   pl.BlockSpec((1, tm, d), lambda b, i, j: (b, i, 0))]
        + [pl.BlockSpec((1, tm * KV_PARTS, HEAD_DIM), lambda b, i, j: (b, i, 0))] * KV_TILES,
        out_shape=[jax.ShapeDtypeStruct((g, t, n), jnp.float32), jax.ShapeDtypeStruct((g, t, d), jnp.float32)]
        + [jax.ShapeDtypeStruct((g, t * KV_PARTS, HEAD_DIM), jnp.float32)] * KV_TILES,
        scratch_shapes=[pltpu.VMEM((tm, d), jnp.bfloat16)],
        compiler_params=pltpu.CompilerParams(
            dimension_semantics=("arbitrary", "arbitrary", "arbitrary"), vmem_limit_bytes=VMEM_LIMIT_BYTES),
        name="norm_in",
    )(x, norm1.reshape(1, d), sc, sh, w_in_b)


def _mix_out_kernel(on_ref, orw_ref, w_ref, x_ref, gt_ref, sc_ref, sh_ref, n2_ref, wr_hi_ref, wr_lo_ref, br_ref,
                    x1_ref, h2_ref, lg_ref):
    f32, bf16 = jnp.float32, jnp.bfloat16
    half = on_ref.shape[2]
    mixed = (jnp.dot(on_ref[0].astype(bf16), w_ref[0:half, :], preferred_element_type=f32)
             + jnp.dot(orw_ref[0].astype(bf16), w_ref[half:, :], preferred_element_type=f32))
    x1 = x_ref[0] + gt_ref[0] * mixed
    x1_ref[0] = x1
    y = x1 * lax.rsqrt(jnp.mean(x1 * x1, axis=-1, keepdims=True) + NORM_EPS) * n2_ref[...]
    h2 = y * (1.0 + sc_ref[0]) + sh_ref[0]
    hi = h2.astype(bf16)
    h2_ref[0] = hi
    lo = (h2 - hi.astype(f32)).astype(bf16)
    lg_ref[0] = (jnp.dot(hi, wr_hi_ref[...], preferred_element_type=f32)
                 + jnp.dot(hi, wr_lo_ref[...], preferred_element_type=f32)
                 + jnp.dot(lo, wr_hi_ref[...], preferred_element_type=f32) + br_ref[...])


def _mix_out(o_nsa, o_rwkv, w_out_b, x, gt, sc, sh, norm2, wr_hi, wr_lo, br, tm):
    g, t, d = x.shape
    half = o_nsa.shape[2]
    mrows = sc.shape[1]
    mod_spec = pl.BlockSpec((1, mrows if mrows == 1 else tm, d),
                            (lambda b, i: (b, 0, 0)) if mrows == 1 else (lambda b, i: (b, i, 0)))
    row = lambda w: pl.BlockSpec((1, tm, w), lambda b, i: (b, i, 0))
    full = lambda a: pl.BlockSpec(a.shape, lambda b, i: (0,) * a.ndim)
    n2 = norm2.reshape(1, d)
    return pl.pallas_call(
        _mix_out_kernel,
        grid=(g, t // tm),
        in_specs=[row(half), row(half), full(w_out_b), row(d), mod_spec, mod_spec, mod_spec, full(n2),
                  full(wr_hi), full(wr_lo), full(br)],
        out_specs=[row(d), row(d), row(ROUTER_LANES)],
        out_shape=[jax.ShapeDtypeStruct((g, t, d), jnp.float32), jax.ShapeDtypeStruct((g, t, d), jnp.bfloat16),
                   jax.ShapeDtypeStruct((g, t, ROUTER_LANES), jnp.float32)],
        compiler_params=pltpu.CompilerParams(
            dimension_semantics=("arbitrary", "arbitrary"), vmem_limit_bytes=VMEM_LIMIT_BYTES),
        name="mix_out",
    )(o_nsa, o_rwkv, w_out_b, x, gt, sc, sh, n2, wr_hi, wr_lo, br)


def _final_kernel(x_ref, y0_ref, y1_ref, ew_ref, gt_ref, nf_ref, o_ref):
    ew = ew_ref[...]
    ffn = y0_ref[...] * ew[:, 0:1] + y1_ref[...] * ew[:, 1:2]
    x2 = x_ref[0] + gt_ref[0] * ffn
    o_ref[0] = x2 * lax.rsqrt(jnp.mean(x2 * x2, axis=-1, keepdims=True) + NORM_EPS) * nf_ref[...]


def _final(x1, y0, y1, ew, row_off, gt, norm_f, tm):
    g, t, d = x1.shape
    assert row_off % tm == 0
    mrows = gt.shape[1]
    mod_spec = pl.BlockSpec((1, mrows if mrows == 1 else tm, d),
                            (lambda b, i: (b, 0, 0)) if mrows == 1 else (lambda b, i: (b, i, 0)))
    row = pl.BlockSpec((1, tm, d), lambda b, i: (b, i, 0))
    flat = lambda w: pl.BlockSpec((tm, w), lambda b, i: (row_off // tm + b * (t // tm) + i, 0))
    return pl.pallas_call(
        _final_kernel,
        grid=(g, t // tm),
        in_specs=[row, flat(d), flat(d), flat(EXPERT_TOP_K), mod_spec, pl.BlockSpec((1, d), lambda b, i: (0, 0))],
        out_specs=row,
        out_shape=jax.ShapeDtypeStruct((g, t, d), jnp.float32),
        compiler_params=pltpu.CompilerParams(
            dimension_semantics=("arbitrary", "arbitrary"), vmem_limit_bytes=VMEM_LIMIT_BYTES),
        name="final_norm",
    )(x1, y0, y1, ew, gt, norm_f.reshape(1, d))


def _cmp_partial_kernel(*refs, n_src, rows_per_src):
    x_refs = refs[:n_src]
    w1k_ref, w1v_ref, a_ref, b_ref = refs[n_src:]
    nch_src = rows_per_src // CMP_STRIDE
    for kvg in range(4):
        w_ref = w1k_ref if kvg < 2 else w1v_ref
        acc = None
        for p in range(CMP_STRIDE):
            parts = [x_refs[s][pl.ds(4 * p + kvg, nch_src, stride=4 * CMP_STRIDE), :] for s in range(n_src)]
            xp = parts[0] if n_src == 1 else jnp.concatenate(parts, axis=0)
            d = jnp.dot(xp.astype(jnp.bfloat16), w_ref[p], preferred_element_type=jnp.float32)
            acc = d if acc is None else acc + d
        a_ref[0, :, kvg * HEAD_DIM:(kvg + 1) * HEAD_DIM] = acc[:, :HEAD_DIM]
        b_ref[0, :, kvg * HEAD_DIM:(kvg + 1) * HEAD_DIM] = acc[:, HEAD_DIM:]


def _cmp_finish_kernel(a_ref, b_ref, pek_ref, pev_ref, w1k_ref, w1v_ref, w2k_ref, w2v_ref, o_ref):
    nch = a_ref.shape[1]
    for kv, (pe_ref, w1_ref, w2_ref) in enumerate(((pek_ref, w1k_ref, w2k_ref), (pev_ref, w1v_ref, w2v_ref))):
        pe8 = jnp.broadcast_to(pe_ref[...], (8, pe_ref.shape[1])).astype(jnp.bfloat16)
        pterm = jnp.dot(pe8, w1_ref[...], preferred_element_type=jnp.float32)[0:1]
        w2 = w2_ref[...]
        for g in range(NSA_KV_HEADS):
            lo = (kv * NSA_KV_HEADS + g) * HEAD_DIM
            nxt = pltpu.roll(b_ref[0, :, lo:lo + HEAD_DIM], nch - 1, 0)
            pre = a_ref[0, :, lo:lo + HEAD_DIM] + nxt + pterm
            act = pre * jax.nn.sigmoid(pre)
            o_ref[0, :, lo:lo + HEAD_DIM] = jnp.dot(act.astype(jnp.bfloat16), w2, preferred_element_type=jnp.float32)


def _cmp_weights(lp):
    bf = jnp.bfloat16
    half = CMP_BLOCK // 2
    cat = lambda w: jnp.concatenate([w[:half], w[half:]], axis=-1).astype(bf)
    flat = lambda w: w.reshape(CMP_BLOCK * HEAD_DIM, HEAD_DIM).astype(bf)
    return dict(w1k_cat=cat(lp['cmp_k_w1']), w1v_cat=cat(lp['cmp_v_w1']),
                w1k_flat=flat(lp['cmp_k_w1']), w1v_flat=flat(lp['cmp_v_w1']),
                pek=lp['cmp_k_pe'].reshape(1, -1), pev=lp['cmp_v_pe'].reshape(1, -1),
                w2k=lp['cmp_k_w2'].astype(bf), w2v=lp['cmp_v_w2'].astype(bf))


def _cmp_finish(a, b, cw):
    bsz, nch, _ = a.shape
    full = lambda arr: pl.BlockSpec(arr.shape, lambda i: (0,) * arr.ndim)
    blk = pl.BlockSpec((1, nch, 4 * HEAD_DIM), lambda i: (i, 0, 0))
    ws = [cw['pek'], cw['pev'], cw['w1k_flat'], cw['w1v_flat'], cw['w2k'], cw['w2v']]
    return pl.pallas_call(
        _cmp_finish_kernel,
        grid=(bsz,),
        in_specs=[blk, blk] + [full(w) for w in ws],
        out_specs=blk,
        out_shape=jax.ShapeDtypeStruct((bsz, nch, 4 * HEAD_DIM), jnp.float32),
        compiler_params=pltpu.CompilerParams(dimension_semantics=("arbitrary",), vmem_limit_bytes=VMEM_LIMIT_BYTES),
        name="cmp_finish",
    )(a, b, *ws)


def _compress_prompt(rows4, cw):
    bsz, seq = rows4.shape[0], rows4.shape[1] // 4
    nch = seq // CMP_STRIDE
    x_specs = [pl.BlockSpec((None, seq * 4, HEAD_DIM), lambda i: (i, 0, 0))]
    w_spec = pl.BlockSpec(cw['w1k_cat'].shape, lambda i: (0, 0, 0))
    out_spec = pl.BlockSpec((1, nch, 4 * HEAD_DIM), lambda i: (i, 0, 0))
    shp = jax.ShapeDtypeStruct((bsz, nch, 4 * HEAD_DIM), jnp.float32)
    a, b = pl.pallas_call(
        functools.partial(_cmp_partial_kernel, n_src=1, rows_per_src=seq),
        grid=(bsz,),
        in_specs=x_specs + [w_spec, w_spec],
        out_specs=[out_spec, out_spec],
        out_shape=[shp, shp],
        compiler_params=pltpu.CompilerParams(dimension_semantics=("arbitrary",), vmem_limit_bytes=VMEM_LIMIT_BYTES),
        name="cmp_partial_prompt",
    )(rows4, cw['w1k_cat'], cw['w1v_cat'])
    return _cmp_finish(a, b, cw)


_NT = (((1,), (1,)), ((), ()))
LOG2E = 1.4426950408889634
SEL_TK = 512
WIN_TK = 256


def _flash_update(s, v, m_ref, l_ref, acc_ref, h):
    tk = s.shape[1]
    m_prev = m_ref[h]
    m_new = jnp.maximum(m_prev, jnp.max(s, axis=-1, keepdims=True))
    alpha = jnp.exp2(m_prev - m_new)
    p = jnp.exp2(s - jnp.concatenate([m_new] * (tk // HEAD_DIM), axis=1))
    l_ref[h] = alpha * l_ref[h] + jnp.sum(p, axis=-1, keepdims=True)
    acc_ref[h] = alpha * acc_ref[h] + jnp.dot(p.astype(jnp.bfloat16), v, preferred_element_type=jnp.float32)
    m_ref[h] = m_new


def _nsa_prompt_kernel(q_ref, slc_ref, win_ref, gate_ref, kvc_ref, o_ref, m_ref, l_ref, acc_ref, *, tq, seq):
    f32, bf16 = jnp.float32, jnp.bfloat16
    qi = pl.program_id(1)
    t0 = qi * tq
    scale = HEAD_DIM ** -0.5
    nc_valid = seq // CMP_STRIDE - CMP_BLOCK // CMP_STRIDE + 1
    n_sel = seq // SEL_BLOCK
    pos = t0 + lax.broadcasted_iota(jnp.int32, (tq, 1), 0)
    lane = lax.broadcasted_iota(jnp.int32, (1, HEAD_DIM), 1)
    gates = jax.nn.sigmoid(gate_ref[...])
    dist_c = pos - (lane * CMP_STRIDE + (CMP_BLOCK - 1))
    valid_c = (dist_c >= 0) & (lane < nc_valid)
    dist_cf = dist_c.astype(f32)
    c_row = lax.broadcasted_iota(jnp.int32, (HEAD_DIM, 1), 0)
    overlap = jnp.where((c_row * CMP_STRIDE <= lane * SEL_BLOCK + (SEL_BLOCK - 1))
                        & (c_row * CMP_STRIDE + (CMP_BLOCK - 1) >= lane * SEL_BLOCK), 1.0, 0.0).astype(bf16)
    j_row = lax.broadcasted_iota(jnp.int32, (n_sel, 1), 0)
    blk_lane = jnp.right_shift(t0 + lax.broadcasted_iota(jnp.int32, (1, tq), 1), 6)
    forced_t = (j_row == 0) | (j_row == blk_lane) | (j_row == blk_lane - 1)

    def gate_col(branch, hh):
        c = branch * NSA_HEADS + hh
        return gates[:, c:c + 1]

    def reset():
        m_ref[...] = jnp.full(m_ref.shape, NEG_INF, f32)
        l_ref[...] = jnp.zeros(l_ref.shape, f32)
        acc_ref[...] = jnp.zeros(acc_ref.shape, f32)

    for g in range(NSA_KV_HEADS):
        kcol = slice(g * HEAD_DIM, (g + 1) * HEAD_DIM)
        vcol = slice((NSA_KV_HEADS + g) * HEAD_DIM, (NSA_KV_HEADS + g + 1) * HEAD_DIM)
        heads = [g * HPG + h for h in range(HPG)]
        slopes = [2.0 ** -(hh + 1) for hh in heads]

        kc = kvc_ref[0, :, kcol].astype(bf16)
        vc = kvc_ref[0, :, vcol].astype(bf16)
        psum = jnp.zeros((tq, HEAD_DIM), f32)
        for h, hh in enumerate(heads):
            qh = q_ref[:, hh * HEAD_DIM:(hh + 1) * HEAD_DIM].astype(bf16)
            s = lax.dot_general(qh, kc, _NT, preferred_element_type=f32) * scale - slopes[h] * dist_cf
            s = jnp.where(valid_c, s, NEG_INF)
            e = jnp.exp(s - jnp.max(s, axis=-1, keepdims=True))
            p = e / jnp.sum(e, axis=-1, keepdims=True)
            p = jnp.where(valid_c, p, 0.0)
            o_cmp = jnp.dot(p.astype(bf16), vc, preferred_element_type=f32)
            o_ref[:, hh * HEAD_DIM:(hh + 1) * HEAD_DIM] = gate_col(0, hh) * o_cmp
            psum = psum + p
        p_hi = psum.astype(bf16)
        p_lo = (psum - p_hi.astype(f32)).astype(bf16)
        imp = (jnp.dot(p_hi, overlap, preferred_element_type=f32)
               + jnp.dot(p_lo, overlap, preferred_element_type=f32))
        imp = imp.T[:n_sel]
        imp = jnp.where(forced_t, imp + FORCE_BONUS, imp)
        imp = jnp.where(j_row <= blk_lane, imp, NEG_INF)
        beaten = jnp.zeros((n_sel, tq), f32)
        for jp in range(n_sel):
            other = imp[jp:jp + 1]
            tie = jnp.where(j_row > jp, 1.0, 0.0)
            beaten = beaten + jnp.where(other > imp, 1.0, jnp.where(other == imp, tie, 0.0))
        sel_t = jnp.where(beaten < N_SELECT, jnp.where(imp > 0.5 * NEG_INF, 1.0, 0.0), 0.0).astype(bf16)

        reset()

        def sel_body(kt, carry):
            k0 = pl.multiple_of(kt * SEL_TK, SEL_TK)
            k = slc_ref[pl.ds(k0, SEL_TK), kcol].astype(bf16)
            v = slc_ref[pl.ds(k0, SEL_TK), vcol].astype(bf16)
            kpos = k0 + lax.broadcasted_iota(jnp.int32, (1, SEL_TK), 1)
            dist = pos - kpos
            expand = jnp.where(jnp.right_shift(kpos, 6) == j_row, 1.0, 0.0).astype(bf16)
            picked = lax.dot_general(sel_t, expand, (((0,), (0,)), ((), ())), preferred_element_type=f32)
            keep = jnp.where(dist >= 0, picked, 0.0) > 0.5
            kpos_f = (pos[0:1] - dist[0:1]).astype(f32)
            for h, hh in enumerate(heads):
                qh = q_ref[:, hh * HEAD_DIM:(hh + 1) * HEAD_DIM].astype(bf16)
                s = (lax.dot_general(qh, k, _NT, preferred_element_type=f32) * (scale * LOG2E)
                     + (slopes[h] * LOG2E) * kpos_f)
                _flash_update(jnp.where(keep, s, NEG_INF), v, m_ref, l_ref, acc_ref, h)
            return carry

        lax.fori_loop(0, (t0 + tq - 1) // SEL_TK + 1, sel_body, 0)
        for h, hh in enumerate(heads):
            hs = slice(hh * HEAD_DIM, (hh + 1) * HEAD_DIM)
            o_ref[:, hs] = o_ref[:, hs] + gate_col(1, hh) * (acc_ref[h] / l_ref[h])

        reset()

        def win_body(kt, carry):
            k0 = pl.multiple_of(kt * WIN_TK, WIN_TK)
            k = win_ref[pl.ds(k0, WIN_TK), kcol].astype(bf16)
            v = win_ref[pl.ds(k0, WIN_TK), vcol].astype(bf16)
            dist = pos - (k0 + lax.broadcasted_iota(jnp.int32, (1, WIN_TK), 1))
            keep = (dist >= 0) & (dist < WINDOW)
            kpos_f = (pos[0:1] - dist[0:1]).astype(f32)
            for h, hh in enumerate(heads):
                qh = q_ref[:, hh * HEAD_DIM:(hh + 1) * HEAD_DIM].astype(bf16)
                s = (lax.dot_general(qh, k, _NT, preferred_element_type=f32) * (scale * LOG2E)
                     + (slopes[h] * LOG2E) * kpos_f)
                _flash_update(jnp.where(keep, s, NEG_INF), v, m_ref, l_ref, acc_ref, h)
            return carry

        lax.fori_loop(jnp.maximum(t0 - (WINDOW - 1), 0) // WIN_TK, (t0 + tq - 1) // WIN_TK + 1, win_body, 0)
        for h, hh in enumerate(heads):
            hs = slice(hh * HEAD_DIM, (hh + 1) * HEAD_DIM)
            o_ref[:, hs] = o_ref[:, hs] + gate_col(2, hh) * (acc_ref[h] / l_ref[h])


def _nsa_prompt(proj2d, kvc, bsz, seq, tq=256):
    nq = seq // tq
    kvw = 2 * KV_WIDTH
    return pl.pallas_call(
        functools.partial(_nsa_prompt_kernel, tq=tq, seq=seq),
        grid=(bsz, nq),
        in_specs=[pl.BlockSpec((tq, NSA_WIDTH), lambda b, i: (b * nq + i, 0)),
                  pl.BlockSpec((seq, kvw), lambda b, i: (b, SLC_OFF // kvw)),
                  pl.BlockSpec((seq, kvw), lambda b, i: (b, WIN_OFF // kvw)),
                  pl.BlockSpec((tq, HEAD_DIM), lambda b, i: (b * nq + i, GATE_OFF // HEAD_DIM)),
                  pl.BlockSpec((1, seq // CMP_STRIDE, kvw), lambda b, i: (b, 0, 0))],
        out_specs=pl.BlockSpec((tq, NSA_WIDTH), lambda b, i: (b * nq + i, 0)),
        out_shape=jax.ShapeDtypeStruct((bsz * seq, NSA_WIDTH), jnp.float32),
        scratch_shapes=[pltpu.VMEM((HPG, tq, HEAD_DIM), jnp.float32),
                        pltpu.VMEM((HPG, tq, HEAD_DIM), jnp.float32),
                        pltpu.VMEM((HPG, tq, HEAD_DIM), jnp.float32)],
        compiler_params=pltpu.CompilerParams(
            dimension_semantics=("arbitrary", "arbitrary"), vmem_limit_bytes=VMEM_LIMIT_BYTES),
        name="nsa_prompt",
    )(proj2d, proj2d, proj2d, proj2d, kvc)


CMP_PAGES_PER_STEP = 32
TOPK_LANES = 384
IDX_LANES = 128


def _cmp_partial_paged_kernel(pt_ref, *refs, n_src, rows_per_src):
    del pt_ref
    _cmp_partial_kernel(*refs, n_src=n_src, rows_per_src=rows_per_src)


def _compress_paged(pool, page_table, cw):
    bsz, n_pages = page_table.shape
    nps = CMP_PAGES_PER_STEP
    n_tiles = n_pages // nps
    nch_tile = nps * PAGE_SIZE // CMP_STRIDE

    def page_spec(s):
        return pl.BlockSpec((PAGE_SIZE * 4, HEAD_DIM), lambda b, i, pt: (pt[b * n_pages + i * nps + s], 0))

    x_specs = [page_spec(s) for s in range(nps)]
    w_spec = pl.BlockSpec(cw['w1k_cat'].shape, lambda b, i, pt: (0, 0, 0))
    out_spec = pl.BlockSpec((1, nch_tile, 4 * HEAD_DIM), lambda b, i, pt: (b, i, 0))
    shp = jax.ShapeDtypeStruct((bsz, n_tiles * nch_tile, 4 * HEAD_DIM), jnp.float32)
    a, b = pl.pallas_call(
        functools.partial(_cmp_partial_paged_kernel, n_src=nps, rows_per_src=PAGE_SIZE),
        grid_spec=pltpu.PrefetchScalarGridSpec(
            num_scalar_prefetch=1, grid=(bsz, n_tiles),
            in_specs=x_specs + [w_spec, w_spec], out_specs=[out_spec, out_spec]),
        out_shape=[shp, shp],
        compiler_params=pltpu.CompilerParams(
            dimension_semantics=("arbitrary", "arbitrary"), vmem_limit_bytes=VMEM_LIMIT_BYTES),
        name="cmp_partial_paged",
    )(page_table.reshape(-1), *([pool] * nps), cw['w1k_cat'], cw['w1v_cat'])
    return _cmp_finish(a, b, cw)


def _nsa_decode_a_kernel(proj_ref, kvc_ref, win_ref, o_ref, gsel_ref, idx_ref, *, t_new, n_past):
    f32, bf16 = jnp.float32, jnp.bfloat16
    scale = HEAD_DIM ** -0.5
    nch = kvc_ref.shape[1]
    n_win = win_ref.shape[1] // KV_PARTS
    rows = HPG * t_new
    r_iota = lax.broadcasted_iota(jnp.int32, (rows, 1), 0)
    t_row = r_iota % t_new
    h_row = r_iota // t_new
    pos_row = n_past + t_row
    gates = jax.nn.sigmoid(proj_ref[0, :, GATE_OFF:IN_WIDTH])
    c_lane = lax.broadcasted_iota(jnp.int32, (1, nch), 1)
    dist_c = pos_row - (c_lane * CMP_STRIDE + (CMP_BLOCK - 1))
    valid_c = (dist_c >= 0) & (c_lane < nch - 1)
    c_col = lax.broadcasted_iota(jnp.int32, (nch, 1), 0)
    j_lane = lax.broadcasted_iota(jnp.int32, (1, TOPK_LANES), 1)
    overlap = jnp.where((c_col * CMP_STRIDE <= j_lane * SEL_BLOCK + (SEL_BLOCK - 1))
                        & (c_col * CMP_STRIDE + (CMP_BLOCK - 1) >= j_lane * SEL_BLOCK), 1.0, 0.0).astype(bf16)
    pos_t = n_past + lax.broadcasted_iota(jnp.int32, (t_new, 1), 0)
    blk_t = pos_t // SEL_BLOCK
    forced = (j_lane == 0) | (j_lane == blk_t) | (j_lane == blk_t - 1)
    j_f = j_lane.astype(f32)
    k_lane = lax.broadcasted_iota(jnp.int32, (1, IDX_LANES), 1)
    i_win = lax.broadcasted_iota(jnp.int32, (1, n_win), 1)
    dist_w = pos_row - (n_past - n_win + i_win)
    keep_w = (dist_w >= 0) & (dist_w < WINDOW)
    j_new = lax.broadcasted_iota(jnp.int32, (1, 8), 1)
    dist_n = t_row - j_new
    keep_n = (dist_n >= 0) & (j_new < t_new)
    zpad = jnp.zeros((8 - t_new, HEAD_DIM), f32)

    for g in range(NSA_KV_HEADS):
        kcol = slice(g * HEAD_DIM, (g + 1) * HEAD_DIM)
        vcol = slice((NSA_KV_HEADS + g) * HEAD_DIM, (NSA_KV_HEADS + g + 1) * HEAD_DIM)
        heads = [g * HPG + h for h in range(HPG)]
        slope_row = jnp.zeros((rows, 1), f32)
        for h, hh in enumerate(heads):
            slope_row = jnp.where(h_row == h, 2.0 ** -(hh + 1), slope_row)
        q = jnp.concatenate([proj_ref[0, :, hh * HEAD_DIM:(hh + 1) * HEAD_DIM] for hh in heads], axis=0).astype(bf16)

        kc = kvc_ref[0, :, kcol].astype(bf16)
        vc = kvc_ref[0, :, vcol].astype(bf16)
        s = lax.dot_general(q, kc, _NT, preferred_element_type=f32) * scale - slope_row * dist_c.astype(f32)
        s = jnp.where(valid_c, s, NEG_INF)
        e = jnp.exp(s - jnp.max(s, axis=-1, keepdims=True))
        p = e / jnp.sum(e, axis=-1, keepdims=True)
        p = jnp.where(valid_c, p, 0.0)
        o_cmp = jnp.dot(p.astype(bf16), vc, preferred_element_type=f32)
        psum = p[0:t_new]
        for h in range(1, HPG):
            psum = psum + p[h * t_new:(h + 1) * t_new]

        p_hi = psum.astype(bf16)
        p_lo = (psum - p_hi.astype(f32)).astype(bf16)
        imp = (jnp.dot(p_hi, overlap, preferred_element_type=f32)
               + jnp.dot(p_lo, overlap, preferred_element_type=f32))
        imp = jnp.where(forced, imp + FORCE_BONUS, imp)
        imp = jnp.where(j_lane <= blk_t, imp, NEG_INF)
        picked = jnp.full((t_new, IDX_LANES), -1.0, f32)
        for k in range(N_SELECT):
            best = jnp.max(imp, axis=-1, keepdims=True)
            first = jnp.min(jnp.where(imp == best, j_f, 1e9), axis=-1, keepdims=True)
            picked = jnp.where(k_lane == k, jnp.where(best > 0.5 * NEG_INF, first, -1.0), picked)
            imp = jnp.where(j_f == first, -3e38, imp)
        idx_ref[0, g * t_new:(g + 1) * t_new, :] = picked.astype(jnp.int32)

        kw = win_ref[0, pl.ds(g, n_win, stride=KV_PARTS), :].astype(bf16)
        vw = win_ref[0, pl.ds(NSA_KV_HEADS + g, n_win, stride=KV_PARTS), :].astype(bf16)
        kn = jnp.concatenate([proj_ref[0, :, WIN_OFF + g * HEAD_DIM:WIN_OFF + (g + 1) * HEAD_DIM], zpad], axis=0)
        vn = jnp.concatenate([proj_ref[0, :, WIN_OFF + KV_WIDTH + g * HEAD_DIM:
                                       WIN_OFF + KV_WIDTH + (g + 1) * HEAD_DIM], zpad], axis=0)
        s_w = lax.dot_general(q, kw, _NT, preferred_element_type=f32) * scale - slope_row * dist_w.astype(f32)
        s_n = (lax.dot_general(q, kn.astype(bf16), _NT, preferred_element_type=f32) * scale
               - slope_row * dist_n.astype(f32))
        s_w = jnp.where(keep_w, s_w, NEG_INF)
        s_n = jnp.where(keep_n, s_n, NEG_INF)
        m = jnp.maximum(jnp.max(s_w, axis=-1, keepdims=True), jnp.max(s_n, axis=-1, keepdims=True))
        e_w = jnp.exp(s_w - m)
        e_n = jnp.exp(s_n - m)
        den = jnp.sum(e_w, axis=-1, keepdims=True) + jnp.sum(e_n, axis=-1, keepdims=True)
        o_win = (jnp.dot(e_w.astype(bf16), vw, preferred_element_type=f32)
                 + jnp.dot(e_n.astype(bf16), vn.astype(bf16), preferred_element_type=f32)) / den

        for h, hh in enumerate(heads):
            rs = slice(h * t_new, (h + 1) * t_new)
            hs = slice(hh * HEAD_DIM, (hh + 1) * HEAD_DIM)
            o_ref[0, :, hs] = (gates[:, hh:hh + 1] * o_cmp[rs]
                               + gates[:, 2 * NSA_HEADS + hh:2 * NSA_HEADS + hh + 1] * o_win[rs])
            gsel_ref[0, :, hs] = jnp.broadcast_to(gates[:, NSA_HEADS + hh:NSA_HEADS + hh + 1], (t_new, HEAD_DIM))


def _nsa_decode_sel_kernel(idx_ref, pt_ref, q_ref, part_ref, gsel_ref, new_ref, *refs, t_new, n_past):
    del pt_ref
    f32, bf16 = jnp.float32, jnp.bfloat16
    o_ref = refs[NSA_KV_HEADS * N_SELECT]
    b, t = pl.program_id(0), pl.program_id(1)
    scale = HEAD_DIM ** -0.5
    n_past_blk = n_past // SEL_BLOCK
    n_keys = N_SELECT * SEL_BLOCK
    lane = lax.broadcasted_iota(jnp.int32, (1, n_keys), 1)
    slot = lane // SEL_BLOCK
    h_row = lax.broadcasted_iota(jnp.int32, (8, 1), 0)
    for g in range(NSA_KV_HEADS):
        base = ((b * NSA_KV_HEADS + g) * t_new + t) * N_SELECT
        blk_of_lane = jnp.full((1, n_keys), -1, jnp.int32)
        ks, vs = [], []
        for k in range(N_SELECT):
            blk = idx_ref[base + k]
            blk_of_lane = jnp.where(slot == k, blk, blk_of_lane)
            is_new = blk >= n_past_blk
            src = refs[g * N_SELECT + k]
            k_old = src[pl.ds(g, SEL_BLOCK, stride=KV_PARTS), :]
            v_old = src[pl.ds(NSA_KV_HEADS + g, SEL_BLOCK, stride=KV_PARTS), :]
            k_new = new_ref[:, g * HEAD_DIM:(g + 1) * HEAD_DIM]
            v_new = new_ref[:, KV_WIDTH + g * HEAD_DIM:KV_WIDTH + (g + 1) * HEAD_DIM]
            ks.append(jnp.where(is_new, k_new, k_old).astype(bf16))
            vs.append(jnp.where(is_new, v_new, v_old).astype(bf16))
        k_all = jnp.concatenate(ks, axis=0)
        v_all = jnp.concatenate(vs, axis=0)
        dist = (n_past + t) - (blk_of_lane * SEL_BLOCK + lane % SEL_BLOCK)
        keep = (dist >= 0) & (blk_of_lane >= 0)
        q = jnp.concatenate([q_ref[:, (g * HPG + h) * HEAD_DIM:(g * HPG + h + 1) * HEAD_DIM] for h in range(HPG)]
                            + [jnp.zeros((8 - HPG, HEAD_DIM), f32)], axis=0).astype(bf16)
        slope_row = jnp.zeros((8, 1), f32)
        for h in range(HPG):
            slope_row = jnp.where(h_row == h, 2.0 ** -(g * HPG + h + 1), slope_row)
        s = lax.dot_general(q, k_all, _NT, preferred_element_type=f32) * scale - slope_row * dist.astype(f32)
        s = jnp.where(keep, s, NEG_INF)
        e = jnp.exp(s - jnp.max(s, axis=-1, keepdims=True))
        p = e / jnp.sum(e, axis=-1, keepdims=True)
        o_sel = jnp.dot(p.astype(bf16), v_all, preferred_element_type=f32)
        for h in range(HPG):
            hs = slice((g * HPG + h) * HEAD_DIM, (g * HPG + h + 1) * HEAD_DIM)
            o_ref[:, hs] = part_ref[:, hs] + gsel_ref[:, hs] * o_sel[h:h + 1]


def _nsa_decode_rows(proj3, kvc, win_rows, pool_rows, page_table, n_past):
    bsz, t_new, _ = proj3.shape
    n_pages = page_table.shape[1]
    full = jax.ShapeDtypeStruct((bsz, t_new, NSA_WIDTH), jnp.float32)
    part, gsel, idx = pl.pallas_call(
        functools.partial(_nsa_decode_a_kernel, t_new=t_new, n_past=n_past),
        grid=(bsz,),
        in_specs=[pl.BlockSpec((1, t_new, IN_WIDTH), lambda b: (b, 0, 0)),
                  pl.BlockSpec((1,) + kvc.shape[1:], lambda b: (b, 0, 0)),
                  pl.BlockSpec((1,) + win_rows.shape[1:], lambda b: (b, 0, 0))],
        out_specs=[pl.BlockSpec((1, t_new, NSA_WIDTH), lambda b: (b, 0, 0)),
                   pl.BlockSpec((1, t_new, NSA_WIDTH), lambda b: (b, 0, 0)),
                   pl.BlockSpec((1, NSA_KV_HEADS * t_new, IDX_LANES), lambda b: (b, 0, 0))],
        out_shape=[full, full, jax.ShapeDtypeStruct((bsz, NSA_KV_HEADS * t_new, IDX_LANES), jnp.int32)],
        compiler_params=pltpu.CompilerParams(dimension_semantics=("arbitrary",), vmem_limit_bytes=VMEM_LIMIT_BYTES),
        name="nsa_decode_a",
    )(proj3, kvc, win_rows)

    n_past_blk = n_past // SEL_BLOCK
    sub = PAGE_SIZE // SEL_BLOCK
    new_rows = jnp.pad(proj3[:, :, SLC_OFF:WIN_OFF], ((0, 0), (0, SEL_BLOCK - t_new), (0, 0)))

    picked = idx[:, :, :N_SELECT]
    past = jnp.clip(picked, 0, n_past_blk - 1)
    pool_blk = jnp.take_along_axis(page_table[:, None, :], past // sub, axis=2) * sub + past % sub

    def pool_spec(g, k):
        def index(b, t, idx, blk):
            return (blk[((b * NSA_KV_HEADS + g) * t_new + t) * N_SELECT + k], 0)
        return pl.BlockSpec((SEL_BLOCK * KV_PARTS, HEAD_DIM), index)

    row_spec = pl.BlockSpec((None, 1, NSA_WIDTH), lambda b, t, idx, pt: (b * t_new + t, 0, 0))
    out = pl.pallas_call(
        functools.partial(_nsa_decode_sel_kernel, t_new=t_new, n_past=n_past),
        grid_spec=pltpu.PrefetchScalarGridSpec(
            num_scalar_prefetch=2, grid=(bsz, t_new),
            in_specs=[row_spec, row_spec, row_spec,
                      pl.BlockSpec((None, SEL_BLOCK, 2 * KV_WIDTH), lambda b, t, idx, pt: (b, 0, 0))]
            + [pool_spec(g, k) for g in range(NSA_KV_HEADS) for k in range(N_SELECT)],
            out_specs=row_spec),
        out_shape=jax.ShapeDtypeStruct((bsz * t_new, 1, NSA_WIDTH), jnp.float32),
        compiler_params=pltpu.CompilerParams(
            dimension_semantics=("arbitrary", "arbitrary"), vmem_limit_bytes=VMEM_LIMIT_BYTES),
        name="nsa_decode_sel",
    )(picked.reshape(-1), pool_blk.reshape(-1).astype(jnp.int32),
      proj3[:, :, Q_OFF:Q_OFF + NSA_WIDTH].reshape(bsz * t_new, 1, NSA_WIDTH),
      part.reshape(bsz * t_new, 1, NSA_WIDTH), gsel.reshape(bsz * t_new, 1, NSA_WIDTH),
      new_rows, *([pool_rows] * (NSA_KV_HEADS * N_SELECT)))
    return out.reshape(bsz, t_new, NSA_WIDTH)


WKV_LANES = 2 * RWKV_HEAD_DIM
WKV_PAIRS = RWKV_HEADS // 2
WKV_STACK = 4
WKV_BB = 4
WKV_CHUNK = 64
WKV_MIN_CHUNK = 16
WKV_FLUSH = RWKV_HEAD_DIM


def _wkv_kernel(wr_ref, w_ref, k_ref, v_ref, kk_ref, kka_ref, c1_ref, c2_ref, s0_ref, y_ref, st_ref,
                s_scr, y_scr, *, tc):
    f32, bf16 = jnp.float32, jnp.bfloat16
    ti = pl.program_id(1)
    hd = RWKV_HEAD_DIM
    n_tiles = WKV_BB * WKV_PAIRS
    n_stacks = n_tiles // WKV_STACK
    tile = lambda q: (q // WKV_PAIRS, q % WKV_PAIRS)

    @pl.when(ti == 0)
    def _():
        for q in range(n_tiles):
            b, p = tile(q)
            s_scr[q] = jnp.concatenate([s0_ref[b, 2 * p], s0_ref[b, 2 * p + 1]], axis=1)

    lane = lax.broadcasted_iota(jnp.int32, (1, WKV_LANES), 1)
    r2 = lax.broadcasted_iota(jnp.int32, (2 * WKV_LANES, 1), 0)
    c2 = lax.broadcasted_iota(jnp.int32, (1, 2 * WKV_LANES), 1)
    same_head2 = jnp.where(r2 // hd == c2 // hd, 1.0, 0.0).astype(bf16)
    on_diag = lax.broadcasted_iota(jnp.int32, (hd, 1), 0) == lane % hd
    n_flush = min(tc, WKV_FLUSH)
    y_scr[...] = jnp.zeros(y_scr.shape, f32)
    pairs = [(2 * i, 2 * i + 1) for i in range(n_stacks // 2)]
    stack_tiles = lambda st: range(st * WKV_STACK, (st + 1) * WKV_STACK)

    def row_sums(per_tile, s0, s1):
        lhs = jnp.concatenate([jnp.concatenate([per_tile[q] for q in stack_tiles(st)], axis=0)
                               for st in (s0, s1)], axis=1)
        res = jnp.dot(lhs.astype(bf16), same_head2, preferred_element_type=f32)
        return {q: res[n * hd:(n + 1) * hd, half * WKV_LANES:(half + 1) * WKV_LANES]
                for half, st in enumerate((s0, s1)) for n, q in enumerate(stack_tiles(st))}

    def step(t, carry):
        here = (lane % hd) == (t % n_flush)
        get = lambda ref, q: ref[tile(q)[0], tile(q)[1], pl.ds(t, 1), :]
        for s0, s1 in pairs:
            tiles = list(stack_tiles(s0)) + list(stack_tiles(s1))
            s_old = {q: s_scr[q] for q in tiles}
            sa = row_sums({q: s_old[q] * get(kk_ref, q) for q in tiles}, s0, s1)
            y_old = row_sums({q: s_old[q] * get(wr_ref, q) for q in tiles}, s0, s1)
            v_col = row_sums({q: jnp.where(on_diag, get(v_ref, q), 0.0) for q in tiles}, s0, s1)
            for q in tiles:
                s_scr[q] = s_old[q] * get(w_ref, q) - sa[q] * get(kka_ref, q) + v_col[q] * get(k_ref, q)
                y_col = y_old[q] - sa[q] * get(c1_ref, q) + v_col[q] * get(c2_ref, q)
                y_scr[q] = jnp.where(here, y_col, y_scr[q])
        return carry

    for sub in range(tc // n_flush):
        lax.fori_loop(sub * n_flush, (sub + 1) * n_flush, step, 0)
        for q in range(n_tiles):
            b, p = tile(q)
            yt = y_scr[q].T
            y_ref[b, p, sub * n_flush:(sub + 1) * n_flush, :] = jnp.concatenate(
                [yt[:n_flush], yt[hd:hd + n_flush]], axis=1)

    @pl.when(ti == pl.num_programs(1) - 1)
    def _():
        for q in range(n_tiles):
            b, p = tile(q)
            st_ref[b, 2 * p] = s_scr[q][:, :hd]
            st_ref[b, 2 * p + 1] = s_scr[q][:, hd:]


def _wkv_scan(wr, w, k, v, kk, kka, c1, c2, s0, tc):
    bsz, n_pairs, seq, _ = wr.shape
    assert n_pairs == WKV_PAIRS and bsz % WKV_BB == 0 and seq % tc == 0
    n_tiles = WKV_BB * WKV_PAIRS
    x_spec = pl.BlockSpec((WKV_BB, WKV_PAIRS, tc, WKV_LANES), lambda b, i: (b, 0, i, 0))
    s_spec = pl.BlockSpec((WKV_BB, RWKV_HEADS, RWKV_HEAD_DIM, RWKV_HEAD_DIM), lambda b, i: (b, 0, 0, 0))
    return pl.pallas_call(
        functools.partial(_wkv_kernel, tc=tc),
        grid=(bsz // WKV_BB, seq // tc),
        in_specs=[x_spec] * 8 + [s_spec],
        out_specs=[x_spec, s_spec],
        out_shape=[jax.ShapeDtypeStruct(wr.shape, jnp.float32),
                   jax.ShapeDtypeStruct(s0.shape, jnp.float32)],
        scratch_shapes=[pltpu.VMEM((n_tiles, RWKV_HEAD_DIM, WKV_LANES), jnp.float32),
                        pltpu.VMEM((n_tiles, RWKV_HEAD_DIM, WKV_LANES), jnp.float32)],
        compiler_params=pltpu.CompilerParams(
            dimension_semantics=("arbitrary", "arbitrary"), vmem_limit_bytes=VMEM_LIMIT_BYTES),
        name="wkv_scan",
    )(wr, w, k, v, kk, kka, c1, c2, s0)


RW_PACKS = RWKV_WIDTH // WKV_LANES


def _head_sums(x, ones2):
    f32, bf16 = jnp.float32, jnp.bfloat16
    hi = x.astype(bf16)
    lo = (x - hi.astype(f32)).astype(bf16)
    w = 2 * WKV_LANES
    out = []
    for c in range(RWKV_WIDTH // w):
        sl = slice(c * w, (c + 1) * w)
        out.append(jnp.dot(hi[:, sl], ones2, preferred_element_type=f32)
                   + jnp.dot(lo[:, sl], ones2, preferred_element_type=f32))
    return jnp.concatenate(out, axis=1)


def _head_ones():
    w = 2 * WKV_LANES
    r = lax.broadcasted_iota(jnp.int32, (w, 1), 0)
    c = lax.broadcasted_iota(jnp.int32, (1, w), 1)
    return jnp.where(r // RWKV_HEAD_DIM == c // RWKV_HEAD_DIM, 1.0, 0.0).astype(jnp.bfloat16)


def _store_rw(ref, val, pack_major):
    if pack_major:
        for p in range(RW_PACKS):
            ref[0, p] = val[:, p * WKV_LANES:(p + 1) * WKV_LANES]
    else:
        ref[0] = val


def _rwkv_prep_kernel(h_ref, hprev_ref, h0_ref, pr_ref, pk_ref, pv_ref, pprev_r, pprev_k, pprev_v, p0_ref,
                      mu_rkv_ref, mu_wag_ref, dw0_ref, dw1_ref, dw2_ref, a0_ref, a1_ref, a2_ref, g1_ref, g2_ref,
                      kk_w_ref, ka_w_ref, rk_w_ref,
                      wr_out, w_out, k_out, v_out, kk_out, kka_out, c1_out, c2_out, g_out, bonus_out,
                      *, period, pack_major):
    f32, bf16 = jnp.float32, jnp.bfloat16
    i = pl.program_id(1)
    tm = h_ref.shape[1]
    row = lax.broadcasted_iota(jnp.int32, (tm, 1), 0)
    per_row_first = h0_ref.shape[1] != 1
    first = (row % period == 0) if per_row_first else None

    def shifted(cur, prev_blk, first_rows):
        rolled = pltpu.roll(cur, 1, 0)
        if per_row_first:
            return jnp.where(first, first_rows, rolled)
        row0 = jnp.where(i == 0, first_rows, prev_blk[7:8])
        return jnp.where(row == 0, row0, rolled)

    h = h_ref[0]
    xx = shifted(h, hprev_ref[0], h0_ref[0]) - h
    xw = (h + xx * mu_wag_ref[0:1]).astype(bf16)
    xa = (h + xx * mu_wag_ref[1:2]).astype(bf16)
    xg = (h + xx * mu_wag_ref[2:3]).astype(bf16)
    dmid = jnp.tanh(jnp.dot(xw, dw1_ref[...], preferred_element_type=f32))
    dlin = dw0_ref[...] + jnp.dot(dmid.astype(bf16), dw2_ref[...], preferred_element_type=f32)
    z = -dlin
    w_log = -(jnp.maximum(z, 0.0) + jnp.log(1.0 + jnp.exp(-jnp.abs(z)))) - 0.5
    decay = jnp.exp(-jnp.exp(w_log))
    amid = jnp.dot(xa, a1_ref[...], preferred_element_type=f32)
    a = jax.nn.sigmoid(a0_ref[...] + jnp.dot(amid.astype(bf16), a2_ref[...], preferred_element_type=f32))
    gmid = jax.nn.sigmoid(jnp.dot(xg, g1_ref[...], preferred_element_type=f32))
    g = jnp.dot(gmid.astype(bf16), g2_ref[...], preferred_element_type=f32)

    def mixed(cur_ref, prev_ref, n):
        cur = cur_ref[0]
        cs = slice(n * RWKV_WIDTH, (n + 1) * RWKV_WIDTH)
        prev = shifted(cur, prev_ref[0], p0_ref[0][:, cs])
        return cur + mu_rkv_ref[:, cs] * (prev - cur)

    r = mixed(pr_ref, pprev_r, 0)
    k = mixed(pk_ref, pprev_k, 1)
    v = mixed(pv_ref, pprev_v, 2)
    ones2 = _head_ones()
    kk = k * kk_w_ref[...]
    kk = kk / jnp.maximum(jnp.sqrt(_head_sums(kk * kk, ones2)), 1e-12)
    k = k * (1.0 + (a - 1.0) * ka_w_ref[...])
    bonus = _head_sums(r * k * rk_w_ref[...], ones2) * v
    _store_rw(wr_out, decay * r, pack_major)
    _store_rw(c1_out, _head_sums(kk * a * r, ones2), pack_major)
    _store_rw(c2_out, _head_sums(k * r, ones2), pack_major)
    _store_rw(w_out, decay, pack_major)
    _store_rw(k_out, k, pack_major)
    _store_rw(v_out, v, pack_major)
    _store_rw(kk_out, kk, pack_major)
    _store_rw(kka_out, kk * a, pack_major)
    g_out[0] = g
    bonus_out[0] = bonus


def _rwkv_prep(h, proj, h0, p0, lp, tm, period, pack_major):
    f32, bf16 = jnp.float32, jnp.bfloat16
    g, t, d = h.shape
    rw = RWKV_WIDTH
    nb = tm // 8
    cur = lambda w, c: pl.BlockSpec((1, tm, w), lambda b, i: (b, i, c))
    prev = lambda w, c: pl.BlockSpec((1, 8, w), lambda b, i: (b, jnp.maximum(i * nb - 1, 0), c))
    per_row = h0.shape[1] != 1
    carry = lambda w: pl.BlockSpec((1, tm if per_row else 1, w), (lambda b, i: (b, i, 0)) if per_row else (lambda b, i: (b, 0, 0)))
    full = lambda a: pl.BlockSpec(a.shape, lambda b, i: (0,) * a.ndim)
    c0 = RKV_OFF // rw
    ws = [lp['mu_rkv'].reshape(1, 3 * rw), lp['mu_wag'], lp['decay_w0'].reshape(1, rw), lp['decay_w1'].astype(bf16),
          lp['decay_w2'].astype(bf16), lp['iclr_a0'].reshape(1, rw), lp['iclr_a1'].astype(bf16),
          lp['iclr_a2'].astype(bf16), lp['gate_g1'].astype(bf16), lp['gate_g2'].astype(bf16),
          lp['k_k'].reshape(1, rw), lp['k_a'].reshape(1, rw), lp['r_k'].reshape(1, rw)]
    if pack_major:
        seq_shape = jax.ShapeDtypeStruct((g, RW_PACKS, t, WKV_LANES), f32)
        seq_spec = pl.BlockSpec((1, RW_PACKS, tm, WKV_LANES), lambda b, i: (b, 0, i, 0))
    else:
        seq_shape = jax.ShapeDtypeStruct((g, t, rw), f32)
        seq_spec = cur(rw, 0)
    flat_shape = jax.ShapeDtypeStruct((g, t, rw), f32)
    return pl.pallas_call(
        functools.partial(_rwkv_prep_kernel, period=period, pack_major=pack_major),
        grid=(g, t // tm),
        in_specs=[cur(d, 0), prev(d, 0), carry(d), cur(rw, c0), cur(rw, c0 + 1), cur(rw, c0 + 2),
                  prev(rw, c0), prev(rw, c0 + 1), prev(rw, c0 + 2), carry(3 * rw)] + [full(w) for w in ws],
        out_specs=[seq_spec] * 8 + [cur(rw, 0), cur(rw, 0)],
        out_shape=[seq_shape] * 8 + [flat_shape, flat_shape],
        compiler_params=pltpu.CompilerParams(
            dimension_semantics=("arbitrary", "arbitrary"), vmem_limit_bytes=VMEM_LIMIT_BYTES),
        name="rwkv_prep",
    )(h, h, h0, proj, proj, proj, proj, proj, proj, p0, *ws)


def _rwkv_post_kernel(y_ref, g_ref, bonus_ref, lnw_ref, lnb_ref, o_ref, *, pack_major):
    if pack_major:
        y = jnp.concatenate([y_ref[0, p] for p in range(RW_PACKS)], axis=1)
    else:
        y = y_ref[0]
    ones2 = _head_ones()
    inv = 1.0 / RWKV_HEAD_DIM
    mu = _head_sums(y, ones2) * inv
    dev = y - mu
    var = _head_sums(dev * dev, ones2) * inv
    yn = dev * lax.rsqrt(var + GN_EPS) * lnw_ref[...] + lnb_ref[...]
    o_ref[0] = (yn + bonus_ref[0]) * g_ref[0]


def _rwkv_post(y, g, bonus, ln_w, ln_b, tm, pack_major):
    gsz, t, rw = g.shape
    flat = pl.BlockSpec((1, tm, rw), lambda b, i: (b, i, 0))
    y_spec = pl.BlockSpec((1, RW_PACKS, tm, WKV_LANES), lambda b, i: (b, 0, i, 0)) if pack_major else flat
    vec = pl.BlockSpec((1, rw), lambda b, i: (0, 0))
    return pl.pallas_call(
        functools.partial(_rwkv_post_kernel, pack_major=pack_major),
        grid=(gsz, t // tm),
        in_specs=[y_spec, flat, flat, vec, vec],
        out_specs=flat,
        out_shape=jax.ShapeDtypeStruct((gsz, t, rw), jnp.float32),
        compiler_params=pltpu.CompilerParams(
            dimension_semantics=("arbitrary", "arbitrary"), vmem_limit_bytes=VMEM_LIMIT_BYTES),
        name="rwkv_post",
    )(y, g, bonus, ln_w.reshape(1, rw), ln_b.reshape(1, rw))


def _hier_route(logits):
    assert EXPERT_TOP_K == 2
    n = logits.shape[0]
    pg = jax.nn.softmax(logits[:, :N_GROUPS], axis=-1)

    def key(x):
        bits = lax.bitcast_convert_type(x, jnp.int32)
        return jnp.where(bits < 0, bits ^ jnp.int32(0x7FFFFFFF), bits)

    g_sel = jnp.argmax(key(pg), axis=-1)[:, None]
    g_val = jnp.take_along_axis(pg, g_sel, axis=1)
    le = logits[:, N_GROUPS:N_GROUPS + N_EXPERTS].reshape(n, N_GROUPS, EXPERTS_PER_GROUP)
    le_g = jnp.take_along_axis(le, g_sel[:, :, None], axis=1)[:, 0]
    le_key = key(le_g)
    e1 = jnp.argmax(le_key, axis=-1)[:, None]
    rest = jnp.where(jnp.arange(EXPERTS_PER_GROUP)[None, :] == e1, jnp.iinfo(jnp.int32).min, le_key)
    e2 = jnp.argmax(rest, axis=-1)[:, None]
    e_sel = jnp.concatenate([e1, e2], axis=1)
    e_val = jnp.take_along_axis(le_g, e_sel, axis=1)
    weights = jax.nn.softmax(e_val, axis=-1) * g_val
    return (g_sel * EXPERTS_PER_GROUP + e_sel).astype(jnp.int32), weights


def _moe_ffn(h_pad, eid, w_gate, w_up, w_down, tm):
    n, d = h_pad.shape[0] - 1, h_pad.shape[1]
    a_tot = n * EXPERT_TOP_K
    flat_e = eid.reshape(-1)
    onehot = (flat_e[:, None] == jnp.arange(N_EXPERTS)[None, :]).astype(jnp.int32)
    csum = jnp.cumsum(onehot, axis=0)
    rank = jnp.take_along_axis(csum, flat_e[:, None], axis=1)[:, 0] - 1
    counts = csum[-1]
    padded = (counts + tm - 1) // tm * tm
    pad_end = jnp.cumsum(padded)
    dest = (pad_end - padded)[flat_e] + rank
    n_blk = (a_tot + N_EXPERTS * (tm - 1)) // tm
    tok_buf = jnp.full((n_blk * tm,), n, jnp.int32).at[dest].set(jnp.arange(a_tot, dtype=jnp.int32) // EXPERT_TOP_K)
    xb = h_pad[tok_buf].reshape(n_blk, tm, d)
    blk_start = jnp.arange(n_blk, dtype=jnp.int32) * tm
    blk_e = jnp.minimum(jnp.sum((pad_end[None, :] <= blk_start[:, None]).astype(jnp.int32), axis=1), N_EXPERTS - 1)
    meta = jnp.concatenate([blk_e, pad_end[-1:] // tm]).astype(jnp.int32)
    yb = _moe_blocks(xb, meta, w_gate, w_up, w_down).reshape(n_blk * tm, d)
    dest2 = dest.reshape(n, EXPERT_TOP_K)
    return yb[dest2[:, 0]], yb[dest2[:, 1]]


def _layer_front(x, mod, lp, layer, pool_cmp, pool_slc, page_table, win_buf, wkv0, shift0, past_len, rows):
    B, T, D = x.shape
    groups = B * T // rows
    per_token = rows > T
    mods = jnp.repeat(mod, T, axis=0).reshape(groups, rows, 6 * D) if per_token else mod[:, None, :]
    sh1, sc1, gt1, sh2, sc2, gt2 = jnp.split(mods, 6, axis=-1)
    xg = x.reshape(groups, rows, D)
    tm_in = min(rows, 512)
    proj, h, *kv_rows = _norm_in(xg, lp['norm1'], sc1, sh1, lp['w_in_b'], tm_in)
    kv_shape = (B, T, 2, NSA_KV_HEADS, HEAD_DIM)
    cmp_rows, slc_rows, win_rows = (a.reshape(B, T * KV_PARTS, HEAD_DIM) for a in kv_rows)
    cmp_new, slc_new = cmp_rows.reshape(kv_shape), slc_rows.reshape(kv_shape)

    if pool_cmp is None:
        kvc = _compress_prompt(cmp_rows, _cmp_weights(lp))
        o_nsa = _nsa_prompt(proj.reshape(B * T, IN_WIDTH), kvc, B, T)
        win_len = min(WINDOW, past_len)
        assert T >= win_len
        win_state = win_rows[:, (T - win_len) * KV_PARTS:].reshape((B, win_len) + kv_shape[2:])
    else:
        n_past = page_table.shape[1] * PAGE_SIZE
        assert n_past % CMP_STRIDE == 0 and T < CMP_STRIDE and T <= 8
        assert -(-(n_past + T) // SEL_BLOCK) <= TOPK_LANES and n_past % SEL_BLOCK == 0 and T <= SEL_BLOCK
        assert win_buf.shape[1] == WINDOW
        pages = page_table + layer * pool_cmp.shape[1]
        kvc = _compress_paged(pool_cmp.reshape(-1, HEAD_DIM), pages, _cmp_weights(lp))
        win_buf_rows = win_buf.reshape(B, WINDOW * KV_PARTS, HEAD_DIM)
        o_nsa = _nsa_decode_rows(proj.reshape(B, T, IN_WIDTH), kvc, win_buf_rows, pool_slc.reshape(-1, HEAD_DIM),
                                 pages, n_past)
        win_state = jnp.concatenate([win_buf_rows[:, T * KV_PARTS:], win_rows], axis=1).reshape(
            (B, WINDOW) + kv_shape[2:])

    shift0 = shift0.astype(h.dtype)
    p0 = _matmul(shift0, lp['w_in_b'][:, RKV_OFF:CMP_OFF], B, RWKV_WIDTH)
    if per_token:
        h0 = jnp.repeat(shift0, T, axis=0).reshape(groups, rows, D)
        p0 = jnp.repeat(p0, T, axis=0).reshape(groups, rows, 3 * RWKV_WIDTH)
    else:
        h0, p0 = shift0[:, None], p0[:, None]
    tm_rw = min(rows, 256)
    seqs = _rwkv_prep(h, proj, h0, p0, lp, tm_rw, T, pack_major=not per_token)
    seqs, (gate, bonus) = seqs[:8], seqs[8:]
    if per_token:
        t_pad = -(-T // WKV_MIN_CHUNK) * WKV_MIN_CHUNK

        def pairs(a, fill):
            a = jnp.pad(a.reshape(B, T, RW_PACKS, WKV_LANES), ((0, 0), (0, t_pad - T), (0, 0), (0, 0)),
                        constant_values=fill)
            return a.transpose(0, 2, 1, 3)

        fills = (0.0, 1.0) + (0.0,) * 6
        y, wkv_T = _wkv_scan(*[pairs(a, f) for a, f in zip(seqs, fills)], wkv0.astype(jnp.float32), WKV_MIN_CHUNK)
        y = y.transpose(0, 2, 1, 3)[:, :T].reshape(groups, rows, RWKV_WIDTH)
    else:
        y, wkv_T = _wkv_scan(*seqs, wkv0.astype(jnp.float32), WKV_CHUNK)
    o_rwkv = _rwkv_post(y, gate, bonus, lp['ln_x_w'], lp['ln_x_b'], tm_rw, pack_major=not per_token)

    x1, h2, logits = _mix_out(o_nsa.reshape(groups, rows, NSA_WIDTH), o_rwkv, lp['w_out_b'], xg, gt1, sc2, sh2,
                              lp['norm2'], lp['wr_hi'], lp['wr_lo'], lp['br'], min(rows, 256))
    return (x1, h2, logits, gt2), (cmp_new, slc_new, win_state, wkv_T, h.reshape(B, T, D)[:, -1])


def kernel(x_prompt, x_sample, c_prompt, c_sample, cache_cmp_kv, cache_slc_kv, page_table, state_win_kv, state_wkv, state_shift, w_ada, b_ada, norm1, w_in, cmp_k_w1, cmp_k_pe, cmp_k_w2, cmp_v_w1, cmp_v_pe, cmp_v_w2, mu_rkv, mu_wag, decay_w0, decay_w1, decay_w2, iclr_a0, iclr_a1, iclr_a2, gate_g1, gate_g2, k_k, k_a, r_k, ln_x_w, ln_x_b, w_out, norm2, w_router_group, b_router_group, w_router_expert, b_router_expert, w_gate, w_up, w_down, norm_f):
    bp, tp = x_prompt.shape[:2]
    ts = x_sample.shape[1]
    past_len = page_table.shape[1] * PAGE_SIZE
    assert DEPTH == 1 and w_in.shape[0] == 1
    l = 0
    bs = x_sample.shape[0]
    f32, bf16 = jnp.float32, jnp.bfloat16
    lyr = lambda a: a.reshape(a.shape[1:])
    wr = jnp.concatenate([lyr(w_router_group), lyr(w_router_expert),
                          jnp.zeros((D_MODEL, ROUTER_LANES - N_GROUPS - N_EXPERTS), f32)], axis=1)
    wr_hi = wr.astype(bf16)
    br = jnp.concatenate([lyr(b_router_group), lyr(b_router_expert),
                          jnp.zeros((ROUTER_LANES - N_GROUPS - N_EXPERTS,), f32)]).reshape(1, ROUTER_LANES)
    lp = dict(norm1=lyr(norm1), w_in_b=_permute_w_in(lyr(w_in)).astype(bf16),
              cmp_k_w1=lyr(cmp_k_w1), cmp_k_pe=lyr(cmp_k_pe), cmp_k_w2=lyr(cmp_k_w2),
              cmp_v_w1=lyr(cmp_v_w1), cmp_v_pe=lyr(cmp_v_pe), cmp_v_w2=lyr(cmp_v_w2),
              mu_rkv=lyr(mu_rkv), mu_wag=lyr(mu_wag), decay_w0=lyr(decay_w0), decay_w1=lyr(decay_w1),
              decay_w2=lyr(decay_w2), iclr_a0=lyr(iclr_a0), iclr_a1=lyr(iclr_a1), iclr_a2=lyr(iclr_a2),
              gate_g1=lyr(gate_g1), gate_g2=lyr(gate_g2), k_k=lyr(k_k), k_a=lyr(k_a), r_k=lyr(r_k),
              ln_x_w=lyr(ln_x_w), ln_x_b=lyr(ln_x_b), w_out_b=lyr(w_out).astype(bf16), norm2=lyr(norm2),
              wr_hi=wr_hi, wr_lo=(wr - wr_hi.astype(f32)).astype(bf16), br=br)

    c_all = jnp.concatenate([c_prompt, c_sample], axis=0)
    mod_all = _matmul(jax.nn.silu(c_all), lyr(w_ada), c_all.shape[0], 1024) + lyr(b_ada)
    mod_p, mod_s = mod_all[:bp], mod_all[bp:]

    def experts(h2, logits, tm):
        n = h2.shape[0] * h2.shape[1]
        eid, ew = _hier_route(logits.reshape(n, ROUTER_LANES))
        h_pad = jnp.concatenate([h2.reshape(n, D_MODEL), jnp.zeros((1, D_MODEL), bf16)], axis=0)
        return _moe_ffn(h_pad, eid, lyr(w_gate), lyr(w_up), lyr(w_down), tm) + (ew,)

    wkv_zero = jnp.zeros((bp, RWKV_HEADS, RWKV_HEAD_DIM, RWKV_HEAD_DIM), f32)
    shift_zero = jnp.zeros((bp, D_MODEL), x_prompt.dtype)
    (x1p, h2p, lgp, gt2p), (a1, a2, a3, a4, a5) = _layer_front(
        x_prompt, mod_p, lp, l, None, None, None, None, wkv_zero, shift_zero, past_len, tp)
    y0p, y1p, ewp = experts(h2p, lgp, MOE_TM)
    (x1s, h2s, lgs, gt2s), (b1, b2, b3, b4, b5) = _layer_front(
        x_sample, mod_s, lp, l, cache_cmp_kv, cache_slc_kv, page_table, lyr(state_win_kv), lyr(state_wkv),
        lyr(state_shift), past_len, bs * ts)
    y0s, y1s, ews = experts(h2s, lgs, MOE_TM_DECODE)
    y_prompt = _final(x1p, y0p, y1p, ewp, 0, gt2p, norm_f, 256).reshape(x_prompt.shape)
    y_sample = _final(x1s, y0s, y1s, ews, 0, gt2s, norm_f, x1s.shape[1]).reshape(x_sample.shape)
    st = lambda a: a[None]
    return (y_prompt, y_sample, st(a1), st(b1), st(a2), st(b2), st(a3), st(b3), st(a4), st(b4), st(a5), st(b5))
```

```python
import functools

import jax
import jax.numpy as jnp
from jax import lax
from jax.experimental import pallas as pl
from jax.experimental.pallas import tpu as pltpu

D_MODEL = 2048
DEPTH = 1
PAGE_SIZE = 128
HEAD_DIM = 128
NSA_WIDTH = D_MODEL // 2
NSA_HEADS = NSA_WIDTH // HEAD_DIM
NSA_KV_HEADS = 2
HPG = NSA_HEADS // NSA_KV_HEADS
KV_WIDTH = NSA_KV_HEADS * HEAD_DIM
CMP_BLOCK = 32
CMP_STRIDE = 16
SEL_BLOCK = 64
N_SELECT = 16
WINDOW = 512
FORCE_BONUS = 1e4
RWKV_WIDTH = D_MODEL - NSA_WIDTH
RWKV_HEAD_DIM = 64
RWKV_HEADS = RWKV_WIDTH // RWKV_HEAD_DIM
GN_EPS = 64e-5
N_GROUPS = 4
EXPERTS_PER_GROUP = 8
N_EXPERTS = N_GROUPS * EXPERTS_PER_GROUP
EXPERT_TOP_K = 2
NORM_EPS = 1e-6
NEG_INF = -1e30
W_Q_OFF = 0
W_CMP_OFF = W_Q_OFF + NSA_WIDTH
W_RKV_OFF = W_CMP_OFF + 6 * KV_WIDTH
W_GATE_OFF = W_RKV_OFF + 3 * RWKV_WIDTH
IN_WIDTH = W_GATE_OFF + 3 * NSA_HEADS
Q_OFF = 0
RKV_OFF = Q_OFF + NSA_WIDTH
CMP_OFF = RKV_OFF + 3 * RWKV_WIDTH
SLC_OFF = CMP_OFF + 2 * KV_WIDTH
WIN_OFF = SLC_OFF + 2 * KV_WIDTH
GATE_OFF = WIN_OFF + 2 * KV_WIDTH
assert GATE_OFF == W_GATE_OFF


def _permute_w_in(w):
    return jnp.concatenate([w[:, W_Q_OFF:W_CMP_OFF], w[:, W_RKV_OFF:W_GATE_OFF], w[:, W_CMP_OFF:W_RKV_OFF],
                            w[:, W_GATE_OFF:]], axis=1)

VMEM_LIMIT_BYTES = 48 * 1024 * 1024


def _mm_kernel(x_ref, w_ref, o_ref):
    o_ref[...] = jnp.dot(x_ref[...].astype(jnp.bfloat16), w_ref[...].astype(jnp.bfloat16),
                         preferred_element_type=jnp.float32)


def _matmul(x, w, tm, tn):
    m, k = x.shape
    n = w.shape[1]
    return pl.pallas_call(
        _mm_kernel,
        grid=(pl.cdiv(m, tm), pl.cdiv(n, tn)),
        in_specs=[pl.BlockSpec((tm, k), lambda i, j: (i, 0)),
                  pl.BlockSpec((k, tn), lambda i, j: (0, j))],
        out_specs=pl.BlockSpec((tm, tn), lambda i, j: (i, j)),
        out_shape=jax.ShapeDtypeStruct((m, n), jnp.float32),
        compiler_params=pltpu.CompilerParams(
            dimension_semantics=("arbitrary", "arbitrary"), vmem_limit_bytes=VMEM_LIMIT_BYTES),
        name="matmul",
    )(x, w)


MOE_TM = 256
MOE_TM_DECODE = 128


def _moe_block_kernel(meta_ref, x_ref, wg_ref, wu_ref, wd_ref, o_ref, wg_b, wu_b, wd_b):
    bf16 = jnp.bfloat16
    i = pl.program_id(0)
    used = i < meta_ref[pl.num_programs(0)]
    new_expert = (i == 0) | (meta_ref[i] != meta_ref[jnp.maximum(i - 1, 0)])

    @pl.when(used & new_expert)
    def _():
        wg_b[...] = wg_ref[0].astype(bf16)
        wu_b[...] = wu_ref[0].astype(bf16)
        wd_b[...] = wd_ref[0].astype(bf16)

    @pl.when(used)
    def _():
        x = x_ref[0]
        g = jnp.dot(x, wg_b[...], preferred_element_type=jnp.float32)
        u = jnp.dot(x, wu_b[...], preferred_element_type=jnp.float32)
        hmid = (g * jax.nn.sigmoid(g)) * u
        o_ref[0] = jnp.dot(hmid.astype(bf16), wd_b[...], preferred_element_type=jnp.float32)

    @pl.when(jnp.logical_not(used))
    def _():
        o_ref[...] = jnp.zeros(o_ref.shape, o_ref.dtype)


def _moe_blocks(xb, meta, w_gate, w_up, w_down):
    n_blk, mb, d = xb.shape
    de = w_gate.shape[2]
    grid_spec = pltpu.PrefetchScalarGridSpec(
        num_scalar_prefetch=1,
        grid=(n_blk,),
        in_specs=[pl.BlockSpec((1, mb, d), lambda i, e: (i, 0, 0)),
                  pl.BlockSpec((1, d, de), lambda i, e: (e[i], 0, 0)),
                  pl.BlockSpec((1, d, de), lambda i, e: (e[i], 0, 0)),
                  pl.BlockSpec((1, de, d), lambda i, e: (e[i], 0, 0))],
        out_specs=pl.BlockSpec((1, mb, d), lambda i, e: (i, 0, 0)),
        scratch_shapes=[pltpu.VMEM((d, de), jnp.bfloat16), pltpu.VMEM((d, de), jnp.bfloat16),
                        pltpu.VMEM((de, d), jnp.bfloat16)],
    )
    return pl.pallas_call(
        _moe_block_kernel,
        grid_spec=grid_spec,
        out_shape=jax.ShapeDtypeStruct((n_blk, mb, d), jnp.float32),
        compiler_params=pltpu.CompilerParams(
            dimension_semantics=("arbitrary",), vmem_limit_bytes=VMEM_LIMIT_BYTES),
        name="moe_blocks",
    )(meta, xb, w_gate, w_up, w_down)


ROUTER_LANES = 128


KV_TILES = 3
KV_PARTS = 2 * NSA_KV_HEADS


def _norm_in_kernel(x_ref, n1_ref, sc_ref, sh_ref, w_ref, proj_ref, h_ref, cmp_ref, slc_ref, win_ref, hb_scr,
                    *, kv_tile0):
    j = pl.program_id(2)

    @pl.when(j == 0)
    def _():
        x = x_ref[0]
        y = x * lax.rsqrt(jnp.mean(x * x, axis=-1, keepdims=True) + NORM_EPS) * n1_ref[...]
        h = y * (1.0 + sc_ref[0]) + sh_ref[0]
        h_ref[0] = h
        hb_scr[...] = h.astype(jnp.bfloat16)

    proj_ref[0] = jnp.dot(hb_scr[...], w_ref[...], preferred_element_type=jnp.float32)

    for n, kv_ref in enumerate((cmp_ref, slc_ref, win_ref)):
        @pl.when(j == kv_tile0 + n)
        def _(kv_ref=kv_ref):
            tm = proj_ref.shape[1]
            for part in range(KV_PARTS):
                kv_ref[0, pl.ds(part, tm, stride=KV_PARTS), :] = proj_ref[0, :, part * HEAD_DIM:(part + 1) * HEAD_DIM]


def _norm_in(x, norm1, sc, sh, w_in_b, tm):
    g, t, d = x.shape
    n = w_in_b.shape[1]
    tn = 2 * KV_WIDTH
    kv_tile0 = CMP_OFF // tn
    assert CMP_OFF % tn == 0 and GATE_OFF == CMP_OFF + KV_TILES * tn
    mrows = sc.shape[1]
    mod_spec = pl.BlockSpec((1, mrows if mrows == 1 else tm, d),
                            (lambda b, i, j: (b, 0, 0)) if mrows == 1 else (lambda b, i, j: (b, i, 0)))
    return pl.pallas_call(
        functools.partial(_norm_in_kernel, kv_tile0=kv_tile0),
        grid=(g, t // tm, pl.cdiv(n, tn)),
        in_specs=[pl.BlockSpec((1, tm, d), lambda b, i, j: (b, i, 0)),
                  pl.BlockSpec((1, d), lambda b, i, j: (0, 0)),
                  mod_spec, mod_spec,
                  pl.BlockSpec((d, tn), lambda b, i, j: (0, j))],
        out_specs=[pl.BlockSpec((1, tm, tn), lambda b, i, j: (b, i, j)),
                   pl.BlockSpec((1, tm, d), lambda b, i, j: (b, i, 0))]
        + [pl.BlockSpec((1, tm * KV_PARTS, HEAD_DIM), lambda b, i, j: (b, i, 0))] * KV_TILES,
        out_shape=[jax.ShapeDtypeStruct((g, t, n), jnp.float32), jax.ShapeDtypeStruct((g, t, d), jnp.float32)]
        + [jax.ShapeDtypeStruct((g, t * KV_PARTS, HEAD_DIM), jnp.float32)] * KV_TILES,
        scratch_shapes=[pltpu.VMEM((tm, d), jnp.bfloat16)],
        compiler_params=pltpu.CompilerParams(
            dimension_semantics=("arbitrary", "arbitrary", "arbitrary"), vmem_limit_bytes=VMEM_LIMIT_BYTES),
        name="norm_in",
    )(x, norm1.reshape(1, d), sc, sh, w_in_b)


def _mix_out_kernel(on_ref, orw_ref, w_ref, x_ref, gt_ref, sc_ref, sh_ref, n2_ref, wr_hi_ref, wr_lo_ref, br_ref,
                    x1_ref, h2_ref, lg_ref):
    f32, bf16 = jnp.float32, jnp.bfloat16
    half = on_ref.shape[2]
    mixed = (jnp.dot(on_ref[0].astype(bf16), w_ref[0:half, :], preferred_element_type=f32)
             + jnp.dot(orw_ref[0].astype(bf16), w_ref[half:, :], preferred_element_type=f32))
    x1 = x_ref[0] + gt_ref[0] * mixed
    x1_ref[0] = x1
    y = x1 * lax.rsqrt(jnp.mean(x1 * x1, axis=-1, keepdims=True) + NORM_EPS) * n2_ref[...]
    h2 = y * (1.0 + sc_ref[0]) + sh_ref[0]
    hi = h2.astype(bf16)
    h2_ref[0] = hi
    lo = (h2 - hi.astype(f32)).astype(bf16)
    lg_ref[0] = (jnp.dot(hi, wr_hi_ref[...], preferred_element_type=f32)
                 + jnp.dot(hi, wr_lo_ref[...], preferred_element_type=f32)
                 + jnp.dot(lo, wr_hi_ref[...], preferred_element_type=f32) + br_ref[...])


def _mix_out(o_nsa, o_rwkv, w_out_b, x, gt, sc, sh, norm2, wr_hi, wr_lo, br, tm):
    g, t, d = x.shape
    half = o_nsa.shape[2]
    mrows = sc.shape[1]
    mod_spec = pl.BlockSpec((1, mrows if mrows == 1 else tm, d),
                            (lambda b, i: (b, 0, 0)) if mrows == 1 else (lambda b, i: (b, i, 0)))
    row = lambda w: pl.BlockSpec((1, tm, w), lambda b, i: (b, i, 0))
    full = lambda a: pl.BlockSpec(a.shape, lambda b, i: (0,) * a.ndim)
    n2 = norm2.reshape(1, d)
    return pl.pallas_call(
        _mix_out_kernel,
        grid=(g, t // tm),
        in_specs=[row(half), row(half), full(w_out_b), row(d), mod_spec, mod_spec, mod_spec, full(n2),
                  full(wr_hi), full(wr_lo), full(br)],
        out_specs=[row(d), row(d), row(ROUTER_LANES)],
        out_shape=[jax.ShapeDtypeStruct((g, t, d), jnp.float32), jax.ShapeDtypeStruct((g, t, d), jnp.bfloat16),
                   jax.ShapeDtypeStruct((g, t, ROUTER_LANES), jnp.float32)],
        compiler_params=pltpu.CompilerParams(
            dimension_semantics=("arbitrary", "arbitrary"), vmem_limit_bytes=VMEM_LIMIT_BYTES),
        name="mix_out",
    )(o_nsa, o_rwkv, w_out_b, x, gt, sc, sh, n2, wr_hi, wr_lo, br)


def _final_kernel(x_ref, y0_ref, y1_ref, ew_ref, gt_ref, nf_ref, o_ref):
    ew = ew_ref[...]
    ffn = y0_ref[...] * ew[:, 0:1] + y1_ref[...] * ew[:, 1:2]
    x2 = x_ref[0] + gt_ref[0] * ffn
    o_ref[0] = x2 * lax.rsqrt(jnp.mean(x2 * x2, axis=-1, keepdims=True) + NORM_EPS) * nf_ref[...]


def _final(x1, y0, y1, ew, row_off, gt, norm_f, tm):
    g, t, d = x1.shape
    assert row_off % tm == 0
    mrows = gt.shape[1]
    mod_spec = pl.BlockSpec((1, mrows if mrows == 1 else tm, d),
                            (lambda b, i: (b, 0, 0)) if mrows == 1 else (lambda b, i: (b, i, 0)))
    row = pl.BlockSpec((1, tm, d), lambda b, i: (b, i, 0))
    flat = lambda w: pl.BlockSpec((tm, w), lambda b, i: (row_off // tm + b * (t // tm) + i, 0))
    return pl.pallas_call(
        _final_kernel,
        grid=(g, t // tm),
        in_specs=[row, flat(d), flat(d), flat(EXPERT_TOP_K), mod_spec, pl.BlockSpec((1, d), lambda b, i: (0, 0))],
        out_specs=row,
        out_shape=jax.ShapeDtypeStruct((g, t, d), jnp.float32),
        compiler_params=pltpu.CompilerParams(
            dimension_semantics=("arbitrary", "arbitrary"), vmem_limit_bytes=VMEM_LIMIT_BYTES),
        name="final_norm",
    )(x1, y0, y1, ew, gt, norm_f.reshape(1, d))


def _cmp_partial_kernel(*refs, n_src, rows_per_src):
    x_refs = refs[:n_src]
    w1k_ref, w1v_ref, a_ref, b_ref = refs[n_src:]
    nch_src = rows_per_src // CMP_STRIDE
    for kvg in range(4):
        w_ref = w1k_ref if kvg < 2 else w1v_ref
        acc = None
        for p in range(CMP_STRIDE):
            parts = [x_refs[s][pl.ds(4 * p + kvg, nch_src, stride=4 * CMP_STRIDE), :] for s in range(n_src)]
            xp = parts[0] if n_src == 1 else jnp.concatenate(parts, axis=0)
            d = jnp.dot(xp.astype(jnp.bfloat16), w_ref[p], preferred_element_type=jnp.float32)
            acc = d if acc is None else acc + d
        a_ref[0, :, kvg * HEAD_DIM:(kvg + 1) * HEAD_DIM] = acc[:, :HEAD_DIM]
        b_ref[0, :, kvg * HEAD_DIM:(kvg + 1) * HEAD_DIM] = acc[:, HEAD_DIM:]


def _cmp_finish_kernel(a_ref, b_ref, pek_ref, pev_ref, w1k_ref, w1v_ref, w2k_ref, w2v_ref, o_ref):
    nch = a_ref.shape[1]
    for kv, (pe_ref, w1_ref, w2_ref) in enumerate(((pek_ref, w1k_ref, w2k_ref), (pev_ref, w1v_ref, w2v_ref))):
        pe8 = jnp.broadcast_to(pe_ref[...], (8, pe_ref.shape[1])).astype(jnp.bfloat16)
        pterm = jnp.dot(pe8, w1_ref[...], preferred_element_type=jnp.float32)[0:1]
        w2 = w2_ref[...]
        for g in range(NSA_KV_HEADS):
            lo = (kv * NSA_KV_HEADS + g) * HEAD_DIM
            nxt = pltpu.roll(b_ref[0, :, lo:lo + HEAD_DIM], nch - 1, 0)
            pre = a_ref[0, :, lo:lo + HEAD_DIM] + nxt + pterm
            act = pre * jax.nn.sigmoid(pre)
            o_ref[0, :, lo:lo + HEAD_DIM] = jnp.dot(act.astype(jnp.bfloat16), w2, preferred_element_type=jnp.float32)


def _cmp_weights(lp):
    bf = jnp.bfloat16
    half = CMP_BLOCK // 2
    cat = lambda w: jnp.concatenate([w[:half], w[half:]], axis=-1).astype(bf)
    flat = lambda w: w.reshape(CMP_BLOCK * HEAD_DIM, HEAD_DIM).astype(bf)
    return dict(w1k_cat=cat(lp['cmp_k_w1']), w1v_cat=cat(lp['cmp_v_w1']),
                w1k_flat=flat(lp['cmp_k_w1']), w1v_flat=flat(lp['cmp_v_w1']),
                pek=lp['cmp_k_pe'].reshape(1, -1), pev=lp['cmp_v_pe'].reshape(1, -1),
                w2k=lp['cmp_k_w2'].astype(bf), w2v=lp['cmp_v_w2'].astype(bf))


def _cmp_finish(a, b, cw):
    bsz, nch, _ = a.shape
    full = lambda arr: pl.BlockSpec(arr.shape, lambda i: (0,) * arr.ndim)
    blk = pl.BlockSpec((1, nch, 4 * HEAD_DIM), lambda i: (i, 0, 0))
    ws = [cw['pek'], cw['pev'], cw['w1k_flat'], cw['w1v_flat'], cw['w2k'], cw['w2v']]
    return pl.pallas_call(
        _cmp_finish_kernel,
        grid=(bsz,),
        in_specs=[blk, blk] + [full(w) for w in ws],
        out_specs=blk,
        out_shape=jax.ShapeDtypeStruct((bsz, nch, 4 * HEAD_DIM), jnp.float32),
        compiler_params=pltpu.CompilerParams(dimension_semantics=("arbitrary",), vmem_limit_bytes=VMEM_LIMIT_BYTES),
        name="cmp_finish",
    )(a, b, *ws)


def _compress_prompt(rows4, cw):
    bsz, seq = rows4.shape[0], rows4.shape[1] // 4
    nch = seq // CMP_STRIDE
    x_specs = [pl.BlockSpec((None, seq * 4, HEAD_DIM), lambda i: (i, 0, 0))]
    w_spec = pl.BlockSpec(cw['w1k_cat'].shape, lambda i: (0, 0, 0))
    out_spec = pl.BlockSpec((1, nch, 4 * HEAD_DIM), lambda i: (i, 0, 0))
    shp = jax.ShapeDtypeStruct((bsz, nch, 4 * HEAD_DIM), jnp.float32)
    a, b = pl.pallas_call(
        functools.partial(_cmp_partial_kernel, n_src=1, rows_per_src=seq),
        grid=(bsz,),
        in_specs=x_specs + [w_spec, w_spec],
        out_specs=[out_spec, out_spec],
        out_shape=[shp, shp],
        compiler_params=pltpu.CompilerParams(dimension_semantics=("arbitrary",), vmem_limit_bytes=VMEM_LIMIT_BYTES),
        name="cmp_partial_prompt",
    )(rows4, cw['w1k_cat'], cw['w1v_cat'])
    return _cmp_finish(a, b, cw)


_NT = (((1,), (1,)), ((), ()))
LOG2E = 1.4426950408889634
SEL_TK = 512
WIN_TK = 256


def _flash_update(s, v, m_ref, l_ref, acc_ref, h):
    tk = s.shape[1]
    m_prev = m_ref[h]
    m_new = jnp.maximum(m_prev, jnp.max(s, axis=-1, keepdims=True))
    alpha = jnp.exp2(m_prev - m_new)
    p = jnp.exp2(s - jnp.concatenate([m_new] * (tk // HEAD_DIM), axis=1))
    l_ref[h] = alpha * l_ref[h] + jnp.sum(p, axis=-1, keepdims=True)
    acc_ref[h] = alpha * acc_ref[h] + jnp.dot(p.astype(jnp.bfloat16), v, preferred_element_type=jnp.float32)
    m_ref[h] = m_new


def _nsa_prompt_kernel(q_ref, slc_ref, win_ref, gate_ref, kvc_ref, o_ref, m_ref, l_ref, acc_ref, *, tq, seq):
    f32, bf16 = jnp.float32, jnp.bfloat16
    qi = pl.program_id(1)
    t0 = qi * tq
    scale = HEAD_DIM ** -0.5
    nc_valid = seq // CMP_STRIDE - CMP_BLOCK // CMP_STRIDE + 1
    n_sel = seq // SEL_BLOCK
    pos = t0 + lax.broadcasted_iota(jnp.int32, (tq, 1), 0)
    lane = lax.broadcasted_iota(jnp.int32, (1, HEAD_DIM), 1)
    gates = jax.nn.sigmoid(gate_ref[...])
    dist_c = pos - (lane * CMP_STRIDE + (CMP_BLOCK - 1))
    valid_c = (dist_c >= 0) & (lane < nc_valid)
    dist_cf = dist_c.astype(f32)
    c_row = lax.broadcasted_iota(jnp.int32, (HEAD_DIM, 1), 0)
    overlap = jnp.where((c_row * CMP_STRIDE <= lane * SEL_BLOCK + (SEL_BLOCK - 1))
                        & (c_row * CMP_STRIDE + (CMP_BLOCK - 1) >= lane * SEL_BLOCK), 1.0, 0.0).astype(bf16)
    j_row = lax.broadcasted_iota(jnp.int32, (n_sel, 1), 0)
    blk_lane = jnp.right_shift(t0 + lax.broadcasted_iota(jnp.int32, (1, tq), 1), 6)
    forced_t = (j_row == 0) | (j_row == blk_lane) | (j_row == blk_lane - 1)

    def gate_col(branch, hh):
        c = branch * NSA_HEADS + hh
        return gates[:, c:c + 1]

    def reset():
        m_ref[...] = jnp.full(m_ref.shape, NEG_INF, f32)
        l_ref[...] = jnp.zeros(l_ref.shape, f32)
        acc_ref[...] = jnp.zeros(acc_ref.shape, f32)

    for g in range(NSA_KV_HEADS):
        kcol = slice(g * HEAD_DIM, (g + 1) * HEAD_DIM)
        vcol = slice((NSA_KV_HEADS + g) * HEAD_DIM, (NSA_KV_HEADS + g + 1) * HEAD_DIM)
        heads = [g * HPG + h for h in range(HPG)]
        slopes = [2.0 ** -(hh + 1) for hh in heads]

        kc = kvc_ref[0, :, kcol].astype(bf16)
        vc = kvc_ref[0, :, vcol].astype(bf16)
        psum = jnp.zeros((tq, HEAD_DIM), f32)
        for h, hh in enumerate(heads):
            qh = q_ref[:, hh * HEAD_DIM:(hh + 1) * HEAD_DIM].astype(bf16)
            s = lax.dot_general(qh, kc, _NT, preferred_element_type=f32) * scale - slopes[h] * dist_cf
            s = jnp.where(valid_c, s, NEG_INF)
            e = jnp.exp(s - jnp.max(s, axis=-1, keepdims=True))
            p = e / jnp.sum(e, axis=-1, keepdims=True)
            p = jnp.where(valid_c, p, 0.0)
            o_cmp = jnp.dot(p.astype(bf16), vc, preferred_element_type=f32)
            o_ref[:, hh * HEAD_DIM:(hh + 1) * HEAD_DIM] = gate_col(0, hh) * o_cmp
            psum = psum + p
        p_hi = psum.astype(bf16)
        p_lo = (psum - p_hi.astype(f32)).astype(bf16)
        imp = (jnp.dot(p_hi, overlap, preferred_element_type=f32)
               + jnp.dot(p_lo, overlap, preferred_element_type=f32))
        imp = imp.T[:n_sel]
        imp = jnp.where(forced_t, imp + FORCE_BONUS, imp)
        imp = jnp.where(j_row <= blk_lane, imp, NEG_INF)
        beaten = jnp.zeros((n_sel, tq), f32)
        for jp in range(n_sel):
            other = imp[jp:jp + 1]
            tie = jnp.where(j_row > jp, 1.0, 0.0)
            beaten = beaten + jnp.where(other > imp, 1.0, jnp.where(other == imp, tie, 0.0))
        sel_t = jnp.where(beaten < N_SELECT, jnp.where(imp > 0.5 * NEG_INF, 1.0, 0.0), 0.0).astype(bf16)

        reset()

        def sel_body(kt, carry):
            k0 = pl.multiple_of(kt * SEL_TK, SEL_TK)
            k = slc_ref[pl.ds(k0, SEL_TK), kcol].astype(bf16)
            v = slc_ref[pl.ds(k0, SEL_TK), vcol].astype(bf16)
            kpos = k0 + lax.broadcasted_iota(jnp.int32, (1, SEL_TK), 1)
            dist = pos - kpos
            expand = jnp.where(jnp.right_shift(kpos, 6) == j_row, 1.0, 0.0).astype(bf16)
            picked = lax.dot_general(sel_t, expand, (((0,), (0,)), ((), ())), preferred_element_type=f32)
            keep = jnp.where(dist >= 0, picked, 0.0) > 0.5
            kpos_f = (pos[0:1] - dist[0:1]).astype(f32)
            for h, hh in enumerate(heads):
                qh = q_ref[:, hh * HEAD_DIM:(hh + 1) * HEAD_DIM].astype(bf16)
                s = (lax.dot_general(qh, k, _NT, preferred_element_type=f32) * (scale * LOG2E)
                     + (slopes[h] * LOG2E) * kpos_f)
                _flash_update(jnp.where(keep, s, NEG_INF), v, m_ref, l_ref, acc_ref, h)
            return carry

        lax.fori_loop(0, (t0 + tq - 1) // SEL_TK + 1, sel_body, 0)
        for h, hh in enumerate(heads):
            hs = slice(hh * HEAD_DIM, (hh + 1) * HEAD_DIM)
            o_ref[:, hs] = o_ref[:, hs] + gate_col(1, hh) * (acc_ref[h] / l_ref[h])

        reset()

        def win_body(kt, carry):
            k0 = pl.multiple_of(kt * WIN_TK, WIN_TK)
            k = win_ref[pl.ds(k0, WIN_TK), kcol].astype(bf16)
            v = win_ref[pl.ds(k0, WIN_TK), vcol].astype(bf16)
            dist = pos - (k0 + lax.broadcasted_iota(jnp.int32, (1, WIN_TK), 1))
            keep = (dist >= 0) & (dist < WINDOW)
            kpos_f = (pos[0:1] - dist[0:1]).astype(f32)
            for h, hh in enumerate(heads):
                qh = q_ref[:, hh * HEAD_DIM:(hh + 1) * HEAD_DIM].astype(bf16)
                s = (lax.dot_general(qh, k, _NT, preferred_element_type=f32) * (scale * LOG2E)
                     + (slopes[h] * LOG2E) * kpos_f)
                _flash_update(jnp.where(keep, s, NEG_INF), v, m_ref, l_ref, acc_ref, h)
            return carry

        lax.fori_loop(jnp.maximum(t0 - (WINDOW - 1), 0) // WIN_TK, (t0 + tq - 1) // WIN_TK + 1, win_body, 0)
        for h, hh in enumerate(heads):
            hs = slice(hh * HEAD_DIM, (hh + 1) * HEAD_DIM)
            o_ref[:, hs] = o_ref[:, hs] + gate_col(2, hh) * (acc_ref[h] / l_ref[h])


def _nsa_prompt(proj2d, kvc, bsz, seq, tq=256):
    nq = seq // tq
    kvw = 2 * KV_WIDTH
    return pl.pallas_call(
        functools.partial(_nsa_prompt_kernel, tq=tq, seq=seq),
        grid=(bsz, nq),
        in_specs=[pl.BlockSpec((tq, NSA_WIDTH), lambda b, i: (b * nq + i, 0)),
                  pl.BlockSpec((seq, kvw), lambda b, i: (b, SLC_OFF // kvw)),
                  pl.BlockSpec((seq, kvw), lambda b, i: (b, WIN_OFF // kvw)),
                  pl.BlockSpec((tq, HEAD_DIM), lambda b, i: (b * nq + i, GATE_OFF // HEAD_DIM)),
                  pl.BlockSpec((1, seq // CMP_STRIDE, kvw), lambda b, i: (b, 0, 0))],
        out_specs=pl.BlockSpec((tq, NSA_WIDTH), lambda b, i: (b * nq + i, 0)),
        out_shape=jax.ShapeDtypeStruct((bsz * seq, NSA_WIDTH), jnp.float32),
        scratch_shapes=[pltpu.VMEM((HPG, tq, HEAD_DIM), jnp.float32),
                        pltpu.VMEM((HPG, tq, HEAD_DIM), jnp.float32),
                        pltpu.VMEM((HPG, tq, HEAD_DIM), jnp.float32)],
        compiler_params=pltpu.CompilerParams(
            dimension_semantics=("arbitrary", "arbitrary"), vmem_limit_bytes=VMEM_LIMIT_BYTES),
        name="nsa_prompt",
    )(proj2d, proj2d, proj2d, proj2d, kvc)


CMP_PAGES_PER_STEP = 32
TOPK_LANES = 384
IDX_LANES = 128


def _cmp_partial_paged_kernel(pt_ref, *refs, n_src, rows_per_src):
    del pt_ref
    _cmp_partial_kernel(*refs, n_src=n_src, rows_per_src=rows_per_src)


def _compress_paged(pool, page_table, cw):
    bsz, n_pages = page_table.shape
    nps = CMP_PAGES_PER_STEP
    n_tiles = n_pages // nps
    nch_tile = nps * PAGE_SIZE // CMP_STRIDE

    def page_spec(s):
        return pl.BlockSpec((PAGE_SIZE * 4, HEAD_DIM), lambda b, i, pt: (pt[b * n_pages + i * nps + s], 0))

    x_specs = [page_spec(s) for s in range(nps)]
    w_spec = pl.BlockSpec(cw['w1k_cat'].shape, lambda b, i, pt: (0, 0, 0))
    out_spec = pl.BlockSpec((1, nch_tile, 4 * HEAD_DIM), lambda b, i, pt: (b, i, 0))
    shp = jax.ShapeDtypeStruct((bsz, n_tiles * nch_tile, 4 * HEAD_DIM), jnp.float32)
    a, b = pl.pallas_call(
        functools.partial(_cmp_partial_paged_kernel, n_src=nps, rows_per_src=PAGE_SIZE),
        grid_spec=pltpu.PrefetchScalarGridSpec(
            num_scalar_prefetch=1, grid=(bsz, n_tiles),
            in_specs=x_specs + [w_spec, w_spec], out_specs=[out_spec, out_spec]),
        out_shape=[shp, shp],
        compiler_params=pltpu.CompilerParams(
            dimension_semantics=("arbitrary", "arbitrary"), vmem_limit_bytes=VMEM_LIMIT_BYTES),
        name="cmp_partial_paged",
    )(page_table.reshape(-1), *([pool] * nps), cw['w1k_cat'], cw['w1v_cat'])
    return _cmp_finish(a, b, cw)


def _nsa_decode_a_kernel(proj_ref, kvc_ref, win_ref, o_ref, gsel_ref, idx_ref, *, t_new, n_past):
    f32, bf16 = jnp.float32, jnp.bfloat16
    scale = HEAD_DIM ** -0.5
    nch = kvc_ref.shape[1]
    n_win = win_ref.shape[1] // KV_PARTS
    rows = HPG * t_new
    r_iota = lax.broadcasted_iota(jnp.int32, (rows, 1), 0)
    t_row = r_iota % t_new
    h_row = r_iota // t_new
    pos_row = n_past + t_row
    gates = jax.nn.sigmoid(proj_ref[0, :, GATE_OFF:IN_WIDTH])
    c_lane = lax.broadcasted_iota(jnp.int32, (1, nch), 1)
    dist_c = pos_row - (c_lane * CMP_STRIDE + (CMP_BLOCK - 1))
    valid_c = (dist_c >= 0) & (c_lane < nch - 1)
    c_col = lax.broadcasted_iota(jnp.int32, (nch, 1), 0)
    j_lane = lax.broadcasted_iota(jnp.int32, (1, TOPK_LANES), 1)
    overlap = jnp.where((c_col * CMP_STRIDE <= j_lane * SEL_BLOCK + (SEL_BLOCK - 1))
                        & (c_col * CMP_STRIDE + (CMP_BLOCK - 1) >= j_lane * SEL_BLOCK), 1.0, 0.0).astype(bf16)
    pos_t = n_past + lax.broadcasted_iota(jnp.int32, (t_new, 1), 0)
    blk_t = pos_t // SEL_BLOCK
    forced = (j_lane == 0) | (j_lane == blk_t) | (j_lane == blk_t - 1)
    j_f = j_lane.astype(f32)
    k_lane = lax.broadcasted_iota(jnp.int32, (1, IDX_LANES), 1)
    i_win = lax.broadcasted_iota(jnp.int32, (1, n_win), 1)
    dist_w = pos_row - (n_past - n_win + i_win)
    keep_w = (dist_w >= 0) & (dist_w < WINDOW)
    j_new = lax.broadcasted_iota(jnp.int32, (1, 8), 1)
    dist_n = t_row - j_new
    keep_n = (dist_n >= 0) & (j_new < t_new)
    zpad = jnp.zeros((8 - t_new, HEAD_DIM), f32)

    for g in range(NSA_KV_HEADS):
        kcol = slice(g * HEAD_DIM, (g + 1) * HEAD_DIM)
        vcol = slice((NSA_KV_HEADS + g) * HEAD_DIM, (NSA_KV_HEADS + g + 1) * HEAD_DIM)
        heads = [g * HPG + h for h in range(HPG)]
        slope_row = jnp.zeros((rows, 1), f32)
        for h, hh in enumerate(heads):
            slope_row = jnp.where(h_row == h, 2.0 ** -(hh + 1), slope_row)
        q = jnp.concatenate([proj_ref[0, :, hh * HEAD_DIM:(hh + 1) * HEAD_DIM] for hh in heads], axis=0).astype(bf16)

        kc = kvc_ref[0, :, kcol].astype(bf16)
        vc = kvc_ref[0, :, vcol].astype(bf16)
        s = lax.dot_general(q, kc, _NT, preferred_element_type=f32) * scale - slope_row * dist_c.astype(f32)
        s = jnp.where(valid_c, s, NEG_INF)
        e = jnp.exp(s - jnp.max(s, axis=-1, keepdims=True))
        p = e / jnp.sum(e, axis=-1, keepdims=True)
        p = jnp.where(valid_c, p, 0.0)
        o_cmp = jnp.dot(p.astype(bf16), vc, preferred_element_type=f32)
        psum = p[0:t_new]
        for h in range(1, HPG):
            psum = psum + p[h * t_new:(h + 1) * t_new]

        p_hi = psum.astype(bf16)
        p_lo = (psum - p_hi.astype(f32)).astype(bf16)
        imp = (jnp.dot(p_hi, overlap, preferred_element_type=f32)
               + jnp.dot(p_lo, overlap, preferred_element_type=f32))
        imp = jnp.where(forced, imp + FORCE_BONUS, imp)
        imp = jnp.where(j_lane <= blk_t, imp, NEG_INF)
        picked = jnp.full((t_new, IDX_LANES), -1.0, f32)
        for k in range(N_SELECT):
            best = jnp.max(imp, axis=-1, keepdims=True)
            first = jnp.min(jnp.where(imp == best, j_f, 1e9), axis=-1, keepdims=True)
            picked = jnp.where(k_lane == k, jnp.where(best > 0.5 * NEG_INF, first, -1.0), picked)
            imp = jnp.where(j_f == first, -3e38, imp)
        idx_ref[0, g * t_new:(g + 1) * t_new, :] = picked.astype(jnp.int32)

        kw = win_ref[0, pl.ds(g, n_win, stride=KV_PARTS), :].astype(bf16)
        vw = win_ref[0, pl.ds(NSA_KV_HEADS + g, n_win, stride=KV_PARTS), :].astype(bf16)
        kn = jnp.concatenate([proj_ref[0, :, WIN_OFF + g * HEAD_DIM:WIN_OFF + (g + 1) * HEAD_DIM], zpad], axis=0)
        vn = jnp.concatenate([proj_ref[0, :, WIN_OFF + KV_WIDTH + g * HEAD_DIM:
                                       WIN_OFF + KV_WIDTH + (g + 1) * HEAD_DIM], zpad], axis=0)
        s_w = lax.dot_general(q, kw, _NT, preferred_element_type=f32) * scale - slope_row * dist_w.astype(f32)
        s_n = (lax.dot_general(q, kn.astype(bf16), _NT, preferred_element_type=f32) * scale
               - slope_row * dist_n.astype(f32))
        s_w = jnp.where(keep_w, s_w, NEG_INF)
        s_n = jnp.where(keep_n, s_n, NEG_INF)
        m = jnp.maximum(jnp.max(s_w, axis=-1, keepdims=True), jnp.max(s_n, axis=-1, keepdims=True))
        e_w = jnp.exp(s_w - m)
        e_n = jnp.exp(s_n - m)
        den = jnp.sum(e_w, axis=-1, keepdims=True) + jnp.sum(e_n, axis=-1, keepdims=True)
        o_win = (jnp.dot(e_w.astype(bf16), vw, preferred_element_type=f32)
                 + jnp.dot(e_n.astype(bf16), vn.astype(bf16), preferred_element_type=f32)) / den

        for h, hh in enumerate(heads):
            rs = slice(h * t_new, (h + 1) * t_new)
            hs = slice(hh * HEAD_DIM, (hh + 1) * HEAD_DIM)
            o_ref[0, :, hs] = (gates[:, hh:hh + 1] * o_cmp[rs]
                               + gates[:, 2 * NSA_HEADS + hh:2 * NSA_HEADS + hh + 1] * o_win[rs])
            gsel_ref[0, :, hs] = jnp.broadcast_to(gates[:, NSA_HEADS + hh:NSA_HEADS + hh + 1], (t_new, HEAD_DIM))


def _nsa_decode_sel_kernel(idx_ref, pt_ref, q_ref, part_ref, gsel_ref, new_ref, *refs, t_new, n_past):
    del pt_ref
    f32, bf16 = jnp.float32, jnp.bfloat16
    o_ref = refs[NSA_KV_HEADS * N_SELECT]
    b, t = pl.program_id(0), pl.program_id(1)
    scale = HEAD_DIM ** -0.5
    n_past_blk = n_past // SEL_BLOCK
    n_keys = N_SELECT * SEL_BLOCK
    lane = lax.broadcasted_iota(jnp.int32, (1, n_keys), 1)
    slot = lane // SEL_BLOCK
    h_row = lax.broadcasted_iota(jnp.int32, (8, 1), 0)
    for g in range(NSA_KV_HEADS):
        base = ((b * NSA_KV_HEADS + g) * t_new + t) * N_SELECT
        blk_of_lane = jnp.full((1, n_keys), -1, jnp.int32)
        ks, vs = [], []
        for k in range(N_SELECT):
            blk = idx_ref[base + k]
            blk_of_lane = jnp.where(slot == k, blk, blk_of_lane)
            is_new = blk >= n_past_blk
            src = refs[g * N_SELECT + k]
            k_old = src[pl.ds(g, SEL_BLOCK, stride=KV_PARTS), :]
            v_old = src[pl.ds(NSA_KV_HEADS + g, SEL_BLOCK, stride=KV_PARTS), :]
            k_new = new_ref[:, g * HEAD_DIM:(g + 1) * HEAD_DIM]
            v_new = new_ref[:, KV_WIDTH + g * HEAD_DIM:KV_WIDTH + (g + 1) * HEAD_DIM]
            ks.append(jnp.where(is_new, k_new, k_old).astype(bf16))
            vs.append(jnp.where(is_new, v_new, v_old).astype(bf16))
        k_all = jnp.concatenate(ks, axis=0)
        v_all = jnp.concatenate(vs, axis=0)
        dist = (n_past + t) - (blk_of_lane * SEL_BLOCK + lane % SEL_BLOCK)
        keep = (dist >= 0) & (blk_of_lane >= 0)
        q = jnp.concatenate([q_ref[:, (g * HPG + h) * HEAD_DIM:(g * HPG + h + 1) * HEAD_DIM] for h in range(HPG)]
                            + [jnp.zeros((8 - HPG, HEAD_DIM), f32)], axis=0).astype(bf16)
        slope_row = jnp.zeros((8, 1), f32)
        for h in range(HPG):
            slope_row = jnp.where(h_row == h, 2.0 ** -(g * HPG + h + 1), slope_row)
        s = lax.dot_general(q, k_all, _NT, preferred_element_type=f32) * scale - slope_row * dist.astype(f32)
        s = jnp.where(keep, s, NEG_INF)
        e = jnp.exp(s - jnp.max(s, axis=-1, keepdims=True))
        p = e / jnp.sum(e, axis=-1, keepdims=True)
        o_sel = jnp.dot(p.astype(bf16), v_all, preferred_element_type=f32)
        for h in range(HPG):
            hs = slice((g * HPG + h) * HEAD_DIM, (g * HPG + h + 1) * HEAD_DIM)
            o_ref[:, hs] = part_ref[:, hs] + gsel_ref[:, hs] * o_sel[h:h + 1]


def _nsa_decode_rows(proj3, kvc, win_rows, pool_rows, page_table, n_past):
    bsz, t_new, _ = proj3.shape
    n_pages = page_table.shape[1]
    full = jax.ShapeDtypeStruct((bsz, t_new, NSA_WIDTH), jnp.float32)
    part, gsel, idx = pl.pallas_call(
        functools.partial(_nsa_decode_a_kernel, t_new=t_new, n_past=n_past),
        grid=(bsz,),
        in_specs=[pl.BlockSpec((1, t_new, IN_WIDTH), lambda b: (b, 0, 0)),
                  pl.BlockSpec((1,) + kvc.shape[1:], lambda b: (b, 0, 0)),
                  pl.BlockSpec((1,) + win_rows.shape[1:], lambda b: (b, 0, 0))],
        out_specs=[pl.BlockSpec((1, t_new, NSA_WIDTH), lambda b: (b, 0, 0)),
                   pl.BlockSpec((1, t_new, NSA_WIDTH), lambda b: (b, 0, 0)),
                   pl.BlockSpec((1, NSA_KV_HEADS * t_new, IDX_LANES), lambda b: (b, 0, 0))],
        out_shape=[full, full, jax.ShapeDtypeStruct((bsz, NSA_KV_HEADS * t_new, IDX_LANES), jnp.int32)],
        compiler_params=pltpu.CompilerParams(dimension_semantics=("arbitrary",), vmem_limit_bytes=VMEM_LIMIT_BYTES),
        name="nsa_decode_a",
    )(proj3, kvc, win_rows)

    n_past_blk = n_past // SEL_BLOCK
    sub = PAGE_SIZE // SEL_BLOCK
    new_rows = jnp.pad(proj3[:, :, SLC_OFF:WIN_OFF], ((0, 0), (0, SEL_BLOCK - t_new), (0, 0)))

    picked = idx[:, :, :N_SELECT]
    past = jnp.clip(picked, 0, n_past_blk - 1)
    pool_blk = jnp.take_along_axis(page_table[:, None, :], past // sub, axis=2) * sub + past % sub

    def pool_spec(g, k):
        def index(b, t, idx, blk):
            return (blk[((b * NSA_KV_HEADS + g) * t_new + t) * N_SELECT + k], 0)
        return pl.BlockSpec((SEL_BLOCK * KV_PARTS, HEAD_DIM), index)

    row_spec = pl.BlockSpec((None, 1, NSA_WIDTH), lambda b, t, idx, pt: (b * t_new + t, 0, 0))
    out = pl.pallas_call(
        functools.partial(_nsa_decode_sel_kernel, t_new=t_new, n_past=n_past),
        grid_spec=pltpu.PrefetchScalarGridSpec(
            num_scalar_prefetch=2, grid=(bsz, t_new),
            in_specs=[row_spec, row_spec, row_spec,
                      pl.BlockSpec((None, SEL_BLOCK, 2 * KV_WIDTH), lambda b, t, idx, pt: (b, 0, 0))]
            + [pool_spec(g, k) for g in range(NSA_KV_HEADS) for k in range(N_SELECT)],
            out_specs=row_spec),
        out_shape=jax.ShapeDtypeStruct((bsz * t_new, 1, NSA_WIDTH), jnp.float32),
        compiler_params=pltpu.CompilerParams(
            dimension_semantics=("arbitrary", "arbitrary"), vmem_limit_bytes=VMEM_LIMIT_BYTES),
        name="nsa_decode_sel",
    )(picked.reshape(-1), pool_blk.reshape(-1).astype(jnp.int32),
      proj3[:, :, Q_OFF:Q_OFF + NSA_WIDTH].reshape(bsz * t_new, 1, NSA_WIDTH),
      part.reshape(bsz * t_new, 1, NSA_WIDTH), gsel.reshape(bsz * t_new, 1, NSA_WIDTH),
      new_rows, *([pool_rows] * (NSA_KV_HEADS * N_SELECT)))
    return out.reshape(bsz, t_new, NSA_WIDTH)


WKV_LANES = 2 * RWKV_HEAD_DIM
WKV_PAIRS = RWKV_HEADS // 2
WKV_STACK = 4
WKV_BB = 4
WKV_CHUNK = 64
WKV_MIN_CHUNK = 16
WKV_FLUSH = RWKV_HEAD_DIM


def _wkv_kernel(wr_ref, w_ref, k_ref, v_ref, kk_ref, kka_ref, c1_ref, c2_ref, s0_ref, y_ref, st_ref,
                s_scr, y_scr, *, tc):
    f32, bf16 = jnp.float32, jnp.bfloat16
    ti = pl.program_id(1)
    hd = RWKV_HEAD_DIM
    n_tiles = WKV_BB * WKV_PAIRS
    n_stacks = n_tiles // WKV_STACK
    tile = lambda q: (q // WKV_PAIRS, q % WKV_PAIRS)

    @pl.when(ti == 0)
    def _():
        for q in range(n_tiles):
            b, p = tile(q)
            s_scr[q] = jnp.concatenate([s0_ref[b, 2 * p], s0_ref[b, 2 * p + 1]], axis=1)

    lane = lax.broadcasted_iota(jnp.int32, (1, WKV_LANES), 1)
    r2 = lax.broadcasted_iota(jnp.int32, (2 * WKV_LANES, 1), 0)
    c2 = lax.broadcasted_iota(jnp.int32, (1, 2 * WKV_LANES), 1)
    same_head2 = jnp.where(r2 // hd == c2 // hd, 1.0, 0.0).astype(bf16)
    on_diag = lax.broadcasted_iota(jnp.int32, (hd, 1), 0) == lane % hd
    n_flush = min(tc, WKV_FLUSH)
    y_scr[...] = jnp.zeros(y_scr.shape, f32)
    pairs = [(2 * i, 2 * i + 1) for i in range(n_stacks // 2)]
    stack_tiles = lambda st: range(st * WKV_STACK, (st + 1) * WKV_STACK)

    def row_sums(per_tile, s0, s1):
        lhs = jnp.concatenate([jnp.concatenate([per_tile[q] for q in stack_tiles(st)], axis=0)
                               for st in (s0, s1)], axis=1)
        res = jnp.dot(lhs.astype(bf16), same_head2, preferred_element_type=f32)
        return {q: res[n * hd:(n + 1) * hd, half * WKV_LANES:(half + 1) * WKV_LANES]
                for half, st in enumerate((s0, s1)) for n, q in enumerate(stack_tiles(st))}

    def step(t, carry):
        here = (lane % hd) == (t % n_flush)
        get = lambda ref, q: ref[tile(q)[0], tile(q)[1], pl.ds(t, 1), :]
        for s0, s1 in pairs:
            tiles = list(stack_tiles(s0)) + list(stack_tiles(s1))
            s_old = {q: s_scr[q] for q in tiles}
            sa = row_sums({q: s_old[q] * get(kk_ref, q) for q in tiles}, s0, s1)
            y_old = row_sums({q: s_old[q] * get(wr_ref, q) for q in tiles}, s0, s1)
            v_col = row_sums({q: jnp.where(on_diag, get(v_ref, q), 0.0) for q in tiles}, s0, s1)
            for q in tiles:
                s_scr[q] = s_old[q] * get(w_ref, q) - sa[q] * get(kka_ref, q) + v_col[q] * get(k_ref, q)
                y_col = y_old[q] - sa[q] * get(c1_ref, q) + v_col[q] * get(c2_ref, q)
                y_scr[q] = jnp.where(here, y_col, y_scr[q])
        return carry

    for sub in range(tc // n_flush):
        lax.fori_loop(sub * n_flush, (sub + 1) * n_flush, step, 0)
        for q in range(n_tiles):
            b, p = tile(q)
            yt = y_scr[q].T
            y_ref[b, p, sub * n_flush:(sub + 1) * n_flush, :] = jnp.concatenate(
                [yt[:n_flush], yt[hd:hd + n_flush]], axis=1)

    @pl.when(ti == pl.num_programs(1) - 1)
    def _():
        for q in range(n_tiles):
            b, p = tile(q)
            st_ref[b, 2 * p] = s_scr[q][:, :hd]
            st_ref[b, 2 * p + 1] = s_scr[q][:, hd:]


def _wkv_scan(wr, w, k, v, kk, kka, c1, c2, s0, tc):
    bsz, n_pairs, seq, _ = wr.shape
    assert n_pairs == WKV_PAIRS and bsz % WKV_BB == 0 and seq % tc == 0
    n_tiles = WKV_BB * WKV_PAIRS
    x_spec = pl.BlockSpec((WKV_BB, WKV_PAIRS, tc, WKV_LANES), lambda b, i: (b, 0, i, 0))
    s_spec = pl.BlockSpec((WKV_BB, RWKV_HEADS, RWKV_HEAD_DIM, RWKV_HEAD_DIM), lambda b, i: (b, 0, 0, 0))
    return pl.pallas_call(
        functools.partial(_wkv_kernel, tc=tc),
        grid=(bsz // WKV_BB, seq // tc),
        in_specs=[x_spec] * 8 + [s_spec],
        out_specs=[x_spec, s_spec],
        out_shape=[jax.ShapeDtypeStruct(wr.shape, jnp.float32),
                   jax.ShapeDtypeStruct(s0.shape, jnp.float32)],
        scratch_shapes=[pltpu.VMEM((n_tiles, RWKV_HEAD_DIM, WKV_LANES), jnp.float32),
                        pltpu.VMEM((n_tiles, RWKV_HEAD_DIM, WKV_LANES), jnp.float32)],
        compiler_params=pltpu.CompilerParams(
            dimension_semantics=("arbitrary", "arbitrary"), vmem_limit_bytes=VMEM_LIMIT_BYTES),
        name="wkv_scan",
    )(wr, w, k, v, kk, kka, c1, c2, s0)


RW_PACKS = RWKV_WIDTH // WKV_LANES


def _head_sums(x, ones2):
    f32, bf16 = jnp.float32, jnp.bfloat16
    hi = x.astype(bf16)
    lo = (x - hi.astype(f32)).astype(bf16)
    w = 2 * WKV_LANES
    out = []
    for c in range(RWKV_WIDTH // w):
        sl = slice(c * w, (c + 1) * w)
        out.append(jnp.dot(hi[:, sl], ones2, preferred_element_type=f32)
                   + jnp.dot(lo[:, sl], ones2, preferred_element_type=f32))
    return jnp.concatenate(out, axis=1)


def _head_ones():
    w = 2 * WKV_LANES
    r = lax.broadcasted_iota(jnp.int32, (w, 1), 0)
    c = lax.broadcasted_iota(jnp.int32, (1, w), 1)
    return jnp.where(r // RWKV_HEAD_DIM == c // RWKV_HEAD_DIM, 1.0, 0.0).astype(jnp.bfloat16)


def _store_rw(ref, val, pack_major):
    if pack_major:
        for p in range(RW_PACKS):
            ref[0, p] = val[:, p * WKV_LANES:(p + 1) * WKV_LANES]
    else:
        ref[0] = val


def _rwkv_prep_kernel(h_ref, hprev_ref, h0_ref, pr_ref, pk_ref, pv_ref, pprev_r, pprev_k, pprev_v, p0_ref,
                      mu_rkv_ref, mu_wag_ref, dw0_ref, dw1_ref, dw2_ref, a0_ref, a1_ref, a2_ref, g1_ref, g2_ref,
                      kk_w_ref, ka_w_ref, rk_w_ref,
                      wr_out, w_out, k_out, v_out, kk_out, kka_out, c1_out, c2_out, g_out, bonus_out,
                      *, period, pack_major):
    f32, bf16 = jnp.float32, jnp.bfloat16
    i = pl.program_id(1)
    tm = h_ref.shape[1]
    row = lax.broadcasted_iota(jnp.int32, (tm, 1), 0)
    per_row_first = h0_ref.shape[1] != 1
    first = (row % period == 0) if per_row_first else None

    def shifted(cur, prev_blk, first_rows):
        rolled = pltpu.roll(cur, 1, 0)
        if per_row_first:
            return jnp.where(first, first_rows, rolled)
        row0 = jnp.where(i == 0, first_rows, prev_blk[7:8])
        return jnp.where(row == 0, row0, rolled)

    h = h_ref[0]
    xx = shifted(h, hprev_ref[0], h0_ref[0]) - h
    xw = (h + xx * mu_wag_ref[0:1]).astype(bf16)
    xa = (h + xx * mu_wag_ref[1:2]).astype(bf16)
    xg = (h + xx * mu_wag_ref[2:3]).astype(bf16)
    dmid = jnp.tanh(jnp.dot(xw, dw1_ref[...], preferred_element_type=f32))
    dlin = dw0_ref[...] + jnp.dot(dmid.astype(bf16), dw2_ref[...], preferred_element_type=f32)
    z = -dlin
    w_log = -(jnp.maximum(z, 0.0) + jnp.log(1.0 + jnp.exp(-jnp.abs(z)))) - 0.5
    decay = jnp.exp(-jnp.exp(w_log))
    amid = jnp.dot(xa, a1_ref[...], preferred_element_type=f32)
    a = jax.nn.sigmoid(a0_ref[...] + jnp.dot(amid.astype(bf16), a2_ref[...], preferred_element_type=f32))
    gmid = jax.nn.sigmoid(jnp.dot(xg, g1_ref[...], preferred_element_type=f32))
    g = jnp.dot(gmid.astype(bf16), g2_ref[...], preferred_element_type=f32)

    def mixed(cur_ref, prev_ref, n):
        cur = cur_ref[0]
        cs = slice(n * RWKV_WIDTH, (n + 1) * RWKV_WIDTH)
        prev = shifted(cur, prev_ref[0], p0_ref[0][:, cs])
        return cur + mu_rkv_ref[:, cs] * (prev - cur)

    r = mixed(pr_ref, pprev_r, 0)
    k = mixed(pk_ref, pprev_k, 1)
    v = mixed(pv_ref, pprev_v, 2)
    ones2 = _head_ones()
    kk = k * kk_w_ref[...]
    kk = kk / jnp.maximum(jnp.sqrt(_head_sums(kk * kk, ones2)), 1e-12)
    k = k * (1.0 + (a - 1.0) * ka_w_ref[...])
    bonus = _head_sums(r * k * rk_w_ref[...], ones2) * v
    _store_rw(wr_out, decay * r, pack_major)
    _store_rw(c1_out, _head_sums(kk * a * r, ones2), pack_major)
    _store_rw(c2_out, _head_sums(k * r, ones2), pack_major)
    _store_rw(w_out, decay, pack_major)
    _store_rw(k_out, k, pack_major)
    _store_rw(v_out, v, pack_major)
    _store_rw(kk_out, kk, pack_major)
    _store_rw(kka_out, kk * a, pack_major)
    g_out[0] = g
    bonus_out[0] = bonus


def _rwkv_prep(h, proj, h0, p0, lp, tm, period, pack_major):
    f32, bf16 = jnp.float32, jnp.bfloat16
    g, t, d = h.shape
    rw = RWKV_WIDTH
    nb = tm // 8
    cur = lambda w, c: pl.BlockSpec((1, tm, w), lambda b, i: (b, i, c))
    prev = lambda w, c: pl.BlockSpec((1, 8, w), lambda b, i: (b, jnp.maximum(i * nb - 1, 0), c))
    per_row = h0.shape[1] != 1
    carry = lambda w: pl.BlockSpec((1, tm if per_row else 1, w), (lambda b, i: (b, i, 0)) if per_row else (lambda b, i: (b, 0, 0)))
    full = lambda a: pl.BlockSpec(a.shape, lambda b, i: (0,) * a.ndim)
    c0 = RKV_OFF // rw
    ws = [lp['mu_rkv'].reshape(1, 3 * rw), lp['mu_wag'], lp['decay_w0'].reshape(1, rw), lp['decay_w1'].astype(bf16),
          lp['decay_w2'].astype(bf16), lp['iclr_a0'].reshape(1, rw), lp['iclr_a1'].astype(bf16),
          lp['iclr_a2'].astype(bf16), lp['gate_g1'].astype(bf16), lp['gate_g2'].astype(bf16),
          lp['k_k'].reshape(1, rw), lp['k_a'].reshape(1, rw), lp['r_k'].reshape(1, rw)]
    if pack_major:
        seq_shape = jax.ShapeDtypeStruct((g, RW_PACKS, t, WKV_LANES), f32)
        seq_spec = pl.BlockSpec((1, RW_PACKS, tm, WKV_LANES), lambda b, i: (b, 0, i, 0))
    else:
        seq_shape = jax.ShapeDtypeStruct((g, t, rw), f32)
        seq_spec = cur(rw, 0)
    flat_shape = jax.ShapeDtypeStruct((g, t, rw), f32)
    return pl.pallas_call(
        functools.partial(_rwkv_prep_kernel, period=period, pack_major=pack_major),
        grid=(g, t // tm),
        in_specs=[cur(d, 0), prev(d, 0), carry(d), cur(rw, c0), cur(rw, c0 + 1), cur(rw, c0 + 2),
                  prev(rw, c0), prev(rw, c0 + 1), prev(rw, c0 + 2), carry(3 * rw)] + [full(w) for w in ws],
        out_specs=[seq_spec] * 8 + [cur(rw, 0), cur(rw, 0)],
        out_shape=[seq_shape] * 8 + [flat_shape, flat_shape],
        compiler_params=pltpu.CompilerParams(
            dimension_semantics=("arbitrary", "arbitrary"), vmem_limit_bytes=VMEM_LIMIT_BYTES),
        name="rwkv_prep",
    )(h, h, h0, proj, proj, proj, proj, proj, proj, p0, *ws)


def _rwkv_post_kernel(y_ref, g_ref, bonus_ref, lnw_ref, lnb_ref, o_ref, *, pack_major):
    if pack_major:
        y = jnp.concatenate([y_ref[0, p] for p in range(RW_PACKS)], axis=1)
    else:
        y = y_ref[0]
    ones2 = _head_ones()
    inv = 1.0 / RWKV_HEAD_DIM
    mu = _head_sums(y, ones2) * inv
    dev = y - mu
    var = _head_sums(dev * dev, ones2) * inv
    yn = dev * lax.rsqrt(var + GN_EPS) * lnw_ref[...] + lnb_ref[...]
    o_ref[0] = (yn + bonus_ref[0]) * g_ref[0]


def _rwkv_post(y, g, bonus, ln_w, ln_b, tm, pack_major):
    gsz, t, rw = g.shape
    flat = pl.BlockSpec((1, tm, rw), lambda b, i: (b, i, 0))
    y_spec = pl.BlockSpec((1, RW_PACKS, tm, WKV_LANES), lambda b, i: (b, 0, i, 0)) if pack_major else flat
    vec = pl.BlockSpec((1, rw), lambda b, i: (0, 0))
    return pl.pallas_call(
        functools.partial(_rwkv_post_kernel, pack_major=pack_major),
        grid=(gsz, t // tm),
        in_specs=[y_spec, flat, flat, vec, vec],
        out_specs=flat,
        out_shape=jax.ShapeDtypeStruct((gsz, t, rw), jnp.float32),
        compiler_params=pltpu.CompilerParams(
            dimension_semantics=("arbitrary", "arbitrary"), vmem_limit_bytes=VMEM_LIMIT_BYTES),
        name="rwkv_post",
    )(y, g, bonus, ln_w.reshape(1, rw), ln_b.reshape(1, rw))


def _hier_route(logits):
    assert EXPERT_TOP_K == 2
    n = logits.shape[0]
    pg = jax.nn.softmax(logits[:, :N_GROUPS], axis=-1)

    def key(x):
        bits = lax.bitcast_convert_type(x, jnp.int32)
        return jnp.where(bits < 0, bits ^ jnp.int32(0x7FFFFFFF), bits)

    g_sel = jnp.argmax(key(pg), axis=-1)[:, None]
    g_val = jnp.take_along_axis(pg, g_sel, axis=1)
    le = logits[:, N_GROUPS:N_GROUPS + N_EXPERTS].reshape(n, N_GROUPS, EXPERTS_PER_GROUP)
    le_g = jnp.take_along_axis(le, g_sel[:, :, None], axis=1)[:, 0]
    le_key = key(le_g)
    e1 = jnp.argmax(le_key, axis=-1)[:, None]
    rest = jnp.where(jnp.arange(EXPERTS_PER_GROUP)[None, :] == e1, jnp.iinfo(jnp.int32).min, le_key)
    e2 = jnp.argmax(rest, axis=-1)[:, None]
    e_sel = jnp.concatenate([e1, e2], axis=1)
    e_val = jnp.take_along_axis(le_g, e_sel, axis=1)
    weights = jax.nn.softmax(e_val, axis=-1) * g_val
    return (g_sel * EXPERTS_PER_GROUP + e_sel).astype(jnp.int32), weights


def _moe_ffn(h, eid, w_gate, w_up, w_down, tm):
    n, d = h.shape
    a_tot = n * EXPERT_TOP_K
    flat_e = eid.reshape(-1)
    onehot = (flat_e[:, None] == jnp.arange(N_EXPERTS)[None, :]).astype(jnp.int32)
    csum = jnp.cumsum(onehot, axis=0)
    rank = jnp.take_along_axis(csum, flat_e[:, None], axis=1)[:, 0] - 1
    counts = csum[-1]
    padded = (counts + tm - 1) // tm * tm
    pad_end = jnp.cumsum(padded)
    dest = (pad_end - padded)[flat_e] + rank
    n_blk = (a_tot + N_EXPERTS * (tm - 1)) // tm
    tok_buf = jnp.zeros((n_blk * tm,), jnp.int32).at[dest].set(jnp.arange(a_tot, dtype=jnp.int32) // EXPERT_TOP_K)
    xb = h[tok_buf].reshape(n_blk, tm, d)
    blk_start = jnp.arange(n_blk, dtype=jnp.int32) * tm
    blk_e = jnp.minimum(jnp.sum((pad_end[None, :] <= blk_start[:, None]).astype(jnp.int32), axis=1), N_EXPERTS - 1)
    meta = jnp.concatenate([blk_e, pad_end[-1:] // tm]).astype(jnp.int32)
    yb = _moe_blocks(xb, meta, w_gate, w_up, w_down).reshape(n_blk * tm, d)
    dest2 = dest.reshape(n, EXPERT_TOP_K)
    return yb[dest2[:, 0]], yb[dest2[:, 1]]


def _layer_front(x, mod, lp, layer, pool_cmp, pool_slc, page_table, win_buf, wkv0, shift0, past_len, rows):
    B, T, D = x.shape
    groups = B * T // rows
    per_token = rows > T
    mods = jnp.repeat(mod, T, axis=0).reshape(groups, rows, 6 * D) if per_token else mod[:, None, :]
    sh1, sc1, gt1, sh2, sc2, gt2 = jnp.split(mods, 6, axis=-1)
    xg = x.reshape(groups, rows, D)
    tm_in = min(rows, 512)
    proj, h, *kv_rows = _norm_in(xg, lp['norm1'], sc1, sh1, lp['w_in_b'], tm_in)
    kv_shape = (B, T, 2, NSA_KV_HEADS, HEAD_DIM)
    cmp_rows, slc_rows, win_rows = (a.reshape(B, T * KV_PARTS, HEAD_DIM) for a in kv_rows)
    cmp_new, slc_new = cmp_rows.reshape(kv_shape), slc_rows.reshape(kv_shape)

    if pool_cmp is None:
        kvc = _compress_prompt(cmp_rows, _cmp_weights(lp))
        o_nsa = _nsa_prompt(proj.reshape(B * T, IN_WIDTH), kvc, B, T)
        win_len = min(WINDOW, past_len)
        assert T >= win_len
        win_state = win_rows[:, (T - win_len) * KV_PARTS:].reshape((B, win_len) + kv_shape[2:])
    else:
        n_past = page_table.shape[1] * PAGE_SIZE
        assert n_past % CMP_STRIDE == 0 and T < CMP_STRIDE and T <= 8
        assert -(-(n_past + T) // SEL_BLOCK) <= TOPK_LANES and n_past % SEL_BLOCK == 0 and T <= SEL_BLOCK
        assert win_buf.shape[1] == WINDOW
        pages = page_table + layer * pool_cmp.shape[1]
        kvc = _compress_paged(pool_cmp.reshape(-1, HEAD_DIM), pages, _cmp_weights(lp))
        win_buf_rows = win_buf.reshape(B, WINDOW * KV_PARTS, HEAD_DIM)
        o_nsa = _nsa_decode_rows(proj.reshape(B, T, IN_WIDTH), kvc, win_buf_rows, pool_slc.reshape(-1, HEAD_DIM),
                                 pages, n_past)
        win_state = jnp.concatenate([win_buf_rows[:, T * KV_PARTS:], win_rows], axis=1).reshape(
            (B, WINDOW) + kv_shape[2:])

    shift0 = shift0.astype(h.dtype)
    p0 = _matmul(shift0, lp['w_in_b'][:, RKV_OFF:CMP_OFF], B, RWKV_WIDTH)
    if per_token:
        h0 = jnp.repeat(shift0, T, axis=0).reshape(groups, rows, D)
        p0 = jnp.repeat(p0, T, axis=0).reshape(groups, rows, 3 * RWKV_WIDTH)
    else:
        h0, p0 = shift0[:, None], p0[:, None]
    tm_rw = min(rows, 256)
    seqs = _rwkv_prep(h, proj, h0, p0, lp, tm_rw, T, pack_major=not per_token)
    seqs, (gate, bonus) = seqs[:8], seqs[8:]
    if per_token:
        t_pad = -(-T // WKV_MIN_CHUNK) * WKV_MIN_CHUNK

        def pairs(a, fill):
            a = jnp.pad(a.reshape(B, T, RW_PACKS, WKV_LANES), ((0, 0), (0, t_pad - T), (0, 0), (0, 0)),
                        constant_values=fill)
            return a.transpose(0, 2, 1, 3)

        fills = (0.0, 1.0) + (0.0,) * 6
        y, wkv_T = _wkv_scan(*[pairs(a, f) for a, f in zip(seqs, fills)], wkv0.astype(jnp.float32), WKV_MIN_CHUNK)
        y = y.transpose(0, 2, 1, 3)[:, :T].reshape(groups, rows, RWKV_WIDTH)
    else:
        y, wkv_T = _wkv_scan(*seqs, wkv0.astype(jnp.float32), WKV_CHUNK)
    o_rwkv = _rwkv_post(y, gate, bonus, lp['ln_x_w'], lp['ln_x_b'], tm_rw, pack_major=not per_token)

    x1, h2, logits = _mix_out(o_nsa.reshape(groups, rows, NSA_WIDTH), o_rwkv, lp['w_out_b'], xg, gt1, sc2, sh2,
                              lp['norm2'], lp['wr_hi'], lp['wr_lo'], lp['br'], min(rows, 256))
    return (x1, h2, logits, gt2), (cmp_new, slc_new, win_state, wkv_T, h.reshape(B, T, D)[:, -1])


def kernel(x_prompt, x_sample, c_prompt, c_sample, cache_cmp_kv, cache_slc_kv, page_table, state_win_kv, state_wkv, state_shift, w_ada, b_ada, norm1, w_in, cmp_k_w1, cmp_k_pe, cmp_k_w2, cmp_v_w1, cmp_v_pe, cmp_v_w2, mu_rkv, mu_wag, decay_w0, decay_w1, decay_w2, iclr_a0, iclr_a1, iclr_a2, gate_g1, gate_g2, k_k, k_a, r_k, ln_x_w, ln_x_b, w_out, norm2, w_router_group, b_router_group, w_router_expert, b_router_expert, w_gate, w_up, w_down, norm_f):
    bp, tp = x_prompt.shape[:2]
    ts = x_sample.shape[1]
    past_len = page_table.shape[1] * PAGE_SIZE
    assert DEPTH == 1 and w_in.shape[0] == 1
    l = 0
    bs = x_sample.shape[0]
    f32, bf16 = jnp.float32, jnp.bfloat16
    lyr = lambda a: a.reshape(a.shape[1:])
    wr = jnp.concatenate([lyr(w_router_group), lyr(w_router_expert),
                          jnp.zeros((D_MODEL, ROUTER_LANES - N_GROUPS - N_EXPERTS), f32)], axis=1)
    wr_hi = wr.astype(bf16)
    br = jnp.concatenate([lyr(b_router_group), lyr(b_router_expert),
                          jnp.zeros((ROUTER_LANES - N_GROUPS - N_EXPERTS,), f32)]).reshape(1, ROUTER_LANES)
    lp = dict(norm1=lyr(norm1), w_in_b=_permute_w_in(lyr(w_in)).astype(bf16),
              cmp_k_w1=lyr(cmp_k_w1), cmp_k_pe=lyr(cmp_k_pe), cmp_k_w2=lyr(cmp_k_w2),
              cmp_v_w1=lyr(cmp_v_w1), cmp_v_pe=lyr(cmp_v_pe), cmp_v_w2=lyr(cmp_v_w2),
              mu_rkv=lyr(mu_rkv), mu_wag=lyr(mu_wag), decay_w0=lyr(decay_w0), decay_w1=lyr(decay_w1),
              decay_w2=lyr(decay_w2), iclr_a0=lyr(iclr_a0), iclr_a1=lyr(iclr_a1), iclr_a2=lyr(iclr_a2),
              gate_g1=lyr(gate_g1), gate_g2=lyr(gate_g2), k_k=lyr(k_k), k_a=lyr(k_a), r_k=lyr(r_k),
              ln_x_w=lyr(ln_x_w), ln_x_b=lyr(ln_x_b), w_out_b=lyr(w_out).astype(bf16), norm2=lyr(norm2),
              wr_hi=wr_hi, wr_lo=(wr - wr_hi.astype(f32)).astype(bf16), br=br)

    c_all = jnp.concatenate([c_prompt, c_sample], axis=0)
    mod_all = _matmul(jax.nn.silu(c_all), lyr(w_ada), c_all.shape[0], 1024) + lyr(b_ada)
    mod_p, mod_s = mod_all[:bp], mod_all[bp:]

    def experts(h2, logits, tm):
        n = h2.shape[0] * h2.shape[1]
        eid, ew = _hier_route(logits.reshape(n, ROUTER_LANES))
        return _moe_ffn(h2.reshape(n, D_MODEL), eid, lyr(w_gate), lyr(w_up), lyr(w_down), tm) + (ew,)

    wkv_zero = jnp.zeros((bp, RWKV_HEADS, RWKV_HEAD_DIM, RWKV_HEAD_DIM), f32)
    shift_zero = jnp.zeros((bp, D_MODEL), x_prompt.dtype)
    (x1p, h2p, lgp, gt2p), (a1, a2, a3, a4, a5) = _layer_front(
        x_prompt, mod_p, lp, l, None, None, None, None, wkv_zero, shift_zero, past_len, tp)
    y0p, y1p, ewp = experts(h2p, lgp, MOE_TM)
    (x1s, h2s, lgs, gt2s), (b1, b2, b3, b4, b5) = _layer_front(
        x_sample, mod_s, lp, l, cache_cmp_kv, cache_slc_kv, page_table, lyr(state_win_kv), lyr(state_wkv),
        lyr(state_shift), past_len, bs * ts)
    y0s, y1s, ews = experts(h2s, lgs, MOE_TM_DECODE)
    y_prompt = _final(x1p, y0p, y1p, ewp, 0, gt2p, norm_f, 256).reshape(x_prompt.shape)
    y_sample = _final(x1s, y0s, y1s, ews, 0, gt2s, norm_f, x1s.shape[1]).reshape(x_sample.shape)
    st = lambda a: a[None]
    return (y_prompt, y_sample, st(a1), st(b1), st(a2), st(b2), st(a3), st(b3), st(a4), st(b4), st(a5), st(b5))
```
